```python
import math
import jax
import jax.numpy as jnp
from jax import lax
import numpy as np

D_MODEL = 4096
BATCH = 4
SEQ = 2048
DEPTH = 2
DEC_BATCH = 8
DEC_SEQ = 4
PAST_LEN = 16384
PAGE_SIZE = 128

HEAD_DIM = 128
N_HEADS = D_MODEL // HEAD_DIM
H_A = N_HEADS // 2
DA = HEAD_DIM // 2
HKV_A = H_A // 2
H_B = N_HEADS - H_A
HKV_B = H_B // 2
H_C = N_HEADS
HKV_C = 4
CMP_LEN = 32
CMP_STRIDE = 16
CMP_HID = 2 * HEAD_DIM
SLC_BLOCK = 64
N_SELECT = 16
WINDOW = 512
D_FF = -(-8 * D_MODEL // (3 * 256)) * 256
ROPE_THETA = 10000.0
NORM_EPS = 1e-6
Q_BLOCK = 128
NSA_Q_BLOCK = 32
FORGET_BIAS = 2.0
N_EVEN = (DEPTH + 1) // 2
N_ODD = DEPTH // 2
EVEN_COLS = (H_A * 2 * DA, HKV_A * 2 * DA, HKV_A * HEAD_DIM,
             H_B * HEAD_DIM, HKV_B * HEAD_DIM, HKV_B * HEAD_DIM, H_B)
ODD_COLS = (H_C * HEAD_DIM, 2 * HKV_C * HEAD_DIM, 2 * HKV_C * HEAD_DIM,
            2 * HKV_C * HEAD_DIM, 3 * H_C)

kernel_name = 'hybrid_diff_fox_nsa_decode_step'


def _rmsnorm(x, g):
    xf = x.astype(jnp.float32)
    xf = xf * lax.rsqrt(jnp.mean(jnp.square(xf), axis=-1, keepdims=True) + NORM_EPS)
    return (xf * g.astype(jnp.float32)).astype(x.dtype)


def _rope(x, pos):
    half = x.shape[-1] // 2
    inv = ROPE_THETA ** (-jnp.arange(half, dtype=jnp.float32) / half)
    ang = pos.astype(jnp.float32)[:, None] * inv
    ang = ang.reshape(ang.shape[:1] + (1,) * (x.ndim - 3) + (half,))
    cos, sin = jnp.cos(ang), jnp.sin(ang)
    xf = x.astype(jnp.float32)
    x1, x2 = xf[..., :half], xf[..., half:]
    return jnp.concatenate([x1 * cos - x2 * sin, x2 * cos + x1 * sin], axis=-1).astype(x.dtype)


def _masked_softmax(s, mask):
    s = jnp.where(mask, s, -jnp.inf)
    m = jnp.max(s, axis=-1, keepdims=True)
    m = jnp.where(jnp.isfinite(m), m, 0.0)
    e = jnp.exp(s - m)
    return e / jnp.maximum(jnp.sum(e, axis=-1, keepdims=True), 1e-30)


def _block_len(t, pref):
    return pref if t % pref == 0 else t


def _sweep(fn, xs, qpos, blk):
    t = qpos.shape[0]
    nb = t // blk
    xb = tuple(jnp.moveaxis(x.reshape(x.shape[0], nb, blk, *x.shape[2:]), 1, 0) for x in xs)
    out = lax.map(lambda a: fn(*a[0], a[1]), (xb, qpos.reshape(nb, blk)))
    out = jnp.moveaxis(out, 0, 1)
    return out.reshape(out.shape[0], t, *out.shape[3:])


def _split_cols(z, widths):
    out, start = [], 0
    for w in widths:
        out.append(z[..., start:start + w])
        start += w
    return out


def _gather_pages(cache, page_table):
    rows = cache[page_table]
    return rows.reshape(rows.shape[0], rows.shape[1] * rows.shape[2], *rows.shape[3:])


def _swiglu(h, w_gate, w_up, w_down):
    return (jax.nn.silu(h @ w_gate) * (h @ w_up)) @ w_down


def _diff_attention(q, k, v, lam, qpos, kpos):
    b, t = q.shape[:2]
    qg = q.reshape(b, t, HKV_A, H_A // HKV_A, 2, DA)
    scale = DA ** -0.5

    def blk(qb, pb):
        s = jnp.einsum('btkgcd,bskcd->bkgcts', qb, k, preferred_element_type=jnp.float32) * scale
        p = _masked_softmax(s, kpos[None, :] <= pb[:, None])
        w = p[:, :, :, 0] - lam * p[:, :, :, 1]
        return jnp.einsum('bkgts,bskd->btkgd', w.astype(v.dtype), v)

    o = _sweep(blk, (qg,), qpos, _block_len(t, Q_BLOCK))
    return o.reshape(b, t, H_A, HEAD_DIM)


def _forgetting_attention(q, k, v, dq, dk, qpos, kpos):
    b, t = q.shape[:2]
    s_len = k.shape[1]
    g = H_B // HKV_B
    qg = q.reshape(b, t, HKV_B, g, HEAD_DIM)
    dqg = dq.reshape(b, t, HKV_B, g)
    bias_k = dk.reshape(b, s_len, HKV_B, g).transpose(0, 2, 3, 1)[:, :, :, None, :]
    scale = HEAD_DIM ** -0.5

    def blk(qb, dqb, pb):
        s = jnp.einsum('btkgd,bskd->bkgts', qb, k, preferred_element_type=jnp.float32) * scale
        s = s + bias_k - dqb.transpose(0, 2, 3, 1)[..., None]
        p = _masked_softmax(s, kpos[None, :] <= pb[:, None])
        return jnp.einsum('bkgts,bskd->btkgd', p.astype(v.dtype), v)

    o = _sweep(blk, (qg, dqg), qpos, _block_len(t, Q_BLOCK))
    return o.reshape(b, t, H_B, HEAD_DIM)


def _even_mixer(h, past_a, past_b, past_f, p0, lam_init,
                w_in, b_f, lam_q1, lam_k1, lam_q2, lam_k2, subln_g, w_out):
    b, t, _ = h.shape
    qa, ka, va, qf, kf, vf, fg = _split_cols(h @ w_in, EVEN_COLS)
    qpos = p0 + jnp.arange(t, dtype=jnp.int32)
    qa = _rope(qa.reshape(b, t, H_A, 2, DA), qpos)
    ka = _rope(ka.reshape(b, t, HKV_A, 2, DA), qpos).reshape(b, t, HKV_A, HEAD_DIM)
    kv_a_new = jnp.stack([ka, va.reshape(b, t, HKV_A, HEAD_DIM)], axis=2)
    kv_b_new = jnp.stack([kf.reshape(b, t, HKV_B, HEAD_DIM), vf.reshape(b, t, HKV_B, HEAD_DIM)], axis=2)
    logf_new = jax.nn.log_sigmoid((fg + b_f).astype(jnp.float32))
    if past_a is None:
        kv_a, kv_b, logf = kv_a_new, kv_b_new, logf_new
    else:
        kv_a = jnp.concatenate([past_a, kv_a_new], axis=1)
        kv_b = jnp.concatenate([past_b, kv_b_new], axis=1)
        logf = jnp.concatenate([past_f.astype(jnp.float32), logf_new], axis=1)
    s_len = kv_a.shape[1]
    kpos = jnp.arange(s_len, dtype=jnp.int32)
    lam = (jnp.exp(jnp.sum((lam_q1 * lam_k1).astype(jnp.float32)))
           - jnp.exp(jnp.sum((lam_q2 * lam_k2).astype(jnp.float32))) + lam_init)
    o_a = _diff_attention(qa, kv_a[:, :, 0].reshape(b, s_len, HKV_A, 2, DA), kv_a[:, :, 1], lam, qpos, kpos)
    o_a = _rmsnorm(o_a, subln_g) * (1.0 - lam_init)
    dk = lax.cumsum(logf, axis=1, reverse=True) - logf
    o_b = _forgetting_attention(qf.reshape(b, t, H_B, HEAD_DIM), kv_b[:, :, 0], kv_b[:, :, 1],
                                dk[:, s_len - t:], dk, qpos, kpos)
    o = jnp.concatenate([o_a.reshape(b, t, H_A * HEAD_DIM), o_b.reshape(b, t, H_B * HEAD_DIM)], axis=-1)
    return o.astype(h.dtype) @ w_out, kv_a_new, kv_b_new, logf_new


def _compress(chunks, pe, w1, w2):
    r = CMP_LEN // CMP_STRIDE
    nblk = chunks.shape[1] - r + 1
    pre = 0.0
    for m in range(r):
        rows = slice(m * CMP_STRIDE, (m + 1) * CMP_STRIDE)
        pre = pre + jnp.einsum('bnlkd,ldh->bnkh', chunks[:, m:m + nblk] + pe[rows][:, None, :], w1[rows])
    return jnp.einsum('bnkh,hd->bnkd', jax.nn.gelu(pre), w2)


def _odd_mixer(h, past_c, past_s, win, p0,
               w_in, b_g, pe_k, w1_k, w2_k, pe_v, w1_v, w2_v, w_out):
    b, t, _ = h.shape
    g = H_C // HKV_C
    scale = HEAD_DIM ** -0.5
    q, kvc, kvs, kvw, gl = _split_cols(h @ w_in, ODD_COLS)
    qpos = p0 + jnp.arange(t, dtype=jnp.int32)
    q = q.reshape(b, t, HKV_C, g, HEAD_DIM)
    qr = _rope(q, qpos)
    kvc = kvc.reshape(b, t, 2, HKV_C, HEAD_DIM)
    kvs = kvs.reshape(b, t, 2, HKV_C, HEAD_DIM)
    kvw = kvw.reshape(b, t, 2, HKV_C, HEAD_DIM)
    kvs = jnp.stack([_rope(kvs[:, :, 0], qpos), kvs[:, :, 1]], axis=2)
    kvw = jnp.stack([_rope(kvw[:, :, 0], qpos), kvw[:, :, 1]], axis=2)
    gates = jax.nn.sigmoid((gl + b_g).astype(jnp.float32)).reshape(b, t, 3, HKV_C, g)
    kv_c_all = kvc if past_c is None else jnp.concatenate([past_c, kvc], axis=1)
    kv_s_all = kvs if past_s is None else jnp.concatenate([past_s, kvs], axis=1)
    s_len = kv_c_all.shape[1]
    nch = s_len // CMP_STRIDE
    chunks = kv_c_all[:, :nch * CMP_STRIDE].reshape(b, nch, CMP_STRIDE, 2, HKV_C, HEAD_DIM)
    k_cmp = _compress(chunks[:, :, :, 0], pe_k, w1_k, w2_k)
    v_cmp = _compress(chunks[:, :, :, 1], pe_v, w1_v, w2_v)
    nc = k_cmp.shape[1]
    cmp_end = jnp.arange(nc, dtype=jnp.int32) * CMP_STRIDE + CMP_LEN - 1
    ns = -(-s_len // SLC_BLOCK)
    n_sel = min(N_SELECT, ns)
    kv_s_pad = jnp.pad(kv_s_all, ((0, 0), (0, ns * SLC_BLOCK - s_len), (0, 0), (0, 0), (0, 0)))
    slc_blocks = kv_s_pad.reshape(b, ns, SLC_BLOCK, 2, HKV_C, HEAD_DIM).transpose(0, 4, 1, 2, 3, 5)
    ci = jnp.arange(nc, dtype=jnp.int32)[:, None] * CMP_STRIDE
    sj = jnp.arange(ns, dtype=jnp.int32)[None, :] * SLC_BLOCK
    overlap = (jnp.maximum(jnp.minimum(ci + CMP_LEN, sj + SLC_BLOCK) - jnp.maximum(ci, sj), 0)
               .astype(jnp.float32) / CMP_LEN)
    jb = jnp.arange(ns, dtype=jnp.int32)
    b_idx = jnp.arange(b)[:, None, None, None]
    k_idx = jnp.arange(HKV_C)[None, None, :, None]
    front = 0 if win is None else win.shape[1]
    pad = WINDOW - front
    prev = [] if win is None else [win]
    kv_w_ext = jnp.concatenate([jnp.zeros((b, pad, 2, HKV_C, HEAD_DIM), kvw.dtype)] + prev + [kvw], axis=1)
    wpos_ext = jnp.concatenate([jnp.full((pad,), -1, jnp.int32),
                                p0 - front + jnp.arange(front + t, dtype=jnp.int32)])

    def blk(qb, qrb, gb, pb):
        tb = pb.shape[0]
        s = jnp.einsum('btkgd,bnkd->bkgtn', qb, k_cmp, preferred_element_type=jnp.float32) * scale
        p_cmp = _masked_softmax(s, cmp_end[None, :] <= pb[:, None])
        o_cmp = jnp.einsum('bkgtn,bnkd->btkgd', p_cmp.astype(v_cmp.dtype), v_cmp)
        imp = jnp.einsum('bkgtn,nj->btkj', p_cmp, overlap)
        cur = pb // SLC_BLOCK
        forced = (jb[None, :] == 0) | (jb[None, :] == cur[:, None]) | (jb[None, :] == cur[:, None] - 1)
        valid = jb[None, :] * SLC_BLOCK <= pb[:, None]
        imp = jnp.where(forced[None, :, None, :], jnp.inf, jnp.where(valid[None, :, None, :], imp, -jnp.inf))
        _, idx = lax.top_k(imp, n_sel)
        kg = slc_blocks[b_idx, k_idx, idx]
        kg = kg.reshape(b, tb, HKV_C, n_sel * SLC_BLOCK, 2, HEAD_DIM)
        kpos_s = (idx[..., None] * SLC_BLOCK + jnp.arange(SLC_BLOCK, dtype=jnp.int32)).reshape(b, tb, HKV_C, -1)
        mask_s = (kpos_s <= pb[None, :, None, None]).transpose(0, 2, 1, 3)[:, :, None]
        s = jnp.einsum('btkgd,btkmd->bkgtm', qrb, kg[..., 0, :], preferred_element_type=jnp.float32) * scale
        p = _masked_softmax(s, mask_s)
        o_slc = jnp.einsum('bkgtm,btkmd->btkgd', p.astype(kg.dtype), kg[..., 1, :])
        start = pb[0] - p0
        kw = lax.dynamic_slice_in_dim(kv_w_ext, start, tb + WINDOW, axis=1)
        kwpos = lax.dynamic_slice_in_dim(wpos_ext, start, tb + WINDOW)
        dist = pb[:, None] - kwpos[None, :]
        mask_w = (kwpos[None, :] >= 0) & (dist >= 0) & (dist < WINDOW)
        s = jnp.einsum('btkgd,bskd->bkgts', qrb, kw[:, :, 0], preferred_element_type=jnp.float32) * scale
        p = _masked_softmax(s, mask_w)
        o_win = jnp.einsum('bkgts,bskd->btkgd', p.astype(kw.dtype), kw[:, :, 1])
        gg = gb[..., None]
        return gg[:, :, 0] * o_cmp + gg[:, :, 1] * o_slc + gg[:, :, 2] * o_win

    o = _sweep(blk, (q, qr, gates), qpos, _block_len(t, NSA_Q_BLOCK))
    y = o.reshape(b, t, H_C * HEAD_DIM).astype(h.dtype) @ w_out
    keep = min(WINDOW, front + t)
    win_new = jnp.concatenate(prev + [kvw], axis=1)[:, -keep:]
    return y, kvc, kvs, win_new


def setup_inputs(seed: int = 0) -> dict:
    key = jax.random.key(seed)
    keys = iter(jax.random.split(key, 40))

    def nrm(shape, scale):
        return scale * jax.random.normal(next(keys), shape, jnp.float32)

    def gain(shape):
        return 1.0 + nrm(shape, 0.02)

    n_pages = PAST_LEN // PAGE_SIZE
    n_used = DEC_BATCH * n_pages
    n_pool = n_used + n_used // 4
    win_rows = min(WINDOW, PAST_LEN)
    s_d = D_MODEL ** -0.5
    inp = {}
    inp['x_prompt'] = nrm((BATCH, SEQ, D_MODEL), 1.0)
    inp['x_sample'] = nrm((DEC_BATCH, DEC_SEQ, D_MODEL), 1.0)
    inp['cache_a_kv'] = nrm((N_EVEN, n_pool, PAGE_SIZE, 2, HKV_A, HEAD_DIM), 1.0)
    inp['cache_b_kv'] = nrm((N_EVEN, n_pool, PAGE_SIZE, 2, HKV_B, HEAD_DIM), 1.0)
    inp['cache_b_logf'] = jax.nn.log_sigmoid(FORGET_BIAS + nrm((N_EVEN, n_pool, PAGE_SIZE, H_B), 1.0))
    inp['cache_cmp_kv'] = nrm((N_ODD, n_pool, PAGE_SIZE, 2, HKV_C, HEAD_DIM), 1.0)
    inp['cache_slc_kv'] = nrm((N_ODD, n_pool, PAGE_SIZE, 2, HKV_C, HEAD_DIM), 1.0)
    inp['state_win_kv'] = nrm((N_ODD, DEC_BATCH, win_rows, 2, HKV_C, HEAD_DIM), 1.0)
    inp['page_table'] = jax.random.permutation(next(keys), n_pool)[:n_used].reshape(DEC_BATCH, n_pages).astype(jnp.int32)
    inp['norm_mix_e'] = gain((N_EVEN, D_MODEL))
    inp['w_in_e'] = nrm((N_EVEN, D_MODEL, sum(EVEN_COLS)), s_d)
    inp['b_forget'] = FORGET_BIAS + nrm((N_EVEN, H_B), 0.1)
    inp['lam_q1'] = nrm((N_EVEN, DA), 0.1)
    inp['lam_k1'] = nrm((N_EVEN, DA), 0.1)
    inp['lam_q2'] = nrm((N_EVEN, DA), 0.1)
    inp['lam_k2'] = nrm((N_EVEN, DA), 0.1)
    inp['subln_g'] = gain((N_EVEN, HEAD_DIM))
    inp['w_out_e'] = nrm((N_EVEN, (H_A + H_B) * HEAD_DIM, D_MODEL), s_d)
    inp['norm_mix_o'] = gain((N_ODD, D_MODEL))
    inp['w_in_o'] = nrm((N_ODD, D_MODEL, sum(ODD_COLS)), s_d)
    inp['b_gate'] = nrm((N_ODD, 3 * H_C), 0.1)
    inp['cmp_pe_k'] = nrm((N_ODD, CMP_LEN, HEAD_DIM), 0.1)
    inp['cmp_w1_k'] = nrm((N_ODD, CMP_LEN, HEAD_DIM, CMP_HID), (CMP_LEN * HEAD_DIM) ** -0.5)
    inp['cmp_w2_k'] = nrm((N_ODD, CMP_HID, HEAD_DIM), CMP_HID ** -0.5)
    inp['cmp_pe_v'] = nrm((N_ODD, CMP_LEN, HEAD_DIM), 0.1)
    inp['cmp_w1_v'] = nrm((N_ODD, CMP_LEN, HEAD_DIM, CMP_HID), (CMP_LEN * HEAD_DIM) ** -0.5)
    inp['cmp_w2_v'] = nrm((N_ODD, CMP_HID, HEAD_DIM), CMP_HID ** -0.5)
    inp['w_out_o'] = nrm((N_ODD, H_C * HEAD_DIM, D_MODEL), s_d)
    inp['norm_ffn'] = gain((DEPTH, D_MODEL))
    inp['w_gate'] = nrm((DEPTH, D_MODEL, D_FF), s_d)
    inp['w_up'] = nrm((DEPTH, D_MODEL, D_FF), s_d)
    inp['w_down'] = nrm((DEPTH, D_FF, D_MODEL), D_FF ** -0.5)
    inp['norm_final'] = gain((D_MODEL,))
    return inp


def reference(x_prompt, x_sample, cache_a_kv, cache_b_kv, cache_b_logf, cache_cmp_kv, cache_slc_kv,
              state_win_kv, page_table, norm_mix_e, w_in_e, b_forget, lam_q1, lam_k1, lam_q2, lam_k2,
              subln_g, w_out_e, norm_mix_o, w_in_o, b_gate, cmp_pe_k, cmp_w1_k, cmp_w2_k, cmp_pe_v,
              cmp_w1_v, cmp_w2_v, w_out_o, norm_ffn, w_gate, w_up, w_down, norm_final):
    past_len = page_table.shape[1] * PAGE_SIZE
    xp, xs = x_prompt, x_sample
    a_p, a_s, b_p, b_s, f_p, f_s = [], [], [], [], [], []
    c_p, c_s, s_p, s_s, w_p, w_s = [], [], [], [], [], []
    for l in range(DEPTH):
        i = l // 2
        if l % 2 == 0:
            lam_init = 0.8 - 0.6 * math.exp(-0.3 * l)
            prm = (lam_init, w_in_e[i], b_forget[i], lam_q1[i], lam_k1[i], lam_q2[i], lam_k2[i],
                   subln_g[i], w_out_e[i])
            y, kva, kvb, lf = _even_mixer(_rmsnorm(xp, norm_mix_e[i]), None, None, None, 0, *prm)
            xp = xp + y
            a_p.append(kva)
            b_p.append(kvb)
            f_p.append(lf)
            y, kva, kvb, lf = _even_mixer(_rmsnorm(xs, norm_mix_e[i]),
                                          _gather_pages(cache_a_kv[i], page_table),
                                          _gather_pages(cache_b_kv[i], page_table),
                                          _gather_pages(cache_b_logf[i], page_table),
                                          past_len, *prm)
            xs = xs + y
            a_s.append(kva)
            b_s.append(kvb)
            f_s.append(lf)
        else:
            prm = (w_in_o[i], b_gate[i], cmp_pe_k[i], cmp_w1_k[i], cmp_w2_k[i], cmp_pe_v[i], cmp_w1_v[i],
                   cmp_w2_v[i], w_out_o[i])
            y, kvc, kvs, win = _odd_mixer(_rmsnorm(xp, norm_mix_o[i]), None, None, None, 0, *prm)
            xp = xp + y
            c_p.append(kvc)
            s_p.append(kvs)
            w_p.append(win)
            y, kvc, kvs, win = _odd_mixer(_rmsnorm(xs, norm_mix_o[i]),
                                          _gather_pages(cache_cmp_kv[i], page_table),
                                          _gather_pages(cache_slc_kv[i], page_table),
                                          state_win_kv[i], past_len, *prm)
            xs = xs + y
            c_s.append(kvc)
            s_s.append(kvs)
            w_s.append(win)
        xp = xp + _swiglu(_rmsnorm(xp, norm_ffn[l]), w_gate[l], w_up[l], w_down[l])
        xs = xs + _swiglu(_rmsnorm(xs, norm_ffn[l]), w_gate[l], w_up[l], w_down[l])
    y_prompt = _rmsnorm(xp, norm_final)
    y_sample = _rmsnorm(xs, norm_final)
    return (y_prompt, y_sample, jnp.stack(a_p), jnp.stack(a_s), jnp.stack(b_p), jnp.stack(b_s),
            jnp.stack(f_p), jnp.stack(f_s), jnp.stack(c_p), jnp.stack(c_s), jnp.stack(s_p), jnp.stack(s_s),
            jnp.stack(w_p), jnp.stack(w_s))
```

```python
import functools

import numpy as np
import jax
import jax.numpy as jnp
from jax import lax
from jax.experimental import pallas as pl
from jax.experimental.pallas import tpu as pltpu

F32 = jnp.float32
BF16 = jnp.bfloat16
HIGHEST = lax.Precision.HIGHEST

LANES = 128
SUBLANES = 8
VMEM_LIMIT = 56 * 1024 * 1024

HEAD_DIM = 128
DA = HEAD_DIM // 2
CMP_LEN = 32
CMP_STRIDE = 16
CMP_HID = 2 * HEAD_DIM
SLC_BLOCK = 64
N_SELECT = 16
WINDOW = 512
ROPE_THETA = 10000.0
NORM_EPS = 1e-6
NEG = -1e30
BIG = 1e30
LOWEST = -3e38


def _params(*sem):
    return pltpu.CompilerParams(dimension_semantics=sem, vmem_limit_bytes=VMEM_LIMIT)


def _dot(a, b):
    return jnp.dot(a, b, preferred_element_type=F32)


def _dot_nt(a, b):
    return lax.dot_general(a, b, (((1,), (1,)), ((), ())), preferred_element_type=F32)


def _dot_exact(a, b):
    return jnp.dot(a, b, precision=HIGHEST, preferred_element_type=F32)


def _iota(shape, axis):
    return lax.broadcasted_iota(jnp.int32, shape, axis)


def _round_up(n, m):
    return -(-n // m) * m


def _rope_lanes(a, cos, sin, half):
    if 2 * half == LANES:
        partner = pltpu.roll(a, half, axis=1)
    else:
        first = (_iota(a.shape, 1) & (2 * half - 1)) < half
        partner = jnp.where(first, pltpu.roll(a, LANES - half, axis=1), pltpu.roll(a, half, axis=1))
    return a * cos + partner * sin


def _rmsnorm_rows(x, g):
    ms = jnp.mean(x * x, axis=-1, keepdims=True)
    return x * lax.rsqrt(ms + NORM_EPS) * g


def _proj_kernel(x_ref, g_ref, w_ref, cos_ref, sin_ref, o_ref, xn_ref, *, rope_ranges, half):
    j = pl.program_id(1)

    @pl.when(j == 0)
    def _():
        xn_ref[...] = _rmsnorm_rows(x_ref[...], g_ref[...]).astype(BF16)

    acc = _dot(xn_ref[...], w_ref[...])
    is_rope = functools.reduce(jnp.logical_or, [(j >= a) & (j < b) for a, b in rope_ranges])

    @pl.when(is_rope)
    def _():
        cos, sin = cos_ref[...], sin_ref[...]
        for c in range(acc.shape[1] // LANES):
            sl = slice(c * LANES, (c + 1) * LANES)
            o_ref[:, sl] = _rope_lanes(acc[:, sl], cos, sin, half)

    @pl.when(jnp.logical_not(is_rope))
    def _():
        o_ref[...] = acc


def _proj(x, g, w, cos, sin, rope_ranges, half, tm, tn):
    m, d = x.shape
    n = w.shape[1]
    return pl.pallas_call(
        functools.partial(_proj_kernel, rope_ranges=rope_ranges, half=half),
        grid=(m // tm, n // tn),
        in_specs=[
            pl.BlockSpec((tm, d), lambda i, j: (i, 0)),
            pl.BlockSpec((1, d), lambda i, j: (0, 0)),
            pl.BlockSpec((d, tn), lambda i, j: (0, j)),
            pl.BlockSpec((tm, LANES), lambda i, j: (i, 0)),
            pl.BlockSpec((tm, LANES), lambda i, j: (i, 0)),
        ],
        out_specs=pl.BlockSpec((tm, tn), lambda i, j: (i, j)),
        out_shape=jax.ShapeDtypeStruct((m, n), F32),
        scratch_shapes=[pltpu.VMEM((tm, d), BF16)],
        compiler_params=_params("parallel", "arbitrary"),
    )(x, g.reshape(1, d), w, cos, sin)


def _mm_res_kernel(*refs, n_lhs):
    a_refs, (w_ref, r_ref, o_ref) = refs[:n_lhs], refs[n_lhs:]
    acc = r_ref[...]
    off = 0
    for a_ref in a_refs:
        kk = a_ref.shape[1]
        acc = acc + _dot(a_ref[...], w_ref[off:off + kk, :])
        off += kk
    o_ref[...] = acc


def _mm_res(lhs, w, res, tm, tn):
    m, n = res.shape
    k = w.shape[0]
    return pl.pallas_call(
        functools.partial(_mm_res_kernel, n_lhs=len(lhs)),
        grid=(m // tm, n // tn),
        in_specs=[pl.BlockSpec((tm, a.shape[1]), lambda i, j: (i, 0)) for a in lhs] + [
            pl.BlockSpec((k, tn), lambda i, j: (0, j)),
            pl.BlockSpec((tm, tn), lambda i, j: (i, j)),
        ],
        out_specs=pl.BlockSpec((tm, tn), lambda i, j: (i, j)),
        out_shape=jax.ShapeDtypeStruct((m, n), F32),
        compiler_params=_params("parallel", "arbitrary"),
    )(*lhs, w, res)


def _ffn_kernel(x_ref, g_ref, wg_ref, wu_ref, wd_ref, gf_ref, o_ref, xn_ref, *, final_norm):
    f = pl.program_id(1)

    @pl.when(f == 0)
    def _():
        x = x_ref[...]
        xn_ref[...] = _rmsnorm_rows(x, g_ref[...]).astype(BF16)
        o_ref[...] = x

    xn = xn_ref[...]
    gate = _dot(xn, wg_ref[...])
    up = _dot(xn, wu_ref[...])
    h = (gate * jax.nn.sigmoid(gate) * up).astype(BF16)
    o_ref[...] += _dot(h, wd_ref[...])

    if final_norm:
        @pl.when(f == pl.num_programs(1) - 1)
        def _():
            o_ref[...] = _rmsnorm_rows(o_ref[...], gf_ref[...])


def _ffn(x, g, wg, wu, wd, g_final, tm, tf):
    m, d = x.shape
    ff = wg.shape[1]
    final_norm = g_final is not None
    gf = (g_final if final_norm else g).reshape(1, d)
    return pl.pallas_call(
        functools.partial(_ffn_kernel, final_norm=final_norm),
        grid=(m // tm, ff // tf),
        in_specs=[
            pl.BlockSpec((tm, d), lambda i, f: (i, 0), pipeline_mode=pl.Buffered(1)),
            pl.BlockSpec((1, d), lambda i, f: (0, 0)),
            pl.BlockSpec((d, tf), lambda i, f: (0, f)),
            pl.BlockSpec((d, tf), lambda i, f: (0, f)),
            pl.BlockSpec((tf, d), lambda i, f: (f, 0)),
            pl.BlockSpec((1, d), lambda i, f: (0, 0)),
        ],
        out_specs=pl.BlockSpec((tm, d), lambda i, f: (i, 0)),
        out_shape=jax.ShapeDtypeStruct((m, d), F32),
        scratch_shapes=[pltpu.VMEM((tm, d), BF16)],
        compiler_params=_params("parallel", "arbitrary"),
    )(x, g.reshape(1, d), wg, wu, wd, gf)


def _online_update(carry, s, v):
    m, l, acc = carry
    m_new = jnp.maximum(m, jnp.max(s, axis=-1, keepdims=True))
    alpha = jnp.exp(m - m_new)
    p = jnp.exp(s - m_new)
    l = alpha * l + jnp.sum(p, axis=-1, keepdims=True)
    acc = alpha * acc + _dot(p.astype(BF16), v)
    return m_new, l, acc


def _flash_init(rows, feat):
    return (jnp.full((rows, 1), NEG, F32), jnp.zeros((rows, 1), F32), jnp.zeros((rows, feat), F32))


def _masked_softmax(s, okf):
    ok = okf > 0.5
    sm = jnp.where(ok, s, NEG)
    m = jnp.max(sm, axis=-1, keepdims=True)
    e = jnp.where(ok, jnp.exp(sm - m), 0.0)
    return e / jnp.maximum(jnp.sum(e, axis=-1, keepdims=True), 1e-30)


def _lane_pick(x, lane_idx):
    return jnp.sum(jnp.where(_iota(x.shape, 1) == lane_idx, x, 0.0), axis=-1, keepdims=True)


def _topk_mask(imp, n_sel):
    lane = _iota(imp.shape, 1).astype(F32)
    sel = jnp.zeros(imp.shape, F32)
    x = imp
    for _ in range(n_sel):
        m = jnp.max(x, axis=-1, keepdims=True)
        idx = jnp.min(jnp.where(x == m, lane, 1e9), axis=-1, keepdims=True)
        hit = lane == idx
        sel = jnp.where(hit, 1.0, sel)
        x = jnp.where(hit, LOWEST, x)
    return sel


def _overlap(ci, sj):
    ov = jnp.minimum(ci + CMP_LEN, sj + SLC_BLOCK) - jnp.maximum(ci, sj)
    return jnp.maximum(ov, 0).astype(F32) * (1.0 / CMP_LEN)


def _block_importance(imp, qpos, ns):
    j = _iota(imp.shape, 1)
    cur = qpos >> 6
    forced = (j == 0) | (j == cur) | (j == cur - 1)
    valid = (j << 6) <= qpos
    out = jnp.where(forced, BIG, jnp.where(valid, imp, NEG))
    return jnp.where(j < ns, out, LOWEST)


def _lam_value(lam_ref, lam_init):
    v = lam_ref[...]
    a = jnp.exp(jnp.sum(v[0:1] * v[1:2], axis=-1, keepdims=True))
    b = jnp.exp(jnp.sum(v[2:3] * v[3:4], axis=-1, keepdims=True))
    return a - b + lam_init


def _gelu_tanh(x):
    return 0.5 * x * (1.0 + jnp.tanh(0.7978845608028654 * (x + 0.044715 * (x * x * x))))


def _log_sigmoid(z):
    return jnp.minimum(z, 0.0) - jnp.log1p(jnp.exp(-jnp.abs(z)))


def _diag_blocks(acc, row_head, n_heads):
    out = jnp.zeros((acc.shape[0], HEAD_DIM), F32)
    for k in range(n_heads):
        out = out + jnp.where(row_head == k, acc[:, k * HEAD_DIM:(k + 1) * HEAD_DIM], 0.0)
    return out


def _logf_prompt_kernel(z_ref, b_ref, lf_ref, dkc_ref, dkr_ref):
    t = z_ref.shape[1]
    lf = _log_sigmoid(z_ref[0] + b_ref[...])
    lf_ref[0] = lf
    upper = (_iota((LANES, LANES), 1) > _iota((LANES, LANES), 0)).astype(F32)
    carry = jnp.zeros((1, LANES), F32)
    for blk in reversed(range(t // LANES)):
        sl = slice(blk * LANES, (blk + 1) * LANES)
        x = lf[sl]
        d = _dot_exact(upper, x) + carry
        dkc_ref[0, sl, :] = d
        dkr_ref[0, :, sl] = d.T
        carry = carry + jnp.sum(x, axis=0, keepdims=True)


def _logf_prompt(z3, b_pad, col_blk):
    b, t, _ = z3.shape
    return pl.pallas_call(
        _logf_prompt_kernel,
        grid=(b,),
        in_specs=[pl.BlockSpec((1, t, LANES), lambda i: (i, 0, col_blk)),
                  pl.BlockSpec((1, LANES), lambda i: (0, 0))],
        out_specs=[pl.BlockSpec((1, t, LANES), lambda i: (i, 0, 0)),
                   pl.BlockSpec((1, t, LANES), lambda i: (i, 0, 0)),
                   pl.BlockSpec((1, LANES, t), lambda i: (i, 0, 0))],
        out_shape=[jax.ShapeDtypeStruct((b, t, LANES), F32),
                   jax.ShapeDtypeStruct((b, t, LANES), F32),
                   jax.ShapeDtypeStruct((b, LANES, t), F32)],
        compiler_params=_params("parallel"),
    )(z3, b_pad)


def _diff_prompt_kernel(lam_ref, g_ref, q_ref, k_ref, v_ref, o_ref, *, tq, gsz, lam_init):
    qi = pl.program_id(2)
    rows = gsz * tq
    lam = _lam_value(lam_ref, lam_init)
    q = jnp.concatenate([q_ref[0, :, g * HEAD_DIM:(g + 1) * HEAD_DIM] for g in range(gsz)], axis=0)
    lane = _iota((rows, HEAD_DIM), 1)
    qc = [jnp.where(lane < DA, q, 0.0).astype(BF16), jnp.where(lane >= DA, q, 0.0).astype(BF16)]
    scale = DA ** -0.5
    r = _iota((rows, tq), 0) & (tq - 1)
    diag_bias = jnp.where(_iota((rows, tq), 1) <= r, 0.0, NEG)

    def step(kb, carry, bias):
        k0 = pl.multiple_of(kb * tq, tq)
        ks = k_ref[0, pl.ds(k0, tq), :].astype(BF16)
        vs = v_ref[0, pl.ds(k0, tq), :].astype(BF16)
        out = []
        for c in range(2):
            s = _dot_nt(qc[c], ks) * scale
            if bias is not None:
                s = s + bias
            out.append(_online_update(carry[c], s, vs))
        return tuple(out)

    init = (_flash_init(rows, HEAD_DIM), _flash_init(rows, HEAD_DIM))
    carry = lax.fori_loop(0, qi, lambda kb, cr: step(kb, cr, None), init)
    (_, l0, a0), (_, l1, a1) = step(qi, carry, diag_bias)
    o = a0 / l0 - lam * (a1 / l1)
    o = _rmsnorm_rows(o, g_ref[...]) * (1.0 - lam_init)
    for g in range(gsz):
        o_ref[0, :, g * HEAD_DIM:(g + 1) * HEAD_DIM] = o[g * tq:(g + 1) * tq].astype(BF16)


def _diff_prompt(z3, lam_vec, subln_g, n_kv, gsz, q_blk0, k_blk0, v_blk0, lam_init, tq):
    b, t, _ = z3.shape
    qw = gsz * HEAD_DIM
    return pl.pallas_call(
        functools.partial(_diff_prompt_kernel, tq=tq, gsz=gsz, lam_init=lam_init),
        grid=(b, n_kv, t // tq),
        in_specs=[
            pl.BlockSpec((4, DA), lambda i, k, q: (0, 0)),
            pl.BlockSpec((1, HEAD_DIM), lambda i, k, q: (0, 0)),
            pl.BlockSpec((1, tq, qw), lambda i, k, q: (i, q, q_blk0 + k)),
            pl.BlockSpec((1, t, HEAD_DIM), lambda i, k, q: (i, 0, k_blk0 + k)),
            pl.BlockSpec((1, t, HEAD_DIM), lambda i, k, q: (i, 0, v_blk0 + k)),
        ],
        out_specs=pl.BlockSpec((1, tq, qw), lambda i, k, q: (i, q, k)),
        out_shape=jax.ShapeDtypeStruct((b, t, n_kv * qw), BF16),
        compiler_params=_params("parallel", "parallel", "arbitrary"),
    )(lam_vec, subln_g.reshape(1, HEAD_DIM), z3, z3, z3)


def _fox_prompt_kernel(q_ref, k_ref, v_ref, dkr_ref, dkc_ref, o_ref, *, tq, gsz):
    kh = pl.program_id(1)
    qi = pl.program_id(2)
    scale = HEAD_DIM ** -0.5
    dkc = dkc_ref[0]
    diag_bias = jnp.where(_iota((tq, tq), 1) <= _iota((tq, tq), 0), 0.0, NEG)
    for g in range(gsz):
        q = q_ref[0, :, g * HEAD_DIM:(g + 1) * HEAD_DIM].astype(BF16)
        dq = _lane_pick(dkc, kh * gsz + g)

        def step(kb, carry, bias, q=q, dq=dq, g=g):
            k0 = pl.multiple_of(kb * tq, tq)
            ks = k_ref[0, pl.ds(k0, tq), :].astype(BF16)
            vs = v_ref[0, pl.ds(k0, tq), :].astype(BF16)
            dk = dkr_ref[0, g, :, pl.ds(k0, tq)]
            s = _dot_nt(q, ks) * scale + dk - dq
            if bias is not None:
                s = s + bias
            return _online_update(carry, s, vs)

        carry = lax.fori_loop(0, qi, lambda kb, cr: step(kb, cr, None), _flash_init(tq, HEAD_DIM))
        _, l, acc = step(qi, carry, diag_bias)
        o_ref[0, :, g * HEAD_DIM:(g + 1) * HEAD_DIM] = (acc / l).astype(BF16)


def _fox_prompt(z3, dkr4, dkc, n_kv, gsz, q_blk0, k_blk0, v_blk0, tq):
    b, t, _ = z3.shape
    qw = gsz * HEAD_DIM
    return pl.pallas_call(
        functools.partial(_fox_prompt_kernel, tq=tq, gsz=gsz),
        grid=(b, n_kv, t // tq),
        in_specs=[
            pl.BlockSpec((1, tq, qw), lambda i, k, q: (i, q, q_blk0 + k)),
            pl.BlockSpec((1, t, HEAD_DIM), lambda i, k, q: (i, 0, k_blk0 + k)),
            pl.BlockSpec((1, t, HEAD_DIM), lambda i, k, q: (i, 0, v_blk0 + k)),
            pl.BlockSpec((1, gsz, 1, t), lambda i, k, q: (i, k, 0, 0)),
            pl.BlockSpec((1, tq, LANES), lambda i, k, q: (i, q, 0)),
        ],
        out_specs=pl.BlockSpec((1, tq, qw), lambda i, k, q: (i, q, k)),
        out_shape=jax.ShapeDtypeStruct((b, t, n_kv * qw), BF16),
        compiler_params=_params("parallel", "parallel", "arbitrary"),
    )(z3, z3, z3, dkr4, dkc)


def _logf_sample_kernel(z_ref, b_ref, lf_ref, dq_ref, tot_ref, *, ts):
    m = z_ref.shape[0]
    lf = _log_sigmoid(z_ref[...] + b_ref[...])
    lf_ref[...] = lf
    i, j = _iota((m, m), 0), _iota((m, m), 1)
    same = (i // ts) == (j // ts)
    dq_ref[...] = _dot_exact((same & (j > i)).astype(F32), lf)
    tot_ref[...] = _dot_exact(same.astype(F32), lf)


def _logf_sample(z_fg, b_pad, ts):
    m = z_fg.shape[0]
    sds = jax.ShapeDtypeStruct((m, LANES), F32)
    return pl.pallas_call(
        functools.partial(_logf_sample_kernel, ts=ts),
        out_shape=[sds, sds, sds],
    )(z_fg, b_pad)


def _page_heads(rows_ref, typ, n_kv, n_rows):
    return jnp.concatenate(
        [rows_ref[pl.ds(typ * n_kv + k, n_rows, stride=2 * n_kv), :] for k in range(n_kv)], axis=1).astype(BF16)


def _pad_rows_scratch(ref, x):
    ref[...] = jnp.zeros(ref.shape, ref.dtype)
    ref[0:x.shape[0], :] = x


def _diff_sample_kernel(pt_ref, lam_ref, g_ref, q_ref, kn_ref, vn_ref, pg_ref, o_ref,
                        m_ref, l_ref, acc_ref, kpad_ref, vpad_ref, *, n_kv, n_h, ts, lam_init):
    p = pl.program_id(1)
    npg = pl.num_programs(1) - 1
    rows = q_ref.shape[1]
    page = kpad_ref.shape[0]
    scale = DA ** -0.5

    @pl.when(p == 0)
    def _():
        m_ref[...] = jnp.full(m_ref.shape, NEG, F32)
        l_ref[...] = jnp.zeros(l_ref.shape, F32)
        acc_ref[...] = jnp.zeros(acc_ref.shape, F32)

    def update(s, v):
        m, l, acc = _online_update((m_ref[...], l_ref[...], acc_ref[...]), s, v)
        m_ref[...], l_ref[...], acc_ref[...] = m, l, acc

    @pl.when(p < npg)
    def _():
        s = _dot_nt(q_ref[0], _page_heads(pg_ref, 0, n_kv, page)) * scale
        update(s, _page_heads(pg_ref, 1, n_kv, page))

    @pl.when(p == npg)
    def _():
        _pad_rows_scratch(kpad_ref, kn_ref[0])
        _pad_rows_scratch(vpad_ref, vn_ref[0])
        s = _dot_nt(q_ref[0], kpad_ref[...].astype(BF16)) * scale
        r = _iota(s.shape, 0)
        tok = (r % (ts * n_h)) // n_h
        s = jnp.where(_iota(s.shape, 1) <= tok, s, NEG)
        update(s, vpad_ref[...].astype(BF16))
        r1 = _iota((rows, 1), 0)
        head = (r1 % n_h) // (n_h // n_kv)
        o = _diag_blocks(acc_ref[...], head, n_kv) / l_ref[...]
        half = rows // 2
        o = o[:half] - _lam_value(lam_ref, lam_init) * o[half:]
        o_ref[0] = (_rmsnorm_rows(o, g_ref[...]) * (1.0 - lam_init)).astype(BF16)


def _diff_sample(page_table, lam_vec, subln_g, qbd, k_new, v_new, cache_rows, page, n_kv, n_h, ts, lam_init):
    bs, rows, kw = qbd.shape
    npg = page_table.shape[1]
    grid_spec = pltpu.PrefetchScalarGridSpec(
        num_scalar_prefetch=1,
        grid=(bs, npg + 1),
        in_specs=[
            pl.BlockSpec((4, DA), lambda b, p, pt: (0, 0)),
            pl.BlockSpec((1, HEAD_DIM), lambda b, p, pt: (0, 0)),
            pl.BlockSpec((1, rows, kw), lambda b, p, pt: (b, 0, 0)),
            pl.BlockSpec((1, ts, kw), lambda b, p, pt: (b, 0, 0)),
            pl.BlockSpec((1, ts, kw), lambda b, p, pt: (b, 0, 0)),
            pl.BlockSpec((page * 2 * n_kv, HEAD_DIM), lambda b, p, pt: (pt[b, jnp.minimum(p, npg - 1)], 0)),
        ],
        out_specs=pl.BlockSpec((1, rows // 2, HEAD_DIM), lambda b, p, pt: (b, 0, 0)),
        scratch_shapes=[pltpu.VMEM((rows, 1), F32), pltpu.VMEM((rows, 1), F32), pltpu.VMEM((rows, kw), F32),
                        pltpu.VMEM((page, kw), F32), pltpu.VMEM((page, kw), F32)],
    )
    return pl.pallas_call(
        functools.partial(_diff_sample_kernel, n_kv=n_kv, n_h=n_h, ts=ts, lam_init=lam_init),
        grid_spec=grid_spec,
        out_shape=jax.ShapeDtypeStruct((bs, rows // 2, HEAD_DIM), BF16),
        compiler_params=_params("parallel", "arbitrary"),
    )(page_table, lam_vec, subln_g.reshape(1, HEAD_DIM), qbd, k_new, v_new, cache_rows)


def _fox_sample_kernel(pt_ref, q_ref, kn_ref, vn_ref, dq_ref, dkn_ref, tot_ref, pg_ref, lf_ref, o_ref,
                       m_ref, l_ref, acc_ref, car_ref, kpad_ref, vpad_ref, *, n_kv, n_h, ts):
    p = pl.program_id(1)
    rows = q_ref.shape[1]
    page = kpad_ref.shape[0]
    scale = HEAD_DIM ** -0.5

    def update(s, v):
        m, l, acc = _online_update((m_ref[...], l_ref[...], acc_ref[...]), s, v)
        m_ref[...], l_ref[...], acc_ref[...] = m, l, acc

    @pl.when(p == 0)
    def _():
        m_ref[...] = jnp.full(m_ref.shape, NEG, F32)
        l_ref[...] = jnp.zeros(l_ref.shape, F32)
        acc_ref[...] = jnp.zeros(acc_ref.shape, F32)
        car_ref[...] = tot_ref[0]
        _pad_rows_scratch(kpad_ref, kn_ref[0])
        _pad_rows_scratch(vpad_ref, vn_ref[0])
        s = _dot_nt(q_ref[0], kpad_ref[...].astype(BF16)) * scale + dkn_ref[0] - dq_ref[0]
        tok = _iota(s.shape, 0) // n_h
        s = jnp.where(_iota(s.shape, 1) <= tok, s, NEG)
        update(s, vpad_ref[...].astype(BF16))

    @pl.when(p > 0)
    def _():
        lft = lf_ref[0]
        lower = (_iota((page, page), 0) > _iota((page, page), 1)).astype(F32)
        dk = _dot_exact(lft, lower) + car_ref[...]
        car_ref[...] = car_ref[...] + jnp.sum(lft, axis=1, keepdims=True)
        bias = jnp.concatenate([dk] * ts, axis=0)
        s = _dot_nt(q_ref[0], _page_heads(pg_ref, 0, n_kv, page)) * scale + bias - dq_ref[0]
        update(s, _page_heads(pg_ref, 1, n_kv, page))

    @pl.when(p == pl.num_programs(1) - 1)
    def _():
        head = (_iota((rows, 1), 0) % n_h) // (n_h // n_kv)
        o_ref[0] = (_diag_blocks(acc_ref[...], head, n_kv) / l_ref[...]).astype(BF16)


def _fox_sample(page_table, qbd, k_new, v_new, dq_b, dkn_b, tot, cache_rows, logf_t, page, n_kv, n_h, ts):
    bs, rows, kw = qbd.shape
    npg = page_table.shape[1]

    def pg_idx(b, p, pt):
        return (pt[b, npg - jnp.maximum(p, 1)], 0, 0)

    grid_spec = pltpu.PrefetchScalarGridSpec(
        num_scalar_prefetch=1,
        grid=(bs, npg + 1),
        in_specs=[
            pl.BlockSpec((1, rows, kw), lambda b, p, pt: (b, 0, 0)),
            pl.BlockSpec((1, ts, kw), lambda b, p, pt: (b, 0, 0)),
            pl.BlockSpec((1, ts, kw), lambda b, p, pt: (b, 0, 0)),
            pl.BlockSpec((1, rows, LANES), lambda b, p, pt: (b, 0, 0)),
            pl.BlockSpec((1, rows, LANES), lambda b, p, pt: (b, 0, 0)),
            pl.BlockSpec((1, n_h, 1), lambda b, p, pt: (b, 0, 0)),
            pl.BlockSpec((page * 2 * n_kv, HEAD_DIM), lambda b, p, pt: pg_idx(b, p, pt)[:2]),
            pl.BlockSpec((1, n_h, page), pg_idx),
        ],
        out_specs=pl.BlockSpec((1, rows, HEAD_DIM), lambda b, p, pt: (b, 0, 0)),
        scratch_shapes=[pltpu.VMEM((rows, 1), F32), pltpu.VMEM((rows, 1), F32), pltpu.VMEM((rows, kw), F32),
                        pltpu.VMEM((n_h, 1), F32),
                        pltpu.VMEM((page, kw), F32), pltpu.VMEM((page, kw), F32)],
    )
    return pl.pallas_call(
        functools.partial(_fox_sample_kernel, n_kv=n_kv, n_h=n_h, ts=ts),
        grid_spec=grid_spec,
        out_shape=jax.ShapeDtypeStruct((bs, rows, HEAD_DIM), BF16),
        compiler_params=_params("parallel", "arbitrary"),
    )(page_table, qbd, k_new, v_new, dq_b, dkn_b, tot, cache_rows, logf_t)


def _compress_mlp(acc_a, acc_b_next, w2):
    return _dot(_gelu_tanh(acc_a + acc_b_next).astype(BF16), w2).astype(BF16)


def _compress_prompt_kernel(z_ref, pe_ref, w1_ref, w2_ref, o_ref, *, nch):
    acc_a = jnp.zeros((nch, CMP_HID), F32)
    acc_b = jnp.zeros((nch, CMP_HID), F32)
    for l in range(CMP_STRIDE):
        x = z_ref[0, pl.ds(l, nch, stride=CMP_STRIDE), :]
        xa = (x + pe_ref[0, l:l + 1, :]).astype(BF16)
        xb = (x + pe_ref[0, CMP_STRIDE + l:CMP_STRIDE + l + 1, :]).astype(BF16)
        acc_a = acc_a + _dot(xa, w1_ref[0, l])
        acc_b = acc_b + _dot(xb, w1_ref[0, CMP_STRIDE + l])
    o_ref[0, 0, 0] = _compress_mlp(acc_a, pltpu.roll(acc_b, nch - 1, axis=0), w2_ref[0])


def _compress_prompt(z3, pe, w1, w2, n_kv, col_blk0):
    b, t, _ = z3.shape
    nch = t // CMP_STRIDE
    return pl.pallas_call(
        functools.partial(_compress_prompt_kernel, nch=nch),
        grid=(b, 2 * n_kv),
        in_specs=[
            pl.BlockSpec((1, t, HEAD_DIM), lambda i, c: (i, 0, col_blk0 + c)),
            pl.BlockSpec((1, CMP_LEN, HEAD_DIM), lambda i, c: (c // n_kv, 0, 0)),
            pl.BlockSpec((1, CMP_LEN, HEAD_DIM, CMP_HID), lambda i, c: (c // n_kv, 0, 0, 0)),
            pl.BlockSpec((1, CMP_HID, HEAD_DIM), lambda i, c: (c // n_kv, 0, 0)),
        ],
        out_specs=pl.BlockSpec((1, 1, 1, nch, HEAD_DIM), lambda i, c: (i, c // n_kv, c % n_kv, 0, 0)),
        out_shape=jax.ShapeDtypeStruct((b, 2, n_kv, nch, HEAD_DIM), BF16),
        compiler_params=_params("parallel", "arbitrary"),
    )(z3, pe, w1, w2)


def _nsa_prompt_kernel(q_ref, cos_ref, sin_ref, kc_ref, vc_ref, ks_ref, vs_ref, kw_ref, vw_ref, gl_ref, bg_ref,
                       o_ref, *, tq, gsz, n_kv, nblk, ns, tk):
    kh = pl.program_id(1)
    qi = pl.program_id(2)
    t0 = qi * tq
    scale = HEAD_DIM ** -0.5
    cos, sin = cos_ref[...], sin_ref[...]
    qs = [q_ref[0, :, g * HEAD_DIM:(g + 1) * HEAD_DIM] for g in range(gsz)]
    q = jnp.concatenate(qs, axis=0).astype(BF16)
    qr = jnp.concatenate([_rope_lanes(x, cos, sin, HEAD_DIM // 2) for x in qs], axis=0).astype(BF16)
    qpos = t0 + _iota((tq, 1), 0)

    def tile_g(x):
        return jnp.concatenate([x] * gsz, axis=0)

    nch = kc_ref.shape[3]
    n_i = _iota((tq, nch), 1)
    ok = tile_g(((n_i * CMP_STRIDE + CMP_LEN - 1 <= qpos) & (n_i < nblk)).astype(F32))
    p = _masked_softmax(_dot_nt(q, kc_ref[0, 0, 0]) * scale, ok)
    o_cmp = _dot(p.astype(BF16), vc_ref[0, 0, 0])
    psum = p[0:tq]
    for g in range(1, gsz):
        psum = psum + p[g * tq:(g + 1) * tq]
    ov = _overlap(_iota((nch, LANES), 0) * CMP_STRIDE, _iota((nch, LANES), 1) * SLC_BLOCK)
    imp = _block_importance(_dot_exact(psum, ov), qpos, ns)
    sel = _topk_mask(imp, min(N_SELECT, ns)).astype(BF16)

    def slc_step(kb, carry):
        k0 = pl.multiple_of(kb * tk, tk)
        ks = ks_ref[0, pl.ds(k0, tk), :].astype(BF16)
        vs = vs_ref[0, pl.ds(k0, tk), :].astype(BF16)
        expand = (((k0 + _iota((LANES, tk), 1)) >> 6) == _iota((LANES, tk), 0)).astype(BF16)
        selk = _dot(sel, expand)
        kpos = k0 + _iota((tq, tk), 1)
        bias = jnp.where((selk > 0.5) & (kpos <= qpos), 0.0, NEG)
        s = _dot_nt(qr, ks) * scale + tile_g(bias)
        return _online_update(carry, s, vs)

    nkb = (t0 + tq + tk - 1) // tk
    _, l, acc = lax.fori_loop(0, nkb, slc_step, _flash_init(gsz * tq, HEAD_DIM))
    o_slc = acc / l

    wlen = WINDOW + tq
    w0 = pl.multiple_of(jnp.maximum(t0 - WINDOW, 0), tq)
    kw = kw_ref[0, pl.ds(w0, wlen), :].astype(BF16)
    vw = vw_ref[0, pl.ds(w0, wlen), :].astype(BF16)
    dist = qpos - (w0 + _iota((tq, wlen), 1))
    okw = tile_g(((dist >= 0) & (dist < WINDOW)).astype(F32))
    pw = _masked_softmax(_dot_nt(qr, kw) * scale, okw)
    o_win = _dot(pw.astype(BF16), vw)

    gates = jax.nn.sigmoid(gl_ref[0] + bg_ref[...])
    n_h = n_kv * gsz
    for g in range(gsz):
        rows = slice(g * tq, (g + 1) * tq)
        col = kh * gsz + g
        o = (_lane_pick(gates, col) * o_cmp[rows] + _lane_pick(gates, n_h + col) * o_slc[rows]
             + _lane_pick(gates, 2 * n_h + col) * o_win[rows])
        o_ref[0, :, g * HEAD_DIM:(g + 1) * HEAD_DIM] = o.astype(BF16)


def _nsa_prompt(z3, cos, sin, kvc, bg_pad, n_kv, gsz, ks_blk, vs_blk, kw_blk, vw_blk, gl_blk, tq, tk):
    b, t, _ = z3.shape
    nch = kvc.shape[3]
    qw = gsz * HEAD_DIM
    ns = -(-t // SLC_BLOCK)
    assert t >= WINDOW + tq and t % tk == 0 and tk % SLC_BLOCK == 0 and ns <= LANES
    kv_spec = lambda blk0: pl.BlockSpec((1, t, HEAD_DIM), lambda i, k, q: (i, 0, blk0 + k))
    return pl.pallas_call(
        functools.partial(_nsa_prompt_kernel, tq=tq, gsz=gsz, n_kv=n_kv, nblk=nch - 1, ns=ns, tk=tk),
        grid=(b, n_kv, t // tq),
        in_specs=[
            pl.BlockSpec((1, tq, qw), lambda i, k, q: (i, q, k)),
            pl.BlockSpec((tq, LANES), lambda i, k, q: (q, 0)),
            pl.BlockSpec((tq, LANES), lambda i, k, q: (q, 0)),
            pl.BlockSpec((1, 1, 1, nch, HEAD_DIM), lambda i, k, q: (i, 0, k, 0, 0)),
            pl.BlockSpec((1, 1, 1, nch, HEAD_DIM), lambda i, k, q: (i, 1, k, 0, 0)),
            kv_spec(ks_blk), kv_spec(vs_blk), kv_spec(kw_blk), kv_spec(vw_blk),
            pl.BlockSpec((1, tq, LANES), lambda i, k, q: (i, q, gl_blk)),
            pl.BlockSpec((1, LANES), lambda i, k, q: (0, 0)),
        ],
        out_specs=pl.BlockSpec((1, tq, qw), lambda i, k, q: (i, q, k)),
        out_shape=jax.ShapeDtypeStruct((b, t, n_kv * qw), BF16),
        compiler_params=_params("parallel", "parallel", "arbitrary"),
    )(z3, cos, sin, kvc, kvc, z3, z3, z3, z3, z3, bg_pad)


def _compress_sample_kernel(pt_ref, *refs, n_kv, n_pages, page):
    pg_refs = refs[:n_pages]
    pe_ref, w1_ref, w2_ref, o_ref, last_ref = refs[n_pages:]
    s = pl.program_id(1)
    cpp = page // CMP_STRIDE
    m = n_pages * cpp
    rpr = 2 * n_kv

    @pl.when(s == 0)
    def _():
        last_ref[...] = jnp.zeros(last_ref.shape, F32)

    for typ in range(2):
        acc_a = jnp.zeros((n_kv * m, CMP_HID), F32)
        acc_b = jnp.zeros((n_kv * m, CMP_HID), F32)
        for l in range(CMP_STRIDE):
            x = jnp.concatenate(
                [pg[pl.ds(l * rpr + typ * n_kv + h, cpp, stride=CMP_STRIDE * rpr), :]
                 for h in range(n_kv) for pg in pg_refs], axis=0)
            xa = (x + pe_ref[typ, l:l + 1, :]).astype(BF16)
            xb = (x + pe_ref[typ, CMP_STRIDE + l:CMP_STRIDE + l + 1, :]).astype(BF16)
            acc_a = acc_a + _dot(xa, w1_ref[typ, l])
            acc_b = acc_b + _dot(xb, w1_ref[typ, CMP_STRIDE + l])
        first = _iota((m, 1), 0) == 0
        for h in range(n_kv):
            a = acc_a[h * m:(h + 1) * m]
            idx = typ * n_kv + h
            prev = jnp.where(first, last_ref[idx, SUBLANES - 1:SUBLANES, :], pltpu.roll(a, 1, axis=0))
            last_ref[idx] = a[m - SUBLANES:m]
            o_ref[0, typ, h] = _compress_mlp(prev, acc_b[h * m:(h + 1) * m], w2_ref[typ])


def _compress_sample(page_table, cache_rows, pe, w1, w2, page, n_kv, n_pages):
    bs, npg = page_table.shape
    m = n_pages * (page // CMP_STRIDE)
    pg_spec = lambda r: pl.BlockSpec((page * 2 * n_kv, HEAD_DIM), lambda b, s, pt: (pt[b, s * n_pages + r], 0))
    grid_spec = pltpu.PrefetchScalarGridSpec(
        num_scalar_prefetch=1,
        grid=(bs, npg // n_pages),
        in_specs=[pg_spec(r) for r in range(n_pages)] + [
            pl.BlockSpec((2, CMP_LEN, HEAD_DIM), lambda b, s, pt: (0, 0, 0)),
            pl.BlockSpec((2, CMP_LEN, HEAD_DIM, CMP_HID), lambda b, s, pt: (0, 0, 0, 0)),
            pl.BlockSpec((2, CMP_HID, HEAD_DIM), lambda b, s, pt: (0, 0, 0)),
        ],
        out_specs=pl.BlockSpec((1, 2, n_kv, m, HEAD_DIM), lambda b, s, pt: (b, 0, 0, s, 0)),
        scratch_shapes=[pltpu.VMEM((2 * n_kv, SUBLANES, CMP_HID), F32)],
    )
    return pl.pallas_call(
        functools.partial(_compress_sample_kernel, n_kv=n_kv, n_pages=n_pages, page=page),
        grid_spec=grid_spec,
        out_shape=jax.ShapeDtypeStruct((bs, 2, n_kv, npg * (page // CMP_STRIDE), HEAD_DIM), BF16),
        compiler_params=_params("parallel", "arbitrary"),
    )(page_table, *([cache_rows] * n_pages), pe, w1, w2)


def _nsa_sample_a_kernel(q_ref, cos_ref, sin_ref, kc_ref, vc_ref, win_ref, kwn_ref,
                         ocmp_ref, owin_ref, sel_ref, qr_ref, kpad_ref, vpad_ref, *, n_kv, gsz, ts, p0, ns):
    scale = HEAD_DIM ** -0.5
    rows = ts * gsz
    nr = kc_ref.shape[3]
    nsp = sel_ref.shape[3]
    wrows = win_ref.shape[1] // (2 * n_kv)
    tok = _iota((rows, 1), 0) // gsz
    qpos = p0 + tok
    r_i = _iota((rows, nr), 1)
    ok_c = ((r_i >= 1) & ((r_i - 1) * CMP_STRIDE + CMP_LEN - 1 <= qpos)).astype(F32)
    ov = _overlap((_iota((nr, nsp), 0) - 1) * CMP_STRIDE, _iota((nr, nsp), 1) * SLC_BLOCK)
    group = ((_iota((rows, rows), 0) // gsz) == (_iota((rows, rows), 1) // gsz)).astype(F32)
    ok_w1 = _iota((rows, wrows), 1) > tok + (wrows - WINDOW)
    ok_w2 = _iota((rows, kpad_ref.shape[0]), 1) <= tok
    kvw = n_kv * HEAD_DIM
    for kh in range(n_kv):
        q = q_ref[0, kh]
        qr = _rope_lanes(q, cos_ref[...], sin_ref[...], HEAD_DIM // 2)
        qr_ref[0, kh] = qr
        qb, qrb = q.astype(BF16), qr.astype(BF16)
        p = _masked_softmax(_dot_nt(qb, kc_ref[0, 0, kh]) * scale, ok_c)
        ocmp_ref[0, kh] = _dot(p.astype(BF16), vc_ref[0, 0, kh])
        imp = _dot_exact(group, _dot_exact(p, ov))
        sel_ref[0, kh] = _topk_mask(_block_importance(imp, qpos, ns), min(N_SELECT, ns))
        sl = slice(kh * HEAD_DIM, (kh + 1) * HEAD_DIM)
        slv = slice(kvw + kh * HEAD_DIM, kvw + (kh + 1) * HEAD_DIM)
        _pad_rows_scratch(kpad_ref, kwn_ref[0, :, sl])
        _pad_rows_scratch(vpad_ref, kwn_ref[0, :, slv])
        k_win = win_ref[0, pl.ds(kh, wrows, stride=2 * n_kv), :].astype(BF16)
        v_win = win_ref[0, pl.ds(n_kv + kh, wrows, stride=2 * n_kv), :].astype(BF16)
        s1 = jnp.where(ok_w1, _dot_nt(qrb, k_win) * scale, NEG)
        s2 = jnp.where(ok_w2, _dot_nt(qrb, kpad_ref[...].astype(BF16)) * scale, NEG)
        m = jnp.maximum(jnp.max(s1, axis=-1, keepdims=True), jnp.max(s2, axis=-1, keepdims=True))
        e1, e2 = jnp.exp(s1 - m), jnp.exp(s2 - m)
        l = jnp.sum(e1, axis=-1, keepdims=True) + jnp.sum(e2, axis=-1, keepdims=True)
        o = _dot(e1.astype(BF16), v_win) + _dot(e2.astype(BF16), vpad_ref[...].astype(BF16))
        owin_ref[0, kh] = o / l


def _nsa_sample_a(q_rows, cos, sin, kvc, win_rows, kvw_new, n_kv, gsz, ts, p0, ns):
    bs = q_rows.shape[0]
    rows = ts * gsz
    nr = kvc.shape[3]
    nsp = _round_up(ns, LANES)
    wr, w = win_rows.shape[1], kvw_new.shape[2]
    assert wr == WINDOW * 2 * n_kv
    o_sds = jax.ShapeDtypeStruct((bs, n_kv, rows, HEAD_DIM), F32)
    o_spec = pl.BlockSpec((1, n_kv, rows, HEAD_DIM), lambda b: (b, 0, 0, 0))
    return pl.pallas_call(
        functools.partial(_nsa_sample_a_kernel, n_kv=n_kv, gsz=gsz, ts=ts, p0=p0, ns=ns),
        grid=(bs,),
        in_specs=[
            o_spec,
            pl.BlockSpec((rows, LANES), lambda b: (0, 0)),
            pl.BlockSpec((rows, LANES), lambda b: (0, 0)),
            pl.BlockSpec((1, 1, n_kv, nr, HEAD_DIM), lambda b: (b, 0, 0, 0, 0)),
            pl.BlockSpec((1, 1, n_kv, nr, HEAD_DIM), lambda b: (b, 1, 0, 0, 0)),
            pl.BlockSpec((1, wr, HEAD_DIM), lambda b: (b, 0, 0)),
            pl.BlockSpec((1, ts, w), lambda b: (b, 0, 0)),
        ],
        out_specs=[o_spec, o_spec, pl.BlockSpec((1, n_kv, rows, nsp), lambda b: (b, 0, 0, 0)), o_spec],
        out_shape=[o_sds, o_sds, jax.ShapeDtypeStruct((bs, n_kv, rows, nsp), F32), o_sds],
        scratch_shapes=[pltpu.VMEM((LANES, HEAD_DIM), F32), pltpu.VMEM((LANES, HEAD_DIM), F32)],
        compiler_params=_params("parallel"),
    )(q_rows, cos, sin, kvc, kvc, win_rows, kvw_new)


def _nsa_sample_b_kernel(pt_ref, q_ref, sel_ref, kn_ref, pg_ref, ocmp_ref, owin_ref, gl_ref, bg_ref, o_ref,
                         m_ref, l_ref, acc_ref, kpad_ref, vpad_ref, *, n_kv, gsz, ts):
    p = pl.program_id(1)
    npg = pl.num_programs(1) - 1
    rows = q_ref.shape[1]
    kw = n_kv * HEAD_DIM
    page = kpad_ref.shape[0]
    nsp = sel_ref.shape[2]
    scale = HEAD_DIM ** -0.5
    bpp = page // SLC_BLOCK

    @pl.when(p == 0)
    def _():
        m_ref[...] = jnp.full(m_ref.shape, NEG, F32)
        l_ref[...] = jnp.zeros(l_ref.shape, F32)
        acc_ref[...] = jnp.zeros(acc_ref.shape, F32)

    def update(s, v):
        m, l, acc = _online_update((m_ref[...], l_ref[...], acc_ref[...]), s, v)
        m_ref[...], l_ref[...], acc_ref[...] = m, l, acc

    def selected(first_block):
        expand = ((first_block + (_iota((nsp, page), 1) >> 6)) == _iota((nsp, page), 0)).astype(BF16)
        return _dot(sel_ref[0].astype(BF16), expand) > 0.5

    @pl.when(p < npg)
    def _():
        s = _dot_nt(q_ref[0], _page_heads(pg_ref, 0, n_kv, page)) * scale
        update(jnp.where(selected(p * bpp), s, NEG), _page_heads(pg_ref, 1, n_kv, page))

    @pl.when(p == npg)
    def _():
        _pad_rows_scratch(kpad_ref, kn_ref[0, :, :kw])
        _pad_rows_scratch(vpad_ref, kn_ref[0, :, kw:])
        s = _dot_nt(q_ref[0], kpad_ref[...].astype(BF16)) * scale
        r = _iota(s.shape, 0)
        tok = (r % (ts * gsz)) // gsz
        ok = selected(npg * bpp) & (_iota(s.shape, 1) <= tok)
        update(jnp.where(ok, s, NEG), vpad_ref[...].astype(BF16))
        head = _iota((rows, 1), 0) // (ts * gsz)
        o_slc = _diag_blocks(acc_ref[...], head, n_kv) / l_ref[...]
        gates = jax.nn.sigmoid(gl_ref[0] + bg_ref[...])
        o_ref[0] = gates[:, 0:1] * ocmp_ref[0] + gates[:, 1:2] * o_slc + gates[:, 2:3] * owin_ref[0]


def _nsa_sample_b(page_table, qbd, sel, kvs_new, cache_rows, o_cmp, o_win, gl_rows, bg_rows, page, n_kv, gsz, ts):
    bs, rows, kw = qbd.shape
    npg = page_table.shape[1]
    nsp = sel.shape[2]
    row_spec = lambda width: pl.BlockSpec((1, rows, width), lambda b, p, pt: (b, 0, 0))
    grid_spec = pltpu.PrefetchScalarGridSpec(
        num_scalar_prefetch=1,
        grid=(bs, npg + 1),
        in_specs=[
            row_spec(kw), row_spec(nsp),
            pl.BlockSpec((1, ts, 2 * kw), lambda b, p, pt: (b, 0, 0)),
            pl.BlockSpec((page * 2 * n_kv, HEAD_DIM), lambda b, p, pt: (pt[b, jnp.minimum(p, npg - 1)], 0)),
            row_spec(HEAD_DIM), row_spec(HEAD_DIM), row_spec(LANES),
            pl.BlockSpec((rows, LANES), lambda b, p, pt: (0, 0)),
        ],
        out_specs=row_spec(HEAD_DIM),
        scratch_shapes=[pltpu.VMEM((rows, 1), F32), pltpu.VMEM((rows, 1), F32), pltpu.VMEM((rows, kw), F32),
                        pltpu.VMEM((page, kw), F32), pltpu.VMEM((page, kw), F32)],
    )
    return pl.pallas_call(
        functools.partial(_nsa_sample_b_kernel, n_kv=n_kv, gsz=gsz, ts=ts),
        grid_spec=grid_spec,
        out_shape=jax.ShapeDtypeStruct((bs, rows, HEAD_DIM), F32),
        compiler_params=_params("parallel", "arbitrary"),
    )(page_table, qbd, sel, kvs_new, cache_rows, o_cmp, o_win, gl_rows, bg_rows)


def _rope_tables(pos, half):
    inv = ROPE_THETA ** (-jnp.arange(half, dtype=F32) / half)
    ang = pos.astype(F32)[:, None] * inv
    cos, sin = jnp.cos(ang), jnp.sin(ang)
    reps = LANES // (2 * half)
    return (jnp.tile(jnp.concatenate([cos, cos], axis=-1), (1, reps)),
            jnp.tile(jnp.concatenate([-sin, sin], axis=-1), (1, reps)))


def _block_diag(q_rows, row_head, n_kv):
    onehot = (np.asarray(row_head)[:, None] == np.arange(n_kv)[None, :]).astype(np.float32)
    out = q_rows[:, :, None, :] * jnp.asarray(onehot)[None, :, :, None]
    return out.reshape(q_rows.shape[0], q_rows.shape[1], n_kv * HEAD_DIM).astype(BF16)


def _pad_cols(w, n):
    return jnp.pad(w, ((0, 0), (0, n - w.shape[1])))


def _pick_tile(n, prefs):
    for p in prefs:
        if n % p == 0:
            return p
    return n


def kernel(x_prompt, x_sample, cache_a_kv, cache_b_kv, cache_b_logf, cache_cmp_kv, cache_slc_kv, state_win_kv, page_table, norm_mix_e, w_in_e, b_forget, lam_q1, lam_k1, lam_q2, lam_k2, subln_g, w_out_e, norm_mix_o, w_in_o, b_gate, cmp_pe_k, cmp_w1_k, cmp_w2_k, cmp_pe_v, cmp_w1_v, cmp_w2_v, w_out_o, norm_ffn, w_gate, w_up, w_down, norm_final):
    bp, t, d = x_prompt.shape
    bs, ts, _ = x_sample.shape
    n_even, n_odd = w_in_e.shape[0], w_in_o.shape[0]
    depth = n_even + n_odd
    n_pool, page = cache_a_kv.shape[1], cache_a_kv.shape[2]
    npg = page_table.shape[1]
    p0 = npg * page
    h_all = d // HEAD_DIM
    h_b = cache_b_logf.shape[-1]
    h_a = h_all - h_b
    kv_a, kv_b, kv_c = cache_a_kv.shape[4], cache_b_kv.shape[4], cache_cmp_kv.shape[4]
    g_a, g_b, g_c = h_a // kv_a, h_b // kv_b, h_all // kv_c
    mp, ms = bp * t, bs * ts
    hd = HEAD_DIM

    e_qa, e_ka = 0, h_a * hd
    e_va = e_ka + kv_a * hd
    e_qf = e_va + kv_a * hd
    e_kf = e_qf + h_b * hd
    e_vf = e_kf + kv_b * hd
    e_fg = e_vf + kv_b * hd
    o_kvc = h_all * hd
    o_kvs = o_kvc + 2 * kv_c * hd
    o_kvw = o_kvs + 2 * kv_c * hd
    o_gl = o_kvw + 2 * kv_c * hd
    assert 3 * h_all <= LANES and h_b <= LANES

    tn_e = _pick_tile(e_va, (512, 256, 128))
    tn_o = _pick_tile(kv_c * hd, (512, 256, 128))
    ne_pad = _round_up(e_fg + h_b, tn_e)
    no_pad = _round_up(o_gl + 3 * h_all, tn_o)
    tm_p = _pick_tile(mp, (512, 256, 128))
    tn_out = _pick_tile(d, (512, 256, 128))
    tf = _pick_tile(w_gate.shape[2], (256, 128))
    tq_e = _pick_tile(t, (256, 128))
    tq_o = 128
    tk_o = _pick_tile(t, (512, 256, 128))

    xp = x_prompt.reshape(mp, d)
    xs = x_sample.reshape(ms, d)
    pos_p = jnp.tile(jnp.arange(t, dtype=jnp.int32), bp)
    pos_s = p0 + jnp.tile(jnp.arange(ts, dtype=jnp.int32), bs)
    tab_a_p, tab_a_s = _rope_tables(pos_p, DA // 2), _rope_tables(pos_s, DA // 2)
    tab_c_p, tab_c_s = _rope_tables(pos_p, hd // 2), _rope_tables(pos_s, hd // 2)
    tab_c_q = _rope_tables(jnp.arange(t, dtype=jnp.int32), hd // 2)
    tab_c_rows = _rope_tables(p0 + jnp.repeat(jnp.arange(ts, dtype=jnp.int32), g_c), hd // 2)

    outs = {name: ([], []) for name in ("a", "b", "f", "c", "s", "w")}

    for layer in range(depth):
        i = layer // 2
        if layer % 2 == 0:
            lam_init = 0.8 - 0.6 * float(np.exp(-0.3 * layer))
            w_in = _pad_cols(w_in_e[i], ne_pad).astype(BF16)
            w_out = w_out_e[i].astype(BF16)
            rope_e = ((0, e_va // tn_e),)
            lam_vec = jnp.stack([lam_q1[i], lam_k1[i], lam_q2[i], lam_k2[i]])
            bf_pad = jnp.pad(b_forget[i], (0, LANES - h_b)).reshape(1, LANES)

            z = _proj(xp, norm_mix_e[i], w_in, *tab_a_p, rope_e, DA // 2, tm_p, tn_e)
            z3 = z.reshape(bp, t, ne_pad)
            outs["a"][0].append(z3[:, :, e_ka:e_qf].reshape(bp, t, 2, kv_a, hd))
            outs["b"][0].append(z3[:, :, e_kf:e_fg].reshape(bp, t, 2, kv_b, hd))
            lf, dkc, dkr = _logf_prompt(z3, bf_pad, e_fg // LANES)
            outs["f"][0].append(lf[:, :, :h_b])
            o_a = _diff_prompt(z3, lam_vec, subln_g[i], kv_a, g_a, e_qa // (g_a * hd), e_ka // hd, e_va // hd,
                               lam_init, tq_e)
            o_b = _fox_prompt(z3, dkr[:, :h_b].reshape(bp, h_b, 1, t), dkc, kv_b, g_b,
                              e_qf // (g_b * hd), e_kf // hd, e_vf // hd, tq_e)
            xp = _mm_res([o_a.reshape(mp, h_a * hd), o_b.reshape(mp, h_b * hd)], w_out, xp, tm_p, tn_out)

            z = _proj(xs, norm_mix_e[i], w_in, *tab_a_s, rope_e, DA // 2, ms, tn_e)
            z3 = z.reshape(bs, ts, ne_pad)
            outs["a"][1].append(z3[:, :, e_ka:e_qf].reshape(bs, ts, 2, kv_a, hd))
            outs["b"][1].append(z3[:, :, e_kf:e_fg].reshape(bs, ts, 2, kv_b, hd))
            lf, dq, tot = _logf_sample(z[:, e_fg:e_fg + LANES], bf_pad, ts)
            outs["f"][1].append(lf[:, :h_b].reshape(bs, ts, h_b))
            qa = z3[:, :, e_qa:e_ka].reshape(bs, ts * h_a, hd)
            lane = np.arange(hd)
            qa = jnp.concatenate([qa * jnp.asarray((lane < DA).astype(np.float32)),
                                  qa * jnp.asarray((lane >= DA).astype(np.float32))], axis=1)
            head_a = np.tile(np.arange(h_a) // g_a, 2 * ts)
            o_a = _diff_sample(page_table, lam_vec, subln_g[i], _block_diag(qa, head_a, kv_a),
                               z3[:, :, e_ka:e_va], z3[:, :, e_va:e_qf],
                               cache_a_kv[i].reshape(-1, hd), page, kv_a, h_a, ts, lam_init)
            qf = z3[:, :, e_qf:e_kf].reshape(bs, ts * h_b, hd)
            head_b = np.tile(np.arange(h_b) // g_b, ts)
            dq3 = dq[:, :h_b].reshape(bs, ts, h_b)
            dq_b = jnp.broadcast_to(dq3.reshape(bs, ts * h_b, 1), (bs, ts * h_b, LANES))
            dkn = jnp.broadcast_to(jnp.swapaxes(dq3, 1, 2)[:, None], (bs, ts, h_b, ts)).reshape(bs, ts * h_b, ts)
            dkn_b = jnp.pad(dkn, ((0, 0), (0, 0), (0, LANES - ts)))
            tot3 = tot[:, :h_b].reshape(bs, ts, h_b)[:, 0, :, None]
            o_b = _fox_sample(page_table, _block_diag(qf, head_b, kv_b), z3[:, :, e_kf:e_vf], z3[:, :, e_vf:e_fg],
                              dq_b, dkn_b, tot3, cache_b_kv[i].reshape(-1, hd),
                              jnp.swapaxes(cache_b_logf[i], 1, 2), page, kv_b, h_b, ts)
            xs = _mm_res([o_a.reshape(ms, h_a * hd), o_b.reshape(ms, h_b * hd)], w_out, xs, ms, tn_out)
        else:
            w_in = _pad_cols(w_in_o[i], no_pad).astype(BF16)
            w_out = w_out_o[i].astype(BF16)
            rope_o = ((o_kvs // tn_o, (o_kvs + kv_c * hd) // tn_o), (o_kvw // tn_o, (o_kvw + kv_c * hd) // tn_o))
            bg_pad = jnp.pad(b_gate[i], (0, LANES - 3 * h_all)).reshape(1, LANES)
            pe = jnp.stack([cmp_pe_k[i], cmp_pe_v[i]])
            w1 = jnp.stack([cmp_w1_k[i], cmp_w1_v[i]]).astype(BF16)
            w2 = jnp.stack([cmp_w2_k[i], cmp_w2_v[i]]).astype(BF16)

            z = _proj(xp, norm_mix_o[i], w_in, *tab_c_p, rope_o, hd // 2, tm_p, tn_o)
            z3 = z.reshape(bp, t, no_pad)
            outs["c"][0].append(z3[:, :, o_kvc:o_kvs].reshape(bp, t, 2, kv_c, hd))
            outs["s"][0].append(z3[:, :, o_kvs:o_kvw].reshape(bp, t, 2, kv_c, hd))
            keep = min(WINDOW, t)
            outs["w"][0].append(z3[:, t - keep:, o_kvw:o_gl].reshape(bp, keep, 2, kv_c, hd))
            kvc = _compress_prompt(z3, pe, w1, w2, kv_c, o_kvc // hd)
            o = _nsa_prompt(z3, *tab_c_q, kvc, bg_pad, kv_c, g_c, o_kvs // hd, o_kvs // hd + kv_c,
                            o_kvw // hd, o_kvw // hd + kv_c, o_gl // LANES, tq_o, tk_o)
            xp = _mm_res([o.reshape(mp, d)], w_out, xp, tm_p, tn_out)

            z = _proj(xs, norm_mix_o[i], w_in, *tab_c_s, rope_o, hd // 2, ms, tn_o)
            z3 = z.reshape(bs, ts, no_pad)
            outs["c"][1].append(z3[:, :, o_kvc:o_kvs].reshape(bs, ts, 2, kv_c, hd))
            outs["s"][1].append(z3[:, :, o_kvs:o_kvw].reshape(bs, ts, 2, kv_c, hd))
            win = state_win_kv[i]
            kvw_new = z3[:, :, o_kvw:o_gl]
            win_all = jnp.concatenate([win, kvw_new.reshape(bs, ts, 2, kv_c, hd)], axis=1)
            outs["w"][1].append(win_all[:, -min(WINDOW, win.shape[1] + ts):])
            kvc = _compress_sample(page_table, cache_cmp_kv[i].reshape(-1, hd), pe, w1, w2, page, kv_c, min(16, npg))
            ns = -(-(p0 + ts) // SLC_BLOCK)
            q_rows = z3[:, :, :o_kvc].reshape(bs, ts, kv_c, g_c, hd).transpose(0, 2, 1, 3, 4)
            q_rows = q_rows.reshape(bs, kv_c, ts * g_c, hd)
            o_cmp, o_win, sel, qr = _nsa_sample_a(q_rows, *tab_c_rows, kvc, win.reshape(bs, -1, hd),
                                                  kvw_new, kv_c, g_c, ts, p0, ns)
            rows = kv_c * ts * g_c
            head_c = np.repeat(np.arange(kv_c), ts * g_c)
            gl = z3[:, :, o_gl:o_gl + 3 * h_all].reshape(bs, ts, 3, kv_c, g_c).transpose(0, 3, 1, 4, 2)
            gl_rows = jnp.pad(gl.reshape(bs, rows, 3), ((0, 0), (0, 0), (0, LANES - 3)))
            bg = jnp.broadcast_to(b_gate[i].reshape(1, 3, kv_c, g_c), (ts, 3, kv_c, g_c)).transpose(2, 0, 3, 1)
            bg_rows = jnp.pad(bg.reshape(rows, 3), ((0, 0), (0, LANES - 3)))
            o = _nsa_sample_b(page_table, _block_diag(qr.reshape(bs, rows, hd), head_c, kv_c),
                              sel.reshape(bs, rows, sel.shape[3]), z3[:, :, o_kvs:o_kvw],
                              cache_slc_kv[i].reshape(-1, hd),
                              o_cmp.reshape(bs, rows, hd), o_win.reshape(bs, rows, hd), gl_rows, bg_rows,
                              page, kv_c, g_c, ts)
            o = o.reshape(bs, kv_c, ts, g_c, hd).transpose(0, 2, 1, 3, 4).reshape(ms, d).astype(BF16)
            xs = _mm_res([o], w_out, xs, ms, tn_out)

        g_final = norm_final if layer == depth - 1 else None
        wg, wu, wd = w_gate[layer].astype(BF16), w_up[layer].astype(BF16), w_down[layer].astype(BF16)
        xp = _ffn(xp, norm_ffn[layer], wg, wu, wd, g_final, tm_p, tf)
        xs = _ffn(xs, norm_ffn[layer], wg, wu, wd, g_final, ms, tf)

    def stack(name, which):
        return jnp.stack(outs[name][which])

    return (xp.reshape(bp, t, d), xs.reshape(bs, ts, d),
            stack("a", 0), stack("a", 1), stack("b", 0), stack("b", 1), stack("f", 0), stack("f", 1),
            stack("c", 0), stack("c", 1), stack("s", 0), stack("s", 1), stack("w", 0), stack("w", 1))
```

```python
import functools

import numpy as np
import jax
import jax.numpy as jnp
from jax import lax
from jax.experimental import pallas as pl
from jax.experimental.pallas import tpu as pltpu

F32 = jnp.float32
BF16 = jnp.bfloat16
HIGHEST = lax.Precision.HIGHEST

LANES = 128
SUBLANES = 8
VMEM_LIMIT = 56 * 1024 * 1024

HEAD_DIM = 128
DA = HEAD_DIM // 2
CMP_LEN = 32
CMP_STRIDE = 16
CMP_HID = 2 * HEAD_DIM
SLC_BLOCK = 64
N_SELECT = 16
WINDOW = 512
ROPE_THETA = 10000.0
NORM_EPS = 1e-6
LOG2E = 1.4426950408889634
NEG = -1e30
BIG = 1e30
LOWEST = -3e38


def _params(*sem):
    return pltpu.CompilerParams(dimension_semantics=sem, vmem_limit_bytes=VMEM_LIMIT)


def _dot(a, b):
    return jnp.dot(a, b, preferred_element_type=F32)


def _dot_nt(a, b):
    return lax.dot_general(a, b, (((1,), (1,)), ((), ())), preferred_element_type=F32)


def _dot_exact(a, b):
    return jnp.dot(a, b, precision=HIGHEST, preferred_element_type=F32)


def _dot_nt_exact(a, b):
    return lax.dot_general(a, b, (((1,), (1,)), ((), ())), precision=HIGHEST, preferred_element_type=F32)


def _iota(shape, axis):
    return lax.broadcasted_iota(jnp.int32, shape, axis)


def _round_up(n, m):
    return -(-n // m) * m


def _pick_tile(n, prefs):
    for p in prefs:
        if n % p == 0:
            return p
    return n


def _rope_lanes(a, cos, sin, half):
    if 2 * half == LANES:
        partner = pltpu.roll(a, half, axis=1)
    else:
        first = (_iota(a.shape, 1) & (2 * half - 1)) < half
        partner = jnp.where(first, pltpu.roll(a, LANES - half, axis=1), pltpu.roll(a, half, axis=1))
    return a * cos + partner * sin


def _rmsnorm_rows(x, g):
    ms = jnp.mean(x * x, axis=-1, keepdims=True)
    return x * lax.rsqrt(ms + NORM_EPS) * g


def _proj_kernel(x_ref, g_ref, w_ref, cos_ref, sin_ref, o_ref, xn_ref, *, rope_ranges, half):
    j = pl.program_id(1)

    @pl.when(j == 0)
    def _():
        xn_ref[...] = _rmsnorm_rows(x_ref[...], g_ref[...]).astype(BF16)

    o_ref[...] = _dot(xn_ref[...], w_ref[...])
    is_rope = functools.reduce(jnp.logical_or, [(j >= a) & (j < b) for a, b in rope_ranges])

    @pl.when(is_rope)
    def _():
        cos, sin = cos_ref[...], sin_ref[...]
        for c in range(o_ref.shape[1] // LANES):
            sl = slice(c * LANES, (c + 1) * LANES)
            o_ref[:, sl] = _rope_lanes(o_ref[:, sl], cos, sin, half)


def _proj(x, g, w, cos, sin, rope_ranges, half, tm, tn):
    m, d = x.shape
    n = w.shape[1]
    return pl.pallas_call(
        functools.partial(_proj_kernel, rope_ranges=rope_ranges, half=half),
        grid=(m // tm, n // tn),
        in_specs=[
            pl.BlockSpec((tm, d), lambda i, j: (i, 0)),
            pl.BlockSpec((1, d), lambda i, j: (0, 0)),
            pl.BlockSpec((d, tn), lambda i, j: (0, j)),
            pl.BlockSpec((tm, LANES), lambda i, j: (i, 0)),
            pl.BlockSpec((tm, LANES), lambda i, j: (i, 0)),
        ],
        out_specs=pl.BlockSpec((tm, tn), lambda i, j: (i, j)),
        out_shape=jax.ShapeDtypeStruct((m, n), F32),
        scratch_shapes=[pltpu.VMEM((tm, d), BF16)],
        compiler_params=_params("parallel", "arbitrary"),
    )(x, g.reshape(1, d), w, cos, sin)


def _mm_res_kernel(*refs, n_lhs):
    a_refs, (w_ref, r_ref, o_ref) = refs[:n_lhs], refs[n_lhs:]
    acc = r_ref[...]
    off = 0
    for a_ref in a_refs:
        kk = a_ref.shape[1]
        acc = acc + _dot(a_ref[...], w_ref[off:off + kk, :])
        off += kk
    o_ref[...] = acc


def _mm_res(lhs, w, res, tm, tn):
    m, n = res.shape
    k = w.shape[0]
    return pl.pallas_call(
        functools.partial(_mm_res_kernel, n_lhs=len(lhs)),
        grid=(m // tm, n // tn),
        in_specs=[pl.BlockSpec((tm, a.shape[1]), lambda i, j: (i, 0)) for a in lhs] + [
            pl.BlockSpec((k, tn), lambda i, j: (0, j)),
            pl.BlockSpec((tm, tn), lambda i, j: (i, j)),
        ],
        out_specs=pl.BlockSpec((tm, tn), lambda i, j: (i, j)),
        out_shape=jax.ShapeDtypeStruct((m, n), F32),
        compiler_params=_params("parallel", "arbitrary"),
    )(*lhs, w, res)


def _ffn_kernel(x_ref, g_ref, wg_ref, wu_ref, wd_ref, gf_ref, o_ref, xn_ref, *, final_norm):
    f = pl.program_id(1)

    @pl.when(f == 0)
    def _():
        x = x_ref[...]
        xn_ref[...] = _rmsnorm_rows(x, g_ref[...]).astype(BF16)
        o_ref[...] = x

    xn = xn_ref[...]
    gate = _dot(xn, wg_ref[...])
    up = _dot(xn, wu_ref[...])
    h = (gate * jax.nn.sigmoid(gate) * up).astype(BF16)
    o_ref[...] += _dot(h, wd_ref[...])

    if final_norm:
        @pl.when(f == pl.num_programs(1) - 1)
        def _():
            o_ref[...] = _rmsnorm_rows(o_ref[...], gf_ref[...])


def _ffn(x, g, wg, wu, wd, g_final, tm, tf):
    m, d = x.shape
    ff = wg.shape[1]
    final_norm = g_final is not None
    gf = (g_final if final_norm else g).reshape(1, d)
    return pl.pallas_call(
        functools.partial(_ffn_kernel, final_norm=final_norm),
        grid=(m // tm, ff // tf),
        in_specs=[
            pl.BlockSpec((tm, d), lambda i, f: (i, 0), pipeline_mode=pl.Buffered(1)),
            pl.BlockSpec((1, d), lambda i, f: (0, 0)),
            pl.BlockSpec((d, tf), lambda i, f: (0, f)),
            pl.BlockSpec((d, tf), lambda i, f: (0, f)),
            pl.BlockSpec((tf, d), lambda i, f: (f, 0)),
            pl.BlockSpec((1, d), lambda i, f: (0, 0)),
        ],
        out_specs=pl.BlockSpec((tm, d), lambda i, f: (i, 0)),
        out_shape=jax.ShapeDtypeStruct((m, d), F32),
        scratch_shapes=[pltpu.VMEM((tm, d), BF16)],
        compiler_params=_params("parallel", "arbitrary"),
    )(x, g.reshape(1, d), wg, wu, wd, gf)


def _online_update(carry, s, v, shift=None):
    m, l, acc = carry
    smax = jnp.max(s, axis=-1, keepdims=True)
    if shift is not None:
        smax = smax - shift
    m_new = jnp.maximum(m, smax)
    alpha = jnp.exp2(m - m_new)
    p = jnp.exp2(s - (m_new if shift is None else m_new + shift))
    l = alpha * l + jnp.sum(p, axis=-1, keepdims=True)
    acc = alpha * acc + _dot(p.astype(BF16), v)
    return m_new, l, acc


def _state_init(m_ref, l_ref, acc_ref):
    m_ref[...] = jnp.full(m_ref.shape, NEG, F32)
    l_ref[...] = jnp.zeros(l_ref.shape, F32)
    acc_ref[...] = jnp.zeros(acc_ref.shape, F32)


def _state_update(m_ref, l_ref, acc_ref, k, s, v, shift=None):
    m, l, acc = _online_update((m_ref[k], l_ref[k], acc_ref[k]), s, v, shift)
    m_ref[k], l_ref[k], acc_ref[k] = m, l, acc


def _fold_lanes(x, op):
    out = x[:, :LANES]
    for i in range(1, x.shape[1] // LANES):
        out = op(out, x[:, i * LANES:(i + 1) * LANES])
    return out


def _two_pass_attention(q, kb_ref, vb_ref, n_blocks, tk, s_ref, add_fn=None, tail_fn=None, shift=None):
    rows = q.shape[0]

    def scores(kb, masked):
        k0 = pl.multiple_of(kb * tk, tk)
        s = _dot_nt(q, kb_ref[pl.ds(k0, tk), :])
        if add_fn is not None:
            s = s + add_fn(k0)
        if masked and tail_fn is not None:
            s = s + tail_fn(k0)
        s_ref[kb] = s
        return _fold_lanes(s, jnp.maximum)

    mx = lax.fori_loop(0, n_blocks - 1, lambda kb, mx: jnp.maximum(mx, scores(kb, False)),
                       jnp.full((rows, LANES), NEG, F32))
    mx = jnp.maximum(mx, scores(n_blocks - 1, True))
    m = jnp.max(mx, axis=-1, keepdims=True)
    if shift is not None:
        m = (m - shift) + shift

    def accumulate(kb, carry):
        ls, acc = carry
        k0 = pl.multiple_of(kb * tk, tk)
        p = jnp.exp2(s_ref[kb] - m)
        return ls + _fold_lanes(p, jnp.add), acc + _dot(p.astype(BF16), vb_ref[pl.ds(k0, tk), :])

    ls, acc = lax.fori_loop(0, n_blocks, accumulate,
                            (jnp.zeros((rows, LANES), F32), jnp.zeros((rows, HEAD_DIM), F32)))
    return acc, jnp.sum(ls, axis=-1, keepdims=True)


def _masked_softmax(s, okf):
    ok = okf > 0.5
    sm = jnp.where(ok, s, NEG)
    m = jnp.max(sm, axis=-1, keepdims=True)
    e = jnp.where(ok, jnp.exp2(sm - m), 0.0)
    return e / jnp.maximum(jnp.sum(e, axis=-1, keepdims=True), 1e-30)


def _lane_pick(x, lane_idx):
    return jnp.sum(jnp.where(_iota(x.shape, 1) == lane_idx, x, 0.0), axis=-1, keepdims=True)


def _topk_mask(imp, n_sel):
    lane = _iota(imp.shape, 1).astype(F32)
    sel = jnp.zeros(imp.shape, F32)
    x = imp
    for _ in range(n_sel):
        m = jnp.max(x, axis=-1, keepdims=True)
        idx = jnp.min(jnp.where(x == m, lane, 1e9), axis=-1, keepdims=True)
        hit = lane == idx
        sel = jnp.where(hit, 1.0, sel)
        x = jnp.where(hit, LOWEST, x)
    return sel


def _overlap(ci, sj):
    ov = jnp.minimum(ci + CMP_LEN, sj + SLC_BLOCK) - jnp.maximum(ci, sj)
    return jnp.maximum(ov, 0).astype(F32) * (1.0 / CMP_LEN)


def _topk_mask_t(imp_t, n_sel, ns):
    rows = _round_up(ns, SUBLANES)
    x = imp_t[0:rows]
    j = _iota(x.shape, 0)
    rank = jnp.zeros(x.shape, F32)
    for jp in range(ns):
        a = x[jp:jp + 1, :]
        rank = rank + jnp.where((a > x) | ((a == x) & (jp < j)), 1.0, 0.0)
    sel = jnp.where((rank < n_sel) & (j < ns), 1.0, 0.0)
    return jnp.concatenate([sel, jnp.zeros((imp_t.shape[0] - rows, imp_t.shape[1]), F32)], axis=0)


def _block_importance(imp, qpos, ns, axis=1):
    j = _iota(imp.shape, axis)
    cur = qpos >> 6
    forced = (j == 0) | (j == cur) | (j == cur - 1)
    valid = (j << 6) <= qpos
    out = jnp.where(forced, BIG, jnp.where(valid, imp, NEG))
    return jnp.where(j < ns, out, LOWEST)


def _lam_value(lam_ref, lam_init):
    v = lam_ref[...]
    a = jnp.exp(jnp.sum(v[0:1] * v[1:2], axis=-1, keepdims=True))
    b = jnp.exp(jnp.sum(v[2:3] * v[3:4], axis=-1, keepdims=True))
    return a - b + lam_init


def _gelu_tanh(x):
    return 0.5 * x * (1.0 + jnp.tanh(0.7978845608028654 * (x + 0.044715 * (x * x * x))))


def _log_sigmoid(z):
    return jnp.minimum(z, 0.0) - jnp.log1p(jnp.exp(-jnp.abs(z)))


def _cast_kv_once(step, pairs):
    @pl.when(step == 0)
    def _():
        for src, dst in pairs:
            dst[...] = src[0].astype(BF16)


def _logf_prompt_kernel(z_ref, b_ref, lf_ref, dkc_ref, dkr_ref):
    t = z_ref.shape[1]
    lf = _log_sigmoid(z_ref[0] + b_ref[...])
    lf_ref[0] = lf
    upper = (_iota((LANES, LANES), 1) > _iota((LANES, LANES), 0)).astype(F32)
    carry = jnp.zeros((1, LANES), F32)
    for blk in reversed(range(t // LANES)):
        sl = slice(blk * LANES, (blk + 1) * LANES)
        x = lf[sl]
        d = _dot_exact(upper, x) + carry
        dkc_ref[0, sl, :] = d
        dkr_ref[0, :, sl] = d.T
        carry = carry + jnp.sum(x, axis=0, keepdims=True)


def _logf_prompt(z3, b_pad, col_blk):
    b, t, _ = z3.shape
    return pl.pallas_call(
        _logf_prompt_kernel,
        grid=(b,),
        in_specs=[pl.BlockSpec((1, t, LANES), lambda i: (i, 0, col_blk)),
                  pl.BlockSpec((1, LANES), lambda i: (0, 0))],
        out_specs=[pl.BlockSpec((1, t, LANES), lambda i: (i, 0, 0)),
                   pl.BlockSpec((1, t, LANES), lambda i: (i, 0, 0)),
                   pl.BlockSpec((1, LANES, t), lambda i: (i, 0, 0))],
        out_shape=[jax.ShapeDtypeStruct((b, t, LANES), F32),
                   jax.ShapeDtypeStruct((b, t, LANES), F32),
                   jax.ShapeDtypeStruct((b, LANES, t), F32)],
        compiler_params=_params("parallel"),
    )(z3, b_pad)


def _causal_tail(qpos, tk):
    return lambda k0: jnp.where(k0 + _iota((qpos.shape[0], tk), 1) <= qpos, 0.0, NEG)


def _diff_prompt_kernel(lam_ref, g_ref, q_ref, k_ref, v_ref, o_ref, kb_ref, vb_ref, s_ref, *, tq, gsz, lam_init, tk):
    qi = pl.program_id(2)
    _cast_kv_once(qi, ((k_ref, kb_ref), (v_ref, vb_ref)))
    lam = _lam_value(lam_ref, lam_init)
    lane = _iota((tq, HEAD_DIM), 1)
    qscale = DA ** -0.5 * LOG2E
    qs = [q_ref[0, :, g * HEAD_DIM:(g + 1) * HEAD_DIM] * qscale for g in range(gsz)]
    q = jnp.concatenate([jnp.where(lane < DA, x, 0.0) for x in qs] + [jnp.where(lane >= DA, x, 0.0) for x in qs],
                        axis=0).astype(BF16)
    qpos = qi * tq + (_iota((2 * gsz * tq, 1), 0) & (tq - 1))
    n_blocks = (qi * tq) // tk + 1
    acc, l = _two_pass_attention(q, kb_ref, vb_ref, n_blocks, tk, s_ref, tail_fn=_causal_tail(qpos, tk))
    o = acc / l
    half = gsz * tq
    o = o[:half] - lam * o[half:]
    o = (_rmsnorm_rows(o, g_ref[...]) * (1.0 - lam_init)).astype(BF16)
    for g in range(gsz):
        o_ref[0, :, g * HEAD_DIM:(g + 1) * HEAD_DIM] = o[g * tq:(g + 1) * tq]


def _diff_prompt(z3, lam_vec, subln_g, n_kv, gsz, q_blk0, k_blk0, v_blk0, lam_init, tq, tk):
    b, t, _ = z3.shape
    qw = gsz * HEAD_DIM
    assert tk % tq == 0 and tq & (tq - 1) == 0
    return pl.pallas_call(
        functools.partial(_diff_prompt_kernel, tq=tq, gsz=gsz, lam_init=lam_init, tk=tk),
        grid=(b, n_kv, t // tq),
        in_specs=[
            pl.BlockSpec((4, DA), lambda i, k, q: (0, 0)),
            pl.BlockSpec((1, HEAD_DIM), lambda i, k, q: (0, 0)),
            pl.BlockSpec((1, tq, qw), lambda i, k, q: (i, q, q_blk0 + k)),
            pl.BlockSpec((1, t, HEAD_DIM), lambda i, k, q: (i, 0, k_blk0 + k)),
            pl.BlockSpec((1, t, HEAD_DIM), lambda i, k, q: (i, 0, v_blk0 + k)),
        ],
        out_specs=pl.BlockSpec((1, tq, qw), lambda i, k, q: (i, q, k)),
        out_shape=jax.ShapeDtypeStruct((b, t, n_kv * qw), BF16),
        scratch_shapes=[pltpu.VMEM((t, HEAD_DIM), BF16), pltpu.VMEM((t, HEAD_DIM), BF16),
                        pltpu.VMEM((t // tk, 2 * gsz * tq, tk), F32)],
        compiler_params=_params("parallel", "parallel", "arbitrary"),
    )(lam_vec, subln_g.reshape(1, HEAD_DIM), z3, z3, z3)


def _fox_prompt_kernel(q_ref, k_ref, v_ref, dkr_ref, dkc_ref, o_ref, kb_ref, vb_ref, s_ref, *, tq, gsz, tk):
    kh = pl.program_id(1)
    qi = pl.program_id(2)
    _cast_kv_once(qi, ((k_ref, kb_ref), (v_ref, vb_ref)))
    qscale = HEAD_DIM ** -0.5 * LOG2E
    qpos = qi * tq + (_iota((gsz * tq, 1), 0) & (tq - 1))
    n_blocks = (qi * tq) // tk + 1
    q = jnp.concatenate([q_ref[0, :, g * HEAD_DIM:(g + 1) * HEAD_DIM] * qscale for g in range(gsz)],
                        axis=0).astype(BF16)
    dq = jnp.concatenate([_lane_pick(dkc_ref[0], kh * gsz + g) for g in range(gsz)], axis=0) * LOG2E

    def dk_fn(k0):
        return jnp.concatenate([jnp.broadcast_to(dkr_ref[0, g, :, pl.ds(k0, tk)] * LOG2E, (tq, tk))
                                for g in range(gsz)], axis=0)

    acc, l = _two_pass_attention(q, kb_ref, vb_ref, n_blocks, tk, s_ref, add_fn=dk_fn,
                                 tail_fn=_causal_tail(qpos, tk), shift=dq)
    o = (acc / l).astype(BF16)
    for g in range(gsz):
        o_ref[0, :, g * HEAD_DIM:(g + 1) * HEAD_DIM] = o[g * tq:(g + 1) * tq]


def _fox_prompt(z3, dkr4, dkc, n_kv, gsz, q_blk0, k_blk0, v_blk0, tq, tk):
    b, t, _ = z3.shape
    qw = gsz * HEAD_DIM
    assert tk % tq == 0
    return pl.pallas_call(
        functools.partial(_fox_prompt_kernel, tq=tq, gsz=gsz, tk=tk),
        grid=(b, n_kv, t // tq),
        in_specs=[
            pl.BlockSpec((1, tq, qw), lambda i, k, q: (i, q, q_blk0 + k)),
            pl.BlockSpec((1, t, HEAD_DIM), lambda i, k, q: (i, 0, k_blk0 + k)),
            pl.BlockSpec((1, t, HEAD_DIM), lambda i, k, q: (i, 0, v_blk0 + k)),
            pl.BlockSpec((1, gsz, 1, t), lambda i, k, q: (i, k, 0, 0)),
            pl.BlockSpec((1, tq, LANES), lambda i, k, q: (i, q, 0)),
        ],
        out_specs=pl.BlockSpec((1, tq, qw), lambda i, k, q: (i, q, k)),
        out_shape=jax.ShapeDtypeStruct((b, t, n_kv * qw), BF16),
        scratch_shapes=[pltpu.VMEM((t, HEAD_DIM), BF16), pltpu.VMEM((t, HEAD_DIM), BF16),
                        pltpu.VMEM((t // tk, gsz * tq, tk), F32)],
        compiler_params=_params("parallel", "parallel", "arbitrary"),
    )(z3, z3, z3, dkr4, dkc)


def _logf_sample_kernel(z_ref, b_ref, lf_ref, dq_ref, tot_ref, *, ts):
    m = z_ref.shape[0]
    lf = _log_sigmoid(z_ref[...] + b_ref[...])
    lf_ref[...] = lf
    i, j = _iota((m, m), 0), _iota((m, m), 1)
    same = (i // ts) == (j // ts)
    dq_ref[...] = _dot_exact((same & (j > i)).astype(F32), lf)
    tot_ref[...] = _dot_exact(same.astype(F32), lf)


def _logf_sample(z_fg, b_pad, ts):
    m = z_fg.shape[0]
    sds = jax.ShapeDtypeStruct((m, LANES), F32)
    return pl.pallas_call(
        functools.partial(_logf_sample_kernel, ts=ts),
        out_shape=[sds, sds, sds],
    )(z_fg, b_pad)


def _pad_rows_scratch(ref, x):
    ref[...] = jnp.zeros(ref.shape, ref.dtype)
    ref[0:x.shape[0], :] = x


def _page_specs(page, n_kv, n_pages, first_page):
    rows = page * 2 * n_kv
    return [pl.BlockSpec((rows, HEAD_DIM), lambda b, p, pt, r=r: (pt[b, first_page(b, p, pt) + r], 0))
            for r in range(n_pages)]


def _tile_view(cache):
    n_kv = cache.shape[3]
    assert 2 * n_kv in (SUBLANES, 2 * SUBLANES)
    return cache.reshape(-1, 2 * n_kv // SUBLANES, SUBLANES, HEAD_DIM)


def _tile_specs(page, tiles, n_pages, first_page):
    return [pl.BlockSpec((page, tiles, SUBLANES, HEAD_DIM),
                         lambda b, p, pt, r=r: (pt[b, first_page(b, p, pt) + r], 0, 0, 0))
            for r in range(n_pages)]


def _flat_kv(pg_refs):
    n = pg_refs[0].shape[0] * SUBLANES
    if pg_refs[0].shape[1] == 2:
        ks = [pg[:, 0].reshape(n, HEAD_DIM) for pg in pg_refs]
        vs = [pg[:, 1].reshape(n, HEAD_DIM) for pg in pg_refs]
        off = 0
    else:
        vs = [pg[:, 0].reshape(n, HEAD_DIM) for pg in pg_refs]
        ks = [pltpu.roll(v, SUBLANES // 2, axis=0) for v in vs]
        off = SUBLANES // 2
    return (jnp.concatenate(ks, axis=0).astype(BF16), jnp.concatenate(vs, axis=0).astype(BF16), off)


def _own_head_bias(row_head, n_cols):
    return jnp.where((_iota((row_head.shape[0], n_cols), 1) & (SUBLANES - 1)) == row_head, 0.0, NEG)


def _diff_sample_kernel(pt_ref, lam_ref, g_ref, q_ref, kn_ref, vn_ref, *refs, n_kv, gsz, ts, lam_init, n_pages):
    pg_refs = refs[:n_pages]
    o_ref, m_ref, l_ref, acc_ref, bias_ref, kpad_ref, vpad_ref = refs[n_pages:]
    p = pl.program_id(1)
    last = pl.num_programs(1) - 1
    r1 = n_kv * ts * gsz
    q = q_ref[0] * (DA ** -0.5 * LOG2E)
    lane = _iota((r1, HEAD_DIM), 1)
    qb = jnp.concatenate([jnp.where(lane < DA, q, 0.0), jnp.where(lane >= DA, q, 0.0)], axis=0).astype(BF16)
    rr = _iota((2 * r1, 1), 0) % r1
    row_head = rr // (ts * gsz)
    row_tok = (rr % (ts * gsz)) // gsz

    @pl.when(p == 0)
    def _():
        _state_init(m_ref, l_ref, acc_ref)

    @pl.when(p < last)
    def _():
        ks, vs, off = _flat_kv(pg_refs)

        @pl.when(p == 0)
        def _():
            bias_ref[...] = _own_head_bias(row_head + off, bias_ref.shape[1])

        _state_update(m_ref, l_ref, acc_ref, ..., _dot_nt(qb, ks) + bias_ref[...], vs)

    @pl.when(p == last)
    def _():
        _pad_rows_scratch(kpad_ref, kn_ref[0])
        _pad_rows_scratch(vpad_ref, vn_ref[0])
        s = _dot_nt(qb, kpad_ref[...].astype(BF16))
        c = _iota(s.shape, 1)
        ok = ((c % n_kv) == row_head) & ((c // n_kv) <= row_tok)
        _state_update(m_ref, l_ref, acc_ref, ..., jnp.where(ok, s, NEG), vpad_ref[...].astype(BF16))
        o = acc_ref[...] / l_ref[...]
        o = o[:r1] - _lam_value(lam_ref, lam_init) * o[r1:]
        o_ref[0] = (_rmsnorm_rows(o, g_ref[...]) * (1.0 - lam_init)).astype(BF16)


def _diff_sample(page_table, lam_vec, subln_g, q_rows, k_new, v_new, cache_tiles, page, n_kv, gsz, ts, lam_init, n_pages):
    bs, r1, _ = q_rows.shape
    nsteps = page_table.shape[1] // n_pages
    first = lambda b, p, pt: jnp.minimum(p, nsteps - 1) * n_pages
    row_spec = lambda n: pl.BlockSpec((1, n, HEAD_DIM), lambda b, p, pt: (b, 0, 0))
    assert ts * n_kv <= LANES
    grid_spec = pltpu.PrefetchScalarGridSpec(
        num_scalar_prefetch=1,
        grid=(bs, nsteps + 1),
        in_specs=[
            pl.BlockSpec((4, DA), lambda b, p, pt: (0, 0)),
            pl.BlockSpec((1, HEAD_DIM), lambda b, p, pt: (0, 0)),
            row_spec(r1), row_spec(ts * n_kv), row_spec(ts * n_kv),
        ] + _tile_specs(page, cache_tiles.shape[1], n_pages, first),
        out_specs=row_spec(r1),
        scratch_shapes=[pltpu.VMEM((2 * r1, 1), F32), pltpu.VMEM((2 * r1, 1), F32), pltpu.VMEM((2 * r1, HEAD_DIM), F32),
                        pltpu.VMEM((2 * r1, n_pages * page * SUBLANES), F32),
                        pltpu.VMEM((LANES, HEAD_DIM), F32), pltpu.VMEM((LANES, HEAD_DIM), F32)],
    )
    return pl.pallas_call(
        functools.partial(_diff_sample_kernel, n_kv=n_kv, gsz=gsz, ts=ts, lam_init=lam_init, n_pages=n_pages),
        grid_spec=grid_spec,
        out_shape=jax.ShapeDtypeStruct((bs, r1, HEAD_DIM), BF16),
        compiler_params=_params("parallel", "arbitrary"),
    )(page_table, lam_vec, subln_g.reshape(1, HEAD_DIM), q_rows, k_new, v_new, *([cache_tiles] * n_pages))


def _fox_sample_kernel(pt_ref, q_ref, kn_ref, vn_ref, dq_ref, dkn_ref, tot_ref, *refs, n_kv, gsz, ts, n_pages):
    pg_refs, lf_refs = refs[:n_pages], refs[n_pages:2 * n_pages]
    o_ref, m_ref, l_ref, acc_ref, car_ref, bias_ref, kpad_ref, vpad_ref = refs[2 * n_pages:]
    p = pl.program_id(1)
    page = lf_refs[0].shape[2]
    rows = n_kv * ts * gsz
    n_h = n_kv * gsz
    qb = (q_ref[0] * (HEAD_DIM ** -0.5 * LOG2E)).astype(BF16)
    rr = _iota((rows, 1), 0)
    row_head = rr // (ts * gsz)
    row_tok = (rr % (ts * gsz)) // gsz
    shift = dq_ref[0][:, 0:1] * LOG2E

    @pl.when(p == 0)
    def _():
        _state_init(m_ref, l_ref, acc_ref)
        car_ref[...] = tot_ref[0]
        _pad_rows_scratch(kpad_ref, kn_ref[0])
        _pad_rows_scratch(vpad_ref, vn_ref[0])
        s = _dot_nt(qb, kpad_ref[...].astype(BF16)) + dkn_ref[0] * LOG2E
        c = _iota(s.shape, 1)
        ok = ((c % n_kv) == row_head) & ((c // n_kv) <= row_tok)
        _state_update(m_ref, l_ref, acc_ref, ..., jnp.where(ok, s, NEG), vpad_ref[...].astype(BF16), shift=shift)

    @pl.when(p > 0)
    def _():
        ks, vs, off = _flat_kv(pg_refs)

        @pl.when(p == 1)
        def _():
            bias_ref[...] = _own_head_bias(row_head + off, bias_ref.shape[1])

        lower = (_iota((page, page), 0) > _iota((page, page), 1)).astype(F32)
        spread = ((_iota((page, page * SUBLANES), 1) >> 3) == _iota((page, page * SUBLANES), 0)).astype(F32)
        dks = [None] * n_pages
        for r in reversed(range(n_pages)):
            lft = lf_refs[r][0]
            dks[r] = _dot_exact(_dot_exact(lft, lower) + car_ref[...], spread)
            car_ref[...] = car_ref[...] + jnp.sum(lft, axis=1, keepdims=True)
        hh = _iota((rows, n_h), 0)
        pick = ((hh // (ts * gsz)) * gsz + hh % gsz == _iota((rows, n_h), 1)).astype(F32)
        dk = _dot_exact(pick, jnp.concatenate(dks, axis=1)) * LOG2E
        _state_update(m_ref, l_ref, acc_ref, ..., _dot_nt(qb, ks) + dk + bias_ref[...], vs, shift=shift)

    @pl.when(p == pl.num_programs(1) - 1)
    def _():
        o_ref[0] = (acc_ref[...] / l_ref[...]).astype(BF16)


def _fox_sample(page_table, q_rows, k_new, v_new, dq_b, dkn_b, tot, cache_tiles, logf_t, page, n_kv, gsz, ts, n_pages):
    bs, rows, _ = q_rows.shape
    n_h = n_kv * gsz
    nsteps = page_table.shape[1] // n_pages
    first = lambda b, p, pt: (nsteps - jnp.maximum(p, 1)) * n_pages
    lf_specs = [pl.BlockSpec((1, n_h, page), lambda b, p, pt, r=r: (pt[b, first(b, p, pt) + r], 0, 0))
                for r in range(n_pages)]
    row_spec = lambda n: pl.BlockSpec((1, n, HEAD_DIM), lambda b, p, pt: (b, 0, 0))
    assert ts * n_kv <= LANES
    grid_spec = pltpu.PrefetchScalarGridSpec(
        num_scalar_prefetch=1,
        grid=(bs, nsteps + 1),
        in_specs=[
            row_spec(rows), row_spec(ts * n_kv), row_spec(ts * n_kv), row_spec(rows), row_spec(rows),
            pl.BlockSpec((1, n_h, 1), lambda b, p, pt: (b, 0, 0)),
        ] + _tile_specs(page, cache_tiles.shape[1], n_pages, first) + lf_specs,
        out_specs=row_spec(rows),
        scratch_shapes=[pltpu.VMEM((rows, 1), F32), pltpu.VMEM((rows, 1), F32), pltpu.VMEM((rows, HEAD_DIM), F32),
                        pltpu.VMEM((n_h, 1), F32), pltpu.VMEM((rows, n_pages * page * SUBLANES), F32),
                        pltpu.VMEM((LANES, HEAD_DIM), F32), pltpu.VMEM((LANES, HEAD_DIM), F32)],
    )
    return pl.pallas_call(
        functools.partial(_fox_sample_kernel, n_kv=n_kv, gsz=gsz, ts=ts, n_pages=n_pages),
        grid_spec=grid_spec,
        out_shape=jax.ShapeDtypeStruct((bs, rows, HEAD_DIM), BF16),
        compiler_params=_params("parallel", "arbitrary"),
    )(page_table, q_rows, k_new, v_new, dq_b, dkn_b, tot, *([cache_tiles] * n_pages), *([logf_t] * n_pages))


def _compress_mlp(acc_a, acc_b_next, w2):
    return _dot(_gelu_tanh(acc_a + acc_b_next).astype(BF16), w2).astype(BF16)


def _compress_prompt_kernel(z_ref, pe_ref, w1_ref, w2_ref, o_ref, *, nch):
    acc_a = jnp.zeros((nch, CMP_HID), F32)
    acc_b = jnp.zeros((nch, CMP_HID), F32)
    for l in range(CMP_STRIDE):
        x = z_ref[0, pl.ds(l, nch, stride=CMP_STRIDE), :]
        xa = (x + pe_ref[0, l:l + 1, :]).astype(BF16)
        xb = (x + pe_ref[0, CMP_STRIDE + l:CMP_STRIDE + l + 1, :]).astype(BF16)
        acc_a = acc_a + _dot(xa, w1_ref[0, l])
        acc_b = acc_b + _dot(xb, w1_ref[0, CMP_STRIDE + l])
    o_ref[0, 0, 0] = _compress_mlp(acc_a, pltpu.roll(acc_b, nch - 1, axis=0), w2_ref[0])


def _compress_prompt(z3, pe, w1, w2, n_kv, col_blk0):
    b, t, _ = z3.shape
    nch = t // CMP_STRIDE
    return pl.pallas_call(
        functools.partial(_compress_prompt_kernel, nch=nch),
        grid=(b, 2 * n_kv),
        in_specs=[
            pl.BlockSpec((1, t, HEAD_DIM), lambda i, c: (i, 0, col_blk0 + c)),
            pl.BlockSpec((1, CMP_LEN, HEAD_DIM), lambda i, c: (c // n_kv, 0, 0)),
            pl.BlockSpec((1, CMP_LEN, HEAD_DIM, CMP_HID), lambda i, c: (c // n_kv, 0, 0, 0)),
            pl.BlockSpec((1, CMP_HID, HEAD_DIM), lambda i, c: (c // n_kv, 0, 0)),
        ],
        out_specs=pl.BlockSpec((1, 1, 1, nch, HEAD_DIM), lambda i, c: (i, c // n_kv, c % n_kv, 0, 0)),
        out_shape=jax.ShapeDtypeStruct((b, 2, n_kv, nch, HEAD_DIM), BF16),
        compiler_params=_params("parallel", "arbitrary"),
    )(z3, pe, w1, w2)


def _nsa_prompt_kernel(q_ref, cos_ref, sin_ref, kc_ref, vc_ref, ks_ref, vs_ref, kw_ref, vw_ref, gl_ref, bg_ref,
                       o_ref, ksb_ref, vsb_ref, kwb_ref, vwb_ref, s_ref, *, tq, gsz, n_kv, nblk, ns, tk):
    kh = pl.program_id(1)
    qi = pl.program_id(2)
    _cast_kv_once(qi, ((ks_ref, ksb_ref), (vs_ref, vsb_ref), (kw_ref, kwb_ref), (vw_ref, vwb_ref)))
    t0 = qi * tq
    qscale = HEAD_DIM ** -0.5 * LOG2E
    cos, sin = cos_ref[...], sin_ref[...]
    qs = [q_ref[0, :, g * HEAD_DIM:(g + 1) * HEAD_DIM] * qscale for g in range(gsz)]
    q = jnp.concatenate(qs, axis=0).astype(BF16)
    qr = jnp.concatenate([_rope_lanes(x, cos, sin, HEAD_DIM // 2) for x in qs], axis=0).astype(BF16)
    qpos = t0 + _iota((tq, 1), 0)

    def tile_g(x):
        return jnp.concatenate([x] * gsz, axis=0)

    nch = kc_ref.shape[3]
    n_i = _iota((tq, nch), 1)
    ok_c = ((n_i * CMP_STRIDE + CMP_LEN - 1 <= qpos) & (n_i < nblk)).astype(F32)
    p = _masked_softmax(_dot_nt(q, kc_ref[0, 0, 0]), tile_g(ok_c))
    o_cmp = _dot(p.astype(BF16), vc_ref[0, 0, 0])
    psum = p[0:tq]
    for g in range(1, gsz):
        psum = psum + p[g * tq:(g + 1) * tq]
    ov_t = _overlap(_iota((LANES, nch), 1) * CMP_STRIDE, _iota((LANES, nch), 0) * SLC_BLOCK)
    imp_t = _block_importance(_dot_nt_exact(ov_t, psum), t0 + _iota((1, tq), 1), ns, axis=0)
    sel = _topk_mask_t(imp_t, min(N_SELECT, ns), ns).T.astype(BF16)

    def slc_mask(k0):
        expand = (((k0 + _iota((LANES, tk), 1)) >> 6) == _iota((LANES, tk), 0)).astype(BF16)
        selk = _dot(sel, expand)
        return tile_g(jnp.where((selk > 0.5) & (k0 + _iota((tq, tk), 1) <= qpos), 0.0, NEG))

    n_blocks = (t0 + tq + tk - 1) // tk
    acc, l = _two_pass_attention(qr, ksb_ref, vsb_ref, n_blocks, tk, s_ref, add_fn=slc_mask)
    o_slc = acc / l

    wlen = WINDOW + tq
    w0 = pl.multiple_of(jnp.maximum(t0 - WINDOW, 0), tq)
    dist = qpos - (w0 + _iota((tq, wlen), 1))
    ok_w = ((dist >= 0) & (dist < WINDOW)).astype(F32)
    pw = _masked_softmax(_dot_nt(qr, kwb_ref[pl.ds(w0, wlen), :]), tile_g(ok_w))
    o_win = _dot(pw.astype(BF16), vwb_ref[pl.ds(w0, wlen), :])

    gates = jax.nn.sigmoid(gl_ref[0] + bg_ref[...])
    n_h = n_kv * gsz
    for g in range(gsz):
        rows = slice(g * tq, (g + 1) * tq)
        col = kh * gsz + g
        o = (_lane_pick(gates, col) * o_cmp[rows] + _lane_pick(gates, n_h + col) * o_slc[rows]
             + _lane_pick(gates, 2 * n_h + col) * o_win[rows])
        o_ref[0, :, g * HEAD_DIM:(g + 1) * HEAD_DIM] = o.astype(BF16)


def _nsa_prompt(z3, cos, sin, kvc, bg_pad, n_kv, gsz, ks_blk, vs_blk, kw_blk, vw_blk, gl_blk, tq, tk):
    b, t, _ = z3.shape
    nch = kvc.shape[3]
    qw = gsz * HEAD_DIM
    ns = -(-t // SLC_BLOCK)
    assert t >= WINDOW + tq and t % tk == 0 and tk % SLC_BLOCK == 0 and ns <= LANES
    kv_spec = lambda blk0: pl.BlockSpec((1, t, HEAD_DIM), lambda i, k, q: (i, 0, blk0 + k))
    return pl.pallas_call(
        functools.partial(_nsa_prompt_kernel, tq=tq, gsz=gsz, n_kv=n_kv, nblk=nch - 1, ns=ns, tk=tk),
        grid=(b, n_kv, t // tq),
        in_specs=[
            pl.BlockSpec((1, tq, qw), lambda i, k, q: (i, q, k)),
            pl.BlockSpec((tq, LANES), lambda i, k, q: (q, 0)),
            pl.BlockSpec((tq, LANES), lambda i, k, q: (q, 0)),
            pl.BlockSpec((1, 1, 1, nch, HEAD_DIM), lambda i, k, q: (i, 0, k, 0, 0)),
            pl.BlockSpec((1, 1, 1, nch, HEAD_DIM), lambda i, k, q: (i, 1, k, 0, 0)),
            kv_spec(ks_blk), kv_spec(vs_blk), kv_spec(kw_blk), kv_spec(vw_blk),
            pl.BlockSpec((1, tq, LANES), lambda i, k, q: (i, q, gl_blk)),
            pl.BlockSpec((1, LANES), lambda i, k, q: (0, 0)),
        ],
        out_specs=pl.BlockSpec((1, tq, qw), lambda i, k, q: (i, q, k)),
        out_shape=jax.ShapeDtypeStruct((b, t, n_kv * qw), BF16),
        scratch_shapes=[pltpu.VMEM((t, HEAD_DIM), BF16) for _ in range(4)] + [
            pltpu.VMEM((t // tk, gsz * tq, tk), F32)],
        compiler_params=_params("parallel", "parallel", "arbitrary"),
    )(z3, cos, sin, kvc, kvc, z3, z3, z3, z3, z3, bg_pad)


def _compress_sample_kernel(pt_ref, *refs, n_kv, n_pages, page):
    pg_refs = refs[:n_pages]
    pe_ref, w1_ref, w2_ref, o_ref, last_ref = refs[n_pages:]
    s = pl.program_id(1)
    cpp = page // CMP_STRIDE
    m = n_pages * cpp
    rpr = 2 * n_kv

    @pl.when(s == 0)
    def _():
        last_ref[...] = jnp.zeros(last_ref.shape, F32)

    for typ in range(2):
        acc_a = jnp.zeros((n_kv * m, CMP_HID), F32)
        acc_b = jnp.zeros((n_kv * m, CMP_HID), F32)
        for l in range(CMP_STRIDE):
            x = jnp.concatenate(
                [pg[pl.ds(l * rpr + typ * n_kv + h, cpp, stride=CMP_STRIDE * rpr), :]
                 for h in range(n_kv) for pg in pg_refs], axis=0)
            xa = (x + pe_ref[typ, l:l + 1, :]).astype(BF16)
            xb = (x + pe_ref[typ, CMP_STRIDE + l:CMP_STRIDE + l + 1, :]).astype(BF16)
            acc_a = acc_a + _dot(xa, w1_ref[typ, l])
            acc_b = acc_b + _dot(xb, w1_ref[typ, CMP_STRIDE + l])
        first = _iota((m, 1), 0) == 0
        for h in range(n_kv):
            a = acc_a[h * m:(h + 1) * m]
            idx = typ * n_kv + h
            prev = jnp.where(first, last_ref[idx, SUBLANES - 1:SUBLANES, :], pltpu.roll(a, 1, axis=0))
            last_ref[idx] = a[m - SUBLANES:m]
            o_ref[0, typ, h] = _compress_mlp(prev, acc_b[h * m:(h + 1) * m], w2_ref[typ])


def _compress_sample(page_table, cache_rows, pe, w1, w2, page, n_kv, n_pages):
    bs, npg = page_table.shape
    m = n_pages * (page // CMP_STRIDE)
    grid_spec = pltpu.PrefetchScalarGridSpec(
        num_scalar_prefetch=1,
        grid=(bs, npg // n_pages),
        in_specs=_page_specs(page, n_kv, n_pages, lambda b, s, pt: s * n_pages) + [
            pl.BlockSpec((2, CMP_LEN, HEAD_DIM), lambda b, s, pt: (0, 0, 0)),
            pl.BlockSpec((2, CMP_LEN, HEAD_DIM, CMP_HID), lambda b, s, pt: (0, 0, 0, 0)),
            pl.BlockSpec((2, CMP_HID, HEAD_DIM), lambda b, s, pt: (0, 0, 0)),
        ],
        out_specs=pl.BlockSpec((1, 2, n_kv, m, HEAD_DIM), lambda b, s, pt: (b, 0, 0, s, 0)),
        scratch_shapes=[pltpu.VMEM((2 * n_kv, SUBLANES, CMP_HID), F32)],
    )
    return pl.pallas_call(
        functools.partial(_compress_sample_kernel, n_kv=n_kv, n_pages=n_pages, page=page),
        grid_spec=grid_spec,
        out_shape=jax.ShapeDtypeStruct((bs, 2, n_kv, npg * (page // CMP_STRIDE), HEAD_DIM), BF16),
        compiler_params=_params("parallel", "arbitrary"),
    )(page_table, *([cache_rows] * n_pages), pe, w1, w2)


def _nsa_sample_a_kernel(q_ref, cos_ref, sin_ref, kc_ref, vc_ref, win_ref, kwn_ref,
                         ocmp_ref, owin_ref, sel_ref, qr_ref, kpad_ref, vpad_ref, *, n_kv, gsz, ts, p0, ns):
    qscale = HEAD_DIM ** -0.5 * LOG2E
    rows = ts * gsz
    nr = kc_ref.shape[3]
    nsp = sel_ref.shape[3]
    wrows = win_ref.shape[1] // (2 * n_kv)
    tok = _iota((rows, 1), 0) // gsz
    qpos = p0 + tok
    r_i = _iota((rows, nr), 1)
    ok_c = ((r_i >= 1) & ((r_i - 1) * CMP_STRIDE + CMP_LEN - 1 <= qpos)).astype(F32)
    ov = _overlap((_iota((nr, nsp), 0) - 1) * CMP_STRIDE, _iota((nr, nsp), 1) * SLC_BLOCK)
    group = ((_iota((rows, rows), 0) // gsz) == (_iota((rows, rows), 1) // gsz)).astype(F32)
    ok_w1 = _iota((rows, wrows), 1) > tok + (wrows - WINDOW)
    ok_w2 = _iota((rows, kpad_ref.shape[0]), 1) <= tok
    kvw = n_kv * HEAD_DIM
    for kh in range(n_kv):
        q = q_ref[0, kh]
        qr = _rope_lanes(q, cos_ref[...], sin_ref[...], HEAD_DIM // 2)
        qr_ref[0, kh] = qr
        qb, qrb = (q * qscale).astype(BF16), (qr * qscale).astype(BF16)
        p = _masked_softmax(_dot_nt(qb, kc_ref[0, 0, kh]), ok_c)
        ocmp_ref[0, kh] = _dot(p.astype(BF16), vc_ref[0, 0, kh])
        imp = _dot_exact(group, _dot_exact(p, ov))
        sel_ref[0, kh] = _topk_mask(_block_importance(imp, qpos, ns), min(N_SELECT, ns))
        sl = slice(kh * HEAD_DIM, (kh + 1) * HEAD_DIM)
        slv = slice(kvw + kh * HEAD_DIM, kvw + (kh + 1) * HEAD_DIM)
        _pad_rows_scratch(kpad_ref, kwn_ref[0, :, sl])
        _pad_rows_scratch(vpad_ref, kwn_ref[0, :, slv])
        k_win = win_ref[0, pl.ds(kh, wrows, stride=2 * n_kv), :].astype(BF16)
        v_win = win_ref[0, pl.ds(n_kv + kh, wrows, stride=2 * n_kv), :].astype(BF16)
        s1 = jnp.where(ok_w1, _dot_nt(qrb, k_win), NEG)
        s2 = jnp.where(ok_w2, _dot_nt(qrb, kpad_ref[...].astype(BF16)), NEG)
        m = jnp.maximum(jnp.max(s1, axis=-1, keepdims=True), jnp.max(s2, axis=-1, keepdims=True))
        e1, e2 = jnp.exp2(s1 - m), jnp.exp2(s2 - m)
        l = jnp.sum(e1, axis=-1, keepdims=True) + jnp.sum(e2, axis=-1, keepdims=True)
        o = _dot(e1.astype(BF16), v_win) + _dot(e2.astype(BF16), vpad_ref[...].astype(BF16))
        owin_ref[0, kh] = o / l


def _nsa_sample_a(q_rows, cos, sin, kvc, win_rows, kvw_new, n_kv, gsz, ts, p0, ns):
    bs = q_rows.shape[0]
    rows = ts * gsz
    nr = kvc.shape[3]
    nsp = _round_up(ns, LANES)
    wr, w = win_rows.shape[1], kvw_new.shape[2]
    assert wr == WINDOW * 2 * n_kv
    o_sds = jax.ShapeDtypeStruct((bs, n_kv, rows, HEAD_DIM), F32)
    o_spec = pl.BlockSpec((1, n_kv, rows, HEAD_DIM), lambda b: (b, 0, 0, 0))
    return pl.pallas_call(
        functools.partial(_nsa_sample_a_kernel, n_kv=n_kv, gsz=gsz, ts=ts, p0=p0, ns=ns),
        grid=(bs,),
        in_specs=[
            o_spec,
            pl.BlockSpec((rows, LANES), lambda b: (0, 0)),
            pl.BlockSpec((rows, LANES), lambda b: (0, 0)),
            pl.BlockSpec((1, 1, n_kv, nr, HEAD_DIM), lambda b: (b, 0, 0, 0, 0)),
            pl.BlockSpec((1, 1, n_kv, nr, HEAD_DIM), lambda b: (b, 1, 0, 0, 0)),
            pl.BlockSpec((1, wr, HEAD_DIM), lambda b: (b, 0, 0)),
            pl.BlockSpec((1, ts, w), lambda b: (b, 0, 0)),
        ],
        out_specs=[o_spec, o_spec, pl.BlockSpec((1, n_kv, rows, nsp), lambda b: (b, 0, 0, 0)), o_spec],
        out_shape=[o_sds, o_sds, jax.ShapeDtypeStruct((bs, n_kv, rows, nsp), F32), o_sds],
        scratch_shapes=[pltpu.VMEM((LANES, HEAD_DIM), F32), pltpu.VMEM((LANES, HEAD_DIM), F32)],
        compiler_params=_params("parallel"),
    )(q_rows, cos, sin, kvc, kvc, win_rows, kvw_new)


def _nsa_sample_b_kernel(pt_ref, q_ref, sel_ref, kn_ref, ocmp_ref, owin_ref, gl_ref, bg_ref, *refs,
                         n_kv, gsz, ts, n_pages):
    pg_refs = refs[:n_pages]
    o_ref, m_ref, l_ref, acc_ref, bias_ref, npad_ref = refs[n_pages:]
    p = pl.program_id(1)
    last = pl.num_programs(1) - 1
    rows = n_kv * ts * gsz
    page = pg_refs[0].shape[0]
    nsp = sel_ref.shape[2]
    bps = n_pages * (page // SLC_BLOCK)
    blk_cols = SLC_BLOCK * SUBLANES
    qb = (q_ref[0] * (HEAD_DIM ** -0.5 * LOG2E)).astype(BF16)
    rr = _iota((rows, 1), 0)
    row_head = rr // (ts * gsz)
    row_tok = (rr % (ts * gsz)) // gsz

    @pl.when(p == 0)
    def _():
        _state_init(m_ref, l_ref, acc_ref)

    @pl.when(p < last)
    def _():
        ks, vs, off = _flat_kv(pg_refs)

        @pl.when(p == 0)
        def _():
            bias_ref[...] = _own_head_bias(row_head + off, bias_ref.shape[1])

        shift = (_iota((nsp, LANES), 0) == p * bps + _iota((nsp, LANES), 1)).astype(BF16)
        flags = jnp.where(_dot(sel_ref[0].astype(BF16), shift) > 0.5, 0.0, NEG)
        sel_bias = jnp.concatenate([jnp.broadcast_to(flags[:, i:i + 1], (rows, blk_cols)) for i in range(bps)], axis=1)
        _state_update(m_ref, l_ref, acc_ref, ..., _dot_nt(qb, ks) + bias_ref[...] + sel_bias, vs)

    @pl.when(p == last)
    def _():
        _pad_rows_scratch(npad_ref, kn_ref[0])
        flat = npad_ref[...]
        s = _dot_nt(qb, pltpu.roll(flat, n_kv, axis=0).astype(BF16))
        c = _iota(s.shape, 1)
        ok = (((c % (2 * n_kv)) == n_kv + row_head) & ((c // (2 * n_kv)) <= row_tok)
              & (_lane_pick(sel_ref[0], last * bps) > 0.5))
        _state_update(m_ref, l_ref, acc_ref, ..., jnp.where(ok, s, NEG), flat.astype(BF16))
        gates = jax.nn.sigmoid(gl_ref[0] + bg_ref[...])
        o_ref[0] = (gates[:, 0:1] * ocmp_ref[0] + gates[:, 1:2] * (acc_ref[...] / l_ref[...])
                    + gates[:, 2:3] * owin_ref[0])


def _nsa_sample_b(page_table, q_rows, sel, kvs_new, o_cmp, o_win, gl_rows, bg_rows, cache_tiles,
                  page, n_kv, gsz, ts, n_pages):
    bs, rows, _ = q_rows.shape
    nsp = sel.shape[2]
    nsteps = page_table.shape[1] // n_pages
    first = lambda b, p, pt: jnp.minimum(p, nsteps - 1) * n_pages
    row_spec = lambda n, width: pl.BlockSpec((1, n, width), lambda b, p, pt: (b, 0, 0))
    assert ts * 2 * n_kv <= LANES and n_pages * (page // SLC_BLOCK) <= LANES
    grid_spec = pltpu.PrefetchScalarGridSpec(
        num_scalar_prefetch=1,
        grid=(bs, nsteps + 1),
        in_specs=[
            row_spec(rows, HEAD_DIM), row_spec(rows, nsp), row_spec(ts * 2 * n_kv, HEAD_DIM),
            row_spec(rows, HEAD_DIM), row_spec(rows, HEAD_DIM), row_spec(rows, LANES),
            pl.BlockSpec((rows, LANES), lambda b, p, pt: (0, 0)),
        ] + _tile_specs(page, cache_tiles.shape[1], n_pages, first),
        out_specs=row_spec(rows, HEAD_DIM),
        scratch_shapes=[pltpu.VMEM((rows, 1), F32), pltpu.VMEM((rows, 1), F32), pltpu.VMEM((rows, HEAD_DIM), F32),
                        pltpu.VMEM((rows, n_pages * page * SUBLANES), F32), pltpu.VMEM((LANES, HEAD_DIM), F32)],
    )
    return pl.pallas_call(
        functools.partial(_nsa_sample_b_kernel, n_kv=n_kv, gsz=gsz, ts=ts, n_pages=n_pages),
        grid_spec=grid_spec,
        out_shape=jax.ShapeDtypeStruct((bs, rows, HEAD_DIM), F32),
        compiler_params=_params("parallel", "arbitrary"),
    )(page_table, q_rows, sel, kvs_new, o_cmp, o_win, gl_rows, bg_rows, *([cache_tiles] * n_pages))


def _rope_tables(pos, half):
    inv = ROPE_THETA ** (-jnp.arange(half, dtype=F32) / half)
    ang = pos.astype(F32)[:, None] * inv
    cos, sin = jnp.cos(ang), jnp.sin(ang)
    reps = LANES // (2 * half)
    return (jnp.tile(jnp.concatenate([cos, cos], axis=-1), (1, reps)),
            jnp.tile(jnp.concatenate([-sin, sin], axis=-1), (1, reps)))


def _pad_cols(w, n):
    return jnp.pad(w, ((0, 0), (0, n - w.shape[1])))


def _head_rows(x, n_kv, gsz):
    b, t, _ = x.shape
    return x.reshape(b, t, n_kv, gsz, -1).transpose(0, 2, 1, 3, 4).reshape(b, n_kv, t * gsz, -1)


def _token_cols(x, ts, gsz):
    b, n_kv, _, f = x.shape
    return x.reshape(b, n_kv, ts, gsz, f).transpose(0, 2, 1, 3, 4).reshape(b * ts, n_kv * gsz * f)


def kernel(x_prompt, x_sample, cache_a_kv, cache_b_kv, cache_b_logf, cache_cmp_kv, cache_slc_kv, state_win_kv, page_table, norm_mix_e, w_in_e, b_forget, lam_q1, lam_k1, lam_q2, lam_k2, subln_g, w_out_e, norm_mix_o, w_in_o, b_gate, cmp_pe_k, cmp_w1_k, cmp_w2_k, cmp_pe_v, cmp_w1_v, cmp_w2_v, w_out_o, norm_ffn, w_gate, w_up, w_down, norm_final):
    bp, t, d = x_prompt.shape
    bs, ts, _ = x_sample.shape
    n_even, n_odd = w_in_e.shape[0], w_in_o.shape[0]
    depth = n_even + n_odd
    page = cache_a_kv.shape[2]
    npg = page_table.shape[1]
    p0 = npg * page
    h_all = d // HEAD_DIM
    h_b = cache_b_logf.shape[-1]
    h_a = h_all - h_b
    kv_a, kv_b, kv_c = cache_a_kv.shape[4], cache_b_kv.shape[4], cache_cmp_kv.shape[4]
    g_a, g_b, g_c = h_a // kv_a, h_b // kv_b, h_all // kv_c
    mp, ms = bp * t, bs * ts
    hd = HEAD_DIM

    e_qa, e_ka = 0, h_a * hd
    e_va = e_ka + kv_a * hd
    e_qf = e_va + kv_a * hd
    e_kf = e_qf + h_b * hd
    e_vf = e_kf + kv_b * hd
    e_fg = e_vf + kv_b * hd
    o_kvc = h_all * hd
    o_kvs = o_kvc + 2 * kv_c * hd
    o_kvw = o_kvs + 2 * kv_c * hd
    o_gl = o_kvw + 2 * kv_c * hd
    assert 3 * h_all <= LANES and h_b <= LANES

    tn_e = _pick_tile(e_va, (512, 256, 128))
    tn_o = _pick_tile(kv_c * hd, (512, 256, 128))
    ne_pad = _round_up(e_fg + h_b, tn_e)
    no_pad = _round_up(o_gl + 3 * h_all, tn_o)
    tm_p = _pick_tile(mp, (512, 256, 128))
    tn_out = _pick_tile(d, (512, 256, 128))
    tf = _pick_tile(w_gate.shape[2], (256, 128))
    tq_e = _pick_tile(t, (256, 128))
    tk_e = _pick_tile(t, (512, 256))
    tq_o = 128
    tk_o = _pick_tile(t, (256, 128))
    n_pages = _pick_tile(npg, (8, 4, 2))

    xp = x_prompt.reshape(mp, d)
    xs = x_sample.reshape(ms, d)
    pos_p = jnp.tile(jnp.arange(t, dtype=jnp.int32), bp)
    pos_s = p0 + jnp.tile(jnp.arange(ts, dtype=jnp.int32), bs)
    tab_a_p, tab_a_s = _rope_tables(pos_p, DA // 2), _rope_tables(pos_s, DA // 2)
    tab_c_p, tab_c_s = _rope_tables(pos_p, hd // 2), _rope_tables(pos_s, hd // 2)
    tab_c_q = _rope_tables(jnp.arange(t, dtype=jnp.int32), hd // 2)
    tab_c_rows = _rope_tables(p0 + jnp.repeat(jnp.arange(ts, dtype=jnp.int32), g_c), hd // 2)

    outs = {name: ([], []) for name in ("a", "b", "f", "c", "s", "w")}

    for layer in range(depth):
        i = layer // 2
        if layer % 2 == 0:
            lam_init = 0.8 - 0.6 * float(np.exp(-0.3 * layer))
            w_in = _pad_cols(w_in_e[i], ne_pad).astype(BF16)
            w_out = w_out_e[i].astype(BF16)
            rope_e = ((0, e_va // tn_e),)
            lam_vec = jnp.stack([lam_q1[i], lam_k1[i], lam_q2[i], lam_k2[i]])
            bf_pad = jnp.pad(b_forget[i], (0, LANES - h_b)).reshape(1, LANES)

            z = _proj(xp, norm_mix_e[i], w_in, *tab_a_p, rope_e, DA // 2, tm_p, tn_e)
            z3 = z.reshape(bp, t, ne_pad)
            outs["a"][0].append(z3[:, :, e_ka:e_qf].reshape(bp, t, 2, kv_a, hd))
            outs["b"][0].append(z3[:, :, e_kf:e_fg].reshape(bp, t, 2, kv_b, hd))
            lf, dkc, dkr = _logf_prompt(z3, bf_pad, e_fg // LANES)
            outs["f"][0].append(lf[:, :, :h_b])
            o_a = _diff_prompt(z3, lam_vec, subln_g[i], kv_a, g_a, e_qa // (g_a * hd), e_ka // hd, e_va // hd,
                               lam_init, tq_e, tk_e)
            o_b = _fox_prompt(z3, dkr[:, :h_b].reshape(bp, h_b, 1, t), dkc, kv_b, g_b,
                              e_qf // (g_b * hd), e_kf // hd, e_vf // hd, tq_e, tk_e)
            xp = _mm_res([o_a.reshape(mp, h_a * hd), o_b.reshape(mp, h_b * hd)], w_out, xp, tm_p, tn_out)

            z = _proj(xs, norm_mix_e[i], w_in, *tab_a_s, rope_e, DA // 2, ms, tn_e)
            z3 = z.reshape(bs, ts, ne_pad)
            outs["a"][1].append(z3[:, :, e_ka:e_qf].reshape(bs, ts, 2, kv_a, hd))
            outs["b"][1].append(z3[:, :, e_kf:e_fg].reshape(bs, ts, 2, kv_b, hd))
            lf, dq, tot = _logf_sample(z[:, e_fg:e_fg + LANES], bf_pad, ts)
            outs["f"][1].append(lf[:, :h_b].reshape(bs, ts, h_b))
            o_a = _diff_sample(page_table, lam_vec, subln_g[i],
                               _head_rows(z3[:, :, e_qa:e_ka], kv_a, g_a).reshape(bs, -1, hd),
                               z3[:, :, e_ka:e_va].reshape(bs, ts * kv_a, hd),
                               z3[:, :, e_va:e_qf].reshape(bs, ts * kv_a, hd),
                               _tile_view(cache_a_kv[i]), page, kv_a, g_a, ts, lam_init, n_pages)
            rows_b = kv_b * ts * g_b
            dq3 = dq[:, :h_b].reshape(bs, ts, h_b)
            dq_b = jnp.broadcast_to(_head_rows(dq3, kv_b, g_b).reshape(bs, rows_b, 1), (bs, rows_b, LANES))
            dkn = jnp.swapaxes(dq3, 1, 2).reshape(bs, kv_b, 1, g_b, ts, 1)
            dkn = jnp.broadcast_to(dkn, (bs, kv_b, ts, g_b, ts, kv_b)).reshape(bs, rows_b, ts * kv_b)
            dkn_b = jnp.pad(dkn, ((0, 0), (0, 0), (0, LANES - ts * kv_b)))
            tot3 = tot[:, :h_b].reshape(bs, ts, h_b)[:, 0, :, None]
            o_b = _fox_sample(page_table, _head_rows(z3[:, :, e_qf:e_kf], kv_b, g_b).reshape(bs, -1, hd),
                              z3[:, :, e_kf:e_vf].reshape(bs, ts * kv_b, hd),
                              z3[:, :, e_vf:e_fg].reshape(bs, ts * kv_b, hd), dq_b, dkn_b, tot3,
                              _tile_view(cache_b_kv[i]), jnp.swapaxes(cache_b_logf[i], 1, 2),
                              page, kv_b, g_b, ts, n_pages)
            xs = _mm_res([_token_cols(o_a.reshape(bs, kv_a, ts * g_a, hd), ts, g_a),
                          _token_cols(o_b.reshape(bs, kv_b, ts * g_b, hd), ts, g_b)], w_out, xs, ms, tn_out)
        else:
            w_in = _pad_cols(w_in_o[i], no_pad).astype(BF16)
            w_out = w_out_o[i].astype(BF16)
            rope_o = ((o_kvs // tn_o, (o_kvs + kv_c * hd) // tn_o), (o_kvw // tn_o, (o_kvw + kv_c * hd) // tn_o))
            bg_pad = jnp.pad(b_gate[i], (0, LANES - 3 * h_all)).reshape(1, LANES)
            pe = jnp.stack([cmp_pe_k[i], cmp_pe_v[i]])
            w1 = jnp.stack([cmp_w1_k[i], cmp_w1_v[i]]).astype(BF16)
            w2 = jnp.stack([cmp_w2_k[i], cmp_w2_v[i]]).astype(BF16)

            z = _proj(xp, norm_mix_o[i], w_in, *tab_c_p, rope_o, hd // 2, tm_p, tn_o)
            z3 = z.reshape(bp, t, no_pad)
            outs["c"][0].append(z3[:, :, o_kvc:o_kvs].reshape(bp, t, 2, kv_c, hd))
            outs["s"][0].append(z3[:, :, o_kvs:o_kvw].reshape(bp, t, 2, kv_c, hd))
            keep = min(WINDOW, t)
            outs["w"][0].append(z3[:, t - keep:, o_kvw:o_gl].reshape(bp, keep, 2, kv_c, hd))
            kvc = _compress_prompt(z3, pe, w1, w2, kv_c, o_kvc // hd)
            o = _nsa_prompt(z3, *tab_c_q, kvc, bg_pad, kv_c, g_c, o_kvs // hd, o_kvs // hd + kv_c,
                            o_kvw // hd, o_kvw // hd + kv_c, o_gl // LANES, tq_o, tk_o)
            xp = _mm_res([o.reshape(mp, d)], w_out, xp, tm_p, tn_out)

            z = _proj(xs, norm_mix_o[i], w_in, *tab_c_s, rope_o, hd // 2, ms, tn_o)
            z3 = z.reshape(bs, ts, no_pad)
            outs["c"][1].append(z3[:, :, o_kvc:o_kvs].reshape(bs, ts, 2, kv_c, hd))
            outs["s"][1].append(z3[:, :, o_kvs:o_kvw].reshape(bs, ts, 2, kv_c, hd))
            win = state_win_kv[i]
            kvw_new = z3[:, :, o_kvw:o_gl]
            win_all = jnp.concatenate([win, kvw_new.reshape(bs, ts, 2, kv_c, hd)], axis=1)
            outs["w"][1].append(win_all[:, -min(WINDOW, win.shape[1] + ts):])
            kvc = _compress_sample(page_table, cache_cmp_kv[i].reshape(-1, hd), pe, w1, w2, page, kv_c, min(16, npg))
            ns = -(-(p0 + ts) // SLC_BLOCK)
            o_cmp, o_win, sel, qr = _nsa_sample_a(_head_rows(z3[:, :, :o_kvc], kv_c, g_c), *tab_c_rows, kvc,
                                                  win.reshape(bs, -1, hd), kvw_new, kv_c, g_c, ts, p0, ns)
            rows = kv_c * ts * g_c
            gl = z3[:, :, o_gl:o_gl + 3 * h_all].reshape(bs, ts, 3, kv_c, g_c).transpose(0, 3, 1, 4, 2)
            gl_rows = jnp.pad(gl.reshape(bs, rows, 3), ((0, 0), (0, 0), (0, LANES - 3)))
            bg = jnp.broadcast_to(b_gate[i].reshape(1, 3, kv_c, g_c), (ts, 3, kv_c, g_c)).transpose(2, 0, 3, 1)
            bg_rows = jnp.pad(bg.reshape(rows, 3), ((0, 0), (0, LANES - 3)))
            o = _nsa_sample_b(page_table, qr.reshape(bs, rows, hd), sel.reshape(bs, rows, -1),
                              z3[:, :, o_kvs:o_kvw].reshape(bs, ts * 2 * kv_c, hd),
                              o_cmp.reshape(bs, rows, hd), o_win.reshape(bs, rows, hd), gl_rows, bg_rows,
                              _tile_view(cache_slc_kv[i]), page, kv_c, g_c, ts, n_pages)
            xs = _mm_res([_token_cols(o.reshape(bs, kv_c, ts * g_c, hd), ts, g_c).astype(BF16)], w_out, xs, ms, tn_out)

        g_final = norm_final if layer == depth - 1 else None
        wg, wu, wd = w_gate[layer].astype(BF16), w_up[layer].astype(BF16), w_down[layer].astype(BF16)
        xp = _ffn(xp, norm_ffn[layer], wg, wu, wd, g_final, tm_p, tf)
        xs = _ffn(xs, norm_ffn[layer], wg, wu, wd, g_final, ms, tf)

    def stack(name, which):
        return jnp.stack(outs[name][which])

    return (xp.reshape(bp, t, d), xs.reshape(bs, ts, d),
            stack("a", 0), stack("a", 1), stack("b", 0), stack("b", 1), stack("f", 0), stack("f", 1),
            stack("c", 0), stack("c", 1), stack("s", 0), stack("s", 1), stack("w", 0), stack("w", 1))
```

```python
import functools

import numpy as np
import jax
import jax.numpy as jnp
from jax import lax
from jax.experimental import pallas as pl
from jax.experimental.pallas import tpu as pltpu

F32 = jnp.float32
BF16 = jnp.bfloat16
HIGHEST = lax.Precision.HIGHEST

LANES = 128
SUBLANES = 8
VMEM_LIMIT = 56 * 1024 * 1024

HEAD_DIM = 128
DA = HEAD_DIM // 2
CMP_LEN = 32
CMP_STRIDE = 16
CMP_HID = 2 * HEAD_DIM
SLC_BLOCK = 64
N_SELECT = 16
WINDOW = 512
ROPE_THETA = 10000.0
NORM_EPS = 1e-6
LOG2E = 1.4426950408889634
NEG = -1e30
BIG = 1e30
LOWEST = -3e38


def _params(*sem):
    return pltpu.CompilerParams(dimension_semantics=sem, vmem_limit_bytes=VMEM_LIMIT)


def _dot(a, b):
    return jnp.dot(a, b, preferred_element_type=F32)


def _dot_nt(a, b):
    return lax.dot_general(a, b, (((1,), (1,)), ((), ())), preferred_element_type=F32)


def _dot_exact(a, b):
    return jnp.dot(a, b, precision=HIGHEST, preferred_element_type=F32)


def _dot_nt_exact(a, b):
    return lax.dot_general(a, b, (((1,), (1,)), ((), ())), precision=HIGHEST, preferred_element_type=F32)


def _iota(shape, axis):
    return lax.broadcasted_iota(jnp.int32, shape, axis)


def _round_up(n, m):
    return -(-n // m) * m


def _pick_tile(n, prefs):
    for p in prefs:
        if n % p == 0:
            return p
    return n


def _rope_lanes(a, cos, sin, half):
    if 2 * half == LANES:
        partner = pltpu.roll(a, half, axis=1)
    else:
        first = (_iota(a.shape, 1) & (2 * half - 1)) < half
        partner = jnp.where(first, pltpu.roll(a, LANES - half, axis=1), pltpu.roll(a, half, axis=1))
    return a * cos + partner * sin


def _rmsnorm_rows(x, g):
    ms = jnp.mean(x * x, axis=-1, keepdims=True)
    return x * lax.rsqrt(ms + NORM_EPS) * g


def _proj_kernel(x_ref, g_ref, w_ref, cos_ref, sin_ref, o_ref, xn_ref, *, rope_ranges, half):
    j = pl.program_id(1)

    @pl.when(j == 0)
    def _():
        xn_ref[...] = _rmsnorm_rows(x_ref[...], g_ref[...]).astype(BF16)

    o_ref[...] = _dot(xn_ref[...], w_ref[0])
    is_rope = functools.reduce(jnp.logical_or, [(j >= a) & (j < b) for a, b in rope_ranges])

    @pl.when(is_rope)
    def _():
        cos, sin = cos_ref[...], sin_ref[...]
        for c in range(o_ref.shape[1] // LANES):
            sl = slice(c * LANES, (c + 1) * LANES)
            o_ref[:, sl] = _rope_lanes(o_ref[:, sl], cos, sin, half)


def _proj(x, g, w, cos, sin, rope_ranges, half, tm):
    m, d = x.shape
    n_tiles, _, tn = w.shape
    n = n_tiles * tn
    return pl.pallas_call(
        functools.partial(_proj_kernel, rope_ranges=rope_ranges, half=half),
        grid=(m // tm, n_tiles),
        in_specs=[
            pl.BlockSpec((tm, d), lambda i, j: (i, 0), pipeline_mode=pl.Buffered(1)),
            pl.BlockSpec((1, d), lambda i, j: (0, 0)),
            pl.BlockSpec((1, d, tn), lambda i, j: (j, 0, 0)),
            pl.BlockSpec((tm, LANES), lambda i, j: (i, 0)),
            pl.BlockSpec((tm, LANES), lambda i, j: (i, 0)),
        ],
        out_specs=pl.BlockSpec((tm, tn), lambda i, j: (i, j)),
        out_shape=jax.ShapeDtypeStruct((m, n), F32),
        scratch_shapes=[pltpu.VMEM((tm, d), BF16)],
        compiler_params=_params("parallel", "arbitrary"),
    )(x, g.reshape(1, d), w, cos, sin)


def _mm_res_kernel(*refs, n_lhs):
    a_refs, (w_ref, r_ref, o_ref) = refs[:n_lhs], refs[n_lhs:]
    acc = r_ref[...]
    off = 0
    for a_ref in a_refs:
        kk = a_ref.shape[1]
        acc = acc + _dot(a_ref[...], w_ref[0, off:off + kk, :])
        off += kk
    o_ref[...] = acc


def _mm_res(lhs, w, res, tm):
    m, n = res.shape
    n_tiles, k, tn = w.shape
    return pl.pallas_call(
        functools.partial(_mm_res_kernel, n_lhs=len(lhs)),
        grid=(m // tm, n_tiles),
        in_specs=[pl.BlockSpec((tm, a.shape[1]), lambda i, j: (i, 0)) for a in lhs] + [
            pl.BlockSpec((1, k, tn), lambda i, j: (j, 0, 0)),
            pl.BlockSpec((tm, tn), lambda i, j: (i, j)),
        ],
        out_specs=pl.BlockSpec((tm, tn), lambda i, j: (i, j)),
        out_shape=jax.ShapeDtypeStruct((m, n), F32),
        compiler_params=_params("parallel", "arbitrary"),
    )(*lhs, w, res)


def _ffn_kernel(x_ref, g_ref, wg_ref, wu_ref, wd_ref, gf_ref, o_ref, xn_ref, *, final_norm):
    f = pl.program_id(1)

    @pl.when(f == 0)
    def _():
        x = x_ref[...]
        xn_ref[...] = _rmsnorm_rows(x, g_ref[...]).astype(BF16)
        o_ref[...] = x

    xn = xn_ref[...]
    gate = _dot(xn, wg_ref[0])
    up = _dot(xn, wu_ref[0])
    h = (gate * jax.nn.sigmoid(gate) * up).astype(BF16)
    o_ref[...] += _dot(h, wd_ref[...])

    if final_norm:
        @pl.when(f == pl.num_programs(1) - 1)
        def _():
            o_ref[...] = _rmsnorm_rows(o_ref[...], gf_ref[...])


def _ffn(x, g, wg, wu, wd, g_final, tm):
    m, d = x.shape
    n_tiles, _, tf = wg.shape
    final_norm = g_final is not None
    gf = (g_final if final_norm else g).reshape(1, d)
    return pl.pallas_call(
        functools.partial(_ffn_kernel, final_norm=final_norm),
        grid=(m // tm, n_tiles),
        in_specs=[
            pl.BlockSpec((tm, d), lambda i, f: (i, 0), pipeline_mode=pl.Buffered(1)),
            pl.BlockSpec((1, d), lambda i, f: (0, 0)),
            pl.BlockSpec((1, d, tf), lambda i, f: (f, 0, 0)),
            pl.BlockSpec((1, d, tf), lambda i, f: (f, 0, 0)),
            pl.BlockSpec((tf, d), lambda i, f: (f, 0)),
            pl.BlockSpec((1, d), lambda i, f: (0, 0)),
        ],
        out_specs=pl.BlockSpec((tm, d), lambda i, f: (i, 0)),
        out_shape=jax.ShapeDtypeStruct((m, d), F32),
        scratch_shapes=[pltpu.VMEM((tm, d), BF16)],
        compiler_params=_params("parallel", "arbitrary"),
    )(x, g.reshape(1, d), wg, wu, wd, gf)


def _online_update(carry, s, v, shift=None):
    m, l, acc = carry
    smax = jnp.max(s, axis=-1, keepdims=True)
    if shift is not None:
        smax = smax - shift
    m_new = jnp.maximum(m, smax)
    alpha = jnp.exp2(m - m_new)
    p = jnp.exp2(s - (m_new if shift is None else m_new + shift))
    l = alpha * l + jnp.sum(p, axis=-1, keepdims=True)
    acc = alpha * acc + _dot(p.astype(BF16), v)
    return m_new, l, acc


def _state_init(m_ref, l_ref, acc_ref):
    m_ref[...] = jnp.full(m_ref.shape, NEG, F32)
    l_ref[...] = jnp.zeros(l_ref.shape, F32)
    acc_ref[...] = jnp.zeros(acc_ref.shape, F32)


def _state_update(m_ref, l_ref, acc_ref, k, s, v, shift=None):
    m, l, acc = _online_update((m_ref[k], l_ref[k], acc_ref[k]), s, v, shift)
    m_ref[k], l_ref[k], acc_ref[k] = m, l, acc


def _fold_lanes(x, op):
    out = x[:, :LANES]
    for i in range(1, x.shape[1] // LANES):
        out = op(out, x[:, i * LANES:(i + 1) * LANES])
    return out


def _two_pass_attention(q, kb_ref, vb_ref, n_blocks, tk, s_ref, add_fn=None, tail_fn=None, shift=None):
    rows = q.shape[0]

    def scores(kb, masked):
        k0 = pl.multiple_of(kb * tk, tk)
        s = _dot_nt(q, kb_ref[pl.ds(k0, tk), :])
        if add_fn is not None:
            s = s + add_fn(k0)
        if masked and tail_fn is not None:
            s = s + tail_fn(k0)
        s_ref[kb] = s
        return _fold_lanes(s, jnp.maximum)

    mx = lax.fori_loop(0, n_blocks - 1, lambda kb, mx: jnp.maximum(mx, scores(kb, False)),
                       jnp.full((rows, LANES), NEG, F32))
    mx = jnp.maximum(mx, scores(n_blocks - 1, True))
    m = jnp.max(mx, axis=-1, keepdims=True)
    if shift is not None:
        m = (m - shift) + shift

    def accumulate(kb, carry):
        ls, acc = carry
        k0 = pl.multiple_of(kb * tk, tk)
        p = jnp.exp2(s_ref[kb] - m)
        return ls + _fold_lanes(p, jnp.add), acc + _dot(p.astype(BF16), vb_ref[pl.ds(k0, tk), :])

    ls, acc = lax.fori_loop(0, n_blocks, accumulate,
                            (jnp.zeros((rows, LANES), F32), jnp.zeros((rows, HEAD_DIM), F32)))
    return acc, jnp.sum(ls, axis=-1, keepdims=True)


def _masked_softmax(s, okf):
    ok = okf > 0.5
    sm = jnp.where(ok, s, NEG)
    m = jnp.max(sm, axis=-1, keepdims=True)
    e = jnp.where(ok, jnp.exp2(sm - m), 0.0)
    return e / jnp.maximum(jnp.sum(e, axis=-1, keepdims=True), 1e-30)


def _lane_pick(x, lane_idx):
    return jnp.sum(jnp.where(_iota(x.shape, 1) == lane_idx, x, 0.0), axis=-1, keepdims=True)


def _topk_mask(imp, n_sel):
    lane = _iota(imp.shape, 1).astype(F32)
    sel = jnp.zeros(imp.shape, F32)
    x = imp
    for _ in range(n_sel):
        m = jnp.max(x, axis=-1, keepdims=True)
        idx = jnp.min(jnp.where(x == m, lane, 1e9), axis=-1, keepdims=True)
        hit = lane == idx
        sel = jnp.where(hit, 1.0, sel)
        x = jnp.where(hit, LOWEST, x)
    return sel


def _overlap(ci, sj):
    ov = jnp.minimum(ci + CMP_LEN, sj + SLC_BLOCK) - jnp.maximum(ci, sj)
    return jnp.maximum(ov, 0).astype(F32) * (1.0 / CMP_LEN)


def _topk_mask_t(imp_t, n_sel, ns):
    rows = _round_up(ns, SUBLANES)
    x = imp_t[0:rows]
    j = _iota(x.shape, 0)
    rank = jnp.zeros(x.shape, F32)
    for jp in range(ns):
        a = x[jp:jp + 1, :]
        rank = rank + jnp.where((a > x) | ((a == x) & (jp < j)), 1.0, 0.0)
    sel = jnp.where((rank < n_sel) & (j < ns), 1.0, 0.0)
    return jnp.concatenate([sel, jnp.zeros((imp_t.shape[0] - rows, imp_t.shape[1]), F32)], axis=0)


def _block_importance(imp, qpos, ns, axis=1):
    j = _iota(imp.shape, axis)
    cur = qpos >> 6
    forced = (j == 0) | (j == cur) | (j == cur - 1)
    valid = (j << 6) <= qpos
    out = jnp.where(forced, BIG, jnp.where(valid, imp, NEG))
    return jnp.where(j < ns, out, LOWEST)


def _lam_value(lam_ref, lam_init):
    v = lam_ref[...]
    a = jnp.exp(jnp.sum(v[0:1] * v[1:2], axis=-1, keepdims=True))
    b = jnp.exp(jnp.sum(v[2:3] * v[3:4], axis=-1, keepdims=True))
    return a - b + lam_init


def _gelu_tanh(x):
    return 0.5 * x * (1.0 + jnp.tanh(0.7978845608028654 * (x + 0.044715 * (x * x * x))))


def _log_sigmoid(z):
    return jnp.minimum(z, 0.0) - jnp.log1p(jnp.exp(-jnp.abs(z)))


def _cast_kv_once(step, pairs):
    @pl.when(step == 0)
    def _():
        for src, dst in pairs:
            dst[...] = src[0].astype(BF16)


def _logf_prompt_kernel(z_ref, b_ref, lf_ref, dkc_ref, dkr_ref):
    t = z_ref.shape[1]
    lf = _log_sigmoid(z_ref[0] + b_ref[...])
    lf_ref[0] = lf
    upper = (_iota((LANES, LANES), 1) > _iota((LANES, LANES), 0)).astype(F32)
    carry = jnp.zeros((1, LANES), F32)
    for blk in reversed(range(t // LANES)):
        sl = slice(blk * LANES, (blk + 1) * LANES)
        x = lf[sl]
        d = _dot_exact(upper, x) + carry
        dkc_ref[0, sl, :] = d
        dkr_ref[0, :, sl] = d.T
        carry = carry + jnp.sum(x, axis=0, keepdims=True)


def _logf_prompt(z3, b_pad, col_blk):
    b, t, _ = z3.shape
    return pl.pallas_call(
        _logf_prompt_kernel,
        grid=(b,),
        in_specs=[pl.BlockSpec((1, t, LANES), lambda i: (i, 0, col_blk)),
                  pl.BlockSpec((1, LANES), lambda i: (0, 0))],
        out_specs=[pl.BlockSpec((1, t, LANES), lambda i: (i, 0, 0)),
                   pl.BlockSpec((1, t, LANES), lambda i: (i, 0, 0)),
                   pl.BlockSpec((1, LANES, t), lambda i: (i, 0, 0))],
        out_shape=[jax.ShapeDtypeStruct((b, t, LANES), F32),
                   jax.ShapeDtypeStruct((b, t, LANES), F32),
                   jax.ShapeDtypeStruct((b, LANES, t), F32)],
        compiler_params=_params("parallel"),
    )(z3, b_pad)


def _causal_tail(qpos, tk):
    return lambda k0: jnp.where(k0 + _iota((qpos.shape[0], tk), 1) <= qpos, 0.0, NEG)


def _diff_prompt_kernel(lam_ref, g_ref, q_ref, k_ref, v_ref, o_ref, kb_ref, vb_ref, s_ref, *, tq, gsz, lam_init, tk):
    qi = pl.program_id(2)
    _cast_kv_once(qi, ((k_ref, kb_ref), (v_ref, vb_ref)))
    lam = _lam_value(lam_ref, lam_init)
    lane = _iota((tq, HEAD_DIM), 1)
    qscale = DA ** -0.5 * LOG2E
    qs = [q_ref[0, :, g * HEAD_DIM:(g + 1) * HEAD_DIM] * qscale for g in range(gsz)]
    q = jnp.concatenate([jnp.where(lane < DA, x, 0.0) for x in qs] + [jnp.where(lane >= DA, x, 0.0) for x in qs],
                        axis=0).astype(BF16)
    qpos = qi * tq + (_iota((2 * gsz * tq, 1), 0) & (tq - 1))
    n_blocks = (qi * tq) // tk + 1
    acc, l = _two_pass_attention(q, kb_ref, vb_ref, n_blocks, tk, s_ref, tail_fn=_causal_tail(qpos, tk))
    o = acc / l
    half = gsz * tq
    o = o[:half] - lam * o[half:]
    o = (_rmsnorm_rows(o, g_ref[...]) * (1.0 - lam_init)).astype(BF16)
    for g in range(gsz):
        o_ref[0, :, g * HEAD_DIM:(g + 1) * HEAD_DIM] = o[g * tq:(g + 1) * tq]


def _diff_prompt(z3, lam_vec, subln_g, n_kv, gsz, q_blk0, k_blk0, v_blk0, lam_init, tq, tk):
    b, t, _ = z3.shape
    qw = gsz * HEAD_DIM
    assert tk % tq == 0 and tq & (tq - 1) == 0
    return pl.pallas_call(
        functools.partial(_diff_prompt_kernel, tq=tq, gsz=gsz, lam_init=lam_init, tk=tk),
        grid=(b, n_kv, t // tq),
        in_specs=[
            pl.BlockSpec((4, DA), lambda i, k, q: (0, 0)),
            pl.BlockSpec((1, HEAD_DIM), lambda i, k, q: (0, 0)),
            pl.BlockSpec((1, tq, qw), lambda i, k, q: (i, q, q_blk0 + k)),
            pl.BlockSpec((1, t, HEAD_DIM), lambda i, k, q: (i, 0, k_blk0 + k)),
            pl.BlockSpec((1, t, HEAD_DIM), lambda i, k, q: (i, 0, v_blk0 + k)),
        ],
        out_specs=pl.BlockSpec((1, tq, qw), lambda i, k, q: (i, q, k)),
        out_shape=jax.ShapeDtypeStruct((b, t, n_kv * qw), BF16),
        scratch_shapes=[pltpu.VMEM((t, HEAD_DIM), BF16), pltpu.VMEM((t, HEAD_DIM), BF16),
                        pltpu.VMEM((t // tk, 2 * gsz * tq, tk), F32)],
        compiler_params=_params("parallel", "parallel", "arbitrary"),
    )(lam_vec, subln_g.reshape(1, HEAD_DIM), z3, z3, z3)


def _fox_prompt_kernel(q_ref, k_ref, v_ref, dkr_ref, dkc_ref, o_ref, kb_ref, vb_ref, s_ref, *, tq, gsz, tk):
    kh = pl.program_id(1)
    qi = pl.program_id(2)
    _cast_kv_once(qi, ((k_ref, kb_ref), (v_ref, vb_ref)))
    qscale = HEAD_DIM ** -0.5 * LOG2E
    qpos = qi * tq + (_iota((gsz * tq, 1), 0) & (tq - 1))
    n_blocks = (qi * tq) // tk + 1
    q = jnp.concatenate([q_ref[0, :, g * HEAD_DIM:(g + 1) * HEAD_DIM] * qscale for g in range(gsz)],
                        axis=0).astype(BF16)
    dq = jnp.concatenate([_lane_pick(dkc_ref[0], kh * gsz + g) for g in range(gsz)], axis=0) * LOG2E

    def dk_fn(k0):
        return jnp.concatenate([jnp.broadcast_to(dkr_ref[0, g, :, pl.ds(k0, tk)] * LOG2E, (tq, tk))
                                for g in range(gsz)], axis=0)

    acc, l = _two_pass_attention(q, kb_ref, vb_ref, n_blocks, tk, s_ref, add_fn=dk_fn,
                                 tail_fn=_causal_tail(qpos, tk), shift=dq)
    o = (acc / l).astype(BF16)
    for g in range(gsz):
        o_ref[0, :, g * HEAD_DIM:(g + 1) * HEAD_DIM] = o[g * tq:(g + 1) * tq]


def _fox_prompt(z3, dkr4, dkc, n_kv, gsz, q_blk0, k_blk0, v_blk0, tq, tk):
    b, t, _ = z3.shape
    qw = gsz * HEAD_DIM
    assert tk % tq == 0
    return pl.pallas_call(
        functools.partial(_fox_prompt_kernel, tq=tq, gsz=gsz, tk=tk),
        grid=(b, n_kv, t // tq),
        in_specs=[
            pl.BlockSpec((1, tq, qw), lambda i, k, q: (i, q, q_blk0 + k)),
            pl.BlockSpec((1, t, HEAD_DIM), lambda i, k, q: (i, 0, k_blk0 + k)),
            pl.BlockSpec((1, t, HEAD_DIM), lambda i, k, q: (i, 0, v_blk0 + k)),
            pl.BlockSpec((1, gsz, 1, t), lambda i, k, q: (i, k, 0, 0)),
            pl.BlockSpec((1, tq, LANES), lambda i, k, q: (i, q, 0)),
        ],
        out_specs=pl.BlockSpec((1, tq, qw), lambda i, k, q: (i, q, k)),
        out_shape=jax.ShapeDtypeStruct((b, t, n_kv * qw), BF16),
        scratch_shapes=[pltpu.VMEM((t, HEAD_DIM), BF16), pltpu.VMEM((t, HEAD_DIM), BF16),
                        pltpu.VMEM((t // tk, gsz * tq, tk), F32)],
        compiler_params=_params("parallel", "parallel", "arbitrary"),
    )(z3, z3, z3, dkr4, dkc)


def _logf_sample_kernel(z_ref, b_ref, lf_ref, dq_ref, tot_ref, *, ts):
    m = z_ref.shape[0]
    lf = _log_sigmoid(z_ref[...] + b_ref[...])
    lf_ref[...] = lf
    i, j = _iota((m, m), 0), _iota((m, m), 1)
    same = (i // ts) == (j // ts)
    dq_ref[...] = _dot_exact((same & (j > i)).astype(F32), lf)
    tot_ref[...] = _dot_exact(same.astype(F32), lf)


def _logf_sample(z_fg, b_pad, ts):
    m = z_fg.shape[0]
    sds = jax.ShapeDtypeStruct((m, LANES), F32)
    return pl.pallas_call(
        functools.partial(_logf_sample_kernel, ts=ts),
        out_shape=[sds, sds, sds],
    )(z_fg, b_pad)


def _pad_rows_scratch(ref, x):
    ref[...] = jnp.zeros(ref.shape, ref.dtype)
    ref[0:x.shape[0], :] = x


def _page_specs(page, n_kv, n_pages, first_page):
    rows = page * 2 * n_kv
    return [pl.BlockSpec((rows, HEAD_DIM), lambda b, p, pt, r=r: (pt[b, first_page(b, p, pt) + r], 0))
            for r in range(n_pages)]


def _tile_view(cache):
    n_kv = cache.shape[3]
    assert 2 * n_kv in (SUBLANES, 2 * SUBLANES)
    return cache.reshape(-1, 2 * n_kv // SUBLANES, SUBLANES, HEAD_DIM)


def _tile_specs(page, tiles, n_pages, first_page):
    return [pl.BlockSpec((page, tiles, SUBLANES, HEAD_DIM),
                         lambda b, p, pt, r=r: (pt[b, first_page(b, p, pt) + r], 0, 0, 0))
            for r in range(n_pages)]


def _flat_kv(pg_refs):
    n = pg_refs[0].shape[0] * SUBLANES
    if pg_refs[0].shape[1] == 2:
        ks = [pg[:, 0].reshape(n, HEAD_DIM) for pg in pg_refs]
        vs = [pg[:, 1].reshape(n, HEAD_DIM) for pg in pg_refs]
        off = 0
    else:
        vs = [pg[:, 0].reshape(n, HEAD_DIM) for pg in pg_refs]
        ks = [pltpu.roll(v, SUBLANES // 2, axis=0) for v in vs]
        off = SUBLANES // 2
    return (jnp.concatenate(ks, axis=0).astype(BF16), jnp.concatenate(vs, axis=0).astype(BF16), off)


def _own_head_bias(row_head, n_cols):
    return jnp.where((_iota((row_head.shape[0], n_cols), 1) & (SUBLANES - 1)) == row_head, 0.0, NEG)


def _diff_sample_kernel(pt_ref, lam_ref, g_ref, q_ref, kn_ref, vn_ref, *refs, n_kv, gsz, ts, lam_init, n_pages):
    pg_refs = refs[:n_pages]
    o_ref, m_ref, l_ref, acc_ref, bias_ref, kpad_ref, vpad_ref = refs[n_pages:]
    p = pl.program_id(1)
    last = pl.num_programs(1) - 1
    r1 = n_kv * ts * gsz
    q = q_ref[0] * (DA ** -0.5 * LOG2E)
    lane = _iota((r1, HEAD_DIM), 1)
    qb = jnp.concatenate([jnp.where(lane < DA, q, 0.0), jnp.where(lane >= DA, q, 0.0)], axis=0).astype(BF16)
    rr = _iota((2 * r1, 1), 0) % r1
    row_head = rr // (ts * gsz)
    row_tok = (rr % (ts * gsz)) // gsz

    @pl.when(p == 0)
    def _():
        _state_init(m_ref, l_ref, acc_ref)

    @pl.when(p < last)
    def _():
        ks, vs, off = _flat_kv(pg_refs)

        @pl.when(p == 0)
        def _():
            bias_ref[...] = _own_head_bias(row_head + off, bias_ref.shape[1])

        _state_update(m_ref, l_ref, acc_ref, ..., _dot_nt(qb, ks) + bias_ref[...], vs)

    @pl.when(p == last)
    def _():
        _pad_rows_scratch(kpad_ref, kn_ref[0])
        _pad_rows_scratch(vpad_ref, vn_ref[0])
        s = _dot_nt(qb, kpad_ref[...].astype(BF16))
        c = _iota(s.shape, 1)
        ok = ((c % n_kv) == row_head) & ((c // n_kv) <= row_tok)
        _state_update(m_ref, l_ref, acc_ref, ..., jnp.where(ok, s, NEG), vpad_ref[...].astype(BF16))
        o = acc_ref[...] / l_ref[...]
        o = o[:r1] - _lam_value(lam_ref, lam_init) * o[r1:]
        o_ref[0] = (_rmsnorm_rows(o, g_ref[...]) * (1.0 - lam_init)).astype(BF16)


def _diff_sample(page_table, lam_vec, subln_g, q_rows, k_new, v_new, cache_tiles, page, n_kv, gsz, ts, lam_init, n_pages):
    bs, r1, _ = q_rows.shape
    nsteps = page_table.shape[1] // n_pages
    first = lambda b, p, pt: jnp.minimum(p, nsteps - 1) * n_pages
    row_spec = lambda n: pl.BlockSpec((1, n, HEAD_DIM), lambda b, p, pt: (b, 0, 0))
    assert ts * n_kv <= LANES
    grid_spec = pltpu.PrefetchScalarGridSpec(
        num_scalar_prefetch=1,
        grid=(bs, nsteps + 1),
        in_specs=[
            pl.BlockSpec((4, DA), lambda b, p, pt: (0, 0)),
            pl.BlockSpec((1, HEAD_DIM), lambda b, p, pt: (0, 0)),
            row_spec(r1), row_spec(ts * n_kv), row_spec(ts * n_kv),
        ] + _tile_specs(page, cache_tiles.shape[1], n_pages, first),
        out_specs=row_spec(r1),
        scratch_shapes=[pltpu.VMEM((2 * r1, 1), F32), pltpu.VMEM((2 * r1, 1), F32), pltpu.VMEM((2 * r1, HEAD_DIM), F32),
                        pltpu.VMEM((2 * r1, n_pages * page * SUBLANES), F32),
                        pltpu.VMEM((LANES, HEAD_DIM), F32), pltpu.VMEM((LANES, HEAD_DIM), F32)],
    )
    return pl.pallas_call(
        functools.partial(_diff_sample_kernel, n_kv=n_kv, gsz=gsz, ts=ts, lam_init=lam_init, n_pages=n_pages),
        grid_spec=grid_spec,
        out_shape=jax.ShapeDtypeStruct((bs, r1, HEAD_DIM), BF16),
        compiler_params=_params("parallel", "arbitrary"),
    )(page_table, lam_vec, subln_g.reshape(1, HEAD_DIM), q_rows, k_new, v_new, *([cache_tiles] * n_pages))


def _fox_sample_kernel(pt_ref, q_ref, kn_ref, vn_ref, dq_ref, dkn_ref, tot_ref, *refs, n_kv, gsz, ts, n_pages):
    pg_refs, lf_refs = refs[:n_pages], refs[n_pages:2 * n_pages]
    o_ref, m_ref, l_ref, acc_ref, car_ref, bias_ref, kpad_ref, vpad_ref = refs[2 * n_pages:]
    p = pl.program_id(1)
    page = lf_refs[0].shape[2]
    rows = n_kv * ts * gsz
    n_h = n_kv * gsz
    qb = (q_ref[0] * (HEAD_DIM ** -0.5 * LOG2E)).astype(BF16)
    rr = _iota((rows, 1), 0)
    row_head = rr // (ts * gsz)
    row_tok = (rr % (ts * gsz)) // gsz
    shift = dq_ref[0][:, 0:1] * LOG2E

    @pl.when(p == 0)
    def _():
        _state_init(m_ref, l_ref, acc_ref)
        car_ref[...] = tot_ref[0]
        _pad_rows_scratch(kpad_ref, kn_ref[0])
        _pad_rows_scratch(vpad_ref, vn_ref[0])
        s = _dot_nt(qb, kpad_ref[...].astype(BF16)) + dkn_ref[0] * LOG2E
        c = _iota(s.shape, 1)
        ok = ((c % n_kv) == row_head) & ((c // n_kv) <= row_tok)
        _state_update(m_ref, l_ref, acc_ref, ..., jnp.where(ok, s, NEG), vpad_ref[...].astype(BF16), shift=shift)

    @pl.when(p > 0)
    def _():
        ks, vs, off = _flat_kv(pg_refs)

        @pl.when(p == 1)
        def _():
            bias_ref[...] = _own_head_bias(row_head + off, bias_ref.shape[1])

        lower = (_iota((page, page), 0) > _iota((page, page), 1)).astype(F32)
        dks = [None] * n_pages
        for r in reversed(range(n_pages)):
            lft = lf_refs[r][0]
            dks[r] = _dot_exact(lft, lower) + car_ref[...]
            car_ref[...] = car_ref[...] + jnp.sum(lft, axis=1, keepdims=True)
        dk = jnp.concatenate(dks, axis=1)
        hi = dk.astype(BF16)
        rest = dk - hi.astype(F32)
        mid = rest.astype(BF16)
        terms = (hi, mid, (rest - mid.astype(F32)).astype(BF16))
        hh = _iota((rows, n_h), 0)
        pick = ((hh // (ts * gsz)) * gsz + hh % gsz == _iota((rows, n_h), 1)).astype(BF16)
        spread = ((_iota((page, page * SUBLANES), 1) >> 3) == _iota((page, page * SUBLANES), 0)).astype(BF16)
        on_rows = [_dot(pick, x).astype(BF16) for x in terms]
        pieces = jnp.concatenate([x[:, r * page:(r + 1) * page] for r in range(n_pages) for x in on_rows], axis=0)
        flat = _dot(pieces, spread)
        nt = len(terms)
        dkf = jnp.concatenate(
            [sum(flat[(r * nt + q) * rows:(r * nt + q + 1) * rows] for q in range(nt)) for r in range(n_pages)],
            axis=1) * LOG2E
        _state_update(m_ref, l_ref, acc_ref, ..., _dot_nt(qb, ks) + dkf + bias_ref[...], vs, shift=shift)

    @pl.when(p == pl.num_programs(1) - 1)
    def _():
        o_ref[0] = (acc_ref[...] / l_ref[...]).astype(BF16)


def _fox_sample(page_table, q_rows, k_new, v_new, dq_b, dkn_b, tot, cache_tiles, logf_t, page, n_kv, gsz, ts, n_pages):
    bs, rows, _ = q_rows.shape
    n_h = n_kv * gsz
    nsteps = page_table.shape[1] // n_pages
    first = lambda b, p, pt: (nsteps - jnp.maximum(p, 1)) * n_pages
    lf_specs = [pl.BlockSpec((1, n_h, page), lambda b, p, pt, r=r: (pt[b, first(b, p, pt) + r], 0, 0))
                for r in range(n_pages)]
    row_spec = lambda n: pl.BlockSpec((1, n, HEAD_DIM), lambda b, p, pt: (b, 0, 0))
    assert ts * n_kv <= LANES
    grid_spec = pltpu.PrefetchScalarGridSpec(
        num_scalar_prefetch=1,
        grid=(bs, nsteps + 1),
        in_specs=[
            row_spec(rows), row_spec(ts * n_kv), row_spec(ts * n_kv), row_spec(rows), row_spec(rows),
            pl.BlockSpec((1, n_h, 1), lambda b, p, pt: (b, 0, 0)),
        ] + _tile_specs(page, cache_tiles.shape[1], n_pages, first) + lf_specs,
        out_specs=row_spec(rows),
        scratch_shapes=[pltpu.VMEM((rows, 1), F32), pltpu.VMEM((rows, 1), F32), pltpu.VMEM((rows, HEAD_DIM), F32),
                        pltpu.VMEM((n_h, 1), F32), pltpu.VMEM((rows, n_pages * page * SUBLANES), F32),
                        pltpu.VMEM((LANES, HEAD_DIM), F32), pltpu.VMEM((LANES, HEAD_DIM), F32)],
    )
    return pl.pallas_call(
        functools.partial(_fox_sample_kernel, n_kv=n_kv, gsz=gsz, ts=ts, n_pages=n_pages),
        grid_spec=grid_spec,
        out_shape=jax.ShapeDtypeStruct((bs, rows, HEAD_DIM), BF16),
        compiler_params=_params("parallel", "arbitrary"),
    )(page_table, q_rows, k_new, v_new, dq_b, dkn_b, tot, *([cache_tiles] * n_pages), *([logf_t] * n_pages))


def _compress_mlp(acc_a, acc_b_next, w2):
    return _dot(_gelu_tanh(acc_a + acc_b_next).astype(BF16), w2).astype(BF16)


def _compress_prompt_kernel(z_ref, pe_ref, w1_ref, w2_ref, o_ref, *, nch):
    acc_a = jnp.zeros((nch, CMP_HID), F32)
    acc_b = jnp.zeros((nch, CMP_HID), F32)
    for l in range(CMP_STRIDE):
        x = z_ref[0, pl.ds(l, nch, stride=CMP_STRIDE), :]
        xa = (x + pe_ref[0, l:l + 1, :]).astype(BF16)
        xb = (x + pe_ref[0, CMP_STRIDE + l:CMP_STRIDE + l + 1, :]).astype(BF16)
        acc_a = acc_a + _dot(xa, w1_ref[0, l])
        acc_b = acc_b + _dot(xb, w1_ref[0, CMP_STRIDE + l])
    o_ref[0, 0, 0] = _compress_mlp(acc_a, pltpu.roll(acc_b, nch - 1, axis=0), w2_ref[0])


def _compress_prompt(z3, pe, w1, w2, n_kv, col_blk0):
    b, t, _ = z3.shape
    nch = t // CMP_STRIDE
    return pl.pallas_call(
        functools.partial(_compress_prompt_kernel, nch=nch),
        grid=(b, 2 * n_kv),
        in_specs=[
            pl.BlockSpec((1, t, HEAD_DIM), lambda i, c: (i, 0, col_blk0 + c)),
            pl.BlockSpec((1, CMP_LEN, HEAD_DIM), lambda i, c: (c // n_kv, 0, 0)),
            pl.BlockSpec((1, CMP_LEN, HEAD_DIM, CMP_HID), lambda i, c: (c // n_kv, 0, 0, 0)),
            pl.BlockSpec((1, CMP_HID, HEAD_DIM), lambda i, c: (c // n_kv, 0, 0)),
        ],
        out_specs=pl.BlockSpec((1, 1, 1, nch, HEAD_DIM), lambda i, c: (i, c // n_kv, c % n_kv, 0, 0)),
        out_shape=jax.ShapeDtypeStruct((b, 2, n_kv, nch, HEAD_DIM), BF16),
        compiler_params=_params("parallel", "arbitrary"),
    )(z3, pe, w1, w2)


def _nsa_prompt_kernel(q_ref, cos_ref, sin_ref, kc_ref, vc_ref, ks_ref, vs_ref, kw_ref, vw_ref, gl_ref, bg_ref,
                       o_ref, ksb_ref, vsb_ref, kwb_ref, vwb_ref, s_ref, *, tq, gsz, n_kv, nblk, ns, tk):
    kh = pl.program_id(1)
    qi = pl.program_id(2)
    _cast_kv_once(qi, ((ks_ref, ksb_ref), (vs_ref, vsb_ref), (kw_ref, kwb_ref), (vw_ref, vwb_ref)))
    t0 = qi * tq
    qscale = HEAD_DIM ** -0.5 * LOG2E
    cos, sin = cos_ref[...], sin_ref[...]
    qs = [q_ref[0, :, g * HEAD_DIM:(g + 1) * HEAD_DIM] * qscale for g in range(gsz)]
    q = jnp.concatenate(qs, axis=0).astype(BF16)
    qr = jnp.concatenate([_rope_lanes(x, cos, sin, HEAD_DIM // 2) for x in qs], axis=0).astype(BF16)
    qpos = t0 + _iota((tq, 1), 0)

    def tile_g(x):
        return jnp.concatenate([x] * gsz, axis=0)

    nch = kc_ref.shape[3]
    n_i = _iota((tq, nch), 1)
    ok_c = ((n_i * CMP_STRIDE + CMP_LEN - 1 <= qpos) & (n_i < nblk)).astype(F32)
    p = _masked_softmax(_dot_nt(q, kc_ref[0, 0, 0]), tile_g(ok_c))
    o_cmp = _dot(p.astype(BF16), vc_ref[0, 0, 0])
    psum = p[0:tq]
    for g in range(1, gsz):
        psum = psum + p[g * tq:(g + 1) * tq]
    ov_t = _overlap(_iota((LANES, nch), 1) * CMP_STRIDE, _iota((LANES, nch), 0) * SLC_BLOCK)
    imp_t = _block_importance(_dot_nt_exact(ov_t, psum), t0 + _iota((1, tq), 1), ns, axis=0)
    sel = _topk_mask_t(imp_t, min(N_SELECT, ns), ns).T.astype(BF16)

    def slc_mask(k0):
        expand = (((k0 + _iota((LANES, tk), 1)) >> 6) == _iota((LANES, tk), 0)).astype(BF16)
        selk = _dot(sel, expand)
        return tile_g(jnp.where((selk > 0.5) & (k0 + _iota((tq, tk), 1) <= qpos), 0.0, NEG))

    n_blocks = (t0 + tq + tk - 1) // tk
    acc, l = _two_pass_attention(qr, ksb_ref, vsb_ref, n_blocks, tk, s_ref, add_fn=slc_mask)
    o_slc = acc / l

    wlen = WINDOW + tq
    w0 = pl.multiple_of(jnp.maximum(t0 - WINDOW, 0), tq)
    dist = qpos - (w0 + _iota((tq, wlen), 1))
    sw = _dot_nt(qr, kwb_ref[pl.ds(w0, wlen), :]) + tile_g(jnp.where((dist >= 0) & (dist < WINDOW), 0.0, NEG))
    ew = jnp.exp2(sw - jnp.max(sw, axis=-1, keepdims=True))
    o_win = _dot(ew.astype(BF16), vwb_ref[pl.ds(w0, wlen), :]) / jnp.sum(ew, axis=-1, keepdims=True)

    gates = jax.nn.sigmoid(gl_ref[0] + bg_ref[...])
    n_h = n_kv * gsz
    for g in range(gsz):
        rows = slice(g * tq, (g + 1) * tq)
        col = kh * gsz + g
        o = (_lane_pick(gates, col) * o_cmp[rows] + _lane_pick(gates, n_h + col) * o_slc[rows]
             + _lane_pick(gates, 2 * n_h + col) * o_win[rows])
        o_ref[0, :, g * HEAD_DIM:(g + 1) * HEAD_DIM] = o.astype(BF16)


def _nsa_prompt(z3, cos, sin, kvc, bg_pad, n_kv, gsz, ks_blk, vs_blk, kw_blk, vw_blk, gl_blk, tq, tk):
    b, t, _ = z3.shape
    nch = kvc.shape[3]
    qw = gsz * HEAD_DIM
    ns = -(-t // SLC_BLOCK)
    assert t >= WINDOW + tq and t % tk == 0 and tk % SLC_BLOCK == 0 and ns <= LANES
    kv_spec = lambda blk0: pl.BlockSpec((1, t, HEAD_DIM), lambda i, k, q: (i, 0, blk0 + k))
    return pl.pallas_call(
        functools.partial(_nsa_prompt_kernel, tq=tq, gsz=gsz, n_kv=n_kv, nblk=nch - 1, ns=ns, tk=tk),
        grid=(b, n_kv, t // tq),
        in_specs=[
            pl.BlockSpec((1, tq, qw), lambda i, k, q: (i, q, k)),
            pl.BlockSpec((tq, LANES), lambda i, k, q: (q, 0)),
            pl.BlockSpec((tq, LANES), lambda i, k, q: (q, 0)),
            pl.BlockSpec((1, 1, 1, nch, HEAD_DIM), lambda i, k, q: (i, 0, k, 0, 0)),
            pl.BlockSpec((1, 1, 1, nch, HEAD_DIM), lambda i, k, q: (i, 1, k, 0, 0)),
            kv_spec(ks_blk), kv_spec(vs_blk), kv_spec(kw_blk), kv_spec(vw_blk),
            pl.BlockSpec((1, tq, LANES), lambda i, k, q: (i, q, gl_blk)),
            pl.BlockSpec((1, LANES), lambda i, k, q: (0, 0)),
        ],
        out_specs=pl.BlockSpec((1, tq, qw), lambda i, k, q: (i, q, k)),
        out_shape=jax.ShapeDtypeStruct((b, t, n_kv * qw), BF16),
        scratch_shapes=[pltpu.VMEM((t, HEAD_DIM), BF16) for _ in range(4)] + [
            pltpu.VMEM((t // tk, gsz * tq, tk), F32)],
        compiler_params=_params("parallel", "parallel", "arbitrary"),
    )(z3, cos, sin, kvc, kvc, z3, z3, z3, z3, z3, bg_pad)


def _compress_sample_kernel(pt_ref, *refs, n_kv, n_pages, page):
    pg_refs = refs[:n_pages]
    pe_ref, w1_ref, w2_ref, o_ref, last_ref = refs[n_pages:]
    s = pl.program_id(1)
    cpp = page // CMP_STRIDE
    m = n_pages * cpp
    rpr = 2 * n_kv

    @pl.when(s == 0)
    def _():
        last_ref[...] = jnp.zeros(last_ref.shape, F32)

    for typ in range(2):
        acc_a = jnp.zeros((n_kv * m, CMP_HID), F32)
        acc_b = jnp.zeros((n_kv * m, CMP_HID), F32)
        for l in range(CMP_STRIDE):
            x = jnp.concatenate(
                [pg[pl.ds(l * rpr + typ * n_kv + h, cpp, stride=CMP_STRIDE * rpr), :]
                 for h in range(n_kv) for pg in pg_refs], axis=0)
            xa = (x + pe_ref[typ, l:l + 1, :]).astype(BF16)
            xb = (x + pe_ref[typ, CMP_STRIDE + l:CMP_STRIDE + l + 1, :]).astype(BF16)
            acc_a = acc_a + _dot(xa, w1_ref[typ, l])
            acc_b = acc_b + _dot(xb, w1_ref[typ, CMP_STRIDE + l])
        first = _iota((m, 1), 0) == 0
        for h in range(n_kv):
            a = acc_a[h * m:(h + 1) * m]
            idx = typ * n_kv + h
            prev = jnp.where(first, last_ref[idx, SUBLANES - 1:SUBLANES, :], pltpu.roll(a, 1, axis=0))
            last_ref[idx] = a[m - SUBLANES:m]
            o_ref[0, typ, h] = _compress_mlp(prev, acc_b[h * m:(h + 1) * m], w2_ref[typ])


def _compress_sample(page_table, cache_rows, pe, w1, w2, page, n_kv, n_pages):
    bs, npg = page_table.shape
    m = n_pages * (page // CMP_STRIDE)
    grid_spec = pltpu.PrefetchScalarGridSpec(
        num_scalar_prefetch=1,
        grid=(bs, npg // n_pages),
        in_specs=_page_specs(page, n_kv, n_pages, lambda b, s, pt: s * n_pages) + [
            pl.BlockSpec((2, CMP_LEN, HEAD_DIM), lambda b, s, pt: (0, 0, 0)),
            pl.BlockSpec((2, CMP_LEN, HEAD_DIM, CMP_HID), lambda b, s, pt: (0, 0, 0, 0)),
            pl.BlockSpec((2, CMP_HID, HEAD_DIM), lambda b, s, pt: (0, 0, 0)),
        ],
        out_specs=pl.BlockSpec((1, 2, n_kv, m, HEAD_DIM), lambda b, s, pt: (b, 0, 0, s, 0)),
        scratch_shapes=[pltpu.VMEM((2 * n_kv, SUBLANES, CMP_HID), F32)],
    )
    return pl.pallas_call(
        functools.partial(_compress_sample_kernel, n_kv=n_kv, n_pages=n_pages, page=page),
        grid_spec=grid_spec,
        out_shape=jax.ShapeDtypeStruct((bs, 2, n_kv, npg * (page // CMP_STRIDE), HEAD_DIM), BF16),
        compiler_params=_params("parallel", "arbitrary"),
    )(page_table, *([cache_rows] * n_pages), pe, w1, w2)


def _nsa_sample_a_kernel(q_ref, cos_ref, sin_ref, kc_ref, vc_ref, win_ref, kwn_ref,
                         ocmp_ref, owin_ref, sel_ref, qr_ref, kpad_ref, vpad_ref, *, n_kv, gsz, ts, p0, ns):
    qscale = HEAD_DIM ** -0.5 * LOG2E
    rows = ts * gsz
    nr = kc_ref.shape[3]
    nsp = sel_ref.shape[3]
    wrows = win_ref.shape[1] // (2 * n_kv)
    tok = _iota((rows, 1), 0) // gsz
    qpos = p0 + tok
    r_i = _iota((rows, nr), 1)
    ok_c = ((r_i >= 1) & ((r_i - 1) * CMP_STRIDE + CMP_LEN - 1 <= qpos)).astype(F32)
    ov = _overlap((_iota((nr, nsp), 0) - 1) * CMP_STRIDE, _iota((nr, nsp), 1) * SLC_BLOCK)
    group = ((_iota((rows, rows), 0) // gsz) == (_iota((rows, rows), 1) // gsz)).astype(F32)
    ok_w1 = _iota((rows, wrows), 1) > tok + (wrows - WINDOW)
    ok_w2 = _iota((rows, kpad_ref.shape[0]), 1) <= tok
    kvw = n_kv * HEAD_DIM
    for kh in range(n_kv):
        q = q_ref[0, kh]
        qr = _rope_lanes(q, cos_ref[...], sin_ref[...], HEAD_DIM // 2)
        qr_ref[0, kh] = qr
        qb, qrb = (q * qscale).astype(BF16), (qr * qscale).astype(BF16)
        p = _masked_softmax(_dot_nt(qb, kc_ref[0, 0, kh]), ok_c)
        ocmp_ref[0, kh] = _dot(p.astype(BF16), vc_ref[0, 0, kh])
        imp = _dot_exact(group, _dot_exact(p, ov))
        sel_ref[0, kh] = _topk_mask(_block_importance(imp, qpos, ns), min(N_SELECT, ns))
        sl = slice(kh * HEAD_DIM, (kh + 1) * HEAD_DIM)
        slv = slice(kvw + kh * HEAD_DIM, kvw + (kh + 1) * HEAD_DIM)
        _pad_rows_scratch(kpad_ref, kwn_ref[0, :, sl])
        _pad_rows_scratch(vpad_ref, kwn_ref[0, :, slv])
        k_win = win_ref[0, pl.ds(kh, wrows, stride=2 * n_kv), :].astype(BF16)
        v_win = win_ref[0, pl.ds(n_kv + kh, wrows, stride=2 * n_kv), :].astype(BF16)
        s1 = jnp.where(ok_w1, _dot_nt(qrb, k_win), NEG)
        s2 = jnp.where(ok_w2, _dot_nt(qrb, kpad_ref[...].astype(BF16)), NEG)
        m = jnp.maximum(jnp.max(s1, axis=-1, keepdims=True), jnp.max(s2, axis=-1, keepdims=True))
        e1, e2 = jnp.exp2(s1 - m), jnp.exp2(s2 - m)
        l = jnp.sum(e1, axis=-1, keepdims=True) + jnp.sum(e2, axis=-1, keepdims=True)
        o = _dot(e1.astype(BF16), v_win) + _dot(e2.astype(BF16), vpad_ref[...].astype(BF16))
        owin_ref[0, kh] = o / l


def _nsa_sample_a(q_rows, cos, sin, kvc, win_rows, kvw_new, n_kv, gsz, ts, p0, ns):
    bs = q_rows.shape[0]
    rows = ts * gsz
    nr = kvc.shape[3]
    nsp = _round_up(ns, LANES)
    wr, w = win_rows.shape[1], kvw_new.shape[2]
    assert wr == WINDOW * 2 * n_kv
    o_sds = jax.ShapeDtypeStruct((bs, n_kv, rows, HEAD_DIM), F32)
    o_spec = pl.BlockSpec((1, n_kv, rows, HEAD_DIM), lambda b: (b, 0, 0, 0))
    return pl.pallas_call(
        functools.partial(_nsa_sample_a_kernel, n_kv=n_kv, gsz=gsz, ts=ts, p0=p0, ns=ns),
        grid=(bs,),
        in_specs=[
            o_spec,
            pl.BlockSpec((rows, LANES), lambda b: (0, 0)),
            pl.BlockSpec((rows, LANES), lambda b: (0, 0)),
            pl.BlockSpec((1, 1, n_kv, nr, HEAD_DIM), lambda b: (b, 0, 0, 0, 0)),
            pl.BlockSpec((1, 1, n_kv, nr, HEAD_DIM), lambda b: (b, 1, 0, 0, 0)),
            pl.BlockSpec((1, wr, HEAD_DIM), lambda b: (b, 0, 0)),
            pl.BlockSpec((1, ts, w), lambda b: (b, 0, 0)),
        ],
        out_specs=[o_spec, o_spec, pl.BlockSpec((1, n_kv, rows, nsp), lambda b: (b, 0, 0, 0)), o_spec],
        out_shape=[o_sds, o_sds, jax.ShapeDtypeStruct((bs, n_kv, rows, nsp), F32), o_sds],
        scratch_shapes=[pltpu.VMEM((LANES, HEAD_DIM), F32), pltpu.VMEM((LANES, HEAD_DIM), F32)],
        compiler_params=_params("parallel"),
    )(q_rows, cos, sin, kvc, kvc, win_rows, kvw_new)


def _nsa_sample_b_kernel(pt_ref, q_ref, sel_ref, kn_ref, ocmp_ref, owin_ref, gl_ref, bg_ref, *refs,
                         n_kv, gsz, ts, n_pages):
    pg_refs = refs[:n_pages]
    o_ref, m_ref, l_ref, acc_ref, bias_ref, npad_ref = refs[n_pages:]
    p = pl.program_id(1)
    last = pl.num_programs(1) - 1
    rows = n_kv * ts * gsz
    page = pg_refs[0].shape[0]
    nsp = sel_ref.shape[2]
    bps = n_pages * (page // SLC_BLOCK)
    blk_cols = SLC_BLOCK * SUBLANES
    qb = (q_ref[0] * (HEAD_DIM ** -0.5 * LOG2E)).astype(BF16)
    rr = _iota((rows, 1), 0)
    row_head = rr // (ts * gsz)
    row_tok = (rr % (ts * gsz)) // gsz

    @pl.when(p == 0)
    def _():
        _state_init(m_ref, l_ref, acc_ref)

    @pl.when(p < last)
    def _():
        ks, vs, off = _flat_kv(pg_refs)

        @pl.when(p == 0)
        def _():
            bias_ref[...] = _own_head_bias(row_head + off, bias_ref.shape[1])

        shift = (_iota((nsp, LANES), 0) == p * bps + _iota((nsp, LANES), 1)).astype(BF16)
        flags = jnp.where(_dot(sel_ref[0].astype(BF16), shift) > 0.5, 0.0, NEG)
        sel_bias = jnp.concatenate([jnp.broadcast_to(flags[:, i:i + 1], (rows, blk_cols)) for i in range(bps)], axis=1)
        _state_update(m_ref, l_ref, acc_ref, ..., _dot_nt(qb, ks) + bias_ref[...] + sel_bias, vs)

    @pl.when(p == last)
    def _():
        _pad_rows_scratch(npad_ref, kn_ref[0])
        flat = npad_ref[...]
        s = _dot_nt(qb, pltpu.roll(flat, n_kv, axis=0).astype(BF16))
        c = _iota(s.shape, 1)
        ok = (((c % (2 * n_kv)) == n_kv + row_head) & ((c // (2 * n_kv)) <= row_tok)
              & (_lane_pick(sel_ref[0], last * bps) > 0.5))
        _state_update(m_ref, l_ref, acc_ref, ..., jnp.where(ok, s, NEG), flat.astype(BF16))
        gates = jax.nn.sigmoid(gl_ref[0] + bg_ref[...])
        o_ref[0] = (gates[:, 0:1] * ocmp_ref[0] + gates[:, 1:2] * (acc_ref[...] / l_ref[...])
                    + gates[:, 2:3] * owin_ref[0])


def _nsa_sample_b(page_table, q_rows, sel, kvs_new, o_cmp, o_win, gl_rows, bg_rows, cache_tiles,
                  page, n_kv, gsz, ts, n_pages):
    bs, rows, _ = q_rows.shape
    nsp = sel.shape[2]
    nsteps = page_table.shape[1] // n_pages
    first = lambda b, p, pt: jnp.minimum(p, nsteps - 1) * n_pages
    row_spec = lambda n, width: pl.BlockSpec((1, n, width), lambda b, p, pt: (b, 0, 0))
    assert ts * 2 * n_kv <= LANES and n_pages * (page // SLC_BLOCK) <= LANES
    grid_spec = pltpu.PrefetchScalarGridSpec(
        num_scalar_prefetch=1,
        grid=(bs, nsteps + 1),
        in_specs=[
            row_spec(rows, HEAD_DIM), row_spec(rows, nsp), row_spec(ts * 2 * n_kv, HEAD_DIM),
            row_spec(rows, HEAD_DIM), row_spec(rows, HEAD_DIM), row_spec(rows, LANES),
            pl.BlockSpec((rows, LANES), lambda b, p, pt: (0, 0)),
        ] + _tile_specs(page, cache_tiles.shape[1], n_pages, first),
        out_specs=row_spec(rows, HEAD_DIM),
        scratch_shapes=[pltpu.VMEM((rows, 1), F32), pltpu.VMEM((rows, 1), F32), pltpu.VMEM((rows, HEAD_DIM), F32),
                        pltpu.VMEM((rows, n_pages * page * SUBLANES), F32), pltpu.VMEM((LANES, HEAD_DIM), F32)],
    )
    return pl.pallas_call(
        functools.partial(_nsa_sample_b_kernel, n_kv=n_kv, gsz=gsz, ts=ts, n_pages=n_pages),
        grid_spec=grid_spec,
        out_shape=jax.ShapeDtypeStruct((bs, rows, HEAD_DIM), F32),
        compiler_params=_params("parallel", "arbitrary"),
    )(page_table, q_rows, sel, kvs_new, o_cmp, o_win, gl_rows, bg_rows, *([cache_tiles] * n_pages))


def _rope_tables(pos, half):
    inv = ROPE_THETA ** (-jnp.arange(half, dtype=F32) / half)
    ang = pos.astype(F32)[:, None] * inv
    cos, sin = jnp.cos(ang), jnp.sin(ang)
    reps = LANES // (2 * half)
    return (jnp.tile(jnp.concatenate([cos, cos], axis=-1), (1, reps)),
            jnp.tile(jnp.concatenate([-sin, sin], axis=-1), (1, reps)))


def _pad_cols(w, n):
    return jnp.pad(w, ((0, 0), (0, n - w.shape[1])))


def _tile_cols(w, tn):
    k, n = w.shape
    return w.reshape(k, n // tn, tn).transpose(1, 0, 2).astype(BF16)


def _head_rows(x, n_kv, gsz):
    b, t, _ = x.shape
    return x.reshape(b, t, n_kv, gsz, -1).transpose(0, 2, 1, 3, 4).reshape(b, n_kv, t * gsz, -1)


def _token_cols(x, ts, gsz):
    b, n_kv, _, f = x.shape
    return x.reshape(b, n_kv, ts, gsz, f).transpose(0, 2, 1, 3, 4).reshape(b * ts, n_kv * gsz * f)


def kernel(x_prompt, x_sample, cache_a_kv, cache_b_kv, cache_b_logf, cache_cmp_kv, cache_slc_kv, state_win_kv, page_table, norm_mix_e, w_in_e, b_forget, lam_q1, lam_k1, lam_q2, lam_k2, subln_g, w_out_e, norm_mix_o, w_in_o, b_gate, cmp_pe_k, cmp_w1_k, cmp_w2_k, cmp_pe_v, cmp_w1_v, cmp_w2_v, w_out_o, norm_ffn, w_gate, w_up, w_down, norm_final):
    bp, t, d = x_prompt.shape
    bs, ts, _ = x_sample.shape
    n_even, n_odd = w_in_e.shape[0], w_in_o.shape[0]
    depth = n_even + n_odd
    page = cache_a_kv.shape[2]
    npg = page_table.shape[1]
    p0 = npg * page
    h_all = d // HEAD_DIM
    h_b = cache_b_logf.shape[-1]
    h_a = h_all - h_b
    kv_a, kv_b, kv_c = cache_a_kv.shape[4], cache_b_kv.shape[4], cache_cmp_kv.shape[4]
    g_a, g_b, g_c = h_a // kv_a, h_b // kv_b, h_all // kv_c
    mp, ms = bp * t, bs * ts
    hd = HEAD_DIM

    e_qa, e_ka = 0, h_a * hd
    e_va = e_ka + kv_a * hd
    e_qf = e_va + kv_a * hd
    e_kf = e_qf + h_b * hd
    e_vf = e_kf + kv_b * hd
    e_fg = e_vf + kv_b * hd
    o_kvc = h_all * hd
    o_kvs = o_kvc + 2 * kv_c * hd
    o_kvw = o_kvs + 2 * kv_c * hd
    o_gl = o_kvw + 2 * kv_c * hd
    assert 3 * h_all <= LANES and h_b <= LANES

    tn_e = _pick_tile(e_va, (512, 256, 128))
    tn_o = _pick_tile(kv_c * hd, (512, 256, 128))
    ne_pad = _round_up(e_fg + h_b, tn_e)
    no_pad = _round_up(o_gl + 3 * h_all, tn_o)
    tm_p = _pick_tile(mp, (1024, 512, 256, 128))
    tm_f = _pick_tile(mp, (512, 256, 128))
    tn_out = _pick_tile(d, (512, 256, 128))
    tf = _pick_tile(w_gate.shape[2], (256, 128))
    tq_e = _pick_tile(t, (256, 128))
    tk_e = _pick_tile(t, (512, 256))
    tq_o = 128
    tk_o = _pick_tile(t, (512, 256, 128))
    n_pages = _pick_tile(npg, (8, 4, 2))

    xp = x_prompt.reshape(mp, d)
    xs = x_sample.reshape(ms, d)
    pos_p = jnp.tile(jnp.arange(t, dtype=jnp.int32), bp)
    pos_s = p0 + jnp.tile(jnp.arange(ts, dtype=jnp.int32), bs)
    tab_a_p, tab_a_s = _rope_tables(pos_p, DA // 2), _rope_tables(pos_s, DA // 2)
    tab_c_p, tab_c_s = _rope_tables(pos_p, hd // 2), _rope_tables(pos_s, hd // 2)
    tab_c_q = _rope_tables(jnp.arange(t, dtype=jnp.int32), hd // 2)
    tab_c_rows = _rope_tables(p0 + jnp.repeat(jnp.arange(ts, dtype=jnp.int32), g_c), hd // 2)

    outs = {name: ([], []) for name in ("a", "b", "f", "c", "s", "w")}

    for layer in range(depth):
        i = layer // 2
        if layer % 2 == 0:
            lam_init = 0.8 - 0.6 * float(np.exp(-0.3 * layer))
            w_in = _tile_cols(_pad_cols(w_in_e[i], ne_pad), tn_e)
            w_out = _tile_cols(w_out_e[i], tn_out)
            rope_e = ((0, e_va // tn_e),)
            lam_vec = jnp.stack([lam_q1[i], lam_k1[i], lam_q2[i], lam_k2[i]])
            bf_pad = jnp.pad(b_forget[i], (0, LANES - h_b)).reshape(1, LANES)

            z = _proj(xp, norm_mix_e[i], w_in, *tab_a_p, rope_e, DA // 2, tm_p)
            z3 = z.reshape(bp, t, ne_pad)
            outs["a"][0].append(z3[:, :, e_ka:e_qf].reshape(bp, t, 2, kv_a, hd))
            outs["b"][0].append(z3[:, :, e_kf:e_fg].reshape(bp, t, 2, kv_b, hd))
            lf, dkc, dkr = _logf_prompt(z3, bf_pad, e_fg // LANES)
            outs["f"][0].append(lf[:, :, :h_b])
            o_a = _diff_prompt(z3, lam_vec, subln_g[i], kv_a, g_a, e_qa // (g_a * hd), e_ka // hd, e_va // hd,
                               lam_init, tq_e, tk_e)
            o_b = _fox_prompt(z3, dkr[:, :h_b].reshape(bp, h_b, 1, t), dkc, kv_b, g_b,
                              e_qf // (g_b * hd), e_kf // hd, e_vf // hd, tq_e, tk_e)
            xp = _mm_res([o_a.reshape(mp, h_a * hd), o_b.reshape(mp, h_b * hd)], w_out, xp, tm_p)

            z = _proj(xs, norm_mix_e[i], w_in, *tab_a_s, rope_e, DA // 2, ms)
            z3 = z.reshape(bs, ts, ne_pad)
            outs["a"][1].append(z3[:, :, e_ka:e_qf].reshape(bs, ts, 2, kv_a, hd))
            outs["b"][1].append(z3[:, :, e_kf:e_fg].reshape(bs, ts, 2, kv_b, hd))
            lf, dq, tot = _logf_sample(z[:, e_fg:e_fg + LANES], bf_pad, ts)
            outs["f"][1].append(lf[:, :h_b].reshape(bs, ts, h_b))
            o_a = _diff_sample(page_table, lam_vec, subln_g[i],
                               _head_rows(z3[:, :, e_qa:e_ka], kv_a, g_a).reshape(bs, -1, hd),
                               z3[:, :, e_ka:e_va].reshape(bs, ts * kv_a, hd),
                               z3[:, :, e_va:e_qf].reshape(bs, ts * kv_a, hd),
                               _tile_view(cache_a_kv[i]), page, kv_a, g_a, ts, lam_init, n_pages)
            rows_b = kv_b * ts * g_b
            dq3 = dq[:, :h_b].reshape(bs, ts, h_b)
            dq_b = jnp.broadcast_to(_head_rows(dq3, kv_b, g_b).reshape(bs, rows_b, 1), (bs, rows_b, LANES))
            dkn = jnp.swapaxes(dq3, 1, 2).reshape(bs, kv_b, 1, g_b, ts, 1)
            dkn = jnp.broadcast_to(dkn, (bs, kv_b, ts, g_b, ts, kv_b)).reshape(bs, rows_b, ts * kv_b)
            dkn_b = jnp.pad(dkn, ((0, 0), (0, 0), (0, LANES - ts * kv_b)))
            tot3 = tot[:, :h_b].reshape(bs, ts, h_b)[:, 0, :, None]
            o_b = _fox_sample(page_table, _head_rows(z3[:, :, e_qf:e_kf], kv_b, g_b).reshape(bs, -1, hd),
                              z3[:, :, e_kf:e_vf].reshape(bs, ts * kv_b, hd),
                              z3[:, :, e_vf:e_fg].reshape(bs, ts * kv_b, hd), dq_b, dkn_b, tot3,
                              _tile_view(cache_b_kv[i]), jnp.swapaxes(cache_b_logf[i], 1, 2),
                              page, kv_b, g_b, ts, n_pages)
            xs = _mm_res([_token_cols(o_a.reshape(bs, kv_a, ts * g_a, hd), ts, g_a),
                          _token_cols(o_b.reshape(bs, kv_b, ts * g_b, hd), ts, g_b)], w_out, xs, ms)
        else:
            w_in = _tile_cols(_pad_cols(w_in_o[i], no_pad), tn_o)
            w_out = _tile_cols(w_out_o[i], tn_out)
            rope_o = ((o_kvs // tn_o, (o_kvs + kv_c * hd) // tn_o), (o_kvw // tn_o, (o_kvw + kv_c * hd) // tn_o))
            bg_pad = jnp.pad(b_gate[i], (0, LANES - 3 * h_all)).reshape(1, LANES)
            pe = jnp.stack([cmp_pe_k[i], cmp_pe_v[i]])
            w1 = jnp.stack([cmp_w1_k[i], cmp_w1_v[i]]).astype(BF16)
            w2 = jnp.stack([cmp_w2_k[i], cmp_w2_v[i]]).astype(BF16)

            z = _proj(xp, norm_mix_o[i], w_in, *tab_c_p, rope_o, hd // 2, tm_p)
            z3 = z.reshape(bp, t, no_pad)
            outs["c"][0].append(z3[:, :, o_kvc:o_kvs].reshape(bp, t, 2, kv_c, hd))
            outs["s"][0].append(z3[:, :, o_kvs:o_kvw].reshape(bp, t, 2, kv_c, hd))
            keep = min(WINDOW, t)
            outs["w"][0].append(z3[:, t - keep:, o_kvw:o_gl].reshape(bp, keep, 2, kv_c, hd))
            kvc = _compress_prompt(z3, pe, w1, w2, kv_c, o_kvc // hd)
            o = _nsa_prompt(z3, *tab_c_q, kvc, bg_pad, kv_c, g_c, o_kvs // hd, o_kvs // hd + kv_c,
                            o_kvw // hd, o_kvw // hd + kv_c, o_gl // LANES, tq_o, tk_o)
            xp = _mm_res([o.reshape(mp, d)], w_out, xp, tm_p)

            z = _proj(xs, norm_mix_o[i], w_in, *tab_c_s, rope_o, hd // 2, ms)
            z3 = z.reshape(bs, ts, no_pad)
            outs["c"][1].append(z3[:, :, o_kvc:o_kvs].reshape(bs, ts, 2, kv_c, hd))
            outs["s"][1].append(z3[:, :, o_kvs:o_kvw].reshape(bs, ts, 2, kv_c, hd))
            win = state_win_kv[i]
            kvw_new = z3[:, :, o_kvw:o_gl]
            win_all = jnp.concatenate([win, kvw_new.reshape(bs, ts, 2, kv_c, hd)], axis=1)
            outs["w"][1].append(win_all[:, -min(WINDOW, win.shape[1] + ts):])
            kvc = _compress_sample(page_table, cache_cmp_kv[i].reshape(-1, hd), pe, w1, w2, page, kv_c, min(16, npg))
            ns = -(-(p0 + ts) // SLC_BLOCK)
            o_cmp, o_win, sel, qr = _nsa_sample_a(_head_rows(z3[:, :, :o_kvc], kv_c, g_c), *tab_c_rows, kvc,
                                                  win.reshape(bs, -1, hd), kvw_new, kv_c, g_c, ts, p0, ns)
            rows = kv_c * ts * g_c
            gl = z3[:, :, o_gl:o_gl + 3 * h_all].reshape(bs, ts, 3, kv_c, g_c).transpose(0, 3, 1, 4, 2)
            gl_rows = jnp.pad(gl.reshape(bs, rows, 3), ((0, 0), (0, 0), (0, LANES - 3)))
            bg = jnp.broadcast_to(b_gate[i].reshape(1, 3, kv_c, g_c), (ts, 3, kv_c, g_c)).transpose(2, 0, 3, 1)
            bg_rows = jnp.pad(bg.reshape(rows, 3), ((0, 0), (0, LANES - 3)))
            o = _nsa_sample_b(page_table, qr.reshape(bs, rows, hd), sel.reshape(bs, rows, -1),
                              z3[:, :, o_kvs:o_kvw].reshape(bs, ts * 2 * kv_c, hd),
                              o_cmp.reshape(bs, rows, hd), o_win.reshape(bs, rows, hd), gl_rows, bg_rows,
                              _tile_view(cache_slc_kv[i]), page, kv_c, g_c, ts, n_pages)
            xs = _mm_res([_token_cols(o.reshape(bs, kv_c, ts * g_c, hd), ts, g_c).astype(BF16)], w_out, xs, ms)

        g_final = norm_final if layer == depth - 1 else None
        wg, wu, wd = _tile_cols(w_gate[layer], tf), _tile_cols(w_up[layer], tf), w_down[layer].astype(BF16)
        xp = _ffn(xp, norm_ffn[layer], wg, wu, wd, g_final, tm_f)
        xs = _ffn(xs, norm_ffn[layer], wg, wu, wd, g_final, ms)

    def stack(name, which):
        return jnp.stack(outs[name][which])

    return (xp.reshape(bp, t, d), xs.reshape(bs, ts, d),
            stack("a", 0), stack("a", 1), stack("b", 0), stack("b", 1), stack("f", 0), stack("f", 1),
            stack("c", 0), stack("c", 1), stack("s", 0), stack("s", 1), stack("w", 0), stack("w", 1))
```

```python
import functools

import numpy as np
import jax
import jax.numpy as jnp
from jax import lax
from jax.experimental import pallas as pl
from jax.experimental.pallas import tpu as pltpu

F32 = jnp.float32
BF16 = jnp.bfloat16
HIGHEST = lax.Precision.HIGHEST

LANES = 128
SUBLANES = 8
VMEM_LIMIT = 56 * 1024 * 1024

HEAD_DIM = 128
DA = HEAD_DIM // 2
CMP_LEN = 32
CMP_STRIDE = 16
CMP_HID = 2 * HEAD_DIM
SLC_BLOCK = 64
N_SELECT = 16
WINDOW = 512
ROPE_THETA = 10000.0
NORM_EPS = 1e-6
LOG2E = 1.4426950408889634
NEG = -1e30
BIG = 1e30
LOWEST = -3e38


def _params(*sem):
    return pltpu.CompilerParams(dimension_semantics=sem, vmem_limit_bytes=VMEM_LIMIT)


def _dot(a, b):
    return jnp.dot(a, b, preferred_element_type=F32)


def _dot_nt(a, b):
    return lax.dot_general(a, b, (((1,), (1,)), ((), ())), preferred_element_type=F32)


def _dot_exact(a, b):
    return jnp.dot(a, b, precision=HIGHEST, preferred_element_type=F32)


def _dot_nt_exact(a, b):
    return lax.dot_general(a, b, (((1,), (1,)), ((), ())), precision=HIGHEST, preferred_element_type=F32)


def _iota(shape, axis):
    return lax.broadcasted_iota(jnp.int32, shape, axis)


def _round_up(n, m):
    return -(-n // m) * m


def _pick_tile(n, prefs):
    for p in prefs:
        if n % p == 0:
            return p
    return n


def _rope_lanes(a, cos, sin, half):
    if 2 * half == LANES:
        partner = pltpu.roll(a, half, axis=1)
    else:
        first = (_iota(a.shape, 1) & (2 * half - 1)) < half
        partner = jnp.where(first, pltpu.roll(a, LANES - half, axis=1), pltpu.roll(a, half, axis=1))
    return a * cos + partner * sin


def _rmsnorm_rows(x, g):
    ms = jnp.mean(x * x, axis=-1, keepdims=True)
    return x * lax.rsqrt(ms + NORM_EPS) * g


def _proj_kernel(x_ref, g_ref, w_ref, cos_ref, sin_ref, o_ref, xn_ref, *, rope_ranges, half):
    j = pl.program_id(1)

    @pl.when(j == 0)
    def _():
        xn_ref[...] = _rmsnorm_rows(x_ref[...], g_ref[...]).astype(BF16)

    o_ref[...] = _dot(xn_ref[...], w_ref[...])
    is_rope = functools.reduce(jnp.logical_or, [(j >= a) & (j < b) for a, b in rope_ranges])

    @pl.when(is_rope)
    def _():
        cos, sin = cos_ref[...], sin_ref[...]
        for c in range(o_ref.shape[1] // LANES):
            sl = slice(c * LANES, (c + 1) * LANES)
            o_ref[:, sl] = _rope_lanes(o_ref[:, sl], cos, sin, half)


def _proj(x, g, w, cos, sin, rope_ranges, half, tm, tn):
    m, d = x.shape
    n = w.shape[1]
    return pl.pallas_call(
        functools.partial(_proj_kernel, rope_ranges=rope_ranges, half=half),
        grid=(m // tm, n // tn),
        in_specs=[
            pl.BlockSpec((tm, d), lambda i, j: (i, 0), pipeline_mode=pl.Buffered(1)),
            pl.BlockSpec((1, d), lambda i, j: (0, 0)),
            pl.BlockSpec((d, tn), lambda i, j: (0, j)),
            pl.BlockSpec((tm, LANES), lambda i, j: (i, 0)),
            pl.BlockSpec((tm, LANES), lambda i, j: (i, 0)),
        ],
        out_specs=pl.BlockSpec((tm, tn), lambda i, j: (i, j)),
        out_shape=jax.ShapeDtypeStruct((m, n), F32),
        scratch_shapes=[pltpu.VMEM((tm, d), BF16)],
        compiler_params=_params("parallel", "arbitrary"),
    )(x, g.reshape(1, d), w, cos, sin)


def _mm_res_kernel(*refs, n_lhs):
    a_refs, (w_ref, r_ref, o_ref) = refs[:n_lhs], refs[n_lhs:]
    acc = r_ref[...]
    off = 0
    for a_ref in a_refs:
        kk = a_ref.shape[1]
        acc = acc + _dot(a_ref[...], w_ref[off:off + kk, :])
        off += kk
    o_ref[...] = acc


def _mm_res(lhs, w, res, tm, tn):
    m, n = res.shape
    k = w.shape[0]
    return pl.pallas_call(
        functools.partial(_mm_res_kernel, n_lhs=len(lhs)),
        grid=(m // tm, n // tn),
        in_specs=[pl.BlockSpec((tm, a.shape[1]), lambda i, j: (i, 0)) for a in lhs] + [
            pl.BlockSpec((k, tn), lambda i, j: (0, j)),
            pl.BlockSpec((tm, tn), lambda i, j: (i, j)),
        ],
        out_specs=pl.BlockSpec((tm, tn), lambda i, j: (i, j)),
        out_shape=jax.ShapeDtypeStruct((m, n), F32),
        compiler_params=_params("parallel", "arbitrary"),
    )(*lhs, w, res)


def _ffn_kernel(x_ref, g_ref, wg_ref, wu_ref, wd_ref, gf_ref, o_ref, xn_ref, h_ref, *, final_norm):
    f = pl.program_id(1)
    n_tiles = pl.num_programs(1) - 1

    def hidden():
        xn = xn_ref[...]
        gate = _dot(xn, wg_ref[...])
        up = _dot(xn, wu_ref[...])
        return (gate * jax.nn.sigmoid(gate) * up).astype(BF16)

    @pl.when(f == 0)
    def _():
        x = x_ref[...]
        xn_ref[...] = _rmsnorm_rows(x, g_ref[...]).astype(BF16)
        o_ref[...] = x
        h_ref[0] = hidden()

    @pl.when((f > 0) & (f < n_tiles))
    def _():
        h_ref[f % 2] = hidden()
        o_ref[...] += _dot(h_ref[(f - 1) % 2], wd_ref[...])

    @pl.when(f == n_tiles)
    def _():
        out = o_ref[...] + _dot(h_ref[(f - 1) % 2], wd_ref[...])
        o_ref[...] = _rmsnorm_rows(out, gf_ref[...]) if final_norm else out


def _ffn(x, g, wg, wu, wd, g_final, tm, tf):
    m, d = x.shape
    n_tiles = wg.shape[1] // tf
    final_norm = g_final is not None
    gf = (g_final if final_norm else g).reshape(1, d)
    return pl.pallas_call(
        functools.partial(_ffn_kernel, final_norm=final_norm),
        grid=(m // tm, n_tiles + 1),
        in_specs=[
            pl.BlockSpec((tm, d), lambda i, f: (i, 0), pipeline_mode=pl.Buffered(1)),
            pl.BlockSpec((1, d), lambda i, f: (0, 0)),
            pl.BlockSpec((d, tf), lambda i, f: (0, jnp.minimum(f, n_tiles - 1))),
            pl.BlockSpec((d, tf), lambda i, f: (0, jnp.minimum(f, n_tiles - 1))),
            pl.BlockSpec((tf, d), lambda i, f: (jnp.maximum(f - 1, 0), 0)),
            pl.BlockSpec((1, d), lambda i, f: (0, 0)),
        ],
        out_specs=pl.BlockSpec((tm, d), lambda i, f: (i, 0)),
        out_shape=jax.ShapeDtypeStruct((m, d), F32),
        scratch_shapes=[pltpu.VMEM((tm, d), BF16), pltpu.VMEM((2, tm, tf), BF16)],
        compiler_params=_params("parallel", "arbitrary"),
    )(x, g.reshape(1, d), wg, wu, wd, gf)


def _online_update(carry, s, v, shift=None):
    m, l, acc = carry
    smax = jnp.max(s, axis=-1, keepdims=True)
    if shift is not None:
        smax = smax - shift
    m_new = jnp.maximum(m, smax)
    alpha = jnp.exp2(m - m_new)
    p = jnp.exp2(s - (m_new if shift is None else m_new + shift))
    l = alpha * l + jnp.sum(p, axis=-1, keepdims=True)
    pb = p.astype(BF16)
    if isinstance(v, (list, tuple)):
        n = v[0].shape[0]
        pv = sum(_dot(pb[:, r * n:(r + 1) * n], x) for r, x in enumerate(v))
    else:
        pv = _dot(pb, v)
    return m_new, l, alpha * acc + pv


def _state_init(m_ref, l_ref, acc_ref):
    m_ref[...] = jnp.full(m_ref.shape, NEG, F32)
    l_ref[...] = jnp.zeros(l_ref.shape, F32)
    acc_ref[...] = jnp.zeros(acc_ref.shape, F32)


def _state_update(m_ref, l_ref, acc_ref, k, s, v, shift=None):
    m, l, acc = _online_update((m_ref[k], l_ref[k], acc_ref[k]), s, v, shift)
    m_ref[k], l_ref[k], acc_ref[k] = m, l, acc


def _fold_lanes(x, op):
    out = x[:, :LANES]
    for i in range(1, x.shape[1] // LANES):
        out = op(out, x[:, i * LANES:(i + 1) * LANES])
    return out


def _two_pass_attention(q, kb_ref, vb_ref, n_blocks, tk, s_ref, add_fn=None, tail_fn=None, shift=None):
    rows = q.shape[0]

    def scores(kb, masked):
        k0 = pl.multiple_of(kb * tk, tk)
        s = _dot_nt(q, kb_ref[pl.ds(k0, tk), :])
        if add_fn is not None:
            s = s + add_fn(k0)
        if masked and tail_fn is not None:
            s = s + tail_fn(k0)
        s_ref[kb] = s
        return _fold_lanes(s, jnp.maximum)

    mx = lax.fori_loop(0, n_blocks - 1, lambda kb, mx: jnp.maximum(mx, scores(kb, False)),
                       jnp.full((rows, LANES), NEG, F32))
    mx = jnp.maximum(mx, scores(n_blocks - 1, True))
    m = jnp.max(mx, axis=-1, keepdims=True)
    if shift is not None:
        m = (m - shift) + shift

    def accumulate(kb, carry):
        ls, acc = carry
        k0 = pl.multiple_of(kb * tk, tk)
        p = jnp.exp2(s_ref[kb] - m)
        return ls + _fold_lanes(p, jnp.add), acc + _dot(p.astype(BF16), vb_ref[pl.ds(k0, tk), :])

    ls, acc = lax.fori_loop(0, n_blocks, accumulate,
                            (jnp.zeros((rows, LANES), F32), jnp.zeros((rows, HEAD_DIM), F32)))
    return acc, jnp.sum(ls, axis=-1, keepdims=True)


def _masked_softmax(s, okf):
    ok = okf > 0.5
    sm = jnp.where(ok, s, NEG)
    m = jnp.max(sm, axis=-1, keepdims=True)
    e = jnp.where(ok, jnp.exp2(sm - m), 0.0)
    return e / jnp.maximum(jnp.sum(e, axis=-1, keepdims=True), 1e-30)


def _lane_pick(x, lane_idx):
    return jnp.sum(jnp.where(_iota(x.shape, 1) == lane_idx, x, 0.0), axis=-1, keepdims=True)


def _topk_mask(imp, n_sel):
    lane = _iota(imp.shape, 1).astype(F32)
    sel = jnp.zeros(imp.shape, F32)
    x = imp
    for _ in range(n_sel):
        m = jnp.max(x, axis=-1, keepdims=True)
        idx = jnp.min(jnp.where(x == m, lane, 1e9), axis=-1, keepdims=True)
        hit = lane == idx
        sel = jnp.where(hit, 1.0, sel)
        x = jnp.where(hit, LOWEST, x)
    return sel


def _overlap(ci, sj):
    ov = jnp.minimum(ci + CMP_LEN, sj + SLC_BLOCK) - jnp.maximum(ci, sj)
    return jnp.maximum(ov, 0).astype(F32) * (1.0 / CMP_LEN)


def _topk_mask_t(imp_t, n_sel, ns):
    rows = _round_up(ns, SUBLANES)
    x = imp_t[0:rows]
    j = _iota(x.shape, 0)
    rank = jnp.zeros(x.shape, F32)
    for jp in range(ns):
        a = x[jp:jp + 1, :]
        rank = rank + jnp.where((a > x) | ((a == x) & (jp < j)), 1.0, 0.0)
    sel = jnp.where((rank < n_sel) & (j < ns), 1.0, 0.0)
    return jnp.concatenate([sel, jnp.zeros((imp_t.shape[0] - rows, imp_t.shape[1]), F32)], axis=0)


def _block_importance(imp, qpos, ns, axis=1):
    j = _iota(imp.shape, axis)
    cur = qpos >> 6
    forced = (j == 0) | (j == cur) | (j == cur - 1)
    valid = (j << 6) <= qpos
    out = jnp.where(forced, BIG, jnp.where(valid, imp, NEG))
    return jnp.where(j < ns, out, LOWEST)


def _lam_value(lam_ref, lam_init):
    v = lam_ref[...]
    a = jnp.exp(jnp.sum(v[0:1] * v[1:2], axis=-1, keepdims=True))
    b = jnp.exp(jnp.sum(v[2:3] * v[3:4], axis=-1, keepdims=True))
    return a - b + lam_init


def _gelu_tanh(x):
    return 0.5 * x * (1.0 + jnp.tanh(0.7978845608028654 * (x + 0.044715 * (x * x * x))))


def _log_sigmoid(z):
    return jnp.minimum(z, 0.0) - jnp.log1p(jnp.exp(-jnp.abs(z)))


def _cast_kv_once(step, pairs):
    @pl.when(step == 0)
    def _():
        for src, dst in pairs:
            dst[...] = src[0].astype(BF16)


def _logf_prompt_kernel(z_ref, b_ref, lf_ref, dkc_ref, dkr_ref):
    t = z_ref.shape[1]
    lf = _log_sigmoid(z_ref[0] + b_ref[...])
    lf_ref[0] = lf
    upper = (_iota((LANES, LANES), 1) > _iota((LANES, LANES), 0)).astype(F32)
    carry = jnp.zeros((1, LANES), F32)
    for blk in reversed(range(t // LANES)):
        sl = slice(blk * LANES, (blk + 1) * LANES)
        x = lf[sl]
        d = _dot_exact(upper, x) + carry
        dkc_ref[0, sl, :] = d
        dkr_ref[0, :, sl] = d.T
        carry = carry + jnp.sum(x, axis=0, keepdims=True)


def _logf_prompt(z3, b_pad, col_blk):
    b, t, _ = z3.shape
    return pl.pallas_call(
        _logf_prompt_kernel,
        grid=(b,),
        in_specs=[pl.BlockSpec((1, t, LANES), lambda i: (i, 0, col_blk)),
                  pl.BlockSpec((1, LANES), lambda i: (0, 0))],
        out_specs=[pl.BlockSpec((1, t, LANES), lambda i: (i, 0, 0)),
                   pl.BlockSpec((1, t, LANES), lambda i: (i, 0, 0)),
                   pl.BlockSpec((1, LANES, t), lambda i: (i, 0, 0))],
        out_shape=[jax.ShapeDtypeStruct((b, t, LANES), F32),
                   jax.ShapeDtypeStruct((b, t, LANES), F32),
                   jax.ShapeDtypeStruct((b, LANES, t), F32)],
        compiler_params=_params("parallel"),
    )(z3, b_pad)


def _causal_tail(qpos, tk):
    return lambda k0: jnp.where(k0 + _iota((qpos.shape[0], tk), 1) <= qpos, 0.0, NEG)


def _diff_prompt_kernel(lam_ref, g_ref, q_ref, k_ref, v_ref, o_ref, kb_ref, vb_ref, s_ref, *, tq, gsz, lam_init, tk):
    qi = pl.program_id(2)
    _cast_kv_once(qi, ((k_ref, kb_ref), (v_ref, vb_ref)))
    lam = _lam_value(lam_ref, lam_init)
    lane = _iota((tq, HEAD_DIM), 1)
    qscale = DA ** -0.5 * LOG2E
    qs = [q_ref[0, :, g * HEAD_DIM:(g + 1) * HEAD_DIM] * qscale for g in range(gsz)]
    q = jnp.concatenate([jnp.where(lane < DA, x, 0.0) for x in qs] + [jnp.where(lane >= DA, x, 0.0) for x in qs],
                        axis=0).astype(BF16)
    qpos = qi * tq + (_iota((2 * gsz * tq, 1), 0) & (tq - 1))
    n_blocks = (qi * tq) // tk + 1
    acc, l = _two_pass_attention(q, kb_ref, vb_ref, n_blocks, tk, s_ref, tail_fn=_causal_tail(qpos, tk))
    o = acc / l
    half = gsz * tq
    o = o[:half] - lam * o[half:]
    o = (_rmsnorm_rows(o, g_ref[...]) * (1.0 - lam_init)).astype(BF16)
    for g in range(gsz):
        o_ref[0, :, g * HEAD_DIM:(g + 1) * HEAD_DIM] = o[g * tq:(g + 1) * tq]


def _diff_prompt(z3, lam_vec, subln_g, n_kv, gsz, q_blk0, k_blk0, v_blk0, lam_init, tq, tk):
    b, t, _ = z3.shape
    qw = gsz * HEAD_DIM
    assert tk % tq == 0 and tq & (tq - 1) == 0
    return pl.pallas_call(
        functools.partial(_diff_prompt_kernel, tq=tq, gsz=gsz, lam_init=lam_init, tk=tk),
        grid=(b, n_kv, t // tq),
        in_specs=[
            pl.BlockSpec((4, DA), lambda i, k, q: (0, 0)),
            pl.BlockSpec((1, HEAD_DIM), lambda i, k, q: (0, 0)),
            pl.BlockSpec((1, tq, qw), lambda i, k, q: (i, q, q_blk0 + k)),
            pl.BlockSpec((1, t, HEAD_DIM), lambda i, k, q: (i, 0, k_blk0 + k)),
            pl.BlockSpec((1, t, HEAD_DIM), lambda i, k, q: (i, 0, v_blk0 + k)),
        ],
        out_specs=pl.BlockSpec((1, tq, qw), lambda i, k, q: (i, q, k)),
        out_shape=jax.ShapeDtypeStruct((b, t, n_kv * qw), BF16),
        scratch_shapes=[pltpu.VMEM((t, HEAD_DIM), BF16), pltpu.VMEM((t, HEAD_DIM), BF16),
                        pltpu.VMEM((t // tk, 2 * gsz * tq, tk), F32)],
        compiler_params=_params("parallel", "parallel", "arbitrary"),
    )(lam_vec, subln_g.reshape(1, HEAD_DIM), z3, z3, z3)


def _fox_prompt_kernel(q_ref, k_ref, v_ref, dkr_ref, dkc_ref, o_ref, kb_ref, vb_ref, s_ref, *, tq, gsz, tk):
    kh = pl.program_id(1)
    qi = pl.program_id(2)
    _cast_kv_once(qi, ((k_ref, kb_ref), (v_ref, vb_ref)))
    qscale = HEAD_DIM ** -0.5 * LOG2E
    qpos = qi * tq + (_iota((gsz * tq, 1), 0) & (tq - 1))
    n_blocks = (qi * tq) // tk + 1
    q = jnp.concatenate([q_ref[0, :, g * HEAD_DIM:(g + 1) * HEAD_DIM] * qscale for g in range(gsz)],
                        axis=0).astype(BF16)
    dq = jnp.concatenate([_lane_pick(dkc_ref[0], kh * gsz + g) for g in range(gsz)], axis=0) * LOG2E

    def dk_fn(k0):
        return jnp.concatenate([jnp.broadcast_to(dkr_ref[0, g, :, pl.ds(k0, tk)] * LOG2E, (tq, tk))
                                for g in range(gsz)], axis=0)

    acc, l = _two_pass_attention(q, kb_ref, vb_ref, n_blocks, tk, s_ref, add_fn=dk_fn,
                                 tail_fn=_causal_tail(qpos, tk), shift=dq)
    o = (acc / l).astype(BF16)
    for g in range(gsz):
        o_ref[0, :, g * HEAD_DIM:(g + 1) * HEAD_DIM] = o[g * tq:(g + 1) * tq]


def _fox_prompt(z3, dkr4, dkc, n_kv, gsz, q_blk0, k_blk0, v_blk0, tq, tk):
    b, t, _ = z3.shape
    qw = gsz * HEAD_DIM
    assert tk % tq == 0
    return pl.pallas_call(
        functools.partial(_fox_prompt_kernel, tq=tq, gsz=gsz, tk=tk),
        grid=(b, n_kv, t // tq),
        in_specs=[
            pl.BlockSpec((1, tq, qw), lambda i, k, q: (i, q, q_blk0 + k)),
            pl.BlockSpec((1, t, HEAD_DIM), lambda i, k, q: (i, 0, k_blk0 + k)),
            pl.BlockSpec((1, t, HEAD_DIM), lambda i, k, q: (i, 0, v_blk0 + k)),
            pl.BlockSpec((1, gsz, 1, t), lambda i, k, q: (i, k, 0, 0)),
            pl.BlockSpec((1, tq, LANES), lambda i, k, q: (i, q, 0)),
        ],
        out_specs=pl.BlockSpec((1, tq, qw), lambda i, k, q: (i, q, k)),
        out_shape=jax.ShapeDtypeStruct((b, t, n_kv * qw), BF16),
        scratch_shapes=[pltpu.VMEM((t, HEAD_DIM), BF16), pltpu.VMEM((t, HEAD_DIM), BF16),
                        pltpu.VMEM((t // tk, gsz * tq, tk), F32)],
        compiler_params=_params("parallel", "parallel", "arbitrary"),
    )(z3, z3, z3, dkr4, dkc)


def _logf_sample_kernel(z_ref, b_ref, lf_ref, dq_ref, tot_ref, *, ts):
    m = z_ref.shape[0]
    lf = _log_sigmoid(z_ref[...] + b_ref[...])
    lf_ref[...] = lf
    i, j = _iota((m, m), 0), _iota((m, m), 1)
    same = (i // ts) == (j // ts)
    dq_ref[...] = _dot_exact((same & (j > i)).astype(F32), lf)
    tot_ref[...] = _dot_exact(same.astype(F32), lf)


def _logf_sample(z_fg, b_pad, ts):
    m = z_fg.shape[0]
    sds = jax.ShapeDtypeStruct((m, LANES), F32)
    return pl.pallas_call(
        functools.partial(_logf_sample_kernel, ts=ts),
        out_shape=[sds, sds, sds],
    )(z_fg, b_pad)


def _pad_rows_scratch(ref, x):
    ref[...] = jnp.zeros(ref.shape, ref.dtype)
    ref[0:x.shape[0], :] = x


def _page_specs(page, n_kv, n_pages, first_page):
    rows = page * 2 * n_kv
    return [pl.BlockSpec((rows, HEAD_DIM), lambda b, p, pt, r=r: (pt[b, first_page(b, p, pt) + r], 0))
            for r in range(n_pages)]


def _tile_view(cache):
    n_kv = cache.shape[3]
    assert 2 * n_kv in (SUBLANES, 2 * SUBLANES)
    return cache.reshape(-1, 2 * n_kv // SUBLANES, SUBLANES, HEAD_DIM)


def _tile_specs(page, tiles, n_pages, first_page):
    return [pl.BlockSpec((page, tiles, SUBLANES, HEAD_DIM),
                         lambda b, p, pt, r=r: (pt[b, first_page(b, p, pt) + r], 0, 0, 0))
            for r in range(n_pages)]


def _flat_kv(pg_refs):
    n = pg_refs[0].shape[0] * SUBLANES
    if pg_refs[0].shape[1] == 2:
        ks = [pg[:, 0].reshape(n, HEAD_DIM) for pg in pg_refs]
        vs = [pg[:, 1].reshape(n, HEAD_DIM) for pg in pg_refs]
        off = 0
    else:
        vs = [pg[:, 0].reshape(n, HEAD_DIM) for pg in pg_refs]
        ks = [pltpu.roll(v, SUBLANES // 2, axis=0) for v in vs]
        off = SUBLANES // 2
    return [k.astype(BF16) for k in ks], [v.astype(BF16) for v in vs], off


def _flat_scores(q, ks):
    return jnp.concatenate([_dot_nt(q, k) for k in ks], axis=1)


def _own_head_bias(row_head, n_cols):
    return jnp.where((_iota((row_head.shape[0], n_cols), 1) & (SUBLANES - 1)) == row_head, 0.0, NEG)


def _diff_sample_kernel(pt_ref, lam_ref, g_ref, q_ref, kn_ref, vn_ref, *refs, n_kv, gsz, ts, lam_init, n_pages):
    pg_refs = refs[:n_pages]
    o_ref, m_ref, l_ref, acc_ref, bias_ref, kpad_ref, vpad_ref = refs[n_pages:]
    p = pl.program_id(1)
    last = pl.num_programs(1) - 1
    r1 = n_kv * ts * gsz
    q = q_ref[0] * (DA ** -0.5 * LOG2E)
    lane = _iota((r1, HEAD_DIM), 1)
    qb = jnp.concatenate([jnp.where(lane < DA, q, 0.0), jnp.where(lane >= DA, q, 0.0)], axis=0).astype(BF16)
    rr = _iota((2 * r1, 1), 0) % r1
    row_head = rr // (ts * gsz)
    row_tok = (rr % (ts * gsz)) // gsz

    @pl.when(p == 0)
    def _():
        _state_init(m_ref, l_ref, acc_ref)

    @pl.when(p < last)
    def _():
        ks, vs, off = _flat_kv(pg_refs)

        @pl.when(p == 0)
        def _():
            bias_ref[...] = _own_head_bias(row_head + off, bias_ref.shape[1])

        _state_update(m_ref, l_ref, acc_ref, ..., _flat_scores(qb, ks) + bias_ref[...], vs)

    @pl.when(p == last)
    def _():
        _pad_rows_scratch(kpad_ref, kn_ref[0])
        _pad_rows_scratch(vpad_ref, vn_ref[0])
        s = _dot_nt(qb, kpad_ref[...].astype(BF16))
        c = _iota(s.shape, 1)
        ok = ((c % n_kv) == row_head) & ((c // n_kv) <= row_tok)
        _state_update(m_ref, l_ref, acc_ref, ..., jnp.where(ok, s, NEG), vpad_ref[...].astype(BF16))
        o = acc_ref[...] / l_ref[...]
        o = o[:r1] - _lam_value(lam_ref, lam_init) * o[r1:]
        o_ref[0] = (_rmsnorm_rows(o, g_ref[...]) * (1.0 - lam_init)).astype(BF16)


def _diff_sample(page_table, lam_vec, subln_g, q_rows, k_new, v_new, cache_tiles, page, n_kv, gsz, ts, lam_init, n_pages):
    bs, r1, _ = q_rows.shape
    nsteps = page_table.shape[1] // n_pages
    first = lambda b, p, pt: jnp.minimum(p, nsteps - 1) * n_pages
    row_spec = lambda n: pl.BlockSpec((1, n, HEAD_DIM), lambda b, p, pt: (b, 0, 0))
    assert ts * n_kv <= LANES
    grid_spec = pltpu.PrefetchScalarGridSpec(
        num_scalar_prefetch=1,
        grid=(bs, nsteps + 1),
        in_specs=[
            pl.BlockSpec((4, DA), lambda b, p, pt: (0, 0)),
            pl.BlockSpec((1, HEAD_DIM), lambda b, p, pt: (0, 0)),
            row_spec(r1), row_spec(ts * n_kv), row_spec(ts * n_kv),
        ] + _tile_specs(page, cache_tiles.shape[1], n_pages, first),
        out_specs=row_spec(r1),
        scratch_shapes=[pltpu.VMEM((2 * r1, 1), F32), pltpu.VMEM((2 * r1, 1), F32), pltpu.VMEM((2 * r1, HEAD_DIM), F32),
                        pltpu.VMEM((2 * r1, n_pages * page * SUBLANES), F32),
                        pltpu.VMEM((LANES, HEAD_DIM), F32), pltpu.VMEM((LANES, HEAD_DIM), F32)],
    )
    return pl.pallas_call(
        functools.partial(_diff_sample_kernel, n_kv=n_kv, gsz=gsz, ts=ts, lam_init=lam_init, n_pages=n_pages),
        grid_spec=grid_spec,
        out_shape=jax.ShapeDtypeStruct((bs, r1, HEAD_DIM), BF16),
        compiler_params=_params("parallel", "arbitrary"),
    )(page_table, lam_vec, subln_g.reshape(1, HEAD_DIM), q_rows, k_new, v_new, *([cache_tiles] * n_pages))


def _fox_sample_kernel(pt_ref, q_ref, kn_ref, vn_ref, dq_ref, dkn_ref, tot_ref, *refs, n_kv, gsz, ts, n_pages):
    pg_refs, lf_refs = refs[:n_pages], refs[n_pages:2 * n_pages]
    o_ref, m_ref, l_ref, acc_ref, car_ref, bias_ref, kpad_ref, vpad_ref = refs[2 * n_pages:]
    p = pl.program_id(1)
    page = lf_refs[0].shape[2]
    rows = n_kv * ts * gsz
    n_h = n_kv * gsz
    qb = (q_ref[0] * (HEAD_DIM ** -0.5 * LOG2E)).astype(BF16)
    rr = _iota((rows, 1), 0)
    row_head = rr // (ts * gsz)
    row_tok = (rr % (ts * gsz)) // gsz
    shift = dq_ref[0][:, 0:1] * LOG2E

    @pl.when(p == 0)
    def _():
        _state_init(m_ref, l_ref, acc_ref)
        car_ref[...] = tot_ref[0]
        _pad_rows_scratch(kpad_ref, kn_ref[0])
        _pad_rows_scratch(vpad_ref, vn_ref[0])
        s = _dot_nt(qb, kpad_ref[...].astype(BF16)) + dkn_ref[0] * LOG2E
        c = _iota(s.shape, 1)
        ok = ((c % n_kv) == row_head) & ((c // n_kv) <= row_tok)
        _state_update(m_ref, l_ref, acc_ref, ..., jnp.where(ok, s, NEG), vpad_ref[...].astype(BF16), shift=shift)

    @pl.when(p > 0)
    def _():
        ks, vs, off = _flat_kv(pg_refs)

        @pl.when(p == 1)
        def _():
            bias_ref[...] = _own_head_bias(row_head + off, bias_ref.shape[1])

        lower = (_iota((page, page), 0) > _iota((page, page), 1)).astype(F32)
        dks = [None] * n_pages
        for r in reversed(range(n_pages)):
            lft = lf_refs[r][0]
            dks[r] = _dot_exact(lft, lower) + car_ref[...]
            car_ref[...] = car_ref[...] + jnp.sum(lft, axis=1, keepdims=True)
        dk = jnp.concatenate(dks, axis=1)
        hi = dk.astype(BF16)
        rest = dk - hi.astype(F32)
        mid = rest.astype(BF16)
        terms = (hi, mid, (rest - mid.astype(F32)).astype(BF16))
        hh = _iota((rows, n_h), 0)
        pick = ((hh // (ts * gsz)) * gsz + hh % gsz == _iota((rows, n_h), 1)).astype(BF16)
        spread = ((_iota((page, page * SUBLANES), 1) >> 3) == _iota((page, page * SUBLANES), 0)).astype(BF16)
        on_rows = [_dot(pick, x).astype(BF16) for x in terms]
        pieces = jnp.concatenate([x[:, r * page:(r + 1) * page] for r in range(n_pages) for x in on_rows], axis=0)
        flat = _dot(pieces, spread)
        nt = len(terms)
        dkf = jnp.concatenate(
            [sum(flat[(r * nt + q) * rows:(r * nt + q + 1) * rows] for q in range(nt)) for r in range(n_pages)],
            axis=1) * LOG2E
        _state_update(m_ref, l_ref, acc_ref, ..., _flat_scores(qb, ks) + dkf + bias_ref[...], vs, shift=shift)

    @pl.when(p == pl.num_programs(1) - 1)
    def _():
        o_ref[0] = (acc_ref[...] / l_ref[...]).astype(BF16)


def _fox_sample(page_table, q_rows, k_new, v_new, dq_b, dkn_b, tot, cache_tiles, logf_t, page, n_kv, gsz, ts, n_pages):
    bs, rows, _ = q_rows.shape
    n_h = n_kv * gsz
    nsteps = page_table.shape[1] // n_pages
    first = lambda b, p, pt: (nsteps - jnp.maximum(p, 1)) * n_pages
    lf_specs = [pl.BlockSpec((1, n_h, page), lambda b, p, pt, r=r: (pt[b, first(b, p, pt) + r], 0, 0))
                for r in range(n_pages)]
    row_spec = lambda n: pl.BlockSpec((1, n, HEAD_DIM), lambda b, p, pt: (b, 0, 0))
    assert ts * n_kv <= LANES
    grid_spec = pltpu.PrefetchScalarGridSpec(
        num_scalar_prefetch=1,
        grid=(bs, nsteps + 1),
        in_specs=[
            row_spec(rows), row_spec(ts * n_kv), row_spec(ts * n_kv), row_spec(rows), row_spec(rows),
            pl.BlockSpec((1, n_h, 1), lambda b, p, pt: (b, 0, 0)),
        ] + _tile_specs(page, cache_tiles.shape[1], n_pages, first) + lf_specs,
        out_specs=row_spec(rows),
        scratch_shapes=[pltpu.VMEM((rows, 1), F32), pltpu.VMEM((rows, 1), F32), pltpu.VMEM((rows, HEAD_DIM), F32),
                        pltpu.VMEM((n_h, 1), F32), pltpu.VMEM((rows, n_pages * page * SUBLANES), F32),
                        pltpu.VMEM((LANES, HEAD_DIM), F32), pltpu.VMEM((LANES, HEAD_DIM), F32)],
    )
    return pl.pallas_call(
        functools.partial(_fox_sample_kernel, n_kv=n_kv, gsz=gsz, ts=ts, n_pages=n_pages),
        grid_spec=grid_spec,
        out_shape=jax.ShapeDtypeStruct((bs, rows, HEAD_DIM), BF16),
        compiler_params=_params("parallel", "arbitrary"),
    )(page_table, q_rows, k_new, v_new, dq_b, dkn_b, tot, *([cache_tiles] * n_pages), *([logf_t] * n_pages))


def _compress_mlp(acc_a, acc_b_next, w2):
    return _dot(_gelu_tanh(acc_a + acc_b_next).astype(BF16), w2).astype(BF16)


def _compress_prompt_kernel(z_ref, pe_ref, w1_ref, w2_ref, o_ref, *, nch):
    acc_a = jnp.zeros((nch, CMP_HID), F32)
    acc_b = jnp.zeros((nch, CMP_HID), F32)
    for l in range(CMP_STRIDE):
        x = z_ref[0, pl.ds(l, nch, stride=CMP_STRIDE), :]
        xa = (x + pe_ref[0, l:l + 1, :]).astype(BF16)
        xb = (x + pe_ref[0, CMP_STRIDE + l:CMP_STRIDE + l + 1, :]).astype(BF16)
        acc_a = acc_a + _dot(xa, w1_ref[0, l])
        acc_b = acc_b + _dot(xb, w1_ref[0, CMP_STRIDE + l])
    o_ref[0, 0, 0] = _compress_mlp(acc_a, pltpu.roll(acc_b, nch - 1, axis=0), w2_ref[0])


def _compress_prompt(z3, pe, w1, w2, n_kv, col_blk0):
    b, t, _ = z3.shape
    nch = t // CMP_STRIDE
    return pl.pallas_call(
        functools.partial(_compress_prompt_kernel, nch=nch),
        grid=(b, 2 * n_kv),
        in_specs=[
            pl.BlockSpec((1, t, HEAD_DIM), lambda i, c: (i, 0, col_blk0 + c)),
            pl.BlockSpec((1, CMP_LEN, HEAD_DIM), lambda i, c: (c // n_kv, 0, 0)),
            pl.BlockSpec((1, CMP_LEN, HEAD_DIM, CMP_HID), lambda i, c: (c // n_kv, 0, 0, 0)),
            pl.BlockSpec((1, CMP_HID, HEAD_DIM), lambda i, c: (c // n_kv, 0, 0)),
        ],
        out_specs=pl.BlockSpec((1, 1, 1, nch, HEAD_DIM), lambda i, c: (i, c // n_kv, c % n_kv, 0, 0)),
        out_shape=jax.ShapeDtypeStruct((b, 2, n_kv, nch, HEAD_DIM), BF16),
        compiler_params=_params("parallel", "arbitrary"),
    )(z3, pe, w1, w2)


def _nsa_prompt_kernel(q_ref, cos_ref, sin_ref, kc_ref, vc_ref, ks_ref, vs_ref, kw_ref, vw_ref, gl_ref, bg_ref,
                       o_ref, ksb_ref, vsb_ref, kwb_ref, vwb_ref, s_ref, *, tq, gsz, n_kv, nblk, ns, tk):
    kh = pl.program_id(1)
    qi = pl.program_id(2)
    _cast_kv_once(qi, ((ks_ref, ksb_ref), (vs_ref, vsb_ref), (kw_ref, kwb_ref), (vw_ref, vwb_ref)))
    t0 = qi * tq
    qscale = HEAD_DIM ** -0.5 * LOG2E
    cos, sin = cos_ref[...], sin_ref[...]
    qs = [q_ref[0, :, g * HEAD_DIM:(g + 1) * HEAD_DIM] * qscale for g in range(gsz)]
    q = jnp.concatenate(qs, axis=0).astype(BF16)
    qr = jnp.concatenate([_rope_lanes(x, cos, sin, HEAD_DIM // 2) for x in qs], axis=0).astype(BF16)
    qpos = t0 + _iota((tq, 1), 0)

    def tile_g(x):
        return jnp.concatenate([x] * gsz, axis=0)

    nch = kc_ref.shape[3]
    n_i = _iota((tq, nch), 1)
    ok_c = ((n_i * CMP_STRIDE + CMP_LEN - 1 <= qpos) & (n_i < nblk)).astype(F32)
    p = _masked_softmax(_dot_nt(q, kc_ref[0, 0, 0]), tile_g(ok_c))
    o_cmp = _dot(p.astype(BF16), vc_ref[0, 0, 0])
    psum = p[0:tq]
    for g in range(1, gsz):
        psum = psum + p[g * tq:(g + 1) * tq]
    ov_t = _overlap(_iota((LANES, nch), 1) * CMP_STRIDE, _iota((LANES, nch), 0) * SLC_BLOCK)
    imp_t = _block_importance(_dot_nt_exact(ov_t, psum), t0 + _iota((1, tq), 1), ns, axis=0)
    sel = _topk_mask_t(imp_t, min(N_SELECT, ns), ns).T.astype(BF16)

    def slc_mask(k0):
        expand = (((k0 + _iota((LANES, tk), 1)) >> 6) == _iota((LANES, tk), 0)).astype(BF16)
        selk = _dot(sel, expand)
        return tile_g(jnp.where((selk > 0.5) & (k0 + _iota((tq, tk), 1) <= qpos), 0.0, NEG))

    n_blocks = (t0 + tq + tk - 1) // tk
    acc, l = _two_pass_attention(qr, ksb_ref, vsb_ref, n_blocks, tk, s_ref, add_fn=slc_mask)
    o_slc = acc / l

    wlen = WINDOW + tq
    w0 = pl.multiple_of(jnp.maximum(t0 - WINDOW, 0), tq)
    dist = qpos - (w0 + _iota((tq, wlen), 1))
    sw = _dot_nt(qr, kwb_ref[pl.ds(w0, wlen), :]) + tile_g(jnp.where((dist >= 0) & (dist < WINDOW), 0.0, NEG))
    ew = jnp.exp2(sw - jnp.max(sw, axis=-1, keepdims=True))
    o_win = _dot(ew.astype(BF16), vwb_ref[pl.ds(w0, wlen), :]) / jnp.sum(ew, axis=-1, keepdims=True)

    gates = jax.nn.sigmoid(gl_ref[0] + bg_ref[...])
    n_h = n_kv * gsz
    for g in range(gsz):
        rows = slice(g * tq, (g + 1) * tq)
        col = kh * gsz + g
        o = (_lane_pick(gates, col) * o_cmp[rows] + _lane_pick(gates, n_h + col) * o_slc[rows]
             + _lane_pick(gates, 2 * n_h + col) * o_win[rows])
        o_ref[0, :, g * HEAD_DIM:(g + 1) * HEAD_DIM] = o.astype(BF16)


def _nsa_prompt(z3, cos, sin, kvc, bg_pad, n_kv, gsz, ks_blk, vs_blk, kw_blk, vw_blk, gl_blk, tq, tk):
    b, t, _ = z3.shape
    nch = kvc.shape[3]
    qw = gsz * HEAD_DIM
    ns = -(-t // SLC_BLOCK)
    assert t >= WINDOW + tq and t % tk == 0 and tk % SLC_BLOCK == 0 and ns <= LANES
    kv_spec = lambda blk0: pl.BlockSpec((1, t, HEAD_DIM), lambda i, k, q: (i, 0, blk0 + k))
    return pl.pallas_call(
        functools.partial(_nsa_prompt_kernel, tq=tq, gsz=gsz, n_kv=n_kv, nblk=nch - 1, ns=ns, tk=tk),
        grid=(b, n_kv, t // tq),
        in_specs=[
            pl.BlockSpec((1, tq, qw), lambda i, k, q: (i, q, k)),
            pl.BlockSpec((tq, LANES), lambda i, k, q: (q, 0)),
            pl.BlockSpec((tq, LANES), lambda i, k, q: (q, 0)),
            pl.BlockSpec((1, 1, 1, nch, HEAD_DIM), lambda i, k, q: (i, 0, k, 0, 0)),
            pl.BlockSpec((1, 1, 1, nch, HEAD_DIM), lambda i, k, q: (i, 1, k, 0, 0)),
            kv_spec(ks_blk), kv_spec(vs_blk), kv_spec(kw_blk), kv_spec(vw_blk),
            pl.BlockSpec((1, tq, LANES), lambda i, k, q: (i, q, gl_blk)),
            pl.BlockSpec((1, LANES), lambda i, k, q: (0, 0)),
        ],
        out_specs=pl.BlockSpec((1, tq, qw), lambda i, k, q: (i, q, k)),
        out_shape=jax.ShapeDtypeStruct((b, t, n_kv * qw), BF16),
        scratch_shapes=[pltpu.VMEM((t, HEAD_DIM), BF16) for _ in range(4)] + [
            pltpu.VMEM((t // tk, gsz * tq, tk), F32)],
        compiler_params=_params("parallel", "parallel", "arbitrary"),
    )(z3, cos, sin, kvc, kvc, z3, z3, z3, z3, z3, bg_pad)


def _compress_sample_kernel(pt_ref, *refs, n_kv, n_pages, page):
    pg_refs = refs[:n_pages]
    pe_ref, w1_ref, w2_ref, o_ref, last_ref = refs[n_pages:]
    s = pl.program_id(1)
    cpp = page // CMP_STRIDE
    m = n_pages * cpp
    rpr = 2 * n_kv

    @pl.when(s == 0)
    def _():
        last_ref[...] = jnp.zeros(last_ref.shape, F32)

    for typ in range(2):
        acc_a = jnp.zeros((n_kv * m, CMP_HID), F32)
        acc_b = jnp.zeros((n_kv * m, CMP_HID), F32)
        for l in range(0, CMP_STRIDE, 2):
            xs = [jnp.concatenate(
                [pg[pl.ds((l + i) * rpr + typ * n_kv + h, cpp, stride=CMP_STRIDE * rpr), :]
                 for h in range(n_kv) for pg in pg_refs], axis=0) for i in range(2)]
            for acc_is_b, off in ((False, l), (True, CMP_STRIDE + l)):
                lhs = jnp.concatenate([(xs[i] + pe_ref[typ, off + i:off + i + 1, :]).astype(BF16)
                                       for i in range(2)], axis=1)
                term = _dot(lhs, w1_ref[typ, off:off + 2].reshape(2 * HEAD_DIM, CMP_HID))
                if acc_is_b:
                    acc_b = acc_b + term
                else:
                    acc_a = acc_a + term
        first = _iota((m, 1), 0) == 0
        for h in range(n_kv):
            a = acc_a[h * m:(h + 1) * m]
            idx = typ * n_kv + h
            prev = jnp.where(first, last_ref[idx, SUBLANES - 1:SUBLANES, :], pltpu.roll(a, 1, axis=0))
            last_ref[idx] = a[m - SUBLANES:m]
            o_ref[0, typ, h] = _compress_mlp(prev, acc_b[h * m:(h + 1) * m], w2_ref[typ])


def _compress_sample(page_table, cache_rows, pe, w1, w2, page, n_kv, n_pages):
    bs, npg = page_table.shape
    m = n_pages * (page // CMP_STRIDE)
    grid_spec = pltpu.PrefetchScalarGridSpec(
        num_scalar_prefetch=1,
        grid=(bs, npg // n_pages),
        in_specs=_page_specs(page, n_kv, n_pages, lambda b, s, pt: s * n_pages) + [
            pl.BlockSpec((2, CMP_LEN, HEAD_DIM), lambda b, s, pt: (0, 0, 0)),
            pl.BlockSpec((2, CMP_LEN, HEAD_DIM, CMP_HID), lambda b, s, pt: (0, 0, 0, 0)),
            pl.BlockSpec((2, CMP_HID, HEAD_DIM), lambda b, s, pt: (0, 0, 0)),
        ],
        out_specs=pl.BlockSpec((1, 2, n_kv, m, HEAD_DIM), lambda b, s, pt: (b, 0, 0, s, 0)),
        scratch_shapes=[pltpu.VMEM((2 * n_kv, SUBLANES, CMP_HID), F32)],
    )
    return pl.pallas_call(
        functools.partial(_compress_sample_kernel, n_kv=n_kv, n_pages=n_pages, page=page),
        grid_spec=grid_spec,
        out_shape=jax.ShapeDtypeStruct((bs, 2, n_kv, npg * (page // CMP_STRIDE), HEAD_DIM), BF16),
        compiler_params=_params("parallel", "arbitrary"),
    )(page_table, *([cache_rows] * n_pages), pe, w1, w2)


def _nsa_sample_a_kernel(q_ref, cos_ref, sin_ref, kc_ref, vc_ref, win_ref, kwn_ref,
                         ocmp_ref, owin_ref, sel_ref, qr_ref, kpad_ref, vpad_ref, *, n_kv, gsz, ts, p0, ns):
    qscale = HEAD_DIM ** -0.5 * LOG2E
    rows = ts * gsz
    nr = kc_ref.shape[3]
    nsp = sel_ref.shape[3]
    wrows = win_ref.shape[1] // (2 * n_kv)
    tok = _iota((rows, 1), 0) // gsz
    qpos = p0 + tok
    r_i = _iota((rows, nr), 1)
    ok_c = ((r_i >= 1) & ((r_i - 1) * CMP_STRIDE + CMP_LEN - 1 <= qpos)).astype(F32)
    ov = _overlap((_iota((nr, nsp), 0) - 1) * CMP_STRIDE, _iota((nr, nsp), 1) * SLC_BLOCK)
    group = ((_iota((rows, rows), 0) // gsz) == (_iota((rows, rows), 1) // gsz)).astype(F32)
    ok_w1 = _iota((rows, wrows), 1) > tok + (wrows - WINDOW)
    ok_w2 = _iota((rows, kpad_ref.shape[0]), 1) <= tok
    kvw = n_kv * HEAD_DIM
    for kh in range(n_kv):
        q = q_ref[0, kh]
        qr = _rope_lanes(q, cos_ref[...], sin_ref[...], HEAD_DIM // 2)
        qr_ref[0, kh] = qr
        qb, qrb = (q * qscale).astype(BF16), (qr * qscale).astype(BF16)
        p = _masked_softmax(_dot_nt(qb, kc_ref[0, 0, kh]), ok_c)
        ocmp_ref[0, kh] = _dot(p.astype(BF16), vc_ref[0, 0, kh])
        imp = _dot_exact(group, _dot_exact(p, ov))
        sel_ref[0, kh] = _topk_mask(_block_importance(imp, qpos, ns), min(N_SELECT, ns))
        sl = slice(kh * HEAD_DIM, (kh + 1) * HEAD_DIM)
        slv = slice(kvw + kh * HEAD_DIM, kvw + (kh + 1) * HEAD_DIM)
        _pad_rows_scratch(kpad_ref, kwn_ref[0, :, sl])
        _pad_rows_scratch(vpad_ref, kwn_ref[0, :, slv])
        k_win = win_ref[0, pl.ds(kh, wrows, stride=2 * n_kv), :].astype(BF16)
        v_win = win_ref[0, pl.ds(n_kv + kh, wrows, stride=2 * n_kv), :].astype(BF16)
        s1 = jnp.where(ok_w1, _dot_nt(qrb, k_win), NEG)
        s2 = jnp.where(ok_w2, _dot_nt(qrb, kpad_ref[...].astype(BF16)), NEG)
        m = jnp.maximum(jnp.max(s1, axis=-1, keepdims=True), jnp.max(s2, axis=-1, keepdims=True))
        e1, e2 = jnp.exp2(s1 - m), jnp.exp2(s2 - m)
        l = jnp.sum(e1, axis=-1, keepdims=True) + jnp.sum(e2, axis=-1, keepdims=True)
        o = _dot(e1.astype(BF16), v_win) + _dot(e2.astype(BF16), vpad_ref[...].astype(BF16))
        owin_ref[0, kh] = o / l


def _nsa_sample_a(q_rows, cos, sin, kvc, win_rows, kvw_new, n_kv, gsz, ts, p0, ns):
    bs = q_rows.shape[0]
    rows = ts * gsz
    nr = kvc.shape[3]
    nsp = _round_up(ns, LANES)
    wr, w = win_rows.shape[1], kvw_new.shape[2]
    assert wr == WINDOW * 2 * n_kv
    o_sds = jax.ShapeDtypeStruct((bs, n_kv, rows, HEAD_DIM), F32)
    o_spec = pl.BlockSpec((1, n_kv, rows, HEAD_DIM), lambda b: (b, 0, 0, 0))
    return pl.pallas_call(
        functools.partial(_nsa_sample_a_kernel, n_kv=n_kv, gsz=gsz, ts=ts, p0=p0, ns=ns),
        grid=(bs,),
        in_specs=[
            o_spec,
            pl.BlockSpec((rows, LANES), lambda b: (0, 0)),
            pl.BlockSpec((rows, LANES), lambda b: (0, 0)),
            pl.BlockSpec((1, 1, n_kv, nr, HEAD_DIM), lambda b: (b, 0, 0, 0, 0)),
            pl.BlockSpec((1, 1, n_kv, nr, HEAD_DIM), lambda b: (b, 1, 0, 0, 0)),
            pl.BlockSpec((1, wr, HEAD_DIM), lambda b: (b, 0, 0)),
            pl.BlockSpec((1, ts, w), lambda b: (b, 0, 0)),
        ],
        out_specs=[o_spec, o_spec, pl.BlockSpec((1, n_kv, rows, nsp), lambda b: (b, 0, 0, 0)), o_spec],
        out_shape=[o_sds, o_sds, jax.ShapeDtypeStruct((bs, n_kv, rows, nsp), F32), o_sds],
        scratch_shapes=[pltpu.VMEM((LANES, HEAD_DIM), F32), pltpu.VMEM((LANES, HEAD_DIM), F32)],
        compiler_params=_params("parallel"),
    )(q_rows, cos, sin, kvc, kvc, win_rows, kvw_new)


def _nsa_sample_b_kernel(pt_ref, q_ref, sel_ref, kn_ref, ocmp_ref, owin_ref, gl_ref, bg_ref, *refs,
                         n_kv, gsz, ts, n_pages):
    pg_refs = refs[:n_pages]
    o_ref, m_ref, l_ref, acc_ref, bias_ref, npad_ref = refs[n_pages:]
    p = pl.program_id(1)
    last = pl.num_programs(1) - 1
    rows = n_kv * ts * gsz
    page = pg_refs[0].shape[0]
    nsp = sel_ref.shape[2]
    bps = n_pages * (page // SLC_BLOCK)
    blk_cols = SLC_BLOCK * SUBLANES
    qb = (q_ref[0] * (HEAD_DIM ** -0.5 * LOG2E)).astype(BF16)
    rr = _iota((rows, 1), 0)
    row_head = rr // (ts * gsz)
    row_tok = (rr % (ts * gsz)) // gsz

    @pl.when(p == 0)
    def _():
        _state_init(m_ref, l_ref, acc_ref)

    @pl.when(p < last)
    def _():
        ks, vs, off = _flat_kv(pg_refs)

        @pl.when(p == 0)
        def _():
            bias_ref[...] = _own_head_bias(row_head + off, bias_ref.shape[1])

        shift = (_iota((nsp, LANES), 0) == p * bps + _iota((nsp, LANES), 1)).astype(BF16)
        flags = jnp.where(_dot(sel_ref[0].astype(BF16), shift) > 0.5, 0.0, NEG)
        sel_bias = jnp.concatenate([jnp.broadcast_to(flags[:, i:i + 1], (rows, blk_cols)) for i in range(bps)], axis=1)
        _state_update(m_ref, l_ref, acc_ref, ..., _flat_scores(qb, ks) + bias_ref[...] + sel_bias, vs)

    @pl.when(p == last)
    def _():
        _pad_rows_scratch(npad_ref, kn_ref[0])
        flat = npad_ref[...]
        s = _dot_nt(qb, pltpu.roll(flat, n_kv, axis=0).astype(BF16))
        c = _iota(s.shape, 1)
        ok = (((c % (2 * n_kv)) == n_kv + row_head) & ((c // (2 * n_kv)) <= row_tok)
              & (_lane_pick(sel_ref[0], last * bps) > 0.5))
        _state_update(m_ref, l_ref, acc_ref, ..., jnp.where(ok, s, NEG), flat.astype(BF16))
        gates = jax.nn.sigmoid(gl_ref[0] + bg_ref[...])
        o_ref[0] = (gates[:, 0:1] * ocmp_ref[0] + gates[:, 1:2] * (acc_ref[...] / l_ref[...])
                    + gates[:, 2:3] * owin_ref[0])


def _nsa_sample_b(page_table, q_rows, sel, kvs_new, o_cmp, o_win, gl_rows, bg_rows, cache_tiles,
                  page, n_kv, gsz, ts, n_pages):
    bs, rows, _ = q_rows.shape
    nsp = sel.shape[2]
    nsteps = page_table.shape[1] // n_pages
    first = lambda b, p, pt: jnp.minimum(p, nsteps - 1) * n_pages
    row_spec = lambda n, width: pl.BlockSpec((1, n, width), lambda b, p, pt: (b, 0, 0))
    assert ts * 2 * n_kv <= LANES and n_pages * (page // SLC_BLOCK) <= LANES
    grid_spec = pltpu.PrefetchScalarGridSpec(
        num_scalar_prefetch=1,
        grid=(bs, nsteps + 1),
        in_specs=[
            row_spec(rows, HEAD_DIM), row_spec(rows, nsp), row_spec(ts * 2 * n_kv, HEAD_DIM),
            row_spec(rows, HEAD_DIM), row_spec(rows, HEAD_DIM), row_spec(rows, LANES),
            pl.BlockSpec((rows, LANES), lambda b, p, pt: (0, 0)),
        ] + _tile_specs(page, cache_tiles.shape[1], n_pages, first),
        out_specs=row_spec(rows, HEAD_DIM),
        scratch_shapes=[pltpu.VMEM((rows, 1), F32), pltpu.VMEM((rows, 1), F32), pltpu.VMEM((rows, HEAD_DIM), F32),
                        pltpu.VMEM((rows, n_pages * page * SUBLANES), F32), pltpu.VMEM((LANES, HEAD_DIM), F32)],
    )
    return pl.pallas_call(
        functools.partial(_nsa_sample_b_kernel, n_kv=n_kv, gsz=gsz, ts=ts, n_pages=n_pages),
        grid_spec=grid_spec,
        out_shape=jax.ShapeDtypeStruct((bs, rows, HEAD_DIM), F32),
        compiler_params=_params("parallel", "arbitrary"),
    )(page_table, q_rows, sel, kvs_new, o_cmp, o_win, gl_rows, bg_rows, *([cache_tiles] * n_pages))


def _rope_tables(pos, half):
    inv = ROPE_THETA ** (-jnp.arange(half, dtype=F32) / half)
    ang = pos.astype(F32)[:, None] * inv
    cos, sin = jnp.cos(ang), jnp.sin(ang)
    reps = LANES // (2 * half)
    return (jnp.tile(jnp.concatenate([cos, cos], axis=-1), (1, reps)),
            jnp.tile(jnp.concatenate([-sin, sin], axis=-1), (1, reps)))


def _pad_cols(w, n):
    return jnp.pad(w, ((0, 0), (0, n - w.shape[1])))


def _head_rows(x, n_kv, gsz):
    b, t, _ = x.shape
    return x.reshape(b, t, n_kv, gsz, -1).transpose(0, 2, 1, 3, 4).reshape(b, n_kv, t * gsz, -1)


def _token_cols(x, ts, gsz):
    b, n_kv, _, f = x.shape
    return x.reshape(b, n_kv, ts, gsz, f).transpose(0, 2, 1, 3, 4).reshape(b * ts, n_kv * gsz * f)


def kernel(x_prompt, x_sample, cache_a_kv, cache_b_kv, cache_b_logf, cache_cmp_kv, cache_slc_kv, state_win_kv, page_table, norm_mix_e, w_in_e, b_forget, lam_q1, lam_k1, lam_q2, lam_k2, subln_g, w_out_e, norm_mix_o, w_in_o, b_gate, cmp_pe_k, cmp_w1_k, cmp_w2_k, cmp_pe_v, cmp_w1_v, cmp_w2_v, w_out_o, norm_ffn, w_gate, w_up, w_down, norm_final):
    bp, t, d = x_prompt.shape
    bs, ts, _ = x_sample.shape
    n_even, n_odd = w_in_e.shape[0], w_in_o.shape[0]
    depth = n_even + n_odd
    page = cache_a_kv.shape[2]
    npg = page_table.shape[1]
    p0 = npg * page
    h_all = d // HEAD_DIM
    h_b = cache_b_logf.shape[-1]
    h_a = h_all - h_b
    kv_a, kv_b, kv_c = cache_a_kv.shape[4], cache_b_kv.shape[4], cache_cmp_kv.shape[4]
    g_a, g_b, g_c = h_a // kv_a, h_b // kv_b, h_all // kv_c
    mp, ms = bp * t, bs * ts
    hd = HEAD_DIM

    e_qa, e_ka = 0, h_a * hd
    e_va = e_ka + kv_a * hd
    e_qf = e_va + kv_a * hd
    e_kf = e_qf + h_b * hd
    e_vf = e_kf + kv_b * hd
    e_fg = e_vf + kv_b * hd
    o_kvc = h_all * hd
    o_kvs = o_kvc + 2 * kv_c * hd
    o_kvw = o_kvs + 2 * kv_c * hd
    o_gl = o_kvw + 2 * kv_c * hd
    assert 3 * h_all <= LANES and h_b <= LANES

    tn_e = _pick_tile(e_va, (512, 256, 128))
    tn_o = _pick_tile(kv_c * hd, (512, 256, 128))
    ne_pad = _round_up(e_fg + h_b, tn_e)
    no_pad = _round_up(o_gl + 3 * h_all, tn_o)
    tm_p = _pick_tile(mp, (1024, 512, 256, 128))
    tm_f = _pick_tile(mp, (512, 256, 128))
    tn_out = _pick_tile(d, (512, 256, 128))
    tf = _pick_tile(w_gate.shape[2], (256, 128))
    tq_e = _pick_tile(t, (512, 256, 128))
    tk_e = _pick_tile(t, (512, 256))
    tq_o = 128
    tk_o = _pick_tile(t, (512, 256, 128))
    n_pages = _pick_tile(npg, (8, 4, 2))

    xp = x_prompt.reshape(mp, d)
    xs = x_sample.reshape(ms, d)
    pos_p = jnp.tile(jnp.arange(t, dtype=jnp.int32), bp)
    pos_s = p0 + jnp.tile(jnp.arange(ts, dtype=jnp.int32), bs)
    tab_a_p, tab_a_s = _rope_tables(pos_p, DA // 2), _rope_tables(pos_s, DA // 2)
    tab_c_p, tab_c_s = _rope_tables(pos_p, hd // 2), _rope_tables(pos_s, hd // 2)
    tab_c_q = _rope_tables(jnp.arange(t, dtype=jnp.int32), hd // 2)
    tab_c_rows = _rope_tables(p0 + jnp.repeat(jnp.arange(ts, dtype=jnp.int32), g_c), hd // 2)

    outs = {name: ([], []) for name in ("a", "b", "f", "c", "s", "w")}

    for layer in range(depth):
        i = layer // 2
        if layer % 2 == 0:
            lam_init = 0.8 - 0.6 * float(np.exp(-0.3 * layer))
            w_in = _pad_cols(w_in_e[i], ne_pad).astype(BF16)
            w_out = w_out_e[i].astype(BF16)
            rope_e = ((0, e_va // tn_e),)
            lam_vec = jnp.stack([lam_q1[i], lam_k1[i], lam_q2[i], lam_k2[i]])
            bf_pad = jnp.pad(b_forget[i], (0, LANES - h_b)).reshape(1, LANES)

            z = _proj(xp, norm_mix_e[i], w_in, *tab_a_p, rope_e, DA // 2, tm_p, tn_e)
            z3 = z.reshape(bp, t, ne_pad)
            outs["a"][0].append(z3[:, :, e_ka:e_qf].reshape(bp, t, 2, kv_a, hd))
            outs["b"][0].append(z3[:, :, e_kf:e_fg].reshape(bp, t, 2, kv_b, hd))
            lf, dkc, dkr = _logf_prompt(z3, bf_pad, e_fg // LANES)
            outs["f"][0].append(lf[:, :, :h_b])
            o_a = _diff_prompt(z3, lam_vec, subln_g[i], kv_a, g_a, e_qa // (g_a * hd), e_ka // hd, e_va // hd,
                               lam_init, tq_e, tk_e)
            o_b = _fox_prompt(z3, dkr[:, :h_b].reshape(bp, h_b, 1, t), dkc, kv_b, g_b,
                              e_qf // (g_b * hd), e_kf // hd, e_vf // hd, tq_e, tk_e)
            xp = _mm_res([o_a.reshape(mp, h_a * hd), o_b.reshape(mp, h_b * hd)], w_out, xp, tm_p, tn_out)

            z = _proj(xs, norm_mix_e[i], w_in, *tab_a_s, rope_e, DA // 2, ms, tn_e)
            z3 = z.reshape(bs, ts, ne_pad)
            outs["a"][1].append(z3[:, :, e_ka:e_qf].reshape(bs, ts, 2, kv_a, hd))
            outs["b"][1].append(z3[:, :, e_kf:e_fg].reshape(bs, ts, 2, kv_b, hd))
            lf, dq, tot = _logf_sample(z[:, e_fg:e_fg + LANES], bf_pad, ts)
            outs["f"][1].append(lf[:, :h_b].reshape(bs, ts, h_b))
            o_a = _diff_sample(page_table, lam_vec, subln_g[i],
                               _head_rows(z3[:, :, e_qa:e_ka], kv_a, g_a).reshape(bs, -1, hd),
                               z3[:, :, e_ka:e_va].reshape(bs, ts * kv_a, hd),
                               z3[:, :, e_va:e_qf].reshape(bs, ts * kv_a, hd),
                               _tile_view(cache_a_kv[i]), page, kv_a, g_a, ts, lam_init, n_pages)
            rows_b = kv_b * ts * g_b
            dq3 = dq[:, :h_b].reshape(bs, ts, h_b)
            dq_b = jnp.broadcast_to(_head_rows(dq3, kv_b, g_b).reshape(bs, rows_b, 1), (bs, rows_b, LANES))
            dkn = jnp.swapaxes(dq3, 1, 2).reshape(bs, kv_b, 1, g_b, ts, 1)
            dkn = jnp.broadcast_to(dkn, (bs, kv_b, ts, g_b, ts, kv_b)).reshape(bs, rows_b, ts * kv_b)
            dkn_b = jnp.pad(dkn, ((0, 0), (0, 0), (0, LANES - ts * kv_b)))
            tot3 = tot[:, :h_b].reshape(bs, ts, h_b)[:, 0, :, None]
            o_b = _fox_sample(page_table, _head_rows(z3[:, :, e_qf:e_kf], kv_b, g_b).reshape(bs, -1, hd),
                              z3[:, :, e_kf:e_vf].reshape(bs, ts * kv_b, hd),
                              z3[:, :, e_vf:e_fg].reshape(bs, ts * kv_b, hd), dq_b, dkn_b, tot3,
                              _tile_view(cache_b_kv[i]), jnp.swapaxes(cache_b_logf[i], 1, 2),
                              page, kv_b, g_b, ts, n_pages)
            xs = _mm_res([_token_cols(o_a.reshape(bs, kv_a, ts * g_a, hd), ts, g_a),
                          _token_cols(o_b.reshape(bs, kv_b, ts * g_b, hd), ts, g_b)], w_out, xs, ms, tn_out)
        else:
            w_in = _pad_cols(w_in_o[i], no_pad).astype(BF16)
            w_out = w_out_o[i].astype(BF16)
            rope_o = ((o_kvs // tn_o, (o_kvs + kv_c * hd) // tn_o), (o_kvw // tn_o, (o_kvw + kv_c * hd) // tn_o))
            bg_pad = jnp.pad(b_gate[i], (0, LANES - 3 * h_all)).reshape(1, LANES)
            pe = jnp.stack([cmp_pe_k[i], cmp_pe_v[i]])
            w1 = jnp.stack([cmp_w1_k[i], cmp_w1_v[i]]).astype(BF16)
            w2 = jnp.stack([cmp_w2_k[i], cmp_w2_v[i]]).astype(BF16)

            z = _proj(xp, norm_mix_o[i], w_in, *tab_c_p, rope_o, hd // 2, tm_p, tn_o)
            z3 = z.reshape(bp, t, no_pad)
            outs["c"][0].append(z3[:, :, o_kvc:o_kvs].reshape(bp, t, 2, kv_c, hd))
            outs["s"][0].append(z3[:, :, o_kvs:o_kvw].reshape(bp, t, 2, kv_c, hd))
            keep = min(WINDOW, t)
            outs["w"][0].append(z3[:, t - keep:, o_kvw:o_gl].reshape(bp, keep, 2, kv_c, hd))
            kvc = _compress_prompt(z3, pe, w1, w2, kv_c, o_kvc // hd)
            o = _nsa_prompt(z3, *tab_c_q, kvc, bg_pad, kv_c, g_c, o_kvs // hd, o_kvs // hd + kv_c,
                            o_kvw // hd, o_kvw // hd + kv_c, o_gl // LANES, tq_o, tk_o)
            xp = _mm_res([o.reshape(mp, d)], w_out, xp, tm_p, tn_out)

            z = _proj(xs, norm_mix_o[i], w_in, *tab_c_s, rope_o, hd // 2, ms, tn_o)
            z3 = z.reshape(bs, ts, no_pad)
            outs["c"][1].append(z3[:, :, o_kvc:o_kvs].reshape(bs, ts, 2, kv_c, hd))
            outs["s"][1].append(z3[:, :, o_kvs:o_kvw].reshape(bs, ts, 2, kv_c, hd))
            win = state_win_kv[i]
            kvw_new = z3[:, :, o_kvw:o_gl]
            win_all = jnp.concatenate([win, kvw_new.reshape(bs, ts, 2, kv_c, hd)], axis=1)
            outs["w"][1].append(win_all[:, -min(WINDOW, win.shape[1] + ts):])
            kvc = _compress_sample(page_table, cache_cmp_kv[i].reshape(-1, hd), pe, w1, w2, page, kv_c, min(16, npg))
            ns = -(-(p0 + ts) // SLC_BLOCK)
            o_cmp, o_win, sel, qr = _nsa_sample_a(_head_rows(z3[:, :, :o_kvc], kv_c, g_c), *tab_c_rows, kvc,
                                                  win.reshape(bs, -1, hd), kvw_new, kv_c, g_c, ts, p0, ns)
            rows = kv_c * ts * g_c
            gl = z3[:, :, o_gl:o_gl + 3 * h_all].reshape(bs, ts, 3, kv_c, g_c).transpose(0, 3, 1, 4, 2)
            gl_rows = jnp.pad(gl.reshape(bs, rows, 3), ((0, 0), (0, 0), (0, LANES - 3)))
            bg = jnp.broadcast_to(b_gate[i].reshape(1, 3, kv_c, g_c), (ts, 3, kv_c, g_c)).transpose(2, 0, 3, 1)
            bg_rows = jnp.pad(bg.reshape(rows, 3), ((0, 0), (0, LANES - 3)))
            o = _nsa_sample_b(page_table, qr.reshape(bs, rows, hd), sel.reshape(bs, rows, -1),
                              z3[:, :, o_kvs:o_kvw].reshape(bs, ts * 2 * kv_c, hd),
                              o_cmp.reshape(bs, rows, hd), o_win.reshape(bs, rows, hd), gl_rows, bg_rows,
                              _tile_view(cache_slc_kv[i]), page, kv_c, g_c, ts, n_pages)
            xs = _mm_res([_token_cols(o.reshape(bs, kv_c, ts * g_c, hd), ts, g_c).astype(BF16)], w_out, xs, ms, tn_out)

        g_final = norm_final if layer == depth - 1 else None
        wg, wu, wd = w_gate[layer].astype(BF16), w_up[layer].astype(BF16), w_down[layer].astype(BF16)
        xp = _ffn(xp, norm_ffn[layer], wg, wu, wd, g_final, tm_f, tf)
        xs = _ffn(xs, norm_ffn[layer], wg, wu, wd, g_final, ms, tf)

    def stack(name, which):
        return jnp.stack(outs[name][which])

    return (xp.reshape(bp, t, d), xs.reshape(bs, ts, d),
            stack("a", 0), stack("a", 1), stack("b", 0), stack("b", 1), stack("f", 0), stack("f", 1),
            stack("c", 0), stack("c", 1), stack("s", 0), stack("s", 1), stack("w", 0), stack("w", 1))
```

```python
import functools

import numpy as np
import jax
import jax.numpy as jnp
from jax import lax
from jax.experimental import pallas as pl
from jax.experimental.pallas import tpu as pltpu

F32 = jnp.float32
BF16 = jnp.bfloat16
HIGHEST = lax.Precision.HIGHEST

LANES = 128
SUBLANES = 8
VMEM_LIMIT = 56 * 1024 * 1024
VMEM_LIMIT_FFN = 63 * 1024 * 1024

HEAD_DIM = 128
DA = HEAD_DIM // 2
CMP_LEN = 32
CMP_STRIDE = 16
CMP_HID = 2 * HEAD_DIM
SLC_BLOCK = 64
N_SELECT = 16
WINDOW = 512
ROPE_THETA = 10000.0
NORM_EPS = 1e-6
LOG2E = 1.4426950408889634
NEG = -1e30
BIG = 1e30
LOWEST = -3e38


def _params(*sem):
    return pltpu.CompilerParams(dimension_semantics=sem, vmem_limit_bytes=VMEM_LIMIT)


def _dot(a, b):
    return jnp.dot(a, b, preferred_element_type=F32)


def _dot_nt(a, b):
    return lax.dot_general(a, b, (((1,), (1,)), ((), ())), preferred_element_type=F32)


def _dot_exact(a, b):
    return jnp.dot(a, b, precision=HIGHEST, preferred_element_type=F32)


def _dot_nt_exact(a, b):
    return lax.dot_general(a, b, (((1,), (1,)), ((), ())), precision=HIGHEST, preferred_element_type=F32)


def _iota(shape, axis):
    return lax.broadcasted_iota(jnp.int32, shape, axis)


def _round_up(n, m):
    return -(-n // m) * m


def _pick_tile(n, prefs):
    for p in prefs:
        if n % p == 0:
            return p
    return n


def _rope_lanes(a, cos, sin, half):
    if 2 * half == LANES:
        partner = pltpu.roll(a, half, axis=1)
    else:
        first = (_iota(a.shape, 1) & (2 * half - 1)) < half
        partner = jnp.where(first, pltpu.roll(a, LANES - half, axis=1), pltpu.roll(a, half, axis=1))
    return a * cos + partner * sin


def _rmsnorm_rows(x, g):
    ms = jnp.mean(x * x, axis=-1, keepdims=True)
    return x * lax.rsqrt(ms + NORM_EPS) * g


def _proj_kernel(x_ref, g_ref, w_ref, cos_ref, sin_ref, o_ref, xn_ref, *, rope_ranges, half):
    j = pl.program_id(1)

    @pl.when(j == 0)
    def _():
        xn_ref[...] = _rmsnorm_rows(x_ref[...], g_ref[...]).astype(BF16)

    o_ref[...] = _dot(xn_ref[...], w_ref[...])
    is_rope = functools.reduce(jnp.logical_or, [(j >= a) & (j < b) for a, b in rope_ranges])

    @pl.when(is_rope)
    def _():
        cos, sin = cos_ref[...], sin_ref[...]
        for c in range(o_ref.shape[1] // LANES):
            sl = slice(c * LANES, (c + 1) * LANES)
            o_ref[:, sl] = _rope_lanes(o_ref[:, sl], cos, sin, half)


def _proj(x, g, w, cos, sin, rope_ranges, half, tm, tn):
    m, d = x.shape
    n = w.shape[1]
    return pl.pallas_call(
        functools.partial(_proj_kernel, rope_ranges=rope_ranges, half=half),
        grid=(m // tm, n // tn),
        in_specs=[
            pl.BlockSpec((tm, d), lambda i, j: (i, 0), pipeline_mode=pl.Buffered(1)),
            pl.BlockSpec((1, d), lambda i, j: (0, 0)),
            pl.BlockSpec((d, tn), lambda i, j: (0, j)),
            pl.BlockSpec((tm, LANES), lambda i, j: (i, 0)),
            pl.BlockSpec((tm, LANES), lambda i, j: (i, 0)),
        ],
        out_specs=pl.BlockSpec((tm, tn), lambda i, j: (i, j)),
        out_shape=jax.ShapeDtypeStruct((m, n), F32),
        scratch_shapes=[pltpu.VMEM((tm, d), BF16)],
        compiler_params=_params("parallel", "arbitrary"),
    )(x, g.reshape(1, d), w, cos, sin)


def _mm_res_kernel(*refs, n_lhs):
    a_refs, (w_ref, r_ref, o_ref) = refs[:n_lhs], refs[n_lhs:]
    acc = r_ref[...]
    off = 0
    for a_ref in a_refs:
        kk = a_ref.shape[1]
        acc = acc + _dot(a_ref[...], w_ref[off:off + kk, :])
        off += kk
    o_ref[...] = acc


def _mm_res(lhs, w, res, tm, tn):
    m, n = res.shape
    k = w.shape[0]
    return pl.pallas_call(
        functools.partial(_mm_res_kernel, n_lhs=len(lhs)),
        grid=(m // tm, n // tn),
        in_specs=[pl.BlockSpec((tm, a.shape[1]), lambda i, j: (i, 0)) for a in lhs] + [
            pl.BlockSpec((k, tn), lambda i, j: (0, j)),
            pl.BlockSpec((tm, tn), lambda i, j: (i, j)),
        ],
        out_specs=pl.BlockSpec((tm, tn), lambda i, j: (i, j)),
        out_shape=jax.ShapeDtypeStruct((m, n), F32),
        compiler_params=_params("parallel", "arbitrary"),
    )(*lhs, w, res)


def _ffn_kernel(x_ref, g_ref, wg_ref, wu_ref, wd_ref, gf_ref, o_ref, xn_ref, h_ref, *, final_norm):
    f = pl.program_id(1)
    n_tiles = pl.num_programs(1) - 1

    def hidden():
        xn = xn_ref[...]
        gate = _dot(xn, wg_ref[...])
        up = _dot(xn, wu_ref[...])
        return (gate * jax.nn.sigmoid(gate) * up).astype(BF16)

    rc = min(256, x_ref.shape[0])
    chunks = [slice(r * rc, (r + 1) * rc) for r in range(x_ref.shape[0] // rc)]

    @pl.when(f == 0)
    def _():
        for rows in chunks:
            x = x_ref[rows, :]
            xn_ref[rows, :] = _rmsnorm_rows(x, g_ref[...]).astype(BF16)
            o_ref[rows, :] = x
        h_ref[0] = hidden()

    @pl.when((f > 0) & (f < n_tiles))
    def _():
        h_ref[f % 2] = hidden()
        o_ref[...] += _dot(h_ref[(f - 1) % 2], wd_ref[...])

    @pl.when(f == n_tiles)
    def _():
        o_ref[...] += _dot(h_ref[(f - 1) % 2], wd_ref[...])
        if final_norm:
            for rows in chunks:
                o_ref[rows, :] = _rmsnorm_rows(o_ref[rows, :], gf_ref[...])


def _ffn(x, g, wg, wu, wd, g_final, tm, tf):
    m, d = x.shape
    n_tiles = wg.shape[1] // tf
    final_norm = g_final is not None
    gf = (g_final if final_norm else g).reshape(1, d)
    return pl.pallas_call(
        functools.partial(_ffn_kernel, final_norm=final_norm),
        grid=(m // tm, n_tiles + 1),
        in_specs=[
            pl.BlockSpec((tm, d), lambda i, f: (i, 0), pipeline_mode=pl.Buffered(1)),
            pl.BlockSpec((1, d), lambda i, f: (0, 0)),
            pl.BlockSpec((d, tf), lambda i, f: (0, jnp.minimum(f, n_tiles - 1))),
            pl.BlockSpec((d, tf), lambda i, f: (0, jnp.minimum(f, n_tiles - 1))),
            pl.BlockSpec((tf, d), lambda i, f: (jnp.maximum(f - 1, 0), 0)),
            pl.BlockSpec((1, d), lambda i, f: (0, 0)),
        ],
        out_specs=pl.BlockSpec((tm, d), lambda i, f: (i, 0), pipeline_mode=pl.Buffered(1)),
        out_shape=jax.ShapeDtypeStruct((m, d), F32),
        scratch_shapes=[pltpu.VMEM((tm, d), BF16), pltpu.VMEM((2, tm, tf), BF16)],
        compiler_params=pltpu.CompilerParams(dimension_semantics=("parallel", "arbitrary"),
                                             vmem_limit_bytes=VMEM_LIMIT_FFN),
    )(x, g.reshape(1, d), wg, wu, wd, gf)


def _online_update(carry, s, v, shift=None):
    m, l, acc = carry
    smax = jnp.max(s, axis=-1, keepdims=True)
    if shift is not None:
        smax = smax - shift
    m_new = jnp.maximum(m, smax)
    alpha = jnp.exp2(m - m_new)
    p = jnp.exp2(s - (m_new if shift is None else m_new + shift))
    l = alpha * l + jnp.sum(p, axis=-1, keepdims=True)
    pb = p.astype(BF16)
    if isinstance(v, (list, tuple)):
        n = v[0].shape[0]
        pv = sum(_dot(pb[:, r * n:(r + 1) * n], x) for r, x in enumerate(v))
    else:
        pv = _dot(pb, v)
    return m_new, l, alpha * acc + pv


def _state_init(m_ref, l_ref, acc_ref):
    m_ref[...] = jnp.full(m_ref.shape, NEG, F32)
    l_ref[...] = jnp.zeros(l_ref.shape, F32)
    acc_ref[...] = jnp.zeros(acc_ref.shape, F32)


def _state_update(m_ref, l_ref, acc_ref, k, s, v, shift=None):
    m, l, acc = _online_update((m_ref[k], l_ref[k], acc_ref[k]), s, v, shift)
    m_ref[k], l_ref[k], acc_ref[k] = m, l, acc


def _fold_lanes(x, op):
    out = x[:, :LANES]
    for i in range(1, x.shape[1] // LANES):
        out = op(out, x[:, i * LANES:(i + 1) * LANES])
    return out


def _two_pass_attention(q, kb_ref, vb_ref, n_blocks, tk, s_ref, add_fn=None, tail_fn=None, shift=None):
    rows = q.shape[0]

    def scores(kb, masked):
        k0 = pl.multiple_of(kb * tk, tk)
        s = _dot_nt(q, kb_ref[pl.ds(k0, tk), :])
        if add_fn is not None:
            s = s + add_fn(k0)
        if masked and tail_fn is not None:
            s = s + tail_fn(k0)
        s_ref[kb] = s
        return _fold_lanes(s, jnp.maximum)

    mx = lax.fori_loop(0, n_blocks - 1, lambda kb, mx: jnp.maximum(mx, scores(kb, False)),
                       jnp.full((rows, LANES), NEG, F32))
    mx = jnp.maximum(mx, scores(n_blocks - 1, True))
    m = jnp.max(mx, axis=-1, keepdims=True)
    if shift is not None:
        m = (m - shift) + shift

    def accumulate(kb, carry):
        ls, acc = carry
        k0 = pl.multiple_of(kb * tk, tk)
        p = jnp.exp2(s_ref[kb] - m)
        return ls + _fold_lanes(p, jnp.add), acc + _dot(p.astype(BF16), vb_ref[pl.ds(k0, tk), :])

    ls, acc = lax.fori_loop(0, n_blocks, accumulate,
                            (jnp.zeros((rows, LANES), F32), jnp.zeros((rows, HEAD_DIM), F32)))
    return acc, jnp.sum(ls, axis=-1, keepdims=True)


def _masked_softmax(s, okf):
    ok = okf > 0.5
    sm = jnp.where(ok, s, NEG)
    m = jnp.max(sm, axis=-1, keepdims=True)
    e = jnp.where(ok, jnp.exp2(sm - m), 0.0)
    return e / jnp.maximum(jnp.sum(e, axis=-1, keepdims=True), 1e-30)


def _lane_pick(x, lane_idx):
    return jnp.sum(jnp.where(_iota(x.shape, 1) == lane_idx, x, 0.0), axis=-1, keepdims=True)


def _topk_mask(imp, n_sel):
    lane = _iota(imp.shape, 1).astype(F32)
    sel = jnp.zeros(imp.shape, F32)
    x = imp
    for _ in range(n_sel):
        m = jnp.max(x, axis=-1, keepdims=True)
        idx = jnp.min(jnp.where(x == m, lane, 1e9), axis=-1, keepdims=True)
        hit = lane == idx
        sel = jnp.where(hit, 1.0, sel)
        x = jnp.where(hit, LOWEST, x)
    return sel


def _overlap(ci, sj):
    ov = jnp.minimum(ci + CMP_LEN, sj + SLC_BLOCK) - jnp.maximum(ci, sj)
    return jnp.maximum(ov, 0).astype(F32) * (1.0 / CMP_LEN)


def _topk_mask_t(imp_t, n_sel, ns):
    rows = _round_up(ns, SUBLANES)
    x = imp_t[0:rows]
    j = _iota(x.shape, 0)
    rank = jnp.zeros(x.shape, F32)
    for jp in range(ns):
        a = x[jp:jp + 1, :]
        rank = rank + jnp.where((a > x) | ((a == x) & (jp < j)), 1.0, 0.0)
    sel = jnp.where((rank < n_sel) & (j < ns), 1.0, 0.0)
    return jnp.concatenate([sel, jnp.zeros((imp_t.shape[0] - rows, imp_t.shape[1]), F32)], axis=0)


def _block_importance(imp, qpos, ns, axis=1):
    j = _iota(imp.shape, axis)
    cur = qpos >> 6
    forced = (j == 0) | (j == cur) | (j == cur - 1)
    valid = (j << 6) <= qpos
    out = jnp.where(forced, BIG, jnp.where(valid, imp, NEG))
    return jnp.where(j < ns, out, LOWEST)


def _lam_value(lam_ref, lam_init):
    v = lam_ref[...]
    a = jnp.exp(jnp.sum(v[0:1] * v[1:2], axis=-1, keepdims=True))
    b = jnp.exp(jnp.sum(v[2:3] * v[3:4], axis=-1, keepdims=True))
    return a - b + lam_init


def _gelu_tanh(x):
    return 0.5 * x * (1.0 + jnp.tanh(0.7978845608028654 * (x + 0.044715 * (x * x * x))))


def _log_sigmoid(z):
    return jnp.minimum(z, 0.0) - jnp.log1p(jnp.exp(-jnp.abs(z)))


def _cast_kv_once(step, pairs):
    @pl.when(step == 0)
    def _():
        for src, dst in pairs:
            dst[...] = src[0].astype(BF16)


def _logf_prompt_kernel(z_ref, b_ref, lf_ref, dkc_ref, dkr_ref):
    t = z_ref.shape[1]
    lf = _log_sigmoid(z_ref[0] + b_ref[...])
    lf_ref[0] = lf
    upper = (_iota((LANES, LANES), 1) > _iota((LANES, LANES), 0)).astype(F32)
    carry = jnp.zeros((1, LANES), F32)
    for blk in reversed(range(t // LANES)):
        sl = slice(blk * LANES, (blk + 1) * LANES)
        x = lf[sl]
        d = _dot_exact(upper, x) + carry
        dkc_ref[0, sl, :] = d
        dkr_ref[0, :, sl] = d.T
        carry = carry + jnp.sum(x, axis=0, keepdims=True)


def _logf_prompt(z3, b_pad, col_blk):
    b, t, _ = z3.shape
    return pl.pallas_call(
        _logf_prompt_kernel,
        grid=(b,),
        in_specs=[pl.BlockSpec((1, t, LANES), lambda i: (i, 0, col_blk)),
                  pl.BlockSpec((1, LANES), lambda i: (0, 0))],
        out_specs=[pl.BlockSpec((1, t, LANES), lambda i: (i, 0, 0)),
                   pl.BlockSpec((1, t, LANES), lambda i: (i, 0, 0)),
                   pl.BlockSpec((1, LANES, t), lambda i: (i, 0, 0))],
        out_shape=[jax.ShapeDtypeStruct((b, t, LANES), F32),
                   jax.ShapeDtypeStruct((b, t, LANES), F32),
                   jax.ShapeDtypeStruct((b, LANES, t), F32)],
        compiler_params=_params("parallel"),
    )(z3, b_pad)


def _causal_tail(qpos, tk):
    return lambda k0: jnp.where(k0 + _iota((qpos.shape[0], tk), 1) <= qpos, 0.0, NEG)


def _diff_prompt_kernel(lam_ref, g_ref, q_ref, k_ref, v_ref, o_ref, kb_ref, vb_ref, s_ref, *, tq, gsz, lam_init, tk):
    qi = pl.program_id(2)
    _cast_kv_once(qi, ((k_ref, kb_ref), (v_ref, vb_ref)))
    lam = _lam_value(lam_ref, lam_init)
    lane = _iota((tq, HEAD_DIM), 1)
    qscale = DA ** -0.5 * LOG2E
    qs = [q_ref[0, :, g * HEAD_DIM:(g + 1) * HEAD_DIM] * qscale for g in range(gsz)]
    q = jnp.concatenate([jnp.where(lane < DA, x, 0.0) for x in qs] + [jnp.where(lane >= DA, x, 0.0) for x in qs],
                        axis=0).astype(BF16)
    qpos = qi * tq + (_iota((2 * gsz * tq, 1), 0) & (tq - 1))
    n_blocks = (qi * tq) // tk + 1
    acc, l = _two_pass_attention(q, kb_ref, vb_ref, n_blocks, tk, s_ref, tail_fn=_causal_tail(qpos, tk))
    o = acc / l
    half = gsz * tq
    o = o[:half] - lam * o[half:]
    o = (_rmsnorm_rows(o, g_ref[...]) * (1.0 - lam_init)).astype(BF16)
    for g in range(gsz):
        o_ref[0, :, g * HEAD_DIM:(g + 1) * HEAD_DIM] = o[g * tq:(g + 1) * tq]


def _diff_prompt(z3, lam_vec, subln_g, n_kv, gsz, q_blk0, k_blk0, v_blk0, lam_init, tq, tk):
    b, t, _ = z3.shape
    qw = gsz * HEAD_DIM
    assert tk % tq == 0 and tq & (tq - 1) == 0
    return pl.pallas_call(
        functools.partial(_diff_prompt_kernel, tq=tq, gsz=gsz, lam_init=lam_init, tk=tk),
        grid=(b, n_kv, t // tq),
        in_specs=[
            pl.BlockSpec((4, DA), lambda i, k, q: (0, 0)),
            pl.BlockSpec((1, HEAD_DIM), lambda i, k, q: (0, 0)),
            pl.BlockSpec((1, tq, qw), lambda i, k, q: (i, q, q_blk0 + k)),
            pl.BlockSpec((1, t, HEAD_DIM), lambda i, k, q: (i, 0, k_blk0 + k)),
            pl.BlockSpec((1, t, HEAD_DIM), lambda i, k, q: (i, 0, v_blk0 + k)),
        ],
        out_specs=pl.BlockSpec((1, tq, qw), lambda i, k, q: (i, q, k)),
        out_shape=jax.ShapeDtypeStruct((b, t, n_kv * qw), BF16),
        scratch_shapes=[pltpu.VMEM((t, HEAD_DIM), BF16), pltpu.VMEM((t, HEAD_DIM), BF16),
                        pltpu.VMEM((t // tk, 2 * gsz * tq, tk), F32)],
        compiler_params=_params("parallel", "parallel", "arbitrary"),
    )(lam_vec, subln_g.reshape(1, HEAD_DIM), z3, z3, z3)


def _fox_prompt_kernel(q_ref, k_ref, v_ref, dkr_ref, dkc_ref, o_ref, kb_ref, vb_ref, s_ref, *, tq, gsz, tk):
    kh = pl.program_id(1)
    qi = pl.program_id(2)
    _cast_kv_once(qi, ((k_ref, kb_ref), (v_ref, vb_ref)))
    qscale = HEAD_DIM ** -0.5 * LOG2E
    qpos = qi * tq + (_iota((gsz * tq, 1), 0) & (tq - 1))
    n_blocks = (qi * tq) // tk + 1
    q = jnp.concatenate([q_ref[0, :, g * HEAD_DIM:(g + 1) * HEAD_DIM] * qscale for g in range(gsz)],
                        axis=0).astype(BF16)
    dq = jnp.concatenate([_lane_pick(dkc_ref[0], kh * gsz + g) for g in range(gsz)], axis=0) * LOG2E

    def dk_fn(k0):
        return jnp.concatenate([jnp.broadcast_to(dkr_ref[0, g, :, pl.ds(k0, tk)] * LOG2E, (tq, tk))
                                for g in range(gsz)], axis=0)

    acc, l = _two_pass_attention(q, kb_ref, vb_ref, n_blocks, tk, s_ref, add_fn=dk_fn,
                                 tail_fn=_causal_tail(qpos, tk), shift=dq)
    o = (acc / l).astype(BF16)
    for g in range(gsz):
        o_ref[0, :, g * HEAD_DIM:(g + 1) * HEAD_DIM] = o[g * tq:(g + 1) * tq]


def _fox_prompt(z3, dkr4, dkc, n_kv, gsz, q_blk0, k_blk0, v_blk0, tq, tk):
    b, t, _ = z3.shape
    qw = gsz * HEAD_DIM
    assert tk % tq == 0
    return pl.pallas_call(
        functools.partial(_fox_prompt_kernel, tq=tq, gsz=gsz, tk=tk),
        grid=(b, n_kv, t // tq),
        in_specs=[
            pl.BlockSpec((1, tq, qw), lambda i, k, q: (i, q, q_blk0 + k)),
            pl.BlockSpec((1, t, HEAD_DIM), lambda i, k, q: (i, 0, k_blk0 + k)),
            pl.BlockSpec((1, t, HEAD_DIM), lambda i, k, q: (i, 0, v_blk0 + k)),
            pl.BlockSpec((1, gsz, 1, t), lambda i, k, q: (i, k, 0, 0)),
            pl.BlockSpec((1, tq, LANES), lambda i, k, q: (i, q, 0)),
        ],
        out_specs=pl.BlockSpec((1, tq, qw), lambda i, k, q: (i, q, k)),
        out_shape=jax.ShapeDtypeStruct((b, t, n_kv * qw), BF16),
        scratch_shapes=[pltpu.VMEM((t, HEAD_DIM), BF16), pltpu.VMEM((t, HEAD_DIM), BF16),
                        pltpu.VMEM((t // tk, gsz * tq, tk), F32)],
        compiler_params=_params("parallel", "parallel", "arbitrary"),
    )(z3, z3, z3, dkr4, dkc)


def _logf_sample_kernel(z_ref, b_ref, lf_ref, dq_ref, tot_ref, *, ts):
    m = z_ref.shape[0]
    lf = _log_sigmoid(z_ref[...] + b_ref[...])
    lf_ref[...] = lf
    i, j = _iota((m, m), 0), _iota((m, m), 1)
    same = (i // ts) == (j // ts)
    dq_ref[...] = _dot_exact((same & (j > i)).astype(F32), lf)
    tot_ref[...] = _dot_exact(same.astype(F32), lf)


def _logf_sample(z_fg, b_pad, ts):
    m = z_fg.shape[0]
    sds = jax.ShapeDtypeStruct((m, LANES), F32)
    return pl.pallas_call(
        functools.partial(_logf_sample_kernel, ts=ts),
        out_shape=[sds, sds, sds],
    )(z_fg, b_pad)


def _pad_rows_scratch(ref, x):
    ref[...] = jnp.zeros(ref.shape, ref.dtype)
    ref[0:x.shape[0], :] = x


def _page_specs(page, n_kv, n_pages, first_page):
    rows = page * 2 * n_kv
    return [pl.BlockSpec((rows, HEAD_DIM), lambda b, p, pt, r=r: (pt[b, first_page(b, p, pt) + r], 0))
            for r in range(n_pages)]


def _tile_view(cache):
    n_kv = cache.shape[3]
    assert 2 * n_kv in (SUBLANES, 2 * SUBLANES)
    return cache.reshape(-1, 2 * n_kv // SUBLANES, SUBLANES, HEAD_DIM)


def _tile_specs(page, tiles, n_pages, first_page):
    return [pl.BlockSpec((page, tiles, SUBLANES, HEAD_DIM),
                         lambda b, p, pt, r=r: (pt[b, first_page(b, p, pt) + r], 0, 0, 0))
            for r in range(n_pages)]


def _flat_kv(pg_refs):
    n = pg_refs[0].shape[0] * SUBLANES
    if pg_refs[0].shape[1] == 2:
        ks = [pg[:, 0].reshape(n, HEAD_DIM) for pg in pg_refs]
        vs = [pg[:, 1].reshape(n, HEAD_DIM) for pg in pg_refs]
        off = 0
    else:
        vs = [pg[:, 0].reshape(n, HEAD_DIM) for pg in pg_refs]
        ks = [pltpu.roll(v, SUBLANES // 2, axis=0) for v in vs]
        off = SUBLANES // 2
    return [k.astype(BF16) for k in ks], [v.astype(BF16) for v in vs], off


def _flat_scores(q, ks):
    return jnp.concatenate([_dot_nt(q, k) for k in ks], axis=1)


def _own_head_bias(row_head, n_cols):
    return jnp.where((_iota((row_head.shape[0], n_cols), 1) & (SUBLANES - 1)) == row_head, 0.0, NEG)


def _diff_sample_kernel(pt_ref, lam_ref, g_ref, q_ref, kn_ref, vn_ref, *refs, n_kv, gsz, ts, lam_init, n_pages):
    pg_refs = refs[:n_pages]
    o_ref, m_ref, l_ref, acc_ref, bias_ref, kpad_ref, vpad_ref = refs[n_pages:]
    p = pl.program_id(1)
    last = pl.num_programs(1) - 1
    r1 = n_kv * ts * gsz
    q = q_ref[0] * (DA ** -0.5 * LOG2E)
    lane = _iota((r1, HEAD_DIM), 1)
    qb = jnp.concatenate([jnp.where(lane < DA, q, 0.0), jnp.where(lane >= DA, q, 0.0)], axis=0).astype(BF16)
    rr = _iota((2 * r1, 1), 0) % r1
    row_head = rr // (ts * gsz)
    row_tok = (rr % (ts * gsz)) // gsz

    @pl.when(p == 0)
    def _():
        _state_init(m_ref, l_ref, acc_ref)

    @pl.when(p < last)
    def _():
        ks, vs, off = _flat_kv(pg_refs)

        @pl.when(p == 0)
        def _():
            bias_ref[...] = _own_head_bias(row_head + off, bias_ref.shape[1])

        _state_update(m_ref, l_ref, acc_ref, ..., _flat_scores(qb, ks) + bias_ref[...], vs)

    @pl.when(p == last)
    def _():
        _pad_rows_scratch(kpad_ref, kn_ref[0])
        _pad_rows_scratch(vpad_ref, vn_ref[0])
        s = _dot_nt(qb, kpad_ref[...].astype(BF16))
        c = _iota(s.shape, 1)
        ok = ((c % n_kv) == row_head) & ((c // n_kv) <= row_tok)
        _state_update(m_ref, l_ref, acc_ref, ..., jnp.where(ok, s, NEG), vpad_ref[...].astype(BF16))
        o = acc_ref[...] / l_ref[...]
        o = o[:r1] - _lam_value(lam_ref, lam_init) * o[r1:]
        o_ref[0] = (_rmsnorm_rows(o, g_ref[...]) * (1.0 - lam_init)).astype(BF16)


def _diff_sample(page_table, lam_vec, subln_g, q_rows, k_new, v_new, cache_tiles, page, n_kv, gsz, ts, lam_init, n_pages):
    bs, r1, _ = q_rows.shape
    nsteps = page_table.shape[1] // n_pages
    first = lambda b, p, pt: jnp.minimum(p, nsteps - 1) * n_pages
    row_spec = lambda n: pl.BlockSpec((1, n, HEAD_DIM), lambda b, p, pt: (b, 0, 0))
    assert ts * n_kv <= LANES
    grid_spec = pltpu.PrefetchScalarGridSpec(
        num_scalar_prefetch=1,
        grid=(bs, nsteps + 1),
        in_specs=[
            pl.BlockSpec((4, DA), lambda b, p, pt: (0, 0)),
            pl.BlockSpec((1, HEAD_DIM), lambda b, p, pt: (0, 0)),
            row_spec(r1), row_spec(ts * n_kv), row_spec(ts * n_kv),
        ] + _tile_specs(page, cache_tiles.shape[1], n_pages, first),
        out_specs=row_spec(r1),
        scratch_shapes=[pltpu.VMEM((2 * r1, 1), F32), pltpu.VMEM((2 * r1, 1), F32), pltpu.VMEM((2 * r1, HEAD_DIM), F32),
                        pltpu.VMEM((2 * r1, n_pages * page * SUBLANES), F32),
                        pltpu.VMEM((LANES, HEAD_DIM), F32), pltpu.VMEM((LANES, HEAD_DIM), F32)],
    )
    return pl.pallas_call(
        functools.partial(_diff_sample_kernel, n_kv=n_kv, gsz=gsz, ts=ts, lam_init=lam_init, n_pages=n_pages),
        grid_spec=grid_spec,
        out_shape=jax.ShapeDtypeStruct((bs, r1, HEAD_DIM), BF16),
        compiler_params=_params("parallel", "arbitrary"),
    )(page_table, lam_vec, subln_g.reshape(1, HEAD_DIM), q_rows, k_new, v_new, *([cache_tiles] * n_pages))


def _fox_sample_kernel(pt_ref, q_ref, kn_ref, vn_ref, dq_ref, dkn_ref, tot_ref, *refs, n_kv, gsz, ts, n_pages):
    pg_refs, lf_refs = refs[:n_pages], refs[n_pages:2 * n_pages]
    o_ref, m_ref, l_ref, acc_ref, car_ref, bias_ref, kpad_ref, vpad_ref = refs[2 * n_pages:]
    p = pl.program_id(1)
    page = lf_refs[0].shape[2]
    rows = n_kv * ts * gsz
    n_h = n_kv * gsz
    qb = (q_ref[0] * (HEAD_DIM ** -0.5 * LOG2E)).astype(BF16)
    rr = _iota((rows, 1), 0)
    row_head = rr // (ts * gsz)
    row_tok = (rr % (ts * gsz)) // gsz
    shift = dq_ref[0][:, 0:1] * LOG2E

    @pl.when(p == 0)
    def _():
        _state_init(m_ref, l_ref, acc_ref)
        car_ref[...] = tot_ref[0]
        _pad_rows_scratch(kpad_ref, kn_ref[0])
        _pad_rows_scratch(vpad_ref, vn_ref[0])
        s = _dot_nt(qb, kpad_ref[...].astype(BF16)) + dkn_ref[0] * LOG2E
        c = _iota(s.shape, 1)
        ok = ((c % n_kv) == row_head) & ((c // n_kv) <= row_tok)
        _state_update(m_ref, l_ref, acc_ref, ..., jnp.where(ok, s, NEG), vpad_ref[...].astype(BF16), shift=shift)

    @pl.when(p > 0)
    def _():
        ks, vs, off = _flat_kv(pg_refs)

        @pl.when(p == 1)
        def _():
            bias_ref[...] = _own_head_bias(row_head + off, bias_ref.shape[1])

        lower = (_iota((page, page), 0) > _iota((page, page), 1)).astype(F32)
        dks = [None] * n_pages
        for r in reversed(range(n_pages)):
            lft = lf_refs[r][0]
            dks[r] = _dot_exact(lft, lower) + car_ref[...]
            car_ref[...] = car_ref[...] + jnp.sum(lft, axis=1, keepdims=True)
        dk = jnp.concatenate(dks, axis=1)
        hi = dk.astype(BF16)
        rest = dk - hi.astype(F32)
        mid = rest.astype(BF16)
        terms = (hi, mid, (rest - mid.astype(F32)).astype(BF16))
        hh = _iota((rows, n_h), 0)
        pick = ((hh // (ts * gsz)) * gsz + hh % gsz == _iota((rows, n_h), 1)).astype(BF16)
        spread = ((_iota((page, page * SUBLANES), 1) >> 3) == _iota((page, page * SUBLANES), 0)).astype(BF16)
        on_rows = [_dot(pick, x).astype(BF16) for x in terms]
        pieces = jnp.concatenate([x[:, r * page:(r + 1) * page] for r in range(n_pages) for x in on_rows], axis=0)
        flat = _dot(pieces, spread)
        nt = len(terms)
        dkf = jnp.concatenate(
            [sum(flat[(r * nt + q) * rows:(r * nt + q + 1) * rows] for q in range(nt)) for r in range(n_pages)],
            axis=1) * LOG2E
        _state_update(m_ref, l_ref, acc_ref, ..., _flat_scores(qb, ks) + dkf + bias_ref[...], vs, shift=shift)

    @pl.when(p == pl.num_programs(1) - 1)
    def _():
        o_ref[0] = (acc_ref[...] / l_ref[...]).astype(BF16)


def _fox_sample(page_table, q_rows, k_new, v_new, dq_b, dkn_b, tot, cache_tiles, logf_t, page, n_kv, gsz, ts, n_pages):
    bs, rows, _ = q_rows.shape
    n_h = n_kv * gsz
    nsteps = page_table.shape[1] // n_pages
    first = lambda b, p, pt: (nsteps - jnp.maximum(p, 1)) * n_pages
    lf_specs = [pl.BlockSpec((1, n_h, page), lambda b, p, pt, r=r: (pt[b, first(b, p, pt) + r], 0, 0))
                for r in range(n_pages)]
    row_spec = lambda n: pl.BlockSpec((1, n, HEAD_DIM), lambda b, p, pt: (b, 0, 0))
    assert ts * n_kv <= LANES
    grid_spec = pltpu.PrefetchScalarGridSpec(
        num_scalar_prefetch=1,
        grid=(bs, nsteps + 1),
        in_specs=[
            row_spec(rows), row_spec(ts * n_kv), row_spec(ts * n_kv), row_spec(rows), row_spec(rows),
            pl.BlockSpec((1, n_h, 1), lambda b, p, pt: (b, 0, 0)),
        ] + _tile_specs(page, cache_tiles.shape[1], n_pages, first) + lf_specs,
        out_specs=row_spec(rows),
        scratch_shapes=[pltpu.VMEM((rows, 1), F32), pltpu.VMEM((rows, 1), F32), pltpu.VMEM((rows, HEAD_DIM), F32),
                        pltpu.VMEM((n_h, 1), F32), pltpu.VMEM((rows, n_pages * page * SUBLANES), F32),
                        pltpu.VMEM((LANES, HEAD_DIM), F32), pltpu.VMEM((LANES, HEAD_DIM), F32)],
    )
    return pl.pallas_call(
        functools.partial(_fox_sample_kernel, n_kv=n_kv, gsz=gsz, ts=ts, n_pages=n_pages),
        grid_spec=grid_spec,
        out_shape=jax.ShapeDtypeStruct((bs, rows, HEAD_DIM), BF16),
        compiler_params=_params("parallel", "arbitrary"),
    )(page_table, q_rows, k_new, v_new, dq_b, dkn_b, tot, *([cache_tiles] * n_pages), *([logf_t] * n_pages))


def _compress_mlp(acc_a, acc_b_next, w2):
    return _dot(_gelu_tanh(acc_a + acc_b_next).astype(BF16), w2).astype(BF16)


def _compress_prompt_kernel(z_ref, pe_ref, w1_ref, w2_ref, o_ref, *, nch):
    acc_a = jnp.zeros((nch, CMP_HID), F32)
    acc_b = jnp.zeros((nch, CMP_HID), F32)
    for l in range(CMP_STRIDE):
        x = z_ref[0, pl.ds(l, nch, stride=CMP_STRIDE), :]
        xa = (x + pe_ref[0, l:l + 1, :]).astype(BF16)
        xb = (x + pe_ref[0, CMP_STRIDE + l:CMP_STRIDE + l + 1, :]).astype(BF16)
        acc_a = acc_a + _dot(xa, w1_ref[0, l])
        acc_b = acc_b + _dot(xb, w1_ref[0, CMP_STRIDE + l])
    o_ref[0, 0, 0] = _compress_mlp(acc_a, pltpu.roll(acc_b, nch - 1, axis=0), w2_ref[0])


def _compress_prompt(z3, pe, w1, w2, n_kv, col_blk0):
    b, t, _ = z3.shape
    nch = t // CMP_STRIDE
    return pl.pallas_call(
        functools.partial(_compress_prompt_kernel, nch=nch),
        grid=(b, 2 * n_kv),
        in_specs=[
            pl.BlockSpec((1, t, HEAD_DIM), lambda i, c: (i, 0, col_blk0 + c)),
            pl.BlockSpec((1, CMP_LEN, HEAD_DIM), lambda i, c: (c // n_kv, 0, 0)),
            pl.BlockSpec((1, CMP_LEN, HEAD_DIM, CMP_HID), lambda i, c: (c // n_kv, 0, 0, 0)),
            pl.BlockSpec((1, CMP_HID, HEAD_DIM), lambda i, c: (c // n_kv, 0, 0)),
        ],
        out_specs=pl.BlockSpec((1, 1, 1, nch, HEAD_DIM), lambda i, c: (i, c // n_kv, c % n_kv, 0, 0)),
        out_shape=jax.ShapeDtypeStruct((b, 2, n_kv, nch, HEAD_DIM), BF16),
        compiler_params=_params("parallel", "arbitrary"),
    )(z3, pe, w1, w2)


def _nsa_prompt_kernel(q_ref, cos_ref, sin_ref, kc_ref, vc_ref, ks_ref, vs_ref, kw_ref, vw_ref, gl_ref, bg_ref,
                       o_ref, ksb_ref, vsb_ref, kwb_ref, vwb_ref, s_ref, *, tq, gsz, n_kv, nblk, ns, tk):
    kh = pl.program_id(1)
    qi = pl.program_id(2)
    _cast_kv_once(qi, ((ks_ref, ksb_ref), (vs_ref, vsb_ref), (kw_ref, kwb_ref), (vw_ref, vwb_ref)))
    t0 = qi * tq
    qscale = HEAD_DIM ** -0.5 * LOG2E
    cos, sin = cos_ref[...], sin_ref[...]
    qs = [q_ref[0, :, g * HEAD_DIM:(g + 1) * HEAD_DIM] * qscale for g in range(gsz)]
    q = jnp.concatenate(qs, axis=0).astype(BF16)
    qr = jnp.concatenate([_rope_lanes(x, cos, sin, HEAD_DIM // 2) for x in qs], axis=0).astype(BF16)
    qpos = t0 + _iota((tq, 1), 0)

    def tile_g(x):
        return jnp.concatenate([x] * gsz, axis=0)

    nch = kc_ref.shape[3]
    n_i = _iota((tq, nch), 1)
    ok_c = ((n_i * CMP_STRIDE + CMP_LEN - 1 <= qpos) & (n_i < nblk)).astype(F32)
    p = _masked_softmax(_dot_nt(q, kc_ref[0, 0, 0]), tile_g(ok_c))
    o_cmp = _dot(p.astype(BF16), vc_ref[0, 0, 0])
    psum = p[0:tq]
    for g in range(1, gsz):
        psum = psum + p[g * tq:(g + 1) * tq]
    ov_t = _overlap(_iota((LANES, nch), 1) * CMP_STRIDE, _iota((LANES, nch), 0) * SLC_BLOCK)
    imp_t = _block_importance(_dot_nt_exact(ov_t, psum), t0 + _iota((1, tq), 1), ns, axis=0)
    sel = _topk_mask_t(imp_t, min(N_SELECT, ns), ns).T.astype(BF16)

    def slc_mask(k0):
        expand = (((k0 + _iota((LANES, tk), 1)) >> 6) == _iota((LANES, tk), 0)).astype(BF16)
        selk = _dot(sel, expand)
        return tile_g(jnp.where((selk > 0.5) & (k0 + _iota((tq, tk), 1) <= qpos), 0.0, NEG))

    n_blocks = (t0 + tq + tk - 1) // tk
    acc, l = _two_pass_attention(qr, ksb_ref, vsb_ref, n_blocks, tk, s_ref, add_fn=slc_mask)
    o_slc = acc / l

    wlen = WINDOW + tq
    w0 = pl.multiple_of(jnp.maximum(t0 - WINDOW, 0), tq)
    dist = qpos - (w0 + _iota((tq, wlen), 1))
    sw = _dot_nt(qr, kwb_ref[pl.ds(w0, wlen), :]) + tile_g(jnp.where((dist >= 0) & (dist < WINDOW), 0.0, NEG))
    ew = jnp.exp2(sw - jnp.max(sw, axis=-1, keepdims=True))
    o_win = _dot(ew.astype(BF16), vwb_ref[pl.ds(w0, wlen), :]) / jnp.sum(ew, axis=-1, keepdims=True)

    gates = jax.nn.sigmoid(gl_ref[0] + bg_ref[...])
    n_h = n_kv * gsz
    for g in range(gsz):
        rows = slice(g * tq, (g + 1) * tq)
        col = kh * gsz + g
        o = (_lane_pick(gates, col) * o_cmp[rows] + _lane_pick(gates, n_h + col) * o_slc[rows]
             + _lane_pick(gates, 2 * n_h + col) * o_win[rows])
        o_ref[0, :, g * HEAD_DIM:(g + 1) * HEAD_DIM] = o.astype(BF16)


def _nsa_prompt(z3, cos, sin, kvc, bg_pad, n_kv, gsz, ks_blk, vs_blk, kw_blk, vw_blk, gl_blk, tq, tk):
    b, t, _ = z3.shape
    nch = kvc.shape[3]
    qw = gsz * HEAD_DIM
    ns = -(-t // SLC_BLOCK)
    assert t >= WINDOW + tq and t % tk == 0 and tk % SLC_BLOCK == 0 and ns <= LANES
    kv_spec = lambda blk0: pl.BlockSpec((1, t, HEAD_DIM), lambda i, k, q: (i, 0, blk0 + k))
    return pl.pallas_call(
        functools.partial(_nsa_prompt_kernel, tq=tq, gsz=gsz, n_kv=n_kv, nblk=nch - 1, ns=ns, tk=tk),
        grid=(b, n_kv, t // tq),
        in_specs=[
            pl.BlockSpec((1, tq, qw), lambda i, k, q: (i, q, k)),
            pl.BlockSpec((tq, LANES), lambda i, k, q: (q, 0)),
            pl.BlockSpec((tq, LANES), lambda i, k, q: (q, 0)),
            pl.BlockSpec((1, 1, 1, nch, HEAD_DIM), lambda i, k, q: (i, 0, k, 0, 0)),
            pl.BlockSpec((1, 1, 1, nch, HEAD_DIM), lambda i, k, q: (i, 1, k, 0, 0)),
            kv_spec(ks_blk), kv_spec(vs_blk), kv_spec(kw_blk), kv_spec(vw_blk),
            pl.BlockSpec((1, tq, LANES), lambda i, k, q: (i, q, gl_blk)),
            pl.BlockSpec((1, LANES), lambda i, k, q: (0, 0)),
        ],
        out_specs=pl.BlockSpec((1, tq, qw), lambda i, k, q: (i, q, k)),
        out_shape=jax.ShapeDtypeStruct((b, t, n_kv * qw), BF16),
        scratch_shapes=[pltpu.VMEM((t, HEAD_DIM), BF16) for _ in range(4)] + [
            pltpu.VMEM((t // tk, gsz * tq, tk), F32)],
        compiler_params=_params("parallel", "parallel", "arbitrary"),
    )(z3, cos, sin, kvc, kvc, z3, z3, z3, z3, z3, bg_pad)


def _compress_sample_kernel(pt_ref, *refs, n_kv, n_pages, page):
    pg_refs = refs[:n_pages]
    pe_ref, w1_ref, w2_ref, o_ref, last_ref = refs[n_pages:]
    s = pl.program_id(1)
    cpp = page // CMP_STRIDE
    m = n_pages * cpp
    rpr = 2 * n_kv

    @pl.when(s == 0)
    def _():
        last_ref[...] = jnp.zeros(last_ref.shape, F32)

    for typ in range(2):
        acc_a = jnp.zeros((n_kv * m, CMP_HID), F32)
        acc_b = jnp.zeros((n_kv * m, CMP_HID), F32)
        for l in range(0, CMP_STRIDE, 2):
            xs = [jnp.concatenate(
                [pg[pl.ds((l + i) * rpr + typ * n_kv + h, cpp, stride=CMP_STRIDE * rpr), :]
                 for h in range(n_kv) for pg in pg_refs], axis=0) for i in range(2)]
            for acc_is_b, off in ((False, l), (True, CMP_STRIDE + l)):
                lhs = jnp.concatenate([(xs[i] + pe_ref[typ, off + i:off + i + 1, :]).astype(BF16)
                                       for i in range(2)], axis=1)
                term = _dot(lhs, w1_ref[typ, off:off + 2].reshape(2 * HEAD_DIM, CMP_HID))
                if acc_is_b:
                    acc_b = acc_b + term
                else:
                    acc_a = acc_a + term
        first = _iota((m, 1), 0) == 0
        for h in range(n_kv):
            a = acc_a[h * m:(h + 1) * m]
            idx = typ * n_kv + h
            prev = jnp.where(first, last_ref[idx, SUBLANES - 1:SUBLANES, :], pltpu.roll(a, 1, axis=0))
            last_ref[idx] = a[m - SUBLANES:m]
            o_ref[0, typ, h] = _compress_mlp(prev, acc_b[h * m:(h + 1) * m], w2_ref[typ])


def _compress_sample(page_table, cache_rows, pe, w1, w2, page, n_kv, n_pages):
    bs, npg = page_table.shape
    m = n_pages * (page // CMP_STRIDE)
    grid_spec = pltpu.PrefetchScalarGridSpec(
        num_scalar_prefetch=1,
        grid=(bs, npg // n_pages),
        in_specs=_page_specs(page, n_kv, n_pages, lambda b, s, pt: s * n_pages) + [
            pl.BlockSpec((2, CMP_LEN, HEAD_DIM), lambda b, s, pt: (0, 0, 0)),
            pl.BlockSpec((2, CMP_LEN, HEAD_DIM, CMP_HID), lambda b, s, pt: (0, 0, 0, 0)),
            pl.BlockSpec((2, CMP_HID, HEAD_DIM), lambda b, s, pt: (0, 0, 0)),
        ],
        out_specs=pl.BlockSpec((1, 2, n_kv, m, HEAD_DIM), lambda b, s, pt: (b, 0, 0, s, 0)),
        scratch_shapes=[pltpu.VMEM((2 * n_kv, SUBLANES, CMP_HID), F32)],
    )
    return pl.pallas_call(
        functools.partial(_compress_sample_kernel, n_kv=n_kv, n_pages=n_pages, page=page),
        grid_spec=grid_spec,
        out_shape=jax.ShapeDtypeStruct((bs, 2, n_kv, npg * (page // CMP_STRIDE), HEAD_DIM), BF16),
        compiler_params=_params("parallel", "arbitrary"),
    )(page_table, *([cache_rows] * n_pages), pe, w1, w2)


def _nsa_sample_a_kernel(q_ref, cos_ref, sin_ref, kc_ref, vc_ref, win_ref, kwn_ref,
                         ocmp_ref, owin_ref, sel_ref, qr_ref, kpad_ref, vpad_ref, *, n_kv, gsz, ts, p0, ns):
    qscale = HEAD_DIM ** -0.5 * LOG2E
    rows = ts * gsz
    nr = kc_ref.shape[3]
    nsp = sel_ref.shape[3]
    wrows = win_ref.shape[1] // (2 * n_kv)
    tok = _iota((rows, 1), 0) // gsz
    qpos = p0 + tok
    r_i = _iota((rows, nr), 1)
    ok_c = ((r_i >= 1) & ((r_i - 1) * CMP_STRIDE + CMP_LEN - 1 <= qpos)).astype(F32)
    ov = _overlap((_iota((nr, nsp), 0) - 1) * CMP_STRIDE, _iota((nr, nsp), 1) * SLC_BLOCK)
    group = ((_iota((rows, rows), 0) // gsz) == (_iota((rows, rows), 1) // gsz)).astype(F32)
    ok_w1 = _iota((rows, wrows), 1) > tok + (wrows - WINDOW)
    ok_w2 = _iota((rows, kpad_ref.shape[0]), 1) <= tok
    kvw = n_kv * HEAD_DIM
    for kh in range(n_kv):
        q = q_ref[0, kh]
        qr = _rope_lanes(q, cos_ref[...], sin_ref[...], HEAD_DIM // 2)
        qr_ref[0, kh] = qr
        qb, qrb = (q * qscale).astype(BF16), (qr * qscale).astype(BF16)
        p = _masked_softmax(_dot_nt(qb, kc_ref[0, 0, kh]), ok_c)
        ocmp_ref[0, kh] = _dot(p.astype(BF16), vc_ref[0, 0, kh])
        imp = _dot_exact(group, _dot_exact(p, ov))
        sel_ref[0, kh] = _topk_mask(_block_importance(imp, qpos, ns), min(N_SELECT, ns))
        sl = slice(kh * HEAD_DIM, (kh + 1) * HEAD_DIM)
        slv = slice(kvw + kh * HEAD_DIM, kvw + (kh + 1) * HEAD_DIM)
        _pad_rows_scratch(kpad_ref, kwn_ref[0, :, sl])
        _pad_rows_scratch(vpad_ref, kwn_ref[0, :, slv])
        k_win = win_ref[0, pl.ds(kh, wrows, stride=2 * n_kv), :].astype(BF16)
        v_win = win_ref[0, pl.ds(n_kv + kh, wrows, stride=2 * n_kv), :].astype(BF16)
        s1 = jnp.where(ok_w1, _dot_nt(qrb, k_win), NEG)
        s2 = jnp.where(ok_w2, _dot_nt(qrb, kpad_ref[...].astype(BF16)), NEG)
        m = jnp.maximum(jnp.max(s1, axis=-1, keepdims=True), jnp.max(s2, axis=-1, keepdims=True))
        e1, e2 = jnp.exp2(s1 - m), jnp.exp2(s2 - m)
        l = jnp.sum(e1, axis=-1, keepdims=True) + jnp.sum(e2, axis=-1, keepdims=True)
        o = _dot(e1.astype(BF16), v_win) + _dot(e2.astype(BF16), vpad_ref[...].astype(BF16))
        owin_ref[0, kh] = o / l


def _nsa_sample_a(q_rows, cos, sin, kvc, win_rows, kvw_new, n_kv, gsz, ts, p0, ns):
    bs = q_rows.shape[0]
    rows = ts * gsz
    nr = kvc.shape[3]
    nsp = _round_up(ns, LANES)
    wr, w = win_rows.shape[1], kvw_new.shape[2]
    assert wr == WINDOW * 2 * n_kv
    o_sds = jax.ShapeDtypeStruct((bs, n_kv, rows, HEAD_DIM), F32)
    o_spec = pl.BlockSpec((1, n_kv, rows, HEAD_DIM), lambda b: (b, 0, 0, 0))
    return pl.pallas_call(
        functools.partial(_nsa_sample_a_kernel, n_kv=n_kv, gsz=gsz, ts=ts, p0=p0, ns=ns),
        grid=(bs,),
        in_specs=[
            o_spec,
            pl.BlockSpec((rows, LANES), lambda b: (0, 0)),
            pl.BlockSpec((rows, LANES), lambda b: (0, 0)),
            pl.BlockSpec((1, 1, n_kv, nr, HEAD_DIM), lambda b: (b, 0, 0, 0, 0)),
            pl.BlockSpec((1, 1, n_kv, nr, HEAD_DIM), lambda b: (b, 1, 0, 0, 0)),
            pl.BlockSpec((1, wr, HEAD_DIM), lambda b: (b, 0, 0)),
            pl.BlockSpec((1, ts, w), lambda b: (b, 0, 0)),
        ],
        out_specs=[o_spec, o_spec, pl.BlockSpec((1, n_kv, rows, nsp), lambda b: (b, 0, 0, 0)), o_spec],
        out_shape=[o_sds, o_sds, jax.ShapeDtypeStruct((bs, n_kv, rows, nsp), F32), o_sds],
        scratch_shapes=[pltpu.VMEM((LANES, HEAD_DIM), F32), pltpu.VMEM((LANES, HEAD_DIM), F32)],
        compiler_params=_params("parallel"),
    )(q_rows, cos, sin, kvc, kvc, win_rows, kvw_new)


def _nsa_sample_b_kernel(pt_ref, q_ref, sel_ref, kn_ref, ocmp_ref, owin_ref, gl_ref, bg_ref, *refs,
                         n_kv, gsz, ts, n_pages):
    pg_refs = refs[:n_pages]
    o_ref, m_ref, l_ref, acc_ref, bias_ref, npad_ref = refs[n_pages:]
    p = pl.program_id(1)
    last = pl.num_programs(1) - 1
    rows = n_kv * ts * gsz
    page = pg_refs[0].shape[0]
    nsp = sel_ref.shape[2]
    bps = n_pages * (page // SLC_BLOCK)
    blk_cols = SLC_BLOCK * SUBLANES
    qb = (q_ref[0] * (HEAD_DIM ** -0.5 * LOG2E)).astype(BF16)
    rr = _iota((rows, 1), 0)
    row_head = rr // (ts * gsz)
    row_tok = (rr % (ts * gsz)) // gsz

    @pl.when(p == 0)
    def _():
        _state_init(m_ref, l_ref, acc_ref)

    @pl.when(p < last)
    def _():
        ks, vs, off = _flat_kv(pg_refs)

        @pl.when(p == 0)
        def _():
            bias_ref[...] = _own_head_bias(row_head + off, bias_ref.shape[1])

        shift = (_iota((nsp, LANES), 0) == p * bps + _iota((nsp, LANES), 1)).astype(BF16)
        flags = jnp.where(_dot(sel_ref[0].astype(BF16), shift) > 0.5, 0.0, NEG)
        sel_bias = jnp.concatenate([jnp.broadcast_to(flags[:, i:i + 1], (rows, blk_cols)) for i in range(bps)], axis=1)
        _state_update(m_ref, l_ref, acc_ref, ..., _flat_scores(qb, ks) + bias_ref[...] + sel_bias, vs)

    @pl.when(p == last)
    def _():
        _pad_rows_scratch(npad_ref, kn_ref[0])
        flat = npad_ref[...]
        s = _dot_nt(qb, pltpu.roll(flat, n_kv, axis=0).astype(BF16))
        c = _iota(s.shape, 1)
        ok = (((c % (2 * n_kv)) == n_kv + row_head) & ((c // (2 * n_kv)) <= row_tok)
              & (_lane_pick(sel_ref[0], last * bps) > 0.5))
        _state_update(m_ref, l_ref, acc_ref, ..., jnp.where(ok, s, NEG), flat.astype(BF16))
        gates = jax.nn.sigmoid(gl_ref[0] + bg_ref[...])
        o_ref[0] = (gates[:, 0:1] * ocmp_ref[0] + gates[:, 1:2] * (acc_ref[...] / l_ref[...])
                    + gates[:, 2:3] * owin_ref[0])


def _nsa_sample_b(page_table, q_rows, sel, kvs_new, o_cmp, o_win, gl_rows, bg_rows, cache_tiles,
                  page, n_kv, gsz, ts, n_pages):
    bs, rows, _ = q_rows.shape
    nsp = sel.shape[2]
    nsteps = page_table.shape[1] // n_pages
    first = lambda b, p, pt: jnp.minimum(p, nsteps - 1) * n_pages
    row_spec = lambda n, width: pl.BlockSpec((1, n, width), lambda b, p, pt: (b, 0, 0))
    assert ts * 2 * n_kv <= LANES and n_pages * (page // SLC_BLOCK) <= LANES
    grid_spec = pltpu.PrefetchScalarGridSpec(
        num_scalar_prefetch=1,
        grid=(bs, nsteps + 1),
        in_specs=[
            row_spec(rows, HEAD_DIM), row_spec(rows, nsp), row_spec(ts * 2 * n_kv, HEAD_DIM),
            row_spec(rows, HEAD_DIM), row_spec(rows, HEAD_DIM), row_spec(rows, LANES),
            pl.BlockSpec((rows, LANES), lambda b, p, pt: (0, 0)),
        ] + _tile_specs(page, cache_tiles.shape[1], n_pages, first),
        out_specs=row_spec(rows, HEAD_DIM),
        scratch_shapes=[pltpu.VMEM((rows, 1), F32), pltpu.VMEM((rows, 1), F32), pltpu.VMEM((rows, HEAD_DIM), F32),
                        pltpu.VMEM((rows, n_pages * page * SUBLANES), F32), pltpu.VMEM((LANES, HEAD_DIM), F32)],
    )
    return pl.pallas_call(
        functools.partial(_nsa_sample_b_kernel, n_kv=n_kv, gsz=gsz, ts=ts, n_pages=n_pages),
        grid_spec=grid_spec,
        out_shape=jax.ShapeDtypeStruct((bs, rows, HEAD_DIM), F32),
        compiler_params=_params("parallel", "arbitrary"),
    )(page_table, q_rows, sel, kvs_new, o_cmp, o_win, gl_rows, bg_rows, *([cache_tiles] * n_pages))


def _rope_tables(pos, half):
    inv = ROPE_THETA ** (-jnp.arange(half, dtype=F32) / half)
    ang = pos.astype(F32)[:, None] * inv
    cos, sin = jnp.cos(ang), jnp.sin(ang)
    reps = LANES // (2 * half)
    return (jnp.tile(jnp.concatenate([cos, cos], axis=-1), (1, reps)),
            jnp.tile(jnp.concatenate([-sin, sin], axis=-1), (1, reps)))


def _pad_cols(w, n):
    return jnp.pad(w, ((0, 0), (0, n - w.shape[1])))


def _head_rows(x, n_kv, gsz):
    b, t, _ = x.shape
    return x.reshape(b, t, n_kv, gsz, -1).transpose(0, 2, 1, 3, 4).reshape(b, n_kv, t * gsz, -1)


def _token_cols(x, ts, gsz):
    b, n_kv, _, f = x.shape
    return x.reshape(b, n_kv, ts, gsz, f).transpose(0, 2, 1, 3, 4).reshape(b * ts, n_kv * gsz * f)


def kernel(x_prompt, x_sample, cache_a_kv, cache_b_kv, cache_b_logf, cache_cmp_kv, cache_slc_kv, state_win_kv, page_table, norm_mix_e, w_in_e, b_forget, lam_q1, lam_k1, lam_q2, lam_k2, subln_g, w_out_e, norm_mix_o, w_in_o, b_gate, cmp_pe_k, cmp_w1_k, cmp_w2_k, cmp_pe_v, cmp_w1_v, cmp_w2_v, w_out_o, norm_ffn, w_gate, w_up, w_down, norm_final):
    bp, t, d = x_prompt.shape
    bs, ts, _ = x_sample.shape
    n_even, n_odd = w_in_e.shape[0], w_in_o.shape[0]
    depth = n_even + n_odd
    page = cache_a_kv.shape[2]
    npg = page_table.shape[1]
    p0 = npg * page
    h_all = d // HEAD_DIM
    h_b = cache_b_logf.shape[-1]
    h_a = h_all - h_b
    kv_a, kv_b, kv_c = cache_a_kv.shape[4], cache_b_kv.shape[4], cache_cmp_kv.shape[4]
    g_a, g_b, g_c = h_a // kv_a, h_b // kv_b, h_all // kv_c
    mp, ms = bp * t, bs * ts
    hd = HEAD_DIM

    e_qa, e_ka = 0, h_a * hd
    e_va = e_ka + kv_a * hd
    e_qf = e_va + kv_a * hd
    e_kf = e_qf + h_b * hd
    e_vf = e_kf + kv_b * hd
    e_fg = e_vf + kv_b * hd
    o_kvc = h_all * hd
    o_kvs = o_kvc + 2 * kv_c * hd
    o_kvw = o_kvs + 2 * kv_c * hd
    o_gl = o_kvw + 2 * kv_c * hd
    assert 3 * h_all <= LANES and h_b <= LANES

    tn_e = _pick_tile(e_va, (512, 256, 128))
    tn_o = _pick_tile(kv_c * hd, (512, 256, 128))
    ne_pad = _round_up(e_fg + h_b, tn_e)
    no_pad = _round_up(o_gl + 3 * h_all, tn_o)
    tm_p = _pick_tile(mp, (1024, 512, 256, 128))
    tm_f = _pick_tile(mp, (1024, 512, 256, 128))
    tn_out = _pick_tile(d, (1024, 512, 256, 128))
    tf = _pick_tile(w_gate.shape[2], (256, 128))
    tq_e = _pick_tile(t, (512, 256, 128))
    tk_e = _pick_tile(t, (512, 256))
    tq_o = 128
    tk_o = _pick_tile(t, (512, 256, 128))
    n_pages = _pick_tile(npg, (8, 4, 2))

    xp = x_prompt.reshape(mp, d)
    xs = x_sample.reshape(ms, d)
    pos_p = jnp.tile(jnp.arange(t, dtype=jnp.int32), bp)
    pos_s = p0 + jnp.tile(jnp.arange(ts, dtype=jnp.int32), bs)
    tab_a_p, tab_a_s = _rope_tables(pos_p, DA // 2), _rope_tables(pos_s, DA // 2)
    tab_c_p, tab_c_s = _rope_tables(pos_p, hd // 2), _rope_tables(pos_s, hd // 2)
    tab_c_q = _rope_tables(jnp.arange(t, dtype=jnp.int32), hd // 2)
    tab_c_rows = _rope_tables(p0 + jnp.repeat(jnp.arange(ts, dtype=jnp.int32), g_c), hd // 2)

    outs = {name: ([], []) for name in ("a", "b", "f", "c", "s", "w")}

    for layer in range(depth):
        i = layer // 2
        if layer % 2 == 0:
            lam_init = 0.8 - 0.6 * float(np.exp(-0.3 * layer))
            w_in = _pad_cols(w_in_e[i], ne_pad).astype(BF16)
            w_out = w_out_e[i].astype(BF16)
            rope_e = ((0, e_va // tn_e),)
            lam_vec = jnp.stack([lam_q1[i], lam_k1[i], lam_q2[i], lam_k2[i]])
            bf_pad = jnp.pad(b_forget[i], (0, LANES - h_b)).reshape(1, LANES)

            z = _proj(xp, norm_mix_e[i], w_in, *tab_a_p, rope_e, DA // 2, tm_p, tn_e)
            z3 = z.reshape(bp, t, ne_pad)
            outs["a"][0].append(z3[:, :, e_ka:e_qf].reshape(bp, t, 2, kv_a, hd))
            outs["b"][0].append(z3[:, :, e_kf:e_fg].reshape(bp, t, 2, kv_b, hd))
            lf, dkc, dkr = _logf_prompt(z3, bf_pad, e_fg // LANES)
            outs["f"][0].append(lf[:, :, :h_b])
            o_a = _diff_prompt(z3, lam_vec, subln_g[i], kv_a, g_a, e_qa // (g_a * hd), e_ka // hd, e_va // hd,
                               lam_init, tq_e, tk_e)
            o_b = _fox_prompt(z3, dkr[:, :h_b].reshape(bp, h_b, 1, t), dkc, kv_b, g_b,
                              e_qf // (g_b * hd), e_kf // hd, e_vf // hd, tq_e, tk_e)
            xp = _mm_res([o_a.reshape(mp, h_a * hd), o_b.reshape(mp, h_b * hd)], w_out, xp, tm_p, tn_out)

            z = _proj(xs, norm_mix_e[i], w_in, *tab_a_s, rope_e, DA // 2, ms, tn_e)
            z3 = z.reshape(bs, ts, ne_pad)
            outs["a"][1].append(z3[:, :, e_ka:e_qf].reshape(bs, ts, 2, kv_a, hd))
            outs["b"][1].append(z3[:, :, e_kf:e_fg].reshape(bs, ts, 2, kv_b, hd))
            lf, dq, tot = _logf_sample(z[:, e_fg:e_fg + LANES], bf_pad, ts)
            outs["f"][1].append(lf[:, :h_b].reshape(bs, ts, h_b))
            o_a = _diff_sample(page_table, lam_vec, subln_g[i],
                               _head_rows(z3[:, :, e_qa:e_ka], kv_a, g_a).reshape(bs, -1, hd),
                               z3[:, :, e_ka:e_va].reshape(bs, ts * kv_a, hd),
                               z3[:, :, e_va:e_qf].reshape(bs, ts * kv_a, hd),
                               _tile_view(cache_a_kv[i]), page, kv_a, g_a, ts, lam_init, n_pages)
            rows_b = kv_b * ts * g_b
            dq3 = dq[:, :h_b].reshape(bs, ts, h_b)
            dq_b = jnp.broadcast_to(_head_rows(dq3, kv_b, g_b).reshape(bs, rows_b, 1), (bs, rows_b, LANES))
            dkn = jnp.swapaxes(dq3, 1, 2).reshape(bs, kv_b, 1, g_b, ts, 1)
            dkn = jnp.broadcast_to(dkn, (bs, kv_b, ts, g_b, ts, kv_b)).reshape(bs, rows_b, ts * kv_b)
            dkn_b = jnp.pad(dkn, ((0, 0), (0, 0), (0, LANES - ts * kv_b)))
            tot3 = tot[:, :h_b].reshape(bs, ts, h_b)[:, 0, :, None]
            o_b = _fox_sample(page_table, _head_rows(z3[:, :, e_qf:e_kf], kv_b, g_b).reshape(bs, -1, hd),
                              z3[:, :, e_kf:e_vf].reshape(bs, ts * kv_b, hd),
                              z3[:, :, e_vf:e_fg].reshape(bs, ts * kv_b, hd), dq_b, dkn_b, tot3,
                              _tile_view(cache_b_kv[i]), jnp.swapaxes(cache_b_logf[i], 1, 2),
                              page, kv_b, g_b, ts, n_pages)
            xs = _mm_res([_token_cols(o_a.reshape(bs, kv_a, ts * g_a, hd), ts, g_a),
                          _token_cols(o_b.reshape(bs, kv_b, ts * g_b, hd), ts, g_b)], w_out, xs, ms, tn_out)
        else:
            w_in = _pad_cols(w_in_o[i], no_pad).astype(BF16)
            w_out = w_out_o[i].astype(BF16)
            rope_o = ((o_kvs // tn_o, (o_kvs + kv_c * hd) // tn_o), (o_kvw // tn_o, (o_kvw + kv_c * hd) // tn_o))
            bg_pad = jnp.pad(b_gate[i], (0, LANES - 3 * h_all)).reshape(1, LANES)
            pe = jnp.stack([cmp_pe_k[i], cmp_pe_v[i]])
            w1 = jnp.stack([cmp_w1_k[i], cmp_w1_v[i]]).astype(BF16)
            w2 = jnp.stack([cmp_w2_k[i], cmp_w2_v[i]]).astype(BF16)

            z = _proj(xp, norm_mix_o[i], w_in, *tab_c_p, rope_o, hd // 2, tm_p, tn_o)
            z3 = z.reshape(bp, t, no_pad)
            outs["c"][0].append(z3[:, :, o_kvc:o_kvs].reshape(bp, t, 2, kv_c, hd))
            outs["s"][0].append(z3[:, :, o_kvs:o_kvw].reshape(bp, t, 2, kv_c, hd))
            keep = min(WINDOW, t)
            outs["w"][0].append(z3[:, t - keep:, o_kvw:o_gl].reshape(bp, keep, 2, kv_c, hd))
            kvc = _compress_prompt(z3, pe, w1, w2, kv_c, o_kvc // hd)
            o = _nsa_prompt(z3, *tab_c_q, kvc, bg_pad, kv_c, g_c, o_kvs // hd, o_kvs // hd + kv_c,
                            o_kvw // hd, o_kvw // hd + kv_c, o_gl // LANES, tq_o, tk_o)
            xp = _mm_res([o.reshape(mp, d)], w_out, xp, tm_p, tn_out)

            z = _proj(xs, norm_mix_o[i], w_in, *tab_c_s, rope_o, hd // 2, ms, tn_o)
            z3 = z.reshape(bs, ts, no_pad)
            outs["c"][1].append(z3[:, :, o_kvc:o_kvs].reshape(bs, ts, 2, kv_c, hd))
            outs["s"][1].append(z3[:, :, o_kvs:o_kvw].reshape(bs, ts, 2, kv_c, hd))
            win = state_win_kv[i]
            kvw_new = z3[:, :, o_kvw:o_gl]
            win_all = jnp.concatenate([win, kvw_new.reshape(bs, ts, 2, kv_c, hd)], axis=1)
            outs["w"][1].append(win_all[:, -min(WINDOW, win.shape[1] + ts):])
            kvc = _compress_sample(page_table, cache_cmp_kv[i].reshape(-1, hd), pe, w1, w2, page, kv_c, min(16, npg))
            ns = -(-(p0 + ts) // SLC_BLOCK)
            o_cmp, o_win, sel, qr = _nsa_sample_a(_head_rows(z3[:, :, :o_kvc], kv_c, g_c), *tab_c_rows, kvc,
                                                  win.reshape(bs, -1, hd), kvw_new, kv_c, g_c, ts, p0, ns)
            rows = kv_c * ts * g_c
            gl = z3[:, :, o_gl:o_gl + 3 * h_all].reshape(bs, ts, 3, kv_c, g_c).transpose(0, 3, 1, 4, 2)
            gl_rows = jnp.pad(gl.reshape(bs, rows, 3), ((0, 0), (0, 0), (0, LANES - 3)))
            bg = jnp.broadcast_to(b_gate[i].reshape(1, 3, kv_c, g_c), (ts, 3, kv_c, g_c)).transpose(2, 0, 3, 1)
            bg_rows = jnp.pad(bg.reshape(rows, 3), ((0, 0), (0, LANES - 3)))
            o = _nsa_sample_b(page_table, qr.reshape(bs, rows, hd), sel.reshape(bs, rows, -1),
                              z3[:, :, o_kvs:o_kvw].reshape(bs, ts * 2 * kv_c, hd),
                              o_cmp.reshape(bs, rows, hd), o_win.reshape(bs, rows, hd), gl_rows, bg_rows,
                              _tile_view(cache_slc_kv[i]), page, kv_c, g_c, ts, n_pages)
            xs = _mm_res([_token_cols(o.reshape(bs, kv_c, ts * g_c, hd), ts, g_c).astype(BF16)], w_out, xs, ms, tn_out)

        g_final = norm_final if layer == depth - 1 else None
        wg, wu, wd = w_gate[layer].astype(BF16), w_up[layer].astype(BF16), w_down[layer].astype(BF16)
        xp = _ffn(xp, norm_ffn[layer], wg, wu, wd, g_final, tm_f, tf)
        xs = _ffn(xs, norm_ffn[layer], wg, wu, wd, g_final, ms, tf)

    def stack(name, which):
        return jnp.stack(outs[name][which])

    return (xp.reshape(bp, t, d), xs.reshape(bs, ts, d),
            stack("a", 0), stack("a", 1), stack("b", 0), stack("b", 1), stack("f", 0), stack("f", 1),
            stack("c", 0), stack("c", 1), stack("s", 0), stack("s", 1), stack("w", 0), stack("w", 1))
```

```python
import functools

import numpy as np
import jax
import jax.numpy as jnp
from jax import lax
from jax.experimental import pallas as pl
from jax.experimental.pallas import tpu as pltpu

F32 = jnp.float32
BF16 = jnp.bfloat16
HIGHEST = lax.Precision.HIGHEST

LANES = 128
SUBLANES = 8
VMEM_LIMIT = 56 * 1024 * 1024
VMEM_LIMIT_FFN = 63 * 1024 * 1024

HEAD_DIM = 128
DA = HEAD_DIM // 2
CMP_LEN = 32
CMP_STRIDE = 16
CMP_HID = 2 * HEAD_DIM
SLC_BLOCK = 64
N_SELECT = 16
WINDOW = 512
ROPE_THETA = 10000.0
NORM_EPS = 1e-6
LOG2E = 1.4426950408889634
NEG = -1e30
BIG = 1e30
LOWEST = -3e38


def _params(*sem):
    return pltpu.CompilerParams(dimension_semantics=sem, vmem_limit_bytes=VMEM_LIMIT)


def _dot(a, b):
    return jnp.dot(a, b, preferred_element_type=F32)


def _dot_nt(a, b):
    return lax.dot_general(a, b, (((1,), (1,)), ((), ())), preferred_element_type=F32)


def _dot_exact(a, b):
    return jnp.dot(a, b, precision=HIGHEST, preferred_element_type=F32)


def _dot_nt_exact(a, b):
    return lax.dot_general(a, b, (((1,), (1,)), ((), ())), precision=HIGHEST, preferred_element_type=F32)


def _iota(shape, axis):
    return lax.broadcasted_iota(jnp.int32, shape, axis)


def _round_up(n, m):
    return -(-n // m) * m


def _pick_tile(n, prefs):
    for p in prefs:
        if n % p == 0:
            return p
    return n


def _rope_lanes(a, cos, sin, half):
    if 2 * half == LANES:
        partner = pltpu.roll(a, half, axis=1)
    else:
        first = (_iota(a.shape, 1) & (2 * half - 1)) < half
        partner = jnp.where(first, pltpu.roll(a, LANES - half, axis=1), pltpu.roll(a, half, axis=1))
    return a * cos + partner * sin


def _rmsnorm_rows(x, g):
    ms = jnp.mean(x * x, axis=-1, keepdims=True)
    return x * lax.rsqrt(ms + NORM_EPS) * g


def _proj_kernel(x_ref, g_ref, w_ref, cos_ref, sin_ref, o_ref, xn_ref, *, rope_ranges, half):
    j = pl.program_id(1)

    @pl.when(j == 0)
    def _():
        xn_ref[...] = _rmsnorm_rows(x_ref[...], g_ref[...]).astype(BF16)

    o_ref[...] = _dot(xn_ref[...], w_ref[...])
    is_rope = functools.reduce(jnp.logical_or, [(j >= a) & (j < b) for a, b in rope_ranges])

    @pl.when(is_rope)
    def _():
        cos, sin = cos_ref[...], sin_ref[...]
        for c in range(o_ref.shape[1] // LANES):
            sl = slice(c * LANES, (c + 1) * LANES)
            o_ref[:, sl] = _rope_lanes(o_ref[:, sl], cos, sin, half)


def _proj(x, g, w, cos, sin, rope_ranges, half, tm, tn):
    m, d = x.shape
    n = w.shape[1]
    return pl.pallas_call(
        functools.partial(_proj_kernel, rope_ranges=rope_ranges, half=half),
        grid=(m // tm, n // tn),
        in_specs=[
            pl.BlockSpec((tm, d), lambda i, j: (i, 0), pipeline_mode=pl.Buffered(1)),
            pl.BlockSpec((1, d), lambda i, j: (0, 0)),
            pl.BlockSpec((d, tn), lambda i, j: (0, j)),
            pl.BlockSpec((tm, LANES), lambda i, j: (i, 0)),
            pl.BlockSpec((tm, LANES), lambda i, j: (i, 0)),
        ],
        out_specs=pl.BlockSpec((tm, tn), lambda i, j: (i, j)),
        out_shape=jax.ShapeDtypeStruct((m, n), F32),
        scratch_shapes=[pltpu.VMEM((tm, d), BF16)],
        compiler_params=_params("parallel", "arbitrary"),
    )(x, g.reshape(1, d), w, cos, sin)


def _mm_res_kernel(*refs, n_lhs):
    a_refs, (w_ref, r_ref, o_ref) = refs[:n_lhs], refs[n_lhs:]
    acc = r_ref[...]
    off = 0
    for a_ref in a_refs:
        kk = a_ref.shape[1]
        acc = acc + _dot(a_ref[...], w_ref[off:off + kk, :])
        off += kk
    o_ref[...] = acc


def _mm_res(lhs, w, res, tm, tn):
    m, n = res.shape
    k = w.shape[0]
    return pl.pallas_call(
        functools.partial(_mm_res_kernel, n_lhs=len(lhs)),
        grid=(m // tm, n // tn),
        in_specs=[pl.BlockSpec((tm, a.shape[1]), lambda i, j: (i, 0)) for a in lhs] + [
            pl.BlockSpec((k, tn), lambda i, j: (0, j)),
            pl.BlockSpec((tm, tn), lambda i, j: (i, j)),
        ],
        out_specs=pl.BlockSpec((tm, tn), lambda i, j: (i, j)),
        out_shape=jax.ShapeDtypeStruct((m, n), F32),
        compiler_params=_params("parallel", "arbitrary"),
    )(*lhs, w, res)


def _ffn_kernel(x_ref, g_ref, wg_ref, wu_ref, wd_ref, gf_ref, o_ref, xn_ref, h_ref, *, final_norm):
    f = pl.program_id(1)
    n_tiles = pl.num_programs(1) - 1

    def hidden():
        xn = xn_ref[...]
        gate = _dot(xn, wg_ref[...])
        up = _dot(xn, wu_ref[...])
        return (gate * jax.nn.sigmoid(gate) * up).astype(BF16)

    rc = min(256, x_ref.shape[0])
    chunks = [slice(r * rc, (r + 1) * rc) for r in range(x_ref.shape[0] // rc)]

    @pl.when(f == 0)
    def _():
        for rows in chunks:
            x = x_ref[rows, :]
            xn_ref[rows, :] = _rmsnorm_rows(x, g_ref[...]).astype(BF16)
            o_ref[rows, :] = x
        h_ref[0] = hidden()

    @pl.when((f > 0) & (f < n_tiles))
    def _():
        h_ref[f % 2] = hidden()
        o_ref[...] += _dot(h_ref[(f - 1) % 2], wd_ref[...])

    @pl.when(f == n_tiles)
    def _():
        o_ref[...] += _dot(h_ref[(f - 1) % 2], wd_ref[...])
        if final_norm:
            for rows in chunks:
                o_ref[rows, :] = _rmsnorm_rows(o_ref[rows, :], gf_ref[...])


def _ffn(x, g, wg, wu, wd, g_final, tm, tf):
    m, d = x.shape
    n_tiles = wg.shape[1] // tf
    final_norm = g_final is not None
    gf = (g_final if final_norm else g).reshape(1, d)
    return pl.pallas_call(
        functools.partial(_ffn_kernel, final_norm=final_norm),
        grid=(m // tm, n_tiles + 1),
        in_specs=[
            pl.BlockSpec((tm, d), lambda i, f: (i, 0), pipeline_mode=pl.Buffered(1)),
            pl.BlockSpec((1, d), lambda i, f: (0, 0)),
            pl.BlockSpec((d, tf), lambda i, f: (0, jnp.minimum(f, n_tiles - 1))),
            pl.BlockSpec((d, tf), lambda i, f: (0, jnp.minimum(f, n_tiles - 1))),
            pl.BlockSpec((tf, d), lambda i, f: (jnp.maximum(f - 1, 0), 0)),
            pl.BlockSpec((1, d), lambda i, f: (0, 0)),
        ],
        out_specs=pl.BlockSpec((tm, d), lambda i, f: (i, 0), pipeline_mode=pl.Buffered(1)),
        out_shape=jax.ShapeDtypeStruct((m, d), F32),
        scratch_shapes=[pltpu.VMEM((tm, d), BF16), pltpu.VMEM((2, tm, tf), BF16)],
        compiler_params=pltpu.CompilerParams(dimension_semantics=("parallel", "arbitrary"),
                                             vmem_limit_bytes=VMEM_LIMIT_FFN),
    )(x, g.reshape(1, d), wg, wu, wd, gf)


def _online_update(carry, s, v, shift=None):
    m, l, acc = carry
    smax = jnp.max(s, axis=-1, keepdims=True)
    if shift is not None:
        smax = smax - shift
    m_new = jnp.maximum(m, smax)
    alpha = jnp.exp2(m - m_new)
    p = jnp.exp2(s - (m_new if shift is None else m_new + shift))
    l = alpha * l + jnp.sum(p, axis=-1, keepdims=True)
    pb = p.astype(BF16)
    if isinstance(v, (list, tuple)):
        n = v[0].shape[0]
        pv = sum(_dot(pb[:, r * n:(r + 1) * n], x) for r, x in enumerate(v))
    else:
        pv = _dot(pb, v)
    return m_new, l, alpha * acc + pv


def _state_init(m_ref, l_ref, acc_ref):
    m_ref[...] = jnp.full(m_ref.shape, NEG, F32)
    l_ref[...] = jnp.zeros(l_ref.shape, F32)
    acc_ref[...] = jnp.zeros(acc_ref.shape, F32)


def _state_update(m_ref, l_ref, acc_ref, k, s, v, shift=None):
    m, l, acc = _online_update((m_ref[k], l_ref[k], acc_ref[k]), s, v, shift)
    m_ref[k], l_ref[k], acc_ref[k] = m, l, acc


def _fold_lanes(x, op):
    out = x[:, :LANES]
    for i in range(1, x.shape[1] // LANES):
        out = op(out, x[:, i * LANES:(i + 1) * LANES])
    return out


def _two_pass_attention(q, kb_ref, vb_ref, n_blocks, tk, s_ref, add_fn=None, tail_fn=None, shift=None):
    rows = q.shape[0]

    def scores(kb, masked):
        k0 = pl.multiple_of(kb * tk, tk)
        s = _dot_nt(q, kb_ref[pl.ds(k0, tk), :])
        if add_fn is not None:
            s = s + add_fn(k0)
        if masked and tail_fn is not None:
            s = s + tail_fn(k0)
        s_ref[kb] = s
        return _fold_lanes(s, jnp.maximum)

    mx = lax.fori_loop(0, n_blocks - 1, lambda kb, mx: jnp.maximum(mx, scores(kb, False)),
                       jnp.full((rows, LANES), NEG, F32))
    mx = jnp.maximum(mx, scores(n_blocks - 1, True))
    m = jnp.max(mx, axis=-1, keepdims=True)
    if shift is not None:
        m = (m - shift) + shift

    def accumulate(kb, carry):
        ls, acc = carry
        k0 = pl.multiple_of(kb * tk, tk)
        p = jnp.exp2(s_ref[kb] - m)
        return ls + _fold_lanes(p, jnp.add), acc + _dot(p.astype(BF16), vb_ref[pl.ds(k0, tk), :])

    ls, acc = lax.fori_loop(0, n_blocks, accumulate,
                            (jnp.zeros((rows, LANES), F32), jnp.zeros((rows, HEAD_DIM), F32)))
    return acc, jnp.sum(ls, axis=-1, keepdims=True)


def _masked_softmax(s, okf):
    ok = okf > 0.5
    sm = jnp.where(ok, s, NEG)
    m = jnp.max(sm, axis=-1, keepdims=True)
    e = jnp.where(ok, jnp.exp2(sm - m), 0.0)
    return e / jnp.maximum(jnp.sum(e, axis=-1, keepdims=True), 1e-30)


def _lane_pick(x, lane_idx):
    return jnp.sum(jnp.where(_iota(x.shape, 1) == lane_idx, x, 0.0), axis=-1, keepdims=True)


def _topk_mask(imp, n_sel):
    lane = _iota(imp.shape, 1).astype(F32)
    sel = jnp.zeros(imp.shape, F32)
    x = imp
    for _ in range(n_sel):
        m = jnp.max(x, axis=-1, keepdims=True)
        idx = jnp.min(jnp.where(x == m, lane, 1e9), axis=-1, keepdims=True)
        hit = lane == idx
        sel = jnp.where(hit, 1.0, sel)
        x = jnp.where(hit, LOWEST, x)
    return sel


def _overlap(ci, sj):
    ov = jnp.minimum(ci + CMP_LEN, sj + SLC_BLOCK) - jnp.maximum(ci, sj)
    return jnp.maximum(ov, 0).astype(F32) * (1.0 / CMP_LEN)


def _topk_mask_t(imp_t, n_sel, ns):
    rows = _round_up(ns, SUBLANES)
    x = imp_t[0:rows]
    j = _iota(x.shape, 0)
    rank = jnp.zeros(x.shape, F32)
    for jp in range(ns):
        a = x[jp:jp + 1, :]
        rank = rank + jnp.where((a > x) | ((a == x) & (jp < j)), 1.0, 0.0)
    sel = jnp.where((rank < n_sel) & (j < ns), 1.0, 0.0)
    return jnp.concatenate([sel, jnp.zeros((imp_t.shape[0] - rows, imp_t.shape[1]), F32)], axis=0)


def _block_importance(imp, qpos, ns, axis=1):
    j = _iota(imp.shape, axis)
    cur = qpos >> 6
    forced = (j == 0) | (j == cur) | (j == cur - 1)
    valid = (j << 6) <= qpos
    out = jnp.where(forced, BIG, jnp.where(valid, imp, NEG))
    return jnp.where(j < ns, out, LOWEST)


def _lam_value(lam_ref, lam_init):
    v = lam_ref[...]
    a = jnp.exp(jnp.sum(v[0:1] * v[1:2], axis=-1, keepdims=True))
    b = jnp.exp(jnp.sum(v[2:3] * v[3:4], axis=-1, keepdims=True))
    return a - b + lam_init


def _gelu_tanh(x):
    return 0.5 * x * (1.0 + jnp.tanh(0.7978845608028654 * (x + 0.044715 * (x * x * x))))


def _log_sigmoid(z):
    return jnp.minimum(z, 0.0) - jnp.log1p(jnp.exp(-jnp.abs(z)))


def _cast_kv_once(step, pairs):
    @pl.when(step == 0)
    def _():
        for src, dst in pairs:
            dst[...] = src[0].astype(BF16)


def _logf_prompt_kernel(z_ref, b_ref, lf_ref, dkc_ref, dkr_ref):
    t = z_ref.shape[1]
    lf = _log_sigmoid(z_ref[0] + b_ref[...])
    lf_ref[0] = lf
    upper = (_iota((LANES, LANES), 1) > _iota((LANES, LANES), 0)).astype(F32)
    carry = jnp.zeros((1, LANES), F32)
    for blk in reversed(range(t // LANES)):
        sl = slice(blk * LANES, (blk + 1) * LANES)
        x = lf[sl]
        d = _dot_exact(upper, x) + carry
        dkc_ref[0, sl, :] = d
        dkr_ref[0, :, sl] = d.T
        carry = carry + jnp.sum(x, axis=0, keepdims=True)


def _logf_prompt(z3, b_pad, col_blk):
    b, t, _ = z3.shape
    return pl.pallas_call(
        _logf_prompt_kernel,
        grid=(b,),
        in_specs=[pl.BlockSpec((1, t, LANES), lambda i: (i, 0, col_blk)),
                  pl.BlockSpec((1, LANES), lambda i: (0, 0))],
        out_specs=[pl.BlockSpec((1, t, LANES), lambda i: (i, 0, 0)),
                   pl.BlockSpec((1, t, LANES), lambda i: (i, 0, 0)),
                   pl.BlockSpec((1, LANES, t), lambda i: (i, 0, 0))],
        out_shape=[jax.ShapeDtypeStruct((b, t, LANES), F32),
                   jax.ShapeDtypeStruct((b, t, LANES), F32),
                   jax.ShapeDtypeStruct((b, LANES, t), F32)],
        compiler_params=_params("parallel"),
    )(z3, b_pad)


def _causal_tail(qpos, tk):
    return lambda k0: jnp.where(k0 + _iota((qpos.shape[0], tk), 1) <= qpos, 0.0, NEG)


def _diff_prompt_kernel(lam_ref, g_ref, q_ref, k_ref, v_ref, o_ref, kb_ref, vb_ref, s_ref, *, tq, gsz, lam_init, tk):
    qi = pl.program_id(2)
    _cast_kv_once(qi, ((k_ref, kb_ref), (v_ref, vb_ref)))
    lam = _lam_value(lam_ref, lam_init)
    lane = _iota((tq, HEAD_DIM), 1)
    qscale = DA ** -0.5 * LOG2E
    qs = [q_ref[0, :, g * HEAD_DIM:(g + 1) * HEAD_DIM] * qscale for g in range(gsz)]
    q = jnp.concatenate([jnp.where(lane < DA, x, 0.0) for x in qs] + [jnp.where(lane >= DA, x, 0.0) for x in qs],
                        axis=0).astype(BF16)
    qpos = qi * tq + (_iota((2 * gsz * tq, 1), 0) & (tq - 1))
    n_blocks = (qi * tq) // tk + 1
    acc, l = _two_pass_attention(q, kb_ref, vb_ref, n_blocks, tk, s_ref, tail_fn=_causal_tail(qpos, tk))
    o = acc / l
    half = gsz * tq
    o = o[:half] - lam * o[half:]
    o = (_rmsnorm_rows(o, g_ref[...]) * (1.0 - lam_init)).astype(BF16)
    for g in range(gsz):
        o_ref[0, :, g * HEAD_DIM:(g + 1) * HEAD_DIM] = o[g * tq:(g + 1) * tq]


def _diff_prompt(z3, lam_vec, subln_g, n_kv, gsz, q_blk0, k_blk0, v_blk0, lam_init, tq, tk):
    b, t, _ = z3.shape
    qw = gsz * HEAD_DIM
    assert tk % tq == 0 and tq & (tq - 1) == 0
    return pl.pallas_call(
        functools.partial(_diff_prompt_kernel, tq=tq, gsz=gsz, lam_init=lam_init, tk=tk),
        grid=(b, n_kv, t // tq),
        in_specs=[
            pl.BlockSpec((4, DA), lambda i, k, q: (0, 0)),
            pl.BlockSpec((1, HEAD_DIM), lambda i, k, q: (0, 0)),
            pl.BlockSpec((1, tq, qw), lambda i, k, q: (i, q, q_blk0 + k)),
            pl.BlockSpec((1, t, HEAD_DIM), lambda i, k, q: (i, 0, k_blk0 + k)),
            pl.BlockSpec((1, t, HEAD_DIM), lambda i, k, q: (i, 0, v_blk0 + k)),
        ],
        out_specs=pl.BlockSpec((1, tq, qw), lambda i, k, q: (i, q, k)),
        out_shape=jax.ShapeDtypeStruct((b, t, n_kv * qw), BF16),
        scratch_shapes=[pltpu.VMEM((t, HEAD_DIM), BF16), pltpu.VMEM((t, HEAD_DIM), BF16),
                        pltpu.VMEM((t // tk, 2 * gsz * tq, tk), F32)],
        compiler_params=_params("parallel", "parallel", "arbitrary"),
    )(lam_vec, subln_g.reshape(1, HEAD_DIM), z3, z3, z3)


def _fox_prompt_kernel(q_ref, k_ref, v_ref, dkr_ref, dkc_ref, o_ref, kb_ref, vb_ref, s_ref, *, tq, gsz, tk):
    kh = pl.program_id(1)
    qi = pl.program_id(2)
    _cast_kv_once(qi, ((k_ref, kb_ref), (v_ref, vb_ref)))
    qscale = HEAD_DIM ** -0.5 * LOG2E
    qpos = qi * tq + (_iota((gsz * tq, 1), 0) & (tq - 1))
    n_blocks = (qi * tq) // tk + 1
    q = jnp.concatenate([q_ref[0, :, g * HEAD_DIM:(g + 1) * HEAD_DIM] * qscale for g in range(gsz)],
                        axis=0).astype(BF16)
    dq = jnp.concatenate([_lane_pick(dkc_ref[0], kh * gsz + g) for g in range(gsz)], axis=0) * LOG2E

    def dk_fn(k0):
        return jnp.concatenate([jnp.broadcast_to(dkr_ref[0, g, :, pl.ds(k0, tk)] * LOG2E, (tq, tk))
                                for g in range(gsz)], axis=0)

    acc, l = _two_pass_attention(q, kb_ref, vb_ref, n_blocks, tk, s_ref, add_fn=dk_fn,
                                 tail_fn=_causal_tail(qpos, tk), shift=dq)
    o = (acc / l).astype(BF16)
    for g in range(gsz):
        o_ref[0, :, g * HEAD_DIM:(g + 1) * HEAD_DIM] = o[g * tq:(g + 1) * tq]


def _fox_prompt(z3, dkr4, dkc, n_kv, gsz, q_blk0, k_blk0, v_blk0, tq, tk):
    b, t, _ = z3.shape
    qw = gsz * HEAD_DIM
    assert tk % tq == 0
    return pl.pallas_call(
        functools.partial(_fox_prompt_kernel, tq=tq, gsz=gsz, tk=tk),
        grid=(b, n_kv, t // tq),
        in_specs=[
            pl.BlockSpec((1, tq, qw), lambda i, k, q: (i, q, q_blk0 + k)),
            pl.BlockSpec((1, t, HEAD_DIM), lambda i, k, q: (i, 0, k_blk0 + k)),
            pl.BlockSpec((1, t, HEAD_DIM), lambda i, k, q: (i, 0, v_blk0 + k)),
            pl.BlockSpec((1, gsz, 1, t), lambda i, k, q: (i, k, 0, 0)),
            pl.BlockSpec((1, tq, LANES), lambda i, k, q: (i, q, 0)),
        ],
        out_specs=pl.BlockSpec((1, tq, qw), lambda i, k, q: (i, q, k)),
        out_shape=jax.ShapeDtypeStruct((b, t, n_kv * qw), BF16),
        scratch_shapes=[pltpu.VMEM((t, HEAD_DIM), BF16), pltpu.VMEM((t, HEAD_DIM), BF16),
                        pltpu.VMEM((t // tk, gsz * tq, tk), F32)],
        compiler_params=_params("parallel", "parallel", "arbitrary"),
    )(z3, z3, z3, dkr4, dkc)


def _logf_sample_kernel(z_ref, b_ref, lf_ref, dq_ref, tot_ref, *, ts):
    m = z_ref.shape[0]
    lf = _log_sigmoid(z_ref[...] + b_ref[...])
    lf_ref[...] = lf
    i, j = _iota((m, m), 0), _iota((m, m), 1)
    same = (i // ts) == (j // ts)
    dq_ref[...] = _dot_exact((same & (j > i)).astype(F32), lf)
    tot_ref[...] = _dot_exact(same.astype(F32), lf)


def _logf_sample(z_fg, b_pad, ts):
    m = z_fg.shape[0]
    sds = jax.ShapeDtypeStruct((m, LANES), F32)
    return pl.pallas_call(
        functools.partial(_logf_sample_kernel, ts=ts),
        out_shape=[sds, sds, sds],
    )(z_fg, b_pad)


def _pad_rows_scratch(ref, x):
    ref[...] = jnp.zeros(ref.shape, ref.dtype)
    ref[0:x.shape[0], :] = x


def _page_specs(page, n_kv, n_pages, first_page):
    rows = page * 2 * n_kv
    return [pl.BlockSpec((rows, HEAD_DIM), lambda b, p, pt, r=r: (pt[b, first_page(b, p, pt) + r], 0))
            for r in range(n_pages)]


def _tile_view(cache):
    n_kv = cache.shape[3]
    assert 2 * n_kv in (SUBLANES, 2 * SUBLANES)
    return cache.reshape(-1, 2 * n_kv // SUBLANES, SUBLANES, HEAD_DIM)


def _tile_specs(page, tiles, n_pages, first_page):
    return [pl.BlockSpec((page, tiles, SUBLANES, HEAD_DIM),
                         lambda b, p, pt, r=r: (pt[b, first_page(b, p, pt) + r], 0, 0, 0))
            for r in range(n_pages)]


def _flat_kv(pg_refs):
    n = pg_refs[0].shape[0] * SUBLANES
    if pg_refs[0].shape[1] == 2:
        ks = [pg[:, 0].reshape(n, HEAD_DIM) for pg in pg_refs]
        vs = [pg[:, 1].reshape(n, HEAD_DIM) for pg in pg_refs]
        off = 0
    else:
        vs = [pg[:, 0].reshape(n, HEAD_DIM) for pg in pg_refs]
        ks = [pltpu.roll(v, SUBLANES // 2, axis=0) for v in vs]
        off = SUBLANES // 2
    return [k.astype(BF16) for k in ks], [v.astype(BF16) for v in vs], off


def _flat_scores(q, ks):
    return jnp.concatenate([_dot_nt(q, k) for k in ks], axis=1)


def _own_head_bias(row_head, n_cols):
    return jnp.where((_iota((row_head.shape[0], n_cols), 1) & (SUBLANES - 1)) == row_head, 0.0, NEG)


def _diff_sample_kernel(pt_ref, lam_ref, g_ref, q_ref, kn_ref, vn_ref, *refs, n_kv, gsz, ts, lam_init, n_pages):
    pg_refs = refs[:n_pages]
    o_ref, m_ref, l_ref, acc_ref, bias_ref, kpad_ref, vpad_ref = refs[n_pages:]
    p = pl.program_id(1)
    last = pl.num_programs(1) - 1
    r1 = n_kv * ts * gsz
    q = q_ref[0] * (DA ** -0.5 * LOG2E)
    lane = _iota((r1, HEAD_DIM), 1)
    qb = jnp.concatenate([jnp.where(lane < DA, q, 0.0), jnp.where(lane >= DA, q, 0.0)], axis=0).astype(BF16)
    rr = _iota((2 * r1, 1), 0) % r1
    row_head = rr // (ts * gsz)
    row_tok = (rr % (ts * gsz)) // gsz

    @pl.when(p == 0)
    def _():
        _state_init(m_ref, l_ref, acc_ref)

    @pl.when(p < last)
    def _():
        ks, vs, off = _flat_kv(pg_refs)

        @pl.when(p == 0)
        def _():
            bias_ref[...] = _own_head_bias(row_head + off, bias_ref.shape[1])

        _state_update(m_ref, l_ref, acc_ref, ..., _flat_scores(qb, ks) + bias_ref[...], vs)

    @pl.when(p == last)
    def _():
        _pad_rows_scratch(kpad_ref, kn_ref[0])
        _pad_rows_scratch(vpad_ref, vn_ref[0])
        s = _dot_nt(qb, kpad_ref[...].astype(BF16))
        c = _iota(s.shape, 1)
        ok = ((c % n_kv) == row_head) & ((c // n_kv) <= row_tok)
        _state_update(m_ref, l_ref, acc_ref, ..., jnp.where(ok, s, NEG), vpad_ref[...].astype(BF16))
        o = acc_ref[...] / l_ref[...]
        o = o[:r1] - _lam_value(lam_ref, lam_init) * o[r1:]
        o_ref[0] = (_rmsnorm_rows(o, g_ref[...]) * (1.0 - lam_init)).astype(BF16)


def _diff_sample(page_table, lam_vec, subln_g, q_rows, k_new, v_new, cache_tiles, page, n_kv, gsz, ts, lam_init, n_pages):
    bs, r1, _ = q_rows.shape
    nsteps = page_table.shape[1] // n_pages
    first = lambda b, p, pt: jnp.minimum(p, nsteps - 1) * n_pages
    row_spec = lambda n: pl.BlockSpec((1, n, HEAD_DIM), lambda b, p, pt: (b, 0, 0))
    assert ts * n_kv <= LANES
    grid_spec = pltpu.PrefetchScalarGridSpec(
        num_scalar_prefetch=1,
        grid=(bs, nsteps + 1),
        in_specs=[
            pl.BlockSpec((4, DA), lambda b, p, pt: (0, 0)),
            pl.BlockSpec((1, HEAD_DIM), lambda b, p, pt: (0, 0)),
            row_spec(r1), row_spec(ts * n_kv), row_spec(ts * n_kv),
        ] + _tile_specs(page, cache_tiles.shape[1], n_pages, first),
        out_specs=row_spec(r1),
        scratch_shapes=[pltpu.VMEM((2 * r1, 1), F32), pltpu.VMEM((2 * r1, 1), F32), pltpu.VMEM((2 * r1, HEAD_DIM), F32),
                        pltpu.VMEM((2 * r1, n_pages * page * SUBLANES), F32),
                        pltpu.VMEM((LANES, HEAD_DIM), F32), pltpu.VMEM((LANES, HEAD_DIM), F32)],
    )
    return pl.pallas_call(
        functools.partial(_diff_sample_kernel, n_kv=n_kv, gsz=gsz, ts=ts, lam_init=lam_init, n_pages=n_pages),
        grid_spec=grid_spec,
        out_shape=jax.ShapeDtypeStruct((bs, r1, HEAD_DIM), BF16),
        compiler_params=_params("parallel", "arbitrary"),
    )(page_table, lam_vec, subln_g.reshape(1, HEAD_DIM), q_rows, k_new, v_new, *([cache_tiles] * n_pages))


def _fox_sample_kernel(pt_ref, q_ref, kn_ref, vn_ref, dq_ref, dkn_ref, tot_ref, *refs, n_kv, gsz, ts, n_pages):
    pg_refs, lf_refs = refs[:n_pages], refs[n_pages:2 * n_pages]
    o_ref, m_ref, l_ref, acc_ref, car_ref, bias_ref, kpad_ref, vpad_ref = refs[2 * n_pages:]
    p = pl.program_id(1)
    page = lf_refs[0].shape[2]
    rows = n_kv * ts * gsz
    n_h = n_kv * gsz
    qb = (q_ref[0] * (HEAD_DIM ** -0.5 * LOG2E)).astype(BF16)
    rr = _iota((rows, 1), 0)
    row_head = rr // (ts * gsz)
    row_tok = (rr % (ts * gsz)) // gsz
    shift = dq_ref[0][:, 0:1] * LOG2E

    @pl.when(p == 0)
    def _():
        _state_init(m_ref, l_ref, acc_ref)
        car_ref[...] = tot_ref[0]
        _pad_rows_scratch(kpad_ref, kn_ref[0])
        _pad_rows_scratch(vpad_ref, vn_ref[0])
        s = _dot_nt(qb, kpad_ref[...].astype(BF16)) + dkn_ref[0] * LOG2E
        c = _iota(s.shape, 1)
        ok = ((c % n_kv) == row_head) & ((c // n_kv) <= row_tok)
        _state_update(m_ref, l_ref, acc_ref, ..., jnp.where(ok, s, NEG), vpad_ref[...].astype(BF16), shift=shift)

    @pl.when(p > 0)
    def _():
        ks, vs, off = _flat_kv(pg_refs)

        @pl.when(p == 1)
        def _():
            bias_ref[...] = _own_head_bias(row_head + off, bias_ref.shape[1])

        lower = (_iota((page, page), 0) > _iota((page, page), 1)).astype(F32)
        dks = [None] * n_pages
        for r in reversed(range(n_pages)):
            lft = lf_refs[r][0]
            dks[r] = _dot_exact(lft, lower) + car_ref[...]
            car_ref[...] = car_ref[...] + jnp.sum(lft, axis=1, keepdims=True)
        dk = jnp.concatenate(dks, axis=1)
        hi = dk.astype(BF16)
        rest = dk - hi.astype(F32)
        mid = rest.astype(BF16)
        terms = (hi, mid, (rest - mid.astype(F32)).astype(BF16))
        hh = _iota((rows, n_h), 0)
        pick = ((hh // (ts * gsz)) * gsz + hh % gsz == _iota((rows, n_h), 1)).astype(BF16)
        spread = ((_iota((page, page * SUBLANES), 1) >> 3) == _iota((page, page * SUBLANES), 0)).astype(BF16)
        on_rows = [_dot(pick, x).astype(BF16) for x in terms]
        pieces = jnp.concatenate([x[:, r * page:(r + 1) * page] for r in range(n_pages) for x in on_rows], axis=0)
        flat = _dot(pieces, spread)
        nt = len(terms)
        dkf = jnp.concatenate(
            [sum(flat[(r * nt + q) * rows:(r * nt + q + 1) * rows] for q in range(nt)) for r in range(n_pages)],
            axis=1) * LOG2E
        _state_update(m_ref, l_ref, acc_ref, ..., _flat_scores(qb, ks) + dkf + bias_ref[...], vs, shift=shift)

    @pl.when(p == pl.num_programs(1) - 1)
    def _():
        o_ref[0] = (acc_ref[...] / l_ref[...]).astype(BF16)


def _fox_sample(page_table, q_rows, k_new, v_new, dq_b, dkn_b, tot, cache_tiles, logf_t, page, n_kv, gsz, ts, n_pages):
    bs, rows, _ = q_rows.shape
    n_h = n_kv * gsz
    nsteps = page_table.shape[1] // n_pages
    first = lambda b, p, pt: (nsteps - jnp.maximum(p, 1)) * n_pages
    lf_specs = [pl.BlockSpec((1, n_h, page), lambda b, p, pt, r=r: (pt[b, first(b, p, pt) + r], 0, 0))
                for r in range(n_pages)]
    row_spec = lambda n: pl.BlockSpec((1, n, HEAD_DIM), lambda b, p, pt: (b, 0, 0))
    assert ts * n_kv <= LANES
    grid_spec = pltpu.PrefetchScalarGridSpec(
        num_scalar_prefetch=1,
        grid=(bs, nsteps + 1),
        in_specs=[
            row_spec(rows), row_spec(ts * n_kv), row_spec(ts * n_kv), row_spec(rows), row_spec(rows),
            pl.BlockSpec((1, n_h, 1), lambda b, p, pt: (b, 0, 0)),
        ] + _tile_specs(page, cache_tiles.shape[1], n_pages, first) + lf_specs,
        out_specs=row_spec(rows),
        scratch_shapes=[pltpu.VMEM((rows, 1), F32), pltpu.VMEM((rows, 1), F32), pltpu.VMEM((rows, HEAD_DIM), F32),
                        pltpu.VMEM((n_h, 1), F32), pltpu.VMEM((rows, n_pages * page * SUBLANES), F32),
                        pltpu.VMEM((LANES, HEAD_DIM), F32), pltpu.VMEM((LANES, HEAD_DIM), F32)],
    )
    return pl.pallas_call(
        functools.partial(_fox_sample_kernel, n_kv=n_kv, gsz=gsz, ts=ts, n_pages=n_pages),
        grid_spec=grid_spec,
        out_shape=jax.ShapeDtypeStruct((bs, rows, HEAD_DIM), BF16),
        compiler_params=_params("parallel", "arbitrary"),
    )(page_table, q_rows, k_new, v_new, dq_b, dkn_b, tot, *([cache_tiles] * n_pages), *([logf_t] * n_pages))


def _compress_mlp(acc_a, acc_b_next, w2):
    return _dot(_gelu_tanh(acc_a + acc_b_next).astype(BF16), w2).astype(BF16)


def _compress_prompt_kernel(z_ref, pe_ref, w1_ref, w2_ref, o_ref, *, nch):
    acc_a = jnp.zeros((nch, CMP_HID), F32)
    acc_b = jnp.zeros((nch, CMP_HID), F32)
    for l in range(CMP_STRIDE):
        x = z_ref[0, pl.ds(l, nch, stride=CMP_STRIDE), :]
        xa = (x + pe_ref[0, l:l + 1, :]).astype(BF16)
        xb = (x + pe_ref[0, CMP_STRIDE + l:CMP_STRIDE + l + 1, :]).astype(BF16)
        acc_a = acc_a + _dot(xa, w1_ref[0, l])
        acc_b = acc_b + _dot(xb, w1_ref[0, CMP_STRIDE + l])
    o_ref[0, 0, 0] = _compress_mlp(acc_a, pltpu.roll(acc_b, nch - 1, axis=0), w2_ref[0])


def _compress_prompt(z3, pe, w1, w2, n_kv, col_blk0):
    b, t, _ = z3.shape
    nch = t // CMP_STRIDE
    return pl.pallas_call(
        functools.partial(_compress_prompt_kernel, nch=nch),
        grid=(b, 2 * n_kv),
        in_specs=[
            pl.BlockSpec((1, t, HEAD_DIM), lambda i, c: (i, 0, col_blk0 + c)),
            pl.BlockSpec((1, CMP_LEN, HEAD_DIM), lambda i, c: (c // n_kv, 0, 0)),
            pl.BlockSpec((1, CMP_LEN, HEAD_DIM, CMP_HID), lambda i, c: (c // n_kv, 0, 0, 0)),
            pl.BlockSpec((1, CMP_HID, HEAD_DIM), lambda i, c: (c // n_kv, 0, 0)),
        ],
        out_specs=pl.BlockSpec((1, 1, 1, nch, HEAD_DIM), lambda i, c: (i, c // n_kv, c % n_kv, 0, 0)),
        out_shape=jax.ShapeDtypeStruct((b, 2, n_kv, nch, HEAD_DIM), BF16),
        compiler_params=_params("parallel", "arbitrary"),
    )(z3, pe, w1, w2)


def _nsa_prompt_kernel(q_ref, cos_ref, sin_ref, kc_ref, vc_ref, ks_ref, vs_ref, kw_ref, vw_ref, gl_ref, bg_ref,
                       o_ref, ksb_ref, vsb_ref, kwb_ref, vwb_ref, s_ref, *, tq, gsz, n_kv, nblk, ns, tk):
    kh = pl.program_id(1)
    qi = pl.program_id(2)
    _cast_kv_once(qi, ((ks_ref, ksb_ref), (vs_ref, vsb_ref), (kw_ref, kwb_ref), (vw_ref, vwb_ref)))
    t0 = qi * tq
    qscale = HEAD_DIM ** -0.5 * LOG2E
    cos, sin = cos_ref[...], sin_ref[...]
    qs = [q_ref[0, :, g * HEAD_DIM:(g + 1) * HEAD_DIM] * qscale for g in range(gsz)]
    q = jnp.concatenate(qs, axis=0).astype(BF16)
    qr = jnp.concatenate([_rope_lanes(x, cos, sin, HEAD_DIM // 2) for x in qs], axis=0).astype(BF16)
    qpos = t0 + _iota((tq, 1), 0)

    def tile_g(x):
        return jnp.concatenate([x] * gsz, axis=0)

    nch = kc_ref.shape[3]
    n_i = _iota((tq, nch), 1)
    ok_c = ((n_i * CMP_STRIDE + CMP_LEN - 1 <= qpos) & (n_i < nblk)).astype(F32)
    p = _masked_softmax(_dot_nt(q, kc_ref[0, 0, 0]), tile_g(ok_c))
    o_cmp = _dot(p.astype(BF16), vc_ref[0, 0, 0])
    psum = p[0:tq]
    for g in range(1, gsz):
        psum = psum + p[g * tq:(g + 1) * tq]
    ov_t = _overlap(_iota((LANES, nch), 1) * CMP_STRIDE, _iota((LANES, nch), 0) * SLC_BLOCK)
    imp_t = _block_importance(_dot_nt_exact(ov_t, psum), t0 + _iota((1, tq), 1), ns, axis=0)
    sel = _topk_mask_t(imp_t, min(N_SELECT, ns), ns).T.astype(BF16)

    def slc_mask(k0):
        expand = (((k0 + _iota((LANES, tk), 1)) >> 6) == _iota((LANES, tk), 0)).astype(BF16)
        selk = _dot(sel, expand)
        return tile_g(jnp.where((selk > 0.5) & (k0 + _iota((tq, tk), 1) <= qpos), 0.0, NEG))

    n_blocks = (t0 + tq + tk - 1) // tk
    acc, l = _two_pass_attention(qr, ksb_ref, vsb_ref, n_blocks, tk, s_ref, add_fn=slc_mask)
    o_slc = acc / l

    wlen = WINDOW + tq
    w0 = pl.multiple_of(jnp.maximum(t0 - WINDOW, 0), tq)
    dist = qpos - (w0 + _iota((tq, wlen), 1))
    sw = _dot_nt(qr, kwb_ref[pl.ds(w0, wlen), :]) + tile_g(jnp.where((dist >= 0) & (dist < WINDOW), 0.0, NEG))
    ew = jnp.exp2(sw - jnp.max(sw, axis=-1, keepdims=True))
    o_win = _dot(ew.astype(BF16), vwb_ref[pl.ds(w0, wlen), :]) / jnp.sum(ew, axis=-1, keepdims=True)

    gates = jax.nn.sigmoid(gl_ref[0] + bg_ref[...])
    n_h = n_kv * gsz
    for g in range(gsz):
        rows = slice(g * tq, (g + 1) * tq)
        col = kh * gsz + g
        o = (_lane_pick(gates, col) * o_cmp[rows] + _lane_pick(gates, n_h + col) * o_slc[rows]
             + _lane_pick(gates, 2 * n_h + col) * o_win[rows])
        o_ref[0, :, g * HEAD_DIM:(g + 1) * HEAD_DIM] = o.astype(BF16)


def _nsa_prompt(z3, cos, sin, kvc, bg_pad, n_kv, gsz, ks_blk, vs_blk, kw_blk, vw_blk, gl_blk, tq, tk):
    b, t, _ = z3.shape
    nch = kvc.shape[3]
    qw = gsz * HEAD_DIM
    ns = -(-t // SLC_BLOCK)
    assert t >= WINDOW + tq and t % tk == 0 and tk % SLC_BLOCK == 0 and ns <= LANES
    kv_spec = lambda blk0: pl.BlockSpec((1, t, HEAD_DIM), lambda i, k, q: (i, 0, blk0 + k))
    return pl.pallas_call(
        functools.partial(_nsa_prompt_kernel, tq=tq, gsz=gsz, n_kv=n_kv, nblk=nch - 1, ns=ns, tk=tk),
        grid=(b, n_kv, t // tq),
        in_specs=[
            pl.BlockSpec((1, tq, qw), lambda i, k, q: (i, q, k)),
            pl.BlockSpec((tq, LANES), lambda i, k, q: (q, 0)),
            pl.BlockSpec((tq, LANES), lambda i, k, q: (q, 0)),
            pl.BlockSpec((1, 1, 1, nch, HEAD_DIM), lambda i, k, q: (i, 0, k, 0, 0)),
            pl.BlockSpec((1, 1, 1, nch, HEAD_DIM), lambda i, k, q: (i, 1, k, 0, 0)),
            kv_spec(ks_blk), kv_spec(vs_blk), kv_spec(kw_blk), kv_spec(vw_blk),
            pl.BlockSpec((1, tq, LANES), lambda i, k, q: (i, q, gl_blk)),
            pl.BlockSpec((1, LANES), lambda i, k, q: (0, 0)),
        ],
        out_specs=pl.BlockSpec((1, tq, qw), lambda i, k, q: (i, q, k)),
        out_shape=jax.ShapeDtypeStruct((b, t, n_kv * qw), BF16),
        scratch_shapes=[pltpu.VMEM((t, HEAD_DIM), BF16) for _ in range(4)] + [
            pltpu.VMEM((t // tk, gsz * tq, tk), F32)],
        compiler_params=_params("parallel", "parallel", "arbitrary"),
    )(z3, cos, sin, kvc, kvc, z3, z3, z3, z3, z3, bg_pad)


def _compress_sample_kernel(pt_ref, *refs, n_kv, n_pages, page):
    pg_refs = refs[:n_pages]
    pe_ref, w1_ref, w2_ref, o_ref, last_ref = refs[n_pages:]
    s = pl.program_id(1)
    cpp = page // CMP_STRIDE
    m = n_pages * cpp
    rpr = 2 * n_kv

    @pl.when(s == 0)
    def _():
        last_ref[...] = jnp.zeros(last_ref.shape, F32)

    for typ in range(2):
        acc_a = jnp.zeros((n_kv * m, CMP_HID), F32)
        acc_b = jnp.zeros((n_kv * m, CMP_HID), F32)
        for l in range(0, CMP_STRIDE, 2):
            xs = [jnp.concatenate(
                [pg[pl.ds((l + i) * rpr + typ * n_kv + h, cpp, stride=CMP_STRIDE * rpr), :]
                 for h in range(n_kv) for pg in pg_refs], axis=0) for i in range(2)]
            for acc_is_b, off in ((False, l), (True, CMP_STRIDE + l)):
                lhs = jnp.concatenate([(xs[i] + pe_ref[typ, off + i:off + i + 1, :]).astype(BF16)
                                       for i in range(2)], axis=1)
                term = _dot(lhs, w1_ref[typ, off:off + 2].reshape(2 * HEAD_DIM, CMP_HID))
                if acc_is_b:
                    acc_b = acc_b + term
                else:
                    acc_a = acc_a + term
        first = _iota((m, 1), 0) == 0
        for h in range(n_kv):
            a = acc_a[h * m:(h + 1) * m]
            idx = typ * n_kv + h
            prev = jnp.where(first, last_ref[idx, SUBLANES - 1:SUBLANES, :], pltpu.roll(a, 1, axis=0))
            last_ref[idx] = a[m - SUBLANES:m]
            o_ref[0, typ, h] = _compress_mlp(prev, acc_b[h * m:(h + 1) * m], w2_ref[typ])


def _compress_sample(page_table, cache_rows, pe, w1, w2, page, n_kv, n_pages):
    bs, npg = page_table.shape
    m = n_pages * (page // CMP_STRIDE)
    grid_spec = pltpu.PrefetchScalarGridSpec(
        num_scalar_prefetch=1,
        grid=(bs, npg // n_pages),
        in_specs=_page_specs(page, n_kv, n_pages, lambda b, s, pt: s * n_pages) + [
            pl.BlockSpec((2, CMP_LEN, HEAD_DIM), lambda b, s, pt: (0, 0, 0)),
            pl.BlockSpec((2, CMP_LEN, HEAD_DIM, CMP_HID), lambda b, s, pt: (0, 0, 0, 0)),
            pl.BlockSpec((2, CMP_HID, HEAD_DIM), lambda b, s, pt: (0, 0, 0)),
        ],
        out_specs=pl.BlockSpec((1, 2, n_kv, m, HEAD_DIM), lambda b, s, pt: (b, 0, 0, s, 0)),
        scratch_shapes=[pltpu.VMEM((2 * n_kv, SUBLANES, CMP_HID), F32)],
    )
    return pl.pallas_call(
        functools.partial(_compress_sample_kernel, n_kv=n_kv, n_pages=n_pages, page=page),
        grid_spec=grid_spec,
        out_shape=jax.ShapeDtypeStruct((bs, 2, n_kv, npg * (page // CMP_STRIDE), HEAD_DIM), BF16),
        compiler_params=_params("parallel", "arbitrary"),
    )(page_table, *([cache_rows] * n_pages), pe, w1, w2)


def _nsa_sample_a_kernel(q_ref, cos_ref, sin_ref, kc_ref, vc_ref, win_ref, kwn_ref,
                         ocmp_ref, owin_ref, sel_ref, qr_ref, kpad_ref, vpad_ref, *, n_kv, gsz, ts, p0, ns):
    qscale = HEAD_DIM ** -0.5 * LOG2E
    rows = ts * gsz
    nr = kc_ref.shape[3]
    nsp = sel_ref.shape[3]
    wrows = win_ref.shape[1] // (2 * n_kv)
    tok = _iota((rows, 1), 0) // gsz
    qpos = p0 + tok
    r_i = _iota((rows, nr), 1)
    ok_c = ((r_i >= 1) & ((r_i - 1) * CMP_STRIDE + CMP_LEN - 1 <= qpos)).astype(F32)
    ov = _overlap((_iota((nr, nsp), 0) - 1) * CMP_STRIDE, _iota((nr, nsp), 1) * SLC_BLOCK)
    group = ((_iota((rows, rows), 0) // gsz) == (_iota((rows, rows), 1) // gsz)).astype(F32)
    ok_w1 = _iota((rows, wrows), 1) > tok + (wrows - WINDOW)
    ok_w2 = _iota((rows, kpad_ref.shape[0]), 1) <= tok
    kvw = n_kv * HEAD_DIM
    for kh in range(n_kv):
        q = q_ref[0, kh]
        qr = _rope_lanes(q, cos_ref[...], sin_ref[...], HEAD_DIM // 2)
        qr_ref[0, kh] = qr
        qb, qrb = (q * qscale).astype(BF16), (qr * qscale).astype(BF16)
        p = _masked_softmax(_dot_nt(qb, kc_ref[0, 0, kh]), ok_c)
        ocmp_ref[0, kh] = _dot(p.astype(BF16), vc_ref[0, 0, kh])
        imp = _dot_exact(group, _dot_exact(p, ov))
        sel_ref[0, kh] = _topk_mask(_block_importance(imp, qpos, ns), min(N_SELECT, ns))
        sl = slice(kh * HEAD_DIM, (kh + 1) * HEAD_DIM)
        slv = slice(kvw + kh * HEAD_DIM, kvw + (kh + 1) * HEAD_DIM)
        _pad_rows_scratch(kpad_ref, kwn_ref[0, :, sl])
        _pad_rows_scratch(vpad_ref, kwn_ref[0, :, slv])
        k_win = win_ref[0, pl.ds(kh, wrows, stride=2 * n_kv), :].astype(BF16)
        v_win = win_ref[0, pl.ds(n_kv + kh, wrows, stride=2 * n_kv), :].astype(BF16)
        s1 = jnp.where(ok_w1, _dot_nt(qrb, k_win), NEG)
        s2 = jnp.where(ok_w2, _dot_nt(qrb, kpad_ref[...].astype(BF16)), NEG)
        m = jnp.maximum(jnp.max(s1, axis=-1, keepdims=True), jnp.max(s2, axis=-1, keepdims=True))
        e1, e2 = jnp.exp2(s1 - m), jnp.exp2(s2 - m)
        l = jnp.sum(e1, axis=-1, keepdims=True) + jnp.sum(e2, axis=-1, keepdims=True)
        o = _dot(e1.astype(BF16), v_win) + _dot(e2.astype(BF16), vpad_ref[...].astype(BF16))
        owin_ref[0, kh] = o / l


def _nsa_sample_a(q_rows, cos, sin, kvc, win_rows, kvw_new, n_kv, gsz, ts, p0, ns):
    bs = q_rows.shape[0]
    rows = ts * gsz
    nr = kvc.shape[3]
    nsp = _round_up(ns, LANES)
    wr, w = win_rows.shape[1], kvw_new.shape[2]
    assert wr == WINDOW * 2 * n_kv
    o_sds = jax.ShapeDtypeStruct((bs, n_kv, rows, HEAD_DIM), F32)
    o_spec = pl.BlockSpec((1, n_kv, rows, HEAD_DIM), lambda b: (b, 0, 0, 0))
    return pl.pallas_call(
        functools.partial(_nsa_sample_a_kernel, n_kv=n_kv, gsz=gsz, ts=ts, p0=p0, ns=ns),
        grid=(bs,),
        in_specs=[
            o_spec,
            pl.BlockSpec((rows, LANES), lambda b: (0, 0)),
            pl.BlockSpec((rows, LANES), lambda b: (0, 0)),
            pl.BlockSpec((1, 1, n_kv, nr, HEAD_DIM), lambda b: (b, 0, 0, 0, 0)),
            pl.BlockSpec((1, 1, n_kv, nr, HEAD_DIM), lambda b: (b, 1, 0, 0, 0)),
            pl.BlockSpec((1, wr, HEAD_DIM), lambda b: (b, 0, 0)),
            pl.BlockSpec((1, ts, w), lambda b: (b, 0, 0)),
        ],
        out_specs=[o_spec, o_spec, pl.BlockSpec((1, n_kv, rows, nsp), lambda b: (b, 0, 0, 0)), o_spec],
        out_shape=[o_sds, o_sds, jax.ShapeDtypeStruct((bs, n_kv, rows, nsp), F32), o_sds],
        scratch_shapes=[pltpu.VMEM((LANES, HEAD_DIM), F32), pltpu.VMEM((LANES, HEAD_DIM), F32)],
        compiler_params=_params("parallel"),
    )(q_rows, cos, sin, kvc, kvc, win_rows, kvw_new)


def _nsa_sample_b_kernel(pt_ref, q_ref, sel_ref, kn_ref, ocmp_ref, owin_ref, gl_ref, bg_ref, *refs,
                         n_kv, gsz, ts, n_pages):
    pg_refs = refs[:n_pages]
    o_ref, m_ref, l_ref, acc_ref, bias_ref, npad_ref = refs[n_pages:]
    p = pl.program_id(1)
    last = pl.num_programs(1) - 1
    rows = n_kv * ts * gsz
    page = pg_refs[0].shape[0]
    nsp = sel_ref.shape[2]
    bps = n_pages * (page // SLC_BLOCK)
    blk_cols = SLC_BLOCK * SUBLANES
    qb = (q_ref[0] * (HEAD_DIM ** -0.5 * LOG2E)).astype(BF16)
    rr = _iota((rows, 1), 0)
    row_head = rr // (ts * gsz)
    row_tok = (rr % (ts * gsz)) // gsz

    @pl.when(p == 0)
    def _():
        _state_init(m_ref, l_ref, acc_ref)

    @pl.when(p < last)
    def _():
        ks, vs, off = _flat_kv(pg_refs)

        @pl.when(p == 0)
        def _():
            bias_ref[...] = _own_head_bias(row_head + off, bias_ref.shape[1])

        shift = (_iota((nsp, LANES), 0) == p * bps + _iota((nsp, LANES), 1)).astype(BF16)
        flags = jnp.where(_dot(sel_ref[0].astype(BF16), shift) > 0.5, 0.0, NEG)
        sel_bias = jnp.concatenate([jnp.broadcast_to(flags[:, i:i + 1], (rows, blk_cols)) for i in range(bps)], axis=1)
        _state_update(m_ref, l_ref, acc_ref, ..., _flat_scores(qb, ks) + bias_ref[...] + sel_bias, vs)

    @pl.when(p == last)
    def _():
        _pad_rows_scratch(npad_ref, kn_ref[0])
        flat = npad_ref[...]
        s = _dot_nt(qb, pltpu.roll(flat, n_kv, axis=0).astype(BF16))
        c = _iota(s.shape, 1)
        ok = (((c % (2 * n_kv)) == n_kv + row_head) & ((c // (2 * n_kv)) <= row_tok)
              & (_lane_pick(sel_ref[0], last * bps) > 0.5))
        _state_update(m_ref, l_ref, acc_ref, ..., jnp.where(ok, s, NEG), flat.astype(BF16))
        gates = jax.nn.sigmoid(gl_ref[0] + bg_ref[...])
        o_ref[0] = (gates[:, 0:1] * ocmp_ref[0] + gates[:, 1:2] * (acc_ref[...] / l_ref[...])
                    + gates[:, 2:3] * owin_ref[0])


def _nsa_sample_b(page_table, q_rows, sel, kvs_new, o_cmp, o_win, gl_rows, bg_rows, cache_tiles,
                  page, n_kv, gsz, ts, n_pages):
    bs, rows, _ = q_rows.shape
    nsp = sel.shape[2]
    nsteps = page_table.shape[1] // n_pages
    first = lambda b, p, pt: jnp.minimum(p, nsteps - 1) * n_pages
    row_spec = lambda n, width: pl.BlockSpec((1, n, width), lambda b, p, pt: (b, 0, 0))
    assert ts * 2 * n_kv <= LANES and n_pages * (page // SLC_BLOCK) <= LANES
    grid_spec = pltpu.PrefetchScalarGridSpec(
        num_scalar_prefetch=1,
        grid=(bs, nsteps + 1),
        in_specs=[
            row_spec(rows, HEAD_DIM), row_spec(rows, nsp), row_spec(ts * 2 * n_kv, HEAD_DIM),
            row_spec(rows, HEAD_DIM), row_spec(rows, HEAD_DIM), row_spec(rows, LANES),
            pl.BlockSpec((rows, LANES), lambda b, p, pt: (0, 0)),
        ] + _tile_specs(page, cache_tiles.shape[1], n_pages, first),
        out_specs=row_spec(rows, HEAD_DIM),
        scratch_shapes=[pltpu.VMEM((rows, 1), F32), pltpu.VMEM((rows, 1), F32), pltpu.VMEM((rows, HEAD_DIM), F32),
                        pltpu.VMEM((rows, n_pages * page * SUBLANES), F32), pltpu.VMEM((LANES, HEAD_DIM), F32)],
    )
    return pl.pallas_call(
        functools.partial(_nsa_sample_b_kernel, n_kv=n_kv, gsz=gsz, ts=ts, n_pages=n_pages),
        grid_spec=grid_spec,
        out_shape=jax.ShapeDtypeStruct((bs, rows, HEAD_DIM), F32),
        compiler_params=_params("parallel", "arbitrary"),
    )(page_table, q_rows, sel, kvs_new, o_cmp, o_win, gl_rows, bg_rows, *([cache_tiles] * n_pages))


def _rope_tables(pos, half):
    inv = ROPE_THETA ** (-jnp.arange(half, dtype=F32) / half)
    ang = pos.astype(F32)[:, None] * inv
    cos, sin = jnp.cos(ang), jnp.sin(ang)
    reps = LANES // (2 * half)
    return (jnp.tile(jnp.concatenate([cos, cos], axis=-1), (1, reps)),
            jnp.tile(jnp.concatenate([-sin, sin], axis=-1), (1, reps)))


def _pad_cols(w, n):
    return jnp.pad(w, ((0, 0), (0, n - w.shape[1])))


def _head_rows(x, n_kv, gsz):
    b, t, _ = x.shape
    return x.reshape(b, t, n_kv, gsz, -1).transpose(0, 2, 1, 3, 4).reshape(b, n_kv, t * gsz, -1)


def _token_cols(x, ts, gsz):
    b, n_kv, _, f = x.shape
    return x.reshape(b, n_kv, ts, gsz, f).transpose(0, 2, 1, 3, 4).reshape(b * ts, n_kv * gsz * f)


def kernel(x_prompt, x_sample, cache_a_kv, cache_b_kv, cache_b_logf, cache_cmp_kv, cache_slc_kv, state_win_kv, page_table, norm_mix_e, w_in_e, b_forget, lam_q1, lam_k1, lam_q2, lam_k2, subln_g, w_out_e, norm_mix_o, w_in_o, b_gate, cmp_pe_k, cmp_w1_k, cmp_w2_k, cmp_pe_v, cmp_w1_v, cmp_w2_v, w_out_o, norm_ffn, w_gate, w_up, w_down, norm_final):
    bp, t, d = x_prompt.shape
    bs, ts, _ = x_sample.shape
    n_even, n_odd = w_in_e.shape[0], w_in_o.shape[0]
    depth = n_even + n_odd
    page = cache_a_kv.shape[2]
    npg = page_table.shape[1]
    p0 = npg * page
    h_all = d // HEAD_DIM
    h_b = cache_b_logf.shape[-1]
    h_a = h_all - h_b
    kv_a, kv_b, kv_c = cache_a_kv.shape[4], cache_b_kv.shape[4], cache_cmp_kv.shape[4]
    g_a, g_b, g_c = h_a // kv_a, h_b // kv_b, h_all // kv_c
    mp, ms = bp * t, bs * ts
    hd = HEAD_DIM

    e_qa, e_ka = 0, h_a * hd
    e_va = e_ka + kv_a * hd
    e_qf = e_va + kv_a * hd
    e_kf = e_qf + h_b * hd
    e_vf = e_kf + kv_b * hd
    e_fg = e_vf + kv_b * hd
    o_kvc = h_all * hd
    o_kvs = o_kvc + 2 * kv_c * hd
    o_kvw = o_kvs + 2 * kv_c * hd
    o_gl = o_kvw + 2 * kv_c * hd
    assert 3 * h_all <= LANES and h_b <= LANES

    tn_e = _pick_tile(e_va, (512, 256, 128))
    tn_o = _pick_tile(kv_c * hd, (512, 256, 128))
    ne_pad = _round_up(e_fg + h_b, tn_e)
    no_pad = _round_up(o_gl + 3 * h_all, tn_o)
    tm_p = _pick_tile(mp, (1024, 512, 256, 128))
    tm_f = _pick_tile(mp, (1024, 512, 256, 128))
    tn_out = _pick_tile(d, (1024, 512, 256, 128))
    tf = _pick_tile(w_gate.shape[2], (256, 128))
    tq_e = _pick_tile(t, (512, 256, 128))
    tk_e = _pick_tile(t, (512, 256))
    tq_o = _pick_tile(t, (256, 128))
    tk_o = _pick_tile(t, (512, 256, 128))
    n_pages = _pick_tile(npg, (8, 4, 2))
    n_pages_c = _pick_tile(npg, (16, 8, 4, 2))

    xp = x_prompt.reshape(mp, d)
    xs = x_sample.reshape(ms, d)
    pos_p = jnp.tile(jnp.arange(t, dtype=jnp.int32), bp)
    pos_s = p0 + jnp.tile(jnp.arange(ts, dtype=jnp.int32), bs)
    tab_a_p, tab_a_s = _rope_tables(pos_p, DA // 2), _rope_tables(pos_s, DA // 2)
    tab_c_p, tab_c_s = _rope_tables(pos_p, hd // 2), _rope_tables(pos_s, hd // 2)
    tab_c_q = _rope_tables(jnp.arange(t, dtype=jnp.int32), hd // 2)
    tab_c_rows = _rope_tables(p0 + jnp.repeat(jnp.arange(ts, dtype=jnp.int32), g_c), hd // 2)

    outs = {name: ([], []) for name in ("a", "b", "f", "c", "s", "w")}

    for layer in range(depth):
        i = layer // 2
        if layer % 2 == 0:
            lam_init = 0.8 - 0.6 * float(np.exp(-0.3 * layer))
            w_in = _pad_cols(w_in_e[i].astype(BF16), ne_pad)
            w_out = w_out_e[i].astype(BF16)
            rope_e = ((0, e_va // tn_e),)
            lam_vec = jnp.stack([lam_q1[i], lam_k1[i], lam_q2[i], lam_k2[i]])
            bf_pad = jnp.pad(b_forget[i], (0, LANES - h_b)).reshape(1, LANES)

            z = _proj(xp, norm_mix_e[i], w_in, *tab_a_p, rope_e, DA // 2, tm_p, tn_e)
            z3 = z.reshape(bp, t, ne_pad)
            outs["a"][0].append(z3[:, :, e_ka:e_qf].reshape(bp, t, 2, kv_a, hd))
            outs["b"][0].append(z3[:, :, e_kf:e_fg].reshape(bp, t, 2, kv_b, hd))
            lf, dkc, dkr = _logf_prompt(z3, bf_pad, e_fg // LANES)
            outs["f"][0].append(lf[:, :, :h_b])
            o_a = _diff_prompt(z3, lam_vec, subln_g[i], kv_a, g_a, e_qa // (g_a * hd), e_ka // hd, e_va // hd,
                               lam_init, tq_e, tk_e)
            o_b = _fox_prompt(z3, dkr[:, :h_b].reshape(bp, h_b, 1, t), dkc, kv_b, g_b,
                              e_qf // (g_b * hd), e_kf // hd, e_vf // hd, tq_e, tk_e)
            xp = _mm_res([o_a.reshape(mp, h_a * hd), o_b.reshape(mp, h_b * hd)], w_out, xp, tm_p, tn_out)

            z = _proj(xs, norm_mix_e[i], w_in, *tab_a_s, rope_e, DA // 2, ms, tn_e)
            z3 = z.reshape(bs, ts, ne_pad)
            outs["a"][1].append(z3[:, :, e_ka:e_qf].reshape(bs, ts, 2, kv_a, hd))
            outs["b"][1].append(z3[:, :, e_kf:e_fg].reshape(bs, ts, 2, kv_b, hd))
            lf, dq, tot = _logf_sample(z[:, e_fg:e_fg + LANES], bf_pad, ts)
            outs["f"][1].append(lf[:, :h_b].reshape(bs, ts, h_b))
            o_a = _diff_sample(page_table, lam_vec, subln_g[i],
                               _head_rows(z3[:, :, e_qa:e_ka], kv_a, g_a).reshape(bs, -1, hd),
                               z3[:, :, e_ka:e_va].reshape(bs, ts * kv_a, hd),
                               z3[:, :, e_va:e_qf].reshape(bs, ts * kv_a, hd),
                               _tile_view(cache_a_kv[i]), page, kv_a, g_a, ts, lam_init, n_pages)
            rows_b = kv_b * ts * g_b
            dq3 = dq[:, :h_b].reshape(bs, ts, h_b)
            dq_b = jnp.broadcast_to(_head_rows(dq3, kv_b, g_b).reshape(bs, rows_b, 1), (bs, rows_b, LANES))
            dkn = jnp.swapaxes(dq3, 1, 2).reshape(bs, kv_b, 1, g_b, ts, 1)
            dkn = jnp.broadcast_to(dkn, (bs, kv_b, ts, g_b, ts, kv_b)).reshape(bs, rows_b, ts * kv_b)
            dkn_b = jnp.pad(dkn, ((0, 0), (0, 0), (0, LANES - ts * kv_b)))
            tot3 = tot[:, :h_b].reshape(bs, ts, h_b)[:, 0, :, None]
            o_b = _fox_sample(page_table, _head_rows(z3[:, :, e_qf:e_kf], kv_b, g_b).reshape(bs, -1, hd),
                              z3[:, :, e_kf:e_vf].reshape(bs, ts * kv_b, hd),
                              z3[:, :, e_vf:e_fg].reshape(bs, ts * kv_b, hd), dq_b, dkn_b, tot3,
                              _tile_view(cache_b_kv[i]), jnp.swapaxes(cache_b_logf[i], 1, 2),
                              page, kv_b, g_b, ts, n_pages)
            xs = _mm_res([_token_cols(o_a.reshape(bs, kv_a, ts * g_a, hd), ts, g_a),
                          _token_cols(o_b.reshape(bs, kv_b, ts * g_b, hd), ts, g_b)], w_out, xs, ms, tn_out)
        else:
            w_in = _pad_cols(w_in_o[i].astype(BF16), no_pad)
            w_out = w_out_o[i].astype(BF16)
            rope_o = ((o_kvs // tn_o, (o_kvs + kv_c * hd) // tn_o), (o_kvw // tn_o, (o_kvw + kv_c * hd) // tn_o))
            bg_pad = jnp.pad(b_gate[i], (0, LANES - 3 * h_all)).reshape(1, LANES)
            pe = jnp.stack([cmp_pe_k[i], cmp_pe_v[i]])
            w1 = jnp.stack([cmp_w1_k[i], cmp_w1_v[i]]).astype(BF16)
            w2 = jnp.stack([cmp_w2_k[i], cmp_w2_v[i]]).astype(BF16)

            z = _proj(xp, norm_mix_o[i], w_in, *tab_c_p, rope_o, hd // 2, tm_p, tn_o)
            z3 = z.reshape(bp, t, no_pad)
            outs["c"][0].append(z3[:, :, o_kvc:o_kvs].reshape(bp, t, 2, kv_c, hd))
            outs["s"][0].append(z3[:, :, o_kvs:o_kvw].reshape(bp, t, 2, kv_c, hd))
            keep = min(WINDOW, t)
            outs["w"][0].append(z3[:, t - keep:, o_kvw:o_gl].reshape(bp, keep, 2, kv_c, hd))
            kvc = _compress_prompt(z3, pe, w1, w2, kv_c, o_kvc // hd)
            o = _nsa_prompt(z3, *tab_c_q, kvc, bg_pad, kv_c, g_c, o_kvs // hd, o_kvs // hd + kv_c,
                            o_kvw // hd, o_kvw // hd + kv_c, o_gl // LANES, tq_o, tk_o)
            xp = _mm_res([o.reshape(mp, d)], w_out, xp, tm_p, tn_out)

            z = _proj(xs, norm_mix_o[i], w_in, *tab_c_s, rope_o, hd // 2, ms, tn_o)
            z3 = z.reshape(bs, ts, no_pad)
            outs["c"][1].append(z3[:, :, o_kvc:o_kvs].reshape(bs, ts, 2, kv_c, hd))
            outs["s"][1].append(z3[:, :, o_kvs:o_kvw].reshape(bs, ts, 2, kv_c, hd))
            win = state_win_kv[i]
            kvw_new = z3[:, :, o_kvw:o_gl]
            win_all = jnp.concatenate([win, kvw_new.reshape(bs, ts, 2, kv_c, hd)], axis=1)
            outs["w"][1].append(win_all[:, -min(WINDOW, win.shape[1] + ts):])
            kvc = _compress_sample(page_table, cache_cmp_kv[i].reshape(-1, hd), pe, w1, w2, page, kv_c, min(16, npg))
            ns = -(-(p0 + ts) // SLC_BLOCK)
            o_cmp, o_win, sel, qr = _nsa_sample_a(_head_rows(z3[:, :, :o_kvc], kv_c, g_c), *tab_c_rows, kvc,
                                                  win.reshape(bs, -1, hd), kvw_new, kv_c, g_c, ts, p0, ns)
            rows = kv_c * ts * g_c
            gl = z3[:, :, o_gl:o_gl + 3 * h_all].reshape(bs, ts, 3, kv_c, g_c).transpose(0, 3, 1, 4, 2)
            gl_rows = jnp.pad(gl.reshape(bs, rows, 3), ((0, 0), (0, 0), (0, LANES - 3)))
            bg = jnp.broadcast_to(b_gate[i].reshape(1, 3, kv_c, g_c), (ts, 3, kv_c, g_c)).transpose(2, 0, 3, 1)
            bg_rows = jnp.pad(bg.reshape(rows, 3), ((0, 0), (0, LANES - 3)))
            o = _nsa_sample_b(page_table, qr.reshape(bs, rows, hd), sel.reshape(bs, rows, -1),
                              z3[:, :, o_kvs:o_kvw].reshape(bs, ts * 2 * kv_c, hd),
                              o_cmp.reshape(bs, rows, hd), o_win.reshape(bs, rows, hd), gl_rows, bg_rows,
                              _tile_view(cache_slc_kv[i]), page, kv_c, g_c, ts, n_pages_c)
            xs = _mm_res([_token_cols(o.reshape(bs, kv_c, ts * g_c, hd), ts, g_c).astype(BF16)], w_out, xs, ms, tn_out)

        g_final = norm_final if layer == depth - 1 else None
        wg, wu, wd = w_gate[layer].astype(BF16), w_up[layer].astype(BF16), w_down[layer].astype(BF16)
        xp = _ffn(xp, norm_ffn[layer], wg, wu, wd, g_final, tm_f, tf)
        xs = _ffn(xs, norm_ffn[layer], wg, wu, wd, g_final, ms, tf)

    def stack(name, which):
        return jnp.stack(outs[name][which])

    return (xp.reshape(bp, t, d), xs.reshape(bs, ts, d),
            stack("a", 0), stack("a", 1), stack("b", 0), stack("b", 1), stack("f", 0), stack("f", 1),
            stack("c", 0), stack("c", 1), stack("s", 0), stack("s", 1), stack("w", 0), stack("w", 1))
```

```python
import functools

import numpy as np
import jax
import jax.numpy as jnp
from jax import lax
from jax.experimental import pallas as pl
from jax.experimental.pallas import tpu as pltpu

F32 = jnp.float32
BF16 = jnp.bfloat16
HIGHEST = lax.Precision.HIGHEST

LANES = 128
SUBLANES = 8
VMEM_LIMIT = 56 * 1024 * 1024
VMEM_LIMIT_BIG = 63 * 1024 * 1024

HEAD_DIM = 128
DA = HEAD_DIM // 2
CMP_LEN = 32
CMP_STRIDE = 16
CMP_HID = 2 * HEAD_DIM
SLC_BLOCK = 64
N_SELECT = 16
WINDOW = 512
ROPE_THETA = 10000.0
NORM_EPS = 1e-6
LOG2E = 1.4426950408889634
NEG = -1e30
BIG = 1e30
LOWEST = -3e38


def _params(*sem):
    return pltpu.CompilerParams(dimension_semantics=sem, vmem_limit_bytes=VMEM_LIMIT)


def _dot(a, b):
    return jnp.dot(a, b, preferred_element_type=F32)


def _dot_nt(a, b):
    return lax.dot_general(a, b, (((1,), (1,)), ((), ())), preferred_element_type=F32)


def _dot_exact(a, b):
    return jnp.dot(a, b, precision=HIGHEST, preferred_element_type=F32)


def _dot_nt_exact(a, b):
    return lax.dot_general(a, b, (((1,), (1,)), ((), ())), precision=HIGHEST, preferred_element_type=F32)


def _iota(shape, axis):
    return lax.broadcasted_iota(jnp.int32, shape, axis)


def _round_up(n, m):
    return -(-n // m) * m


def _pick_tile(n, prefs):
    for p in prefs:
        if n % p == 0:
            return p
    return n


def _rope_lanes(a, cos, sin, half):
    if 2 * half == LANES:
        partner = pltpu.roll(a, half, axis=1)
    else:
        first = (_iota(a.shape, 1) & (2 * half - 1)) < half
        partner = jnp.where(first, pltpu.roll(a, LANES - half, axis=1), pltpu.roll(a, half, axis=1))
    return a * cos + partner * sin


def _rmsnorm_rows(x, g):
    ms = jnp.mean(x * x, axis=-1, keepdims=True)
    return x * lax.rsqrt(ms + NORM_EPS) * g


def _proj_kernel(x_ref, g_ref, w_ref, cos_ref, sin_ref, o_ref, xn_ref, *, rope_ranges, half):
    j = pl.program_id(1)

    @pl.when(j == 0)
    def _():
        rc = min(256, x_ref.shape[0])
        for r in range(x_ref.shape[0] // rc):
            rows = slice(r * rc, (r + 1) * rc)
            xn_ref[rows, :] = _rmsnorm_rows(x_ref[rows, :], g_ref[...]).astype(BF16)

    o_ref[...] = _dot(xn_ref[...], w_ref[...])
    is_rope = functools.reduce(jnp.logical_or, [(j >= a) & (j < b) for a, b in rope_ranges])

    @pl.when(is_rope)
    def _():
        cos, sin = cos_ref[...], sin_ref[...]
        for c in range(o_ref.shape[1] // LANES):
            sl = slice(c * LANES, (c + 1) * LANES)
            o_ref[:, sl] = _rope_lanes(o_ref[:, sl], cos, sin, half)


def _proj(x, g, w, cos, sin, rope_ranges, half, tm, tn):
    m, d = x.shape
    n = w.shape[1]
    return pl.pallas_call(
        functools.partial(_proj_kernel, rope_ranges=rope_ranges, half=half),
        grid=(m // tm, n // tn),
        in_specs=[
            pl.BlockSpec((tm, d), lambda i, j: (i, 0)),
            pl.BlockSpec((1, d), lambda i, j: (0, 0)),
            pl.BlockSpec((d, tn), lambda i, j: (0, j)),
            pl.BlockSpec((tm, LANES), lambda i, j: (i, 0)),
            pl.BlockSpec((tm, LANES), lambda i, j: (i, 0)),
        ],
        out_specs=pl.BlockSpec((tm, tn), lambda i, j: (i, j)),
        out_shape=jax.ShapeDtypeStruct((m, n), F32),
        scratch_shapes=[pltpu.VMEM((tm, d), BF16)],
        compiler_params=pltpu.CompilerParams(dimension_semantics=("parallel", "arbitrary"),
                                             vmem_limit_bytes=VMEM_LIMIT_BIG),
    )(x, g.reshape(1, d), w, cos, sin)


def _mm_res_kernel(*refs, n_lhs):
    a_refs, (w_ref, r_ref, o_ref) = refs[:n_lhs], refs[n_lhs:]
    acc = r_ref[...]
    off = 0
    for a_ref in a_refs:
        kk = a_ref.shape[1]
        acc = acc + _dot(a_ref[...], w_ref[off:off + kk, :])
        off += kk
    o_ref[...] = acc


def _mm_res(lhs, w, res, tm, tn):
    m, n = res.shape
    k = w.shape[0]
    return pl.pallas_call(
        functools.partial(_mm_res_kernel, n_lhs=len(lhs)),
        grid=(m // tm, n // tn),
        in_specs=[pl.BlockSpec((tm, a.shape[1]), lambda i, j: (i, 0)) for a in lhs] + [
            pl.BlockSpec((k, tn), lambda i, j: (0, j)),
            pl.BlockSpec((tm, tn), lambda i, j: (i, j)),
        ],
        out_specs=pl.BlockSpec((tm, tn), lambda i, j: (i, j)),
        out_shape=jax.ShapeDtypeStruct((m, n), F32),
        compiler_params=_params("parallel", "arbitrary"),
    )(*lhs, w, res)


def _ffn_kernel(x_ref, g_ref, wg_ref, wu_ref, wd_ref, gf_ref, o_ref, xn_ref, h_ref, *, final_norm):
    f = pl.program_id(1)
    n_tiles = pl.num_programs(1) - 1

    def hidden():
        xn = xn_ref[...]
        gate = _dot(xn, wg_ref[...])
        up = _dot(xn, wu_ref[...])
        return (gate * jax.nn.sigmoid(gate) * up).astype(BF16)

    rc = min(256, x_ref.shape[0])
    chunks = [slice(r * rc, (r + 1) * rc) for r in range(x_ref.shape[0] // rc)]

    @pl.when(f == 0)
    def _():
        for rows in chunks:
            x = x_ref[rows, :]
            xn_ref[rows, :] = _rmsnorm_rows(x, g_ref[...]).astype(BF16)
            o_ref[rows, :] = x
        h_ref[0] = hidden()

    @pl.when((f > 0) & (f < n_tiles))
    def _():
        h_ref[f % 2] = hidden()
        o_ref[...] += _dot(h_ref[(f - 1) % 2], wd_ref[...])

    @pl.when(f == n_tiles)
    def _():
        o_ref[...] += _dot(h_ref[(f - 1) % 2], wd_ref[...])
        if final_norm:
            for rows in chunks:
                o_ref[rows, :] = _rmsnorm_rows(o_ref[rows, :], gf_ref[...])


def _ffn(x, g, wg, wu, wd, g_final, tm, tf):
    m, d = x.shape
    n_tiles = wg.shape[1] // tf
    final_norm = g_final is not None
    gf = (g_final if final_norm else g).reshape(1, d)
    return pl.pallas_call(
        functools.partial(_ffn_kernel, final_norm=final_norm),
        grid=(m // tm, n_tiles + 1),
        in_specs=[
            pl.BlockSpec((tm, d), lambda i, f: (i, 0), pipeline_mode=pl.Buffered(1)),
            pl.BlockSpec((1, d), lambda i, f: (0, 0)),
            pl.BlockSpec((d, tf), lambda i, f: (0, jnp.minimum(f, n_tiles - 1))),
            pl.BlockSpec((d, tf), lambda i, f: (0, jnp.minimum(f, n_tiles - 1))),
            pl.BlockSpec((tf, d), lambda i, f: (jnp.maximum(f - 1, 0), 0)),
            pl.BlockSpec((1, d), lambda i, f: (0, 0)),
        ],
        out_specs=pl.BlockSpec((tm, d), lambda i, f: (i, 0), pipeline_mode=pl.Buffered(1)),
        out_shape=jax.ShapeDtypeStruct((m, d), F32),
        scratch_shapes=[pltpu.VMEM((tm, d), BF16), pltpu.VMEM((2, tm, tf), BF16)],
        compiler_params=pltpu.CompilerParams(dimension_semantics=("parallel", "arbitrary"),
                                             vmem_limit_bytes=VMEM_LIMIT_BIG),
    )(x, g.reshape(1, d), wg, wu, wd, gf)


def _online_update(carry, s, v, shift=None):
    m, l, acc = carry
    smax = jnp.max(s, axis=-1, keepdims=True)
    if shift is not None:
        smax = smax - shift
    m_new = jnp.maximum(m, smax)
    alpha = jnp.exp2(m - m_new)
    p = jnp.exp2(s - (m_new if shift is None else m_new + shift))
    l = alpha * l + jnp.sum(p, axis=-1, keepdims=True)
    pb = p.astype(BF16)
    if isinstance(v, (list, tuple)):
        n = v[0].shape[0]
        pv = sum(_dot(pb[:, r * n:(r + 1) * n], x) for r, x in enumerate(v))
    else:
        pv = _dot(pb, v)
    return m_new, l, alpha * acc + pv


def _state_init(m_ref, l_ref, acc_ref):
    m_ref[...] = jnp.full(m_ref.shape, NEG, F32)
    l_ref[...] = jnp.zeros(l_ref.shape, F32)
    acc_ref[...] = jnp.zeros(acc_ref.shape, F32)


def _state_update(m_ref, l_ref, acc_ref, k, s, v, shift=None):
    m, l, acc = _online_update((m_ref[k], l_ref[k], acc_ref[k]), s, v, shift)
    m_ref[k], l_ref[k], acc_ref[k] = m, l, acc


def _fold_lanes(x, op):
    out = x[:, :LANES]
    for i in range(1, x.shape[1] // LANES):
        out = op(out, x[:, i * LANES:(i + 1) * LANES])
    return out


def _two_pass_attention(q, kb_ref, vb_ref, n_blocks, tk, s_ref, add_fn=None, tail_fn=None, shift=None):
    rows = q.shape[0]

    def scores(kb, masked):
        k0 = pl.multiple_of(kb * tk, tk)
        s = _dot_nt(q, kb_ref[pl.ds(k0, tk), :])
        if add_fn is not None:
            s = s + add_fn(k0)
        if masked and tail_fn is not None:
            s = s + tail_fn(k0)
        s_ref[kb] = s
        return _fold_lanes(s, jnp.maximum)

    mx = lax.fori_loop(0, n_blocks - 1, lambda kb, mx: jnp.maximum(mx, scores(kb, False)),
                       jnp.full((rows, LANES), NEG, F32))
    mx = jnp.maximum(mx, scores(n_blocks - 1, True))
    m = jnp.max(mx, axis=-1, keepdims=True)
    if shift is not None:
        m = (m - shift) + shift

    def accumulate(kb, carry):
        ls, acc = carry
        k0 = pl.multiple_of(kb * tk, tk)
        p = jnp.exp2(s_ref[kb] - m)
        return ls + _fold_lanes(p, jnp.add), acc + _dot(p.astype(BF16), vb_ref[pl.ds(k0, tk), :])

    ls, acc = lax.fori_loop(0, n_blocks, accumulate,
                            (jnp.zeros((rows, LANES), F32), jnp.zeros((rows, HEAD_DIM), F32)))
    return acc, jnp.sum(ls, axis=-1, keepdims=True)


def _masked_softmax(s, okf):
    ok = okf > 0.5
    sm = jnp.where(ok, s, NEG)
    m = jnp.max(sm, axis=-1, keepdims=True)
    e = jnp.where(ok, jnp.exp2(sm - m), 0.0)
    return e / jnp.maximum(jnp.sum(e, axis=-1, keepdims=True), 1e-30)


def _lane_pick(x, lane_idx):
    return jnp.sum(jnp.where(_iota(x.shape, 1) == lane_idx, x, 0.0), axis=-1, keepdims=True)


def _topk_mask(imp, n_sel):
    lane = _iota(imp.shape, 1).astype(F32)
    sel = jnp.zeros(imp.shape, F32)
    x = imp
    for _ in range(n_sel):
        m = jnp.max(x, axis=-1, keepdims=True)
        idx = jnp.min(jnp.where(x == m, lane, 1e9), axis=-1, keepdims=True)
        hit = lane == idx
        sel = jnp.where(hit, 1.0, sel)
        x = jnp.where(hit, LOWEST, x)
    return sel


def _overlap(ci, sj):
    ov = jnp.minimum(ci + CMP_LEN, sj + SLC_BLOCK) - jnp.maximum(ci, sj)
    return jnp.maximum(ov, 0).astype(F32) * (1.0 / CMP_LEN)


def _topk_mask_t(imp_t, n_sel, ns):
    rows = _round_up(ns, SUBLANES)
    x = imp_t[0:rows]
    j = _iota(x.shape, 0)
    rank = jnp.zeros(x.shape, F32)
    for jp in range(ns):
        a = x[jp:jp + 1, :]
        rank = rank + jnp.where((a > x) | ((a == x) & (jp < j)), 1.0, 0.0)
    sel = jnp.where((rank < n_sel) & (j < ns), 1.0, 0.0)
    return jnp.concatenate([sel, jnp.zeros((imp_t.shape[0] - rows, imp_t.shape[1]), F32)], axis=0)


def _block_importance(imp, qpos, ns, axis=1):
    j = _iota(imp.shape, axis)
    cur = qpos >> 6
    forced = (j == 0) | (j == cur) | (j == cur - 1)
    valid = (j << 6) <= qpos
    out = jnp.where(forced, BIG, jnp.where(valid, imp, NEG))
    return jnp.where(j < ns, out, LOWEST)


def _lam_value(lam_ref, lam_init):
    v = lam_ref[...]
    a = jnp.exp(jnp.sum(v[0:1] * v[1:2], axis=-1, keepdims=True))
    b = jnp.exp(jnp.sum(v[2:3] * v[3:4], axis=-1, keepdims=True))
    return a - b + lam_init


def _gelu_tanh(x):
    return 0.5 * x * (1.0 + jnp.tanh(0.7978845608028654 * (x + 0.044715 * (x * x * x))))


def _log_sigmoid(z):
    return jnp.minimum(z, 0.0) - jnp.log1p(jnp.exp(-jnp.abs(z)))


def _cast_kv_once(step, pairs):
    @pl.when(step == 0)
    def _():
        for src, dst in pairs:
            dst[...] = src[0].astype(BF16)


def _logf_prompt_kernel(z_ref, b_ref, lf_ref, dkc_ref, dkr_ref):
    t = z_ref.shape[1]
    lf = _log_sigmoid(z_ref[0] + b_ref[...])
    lf_ref[0] = lf
    upper = (_iota((LANES, LANES), 1) > _iota((LANES, LANES), 0)).astype(F32)
    carry = jnp.zeros((1, LANES), F32)
    for blk in reversed(range(t // LANES)):
        sl = slice(blk * LANES, (blk + 1) * LANES)
        x = lf[sl]
        d = _dot_exact(upper, x) + carry
        dkc_ref[0, sl, :] = d
        dkr_ref[0, :, sl] = d.T
        carry = carry + jnp.sum(x, axis=0, keepdims=True)


def _logf_prompt(z3, b_pad, col_blk):
    b, t, _ = z3.shape
    return pl.pallas_call(
        _logf_prompt_kernel,
        grid=(b,),
        in_specs=[pl.BlockSpec((1, t, LANES), lambda i: (i, 0, col_blk)),
                  pl.BlockSpec((1, LANES), lambda i: (0, 0))],
        out_specs=[pl.BlockSpec((1, t, LANES), lambda i: (i, 0, 0)),
                   pl.BlockSpec((1, t, LANES), lambda i: (i, 0, 0)),
                   pl.BlockSpec((1, LANES, t), lambda i: (i, 0, 0))],
        out_shape=[jax.ShapeDtypeStruct((b, t, LANES), F32),
                   jax.ShapeDtypeStruct((b, t, LANES), F32),
                   jax.ShapeDtypeStruct((b, LANES, t), F32)],
        compiler_params=_params("parallel"),
    )(z3, b_pad)


def _causal_tail(qpos, tk):
    return lambda k0: jnp.where(k0 + _iota((qpos.shape[0], tk), 1) <= qpos, 0.0, NEG)


def _diff_prompt_kernel(lam_ref, g_ref, q_ref, k_ref, v_ref, o_ref, kb_ref, vb_ref, s_ref, *, tq, gsz, lam_init, tk):
    qi = pl.program_id(2)
    _cast_kv_once(qi, ((k_ref, kb_ref), (v_ref, vb_ref)))
    lam = _lam_value(lam_ref, lam_init)
    lane = _iota((tq, HEAD_DIM), 1)
    qscale = DA ** -0.5 * LOG2E
    qs = [q_ref[0, :, g * HEAD_DIM:(g + 1) * HEAD_DIM] * qscale for g in range(gsz)]
    q = jnp.concatenate([jnp.where(lane < DA, x, 0.0) for x in qs] + [jnp.where(lane >= DA, x, 0.0) for x in qs],
                        axis=0).astype(BF16)
    qpos = qi * tq + (_iota((2 * gsz * tq, 1), 0) & (tq - 1))
    n_blocks = (qi * tq) // tk + 1
    acc, l = _two_pass_attention(q, kb_ref, vb_ref, n_blocks, tk, s_ref, tail_fn=_causal_tail(qpos, tk))
    o = acc / l
    half = gsz * tq
    o = o[:half] - lam * o[half:]
    o = (_rmsnorm_rows(o, g_ref[...]) * (1.0 - lam_init)).astype(BF16)
    for g in range(gsz):
        o_ref[0, :, g * HEAD_DIM:(g + 1) * HEAD_DIM] = o[g * tq:(g + 1) * tq]


def _diff_prompt(z3, lam_vec, subln_g, n_kv, gsz, q_blk0, k_blk0, v_blk0, lam_init, tq, tk):
    b, t, _ = z3.shape
    qw = gsz * HEAD_DIM
    assert tk % tq == 0 and tq & (tq - 1) == 0
    return pl.pallas_call(
        functools.partial(_diff_prompt_kernel, tq=tq, gsz=gsz, lam_init=lam_init, tk=tk),
        grid=(b, n_kv, t // tq),
        in_specs=[
            pl.BlockSpec((4, DA), lambda i, k, q: (0, 0)),
            pl.BlockSpec((1, HEAD_DIM), lambda i, k, q: (0, 0)),
            pl.BlockSpec((1, tq, qw), lambda i, k, q: (i, q, q_blk0 + k)),
            pl.BlockSpec((1, t, HEAD_DIM), lambda i, k, q: (i, 0, k_blk0 + k)),
            pl.BlockSpec((1, t, HEAD_DIM), lambda i, k, q: (i, 0, v_blk0 + k)),
        ],
        out_specs=pl.BlockSpec((1, tq, qw), lambda i, k, q: (i, q, k)),
        out_shape=jax.ShapeDtypeStruct((b, t, n_kv * qw), BF16),
        scratch_shapes=[pltpu.VMEM((t, HEAD_DIM), BF16), pltpu.VMEM((t, HEAD_DIM), BF16),
                        pltpu.VMEM((t // tk, 2 * gsz * tq, tk), F32)],
        compiler_params=_params("parallel", "parallel", "arbitrary"),
    )(lam_vec, subln_g.reshape(1, HEAD_DIM), z3, z3, z3)


def _fox_prompt_kernel(q_ref, k_ref, v_ref, dkr_ref, dkc_ref, o_ref, kb_ref, vb_ref, s_ref, *, tq, gsz, tk):
    kh = pl.program_id(1)
    qi = pl.program_id(2)
    _cast_kv_once(qi, ((k_ref, kb_ref), (v_ref, vb_ref)))
    qscale = HEAD_DIM ** -0.5 * LOG2E
    qpos = qi * tq + (_iota((gsz * tq, 1), 0) & (tq - 1))
    n_blocks = (qi * tq) // tk + 1
    q = jnp.concatenate([q_ref[0, :, g * HEAD_DIM:(g + 1) * HEAD_DIM] * qscale for g in range(gsz)],
                        axis=0).astype(BF16)
    dq = jnp.concatenate([_lane_pick(dkc_ref[0], kh * gsz + g) for g in range(gsz)], axis=0) * LOG2E

    def dk_fn(k0):
        return jnp.concatenate([jnp.broadcast_to(dkr_ref[0, g, :, pl.ds(k0, tk)] * LOG2E, (tq, tk))
                                for g in range(gsz)], axis=0)

    acc, l = _two_pass_attention(q, kb_ref, vb_ref, n_blocks, tk, s_ref, add_fn=dk_fn,
                                 tail_fn=_causal_tail(qpos, tk), shift=dq)
    o = (acc / l).astype(BF16)
    for g in range(gsz):
        o_ref[0, :, g * HEAD_DIM:(g + 1) * HEAD_DIM] = o[g * tq:(g + 1) * tq]


def _fox_prompt(z3, dkr4, dkc, n_kv, gsz, q_blk0, k_blk0, v_blk0, tq, tk):
    b, t, _ = z3.shape
    qw = gsz * HEAD_DIM
    assert tk % tq == 0
    return pl.pallas_call(
        functools.partial(_fox_prompt_kernel, tq=tq, gsz=gsz, tk=tk),
        grid=(b, n_kv, t // tq),
        in_specs=[
            pl.BlockSpec((1, tq, qw), lambda i, k, q: (i, q, q_blk0 + k)),
            pl.BlockSpec((1, t, HEAD_DIM), lambda i, k, q: (i, 0, k_blk0 + k)),
            pl.BlockSpec((1, t, HEAD_DIM), lambda i, k, q: (i, 0, v_blk0 + k)),
            pl.BlockSpec((1, gsz, 1, t), lambda i, k, q: (i, k, 0, 0)),
            pl.BlockSpec((1, tq, LANES), lambda i, k, q: (i, q, 0)),
        ],
        out_specs=pl.BlockSpec((1, tq, qw), lambda i, k, q: (i, q, k)),
        out_shape=jax.ShapeDtypeStruct((b, t, n_kv * qw), BF16),
        scratch_shapes=[pltpu.VMEM((t, HEAD_DIM), BF16), pltpu.VMEM((t, HEAD_DIM), BF16),
                        pltpu.VMEM((t // tk, gsz * tq, tk), F32)],
        compiler_params=_params("parallel", "parallel", "arbitrary"),
    )(z3, z3, z3, dkr4, dkc)


def _logf_sample_kernel(z_ref, b_ref, lf_ref, dq_ref, tot_ref, *, ts):
    m = z_ref.shape[0]
    lf = _log_sigmoid(z_ref[...] + b_ref[...])
    lf_ref[...] = lf
    i, j = _iota((m, m), 0), _iota((m, m), 1)
    same = (i // ts) == (j // ts)
    dq_ref[...] = _dot_exact((same & (j > i)).astype(F32), lf)
    tot_ref[...] = _dot_exact(same.astype(F32), lf)


def _logf_sample(z_fg, b_pad, ts):
    m = z_fg.shape[0]
    sds = jax.ShapeDtypeStruct((m, LANES), F32)
    return pl.pallas_call(
        functools.partial(_logf_sample_kernel, ts=ts),
        out_shape=[sds, sds, sds],
    )(z_fg, b_pad)


def _pad_rows_scratch(ref, x):
    ref[...] = jnp.zeros(ref.shape, ref.dtype)
    ref[0:x.shape[0], :] = x


def _page_specs(page, n_kv, n_pages, first_page):
    rows = page * 2 * n_kv
    return [pl.BlockSpec((rows, HEAD_DIM), lambda b, p, pt, r=r: (pt[b, first_page(b, p, pt) + r], 0))
            for r in range(n_pages)]


def _tile_view(cache):
    n_kv = cache.shape[3]
    assert 2 * n_kv in (SUBLANES, 2 * SUBLANES)
    return cache.reshape(-1, 2 * n_kv // SUBLANES, SUBLANES, HEAD_DIM)


def _tile_specs(page, tiles, n_pages, first_page):
    return [pl.BlockSpec((page, tiles, SUBLANES, HEAD_DIM),
                         lambda b, p, pt, r=r: (pt[b, first_page(b, p, pt) + r], 0, 0, 0))
            for r in range(n_pages)]


def _flat_kv(pg_refs):
    n = pg_refs[0].shape[0] * SUBLANES
    if pg_refs[0].shape[1] == 2:
        ks = [pg[:, 0].reshape(n, HEAD_DIM) for pg in pg_refs]
        vs = [pg[:, 1].reshape(n, HEAD_DIM) for pg in pg_refs]
        off = 0
    else:
        vs = [pg[:, 0].reshape(n, HEAD_DIM) for pg in pg_refs]
        ks = [pltpu.roll(v, SUBLANES // 2, axis=0) for v in vs]
        off = SUBLANES // 2
    return [k.astype(BF16) for k in ks], [v.astype(BF16) for v in vs], off


def _flat_scores(q, ks):
    return jnp.concatenate([_dot_nt(q, k) for k in ks], axis=1)


def _own_head_bias(row_head, n_cols):
    return jnp.where((_iota((row_head.shape[0], n_cols), 1) & (SUBLANES - 1)) == row_head, 0.0, NEG)


def _diff_sample_kernel(pt_ref, lam_ref, g_ref, q_ref, kn_ref, vn_ref, *refs, n_kv, gsz, ts, lam_init, n_pages):
    pg_refs = refs[:n_pages]
    o_ref, m_ref, l_ref, acc_ref, bias_ref, kpad_ref, vpad_ref = refs[n_pages:]
    p = pl.program_id(1)
    last = pl.num_programs(1) - 1
    r1 = n_kv * ts * gsz
    q = q_ref[0] * (DA ** -0.5 * LOG2E)
    lane = _iota((r1, HEAD_DIM), 1)
    qb = jnp.concatenate([jnp.where(lane < DA, q, 0.0), jnp.where(lane >= DA, q, 0.0)], axis=0).astype(BF16)
    rr = _iota((2 * r1, 1), 0) % r1
    row_head = rr // (ts * gsz)
    row_tok = (rr % (ts * gsz)) // gsz

    @pl.when(p == 0)
    def _():
        _state_init(m_ref, l_ref, acc_ref)

    @pl.when(p < last)
    def _():
        ks, vs, off = _flat_kv(pg_refs)

        @pl.when(p == 0)
        def _():
            bias_ref[...] = _own_head_bias(row_head + off, bias_ref.shape[1])

        _state_update(m_ref, l_ref, acc_ref, ..., _flat_scores(qb, ks) + bias_ref[...], vs)

    @pl.when(p == last)
    def _():
        _pad_rows_scratch(kpad_ref, kn_ref[0])
        _pad_rows_scratch(vpad_ref, vn_ref[0])
        s = _dot_nt(qb, kpad_ref[...].astype(BF16))
        c = _iota(s.shape, 1)
        ok = ((c % n_kv) == row_head) & ((c // n_kv) <= row_tok)
        _state_update(m_ref, l_ref, acc_ref, ..., jnp.where(ok, s, NEG), vpad_ref[...].astype(BF16))
        o = acc_ref[...] / l_ref[...]
        o = o[:r1] - _lam_value(lam_ref, lam_init) * o[r1:]
        o_ref[0] = (_rmsnorm_rows(o, g_ref[...]) * (1.0 - lam_init)).astype(BF16)


def _diff_sample(page_table, lam_vec, subln_g, q_rows, k_new, v_new, cache_tiles, page, n_kv, gsz, ts, lam_init, n_pages):
    bs, r1, _ = q_rows.shape
    nsteps = page_table.shape[1] // n_pages
    first = lambda b, p, pt: jnp.minimum(p, nsteps - 1) * n_pages
    row_spec = lambda n: pl.BlockSpec((1, n, HEAD_DIM), lambda b, p, pt: (b, 0, 0))
    assert ts * n_kv <= LANES
    grid_spec = pltpu.PrefetchScalarGridSpec(
        num_scalar_prefetch=1,
        grid=(bs, nsteps + 1),
        in_specs=[
            pl.BlockSpec((4, DA), lambda b, p, pt: (0, 0)),
            pl.BlockSpec((1, HEAD_DIM), lambda b, p, pt: (0, 0)),
            row_spec(r1), row_spec(ts * n_kv), row_spec(ts * n_kv),
        ] + _tile_specs(page, cache_tiles.shape[1], n_pages, first),
        out_specs=row_spec(r1),
        scratch_shapes=[pltpu.VMEM((2 * r1, 1), F32), pltpu.VMEM((2 * r1, 1), F32), pltpu.VMEM((2 * r1, HEAD_DIM), F32),
                        pltpu.VMEM((2 * r1, n_pages * page * SUBLANES), F32),
                        pltpu.VMEM((LANES, HEAD_DIM), F32), pltpu.VMEM((LANES, HEAD_DIM), F32)],
    )
    return pl.pallas_call(
        functools.partial(_diff_sample_kernel, n_kv=n_kv, gsz=gsz, ts=ts, lam_init=lam_init, n_pages=n_pages),
        grid_spec=grid_spec,
        out_shape=jax.ShapeDtypeStruct((bs, r1, HEAD_DIM), BF16),
        compiler_params=_params("parallel", "arbitrary"),
    )(page_table, lam_vec, subln_g.reshape(1, HEAD_DIM), q_rows, k_new, v_new, *([cache_tiles] * n_pages))


def _fox_sample_kernel(pt_ref, q_ref, kn_ref, vn_ref, dq_ref, dkn_ref, tot_ref, *refs, n_kv, gsz, ts, n_pages):
    pg_refs, lf_refs = refs[:n_pages], refs[n_pages:2 * n_pages]
    o_ref, m_ref, l_ref, acc_ref, car_ref, bias_ref, kpad_ref, vpad_ref = refs[2 * n_pages:]
    p = pl.program_id(1)
    page = lf_refs[0].shape[2]
    rows = n_kv * ts * gsz
    n_h = n_kv * gsz
    qb = (q_ref[0] * (HEAD_DIM ** -0.5 * LOG2E)).astype(BF16)
    rr = _iota((rows, 1), 0)
    row_head = rr // (ts * gsz)
    row_tok = (rr % (ts * gsz)) // gsz
    shift = dq_ref[0][:, 0:1] * LOG2E

    @pl.when(p == 0)
    def _():
        _state_init(m_ref, l_ref, acc_ref)
        car_ref[...] = tot_ref[0]
        _pad_rows_scratch(kpad_ref, kn_ref[0])
        _pad_rows_scratch(vpad_ref, vn_ref[0])
        s = _dot_nt(qb, kpad_ref[...].astype(BF16)) + dkn_ref[0] * LOG2E
        c = _iota(s.shape, 1)
        ok = ((c % n_kv) == row_head) & ((c // n_kv) <= row_tok)
        _state_update(m_ref, l_ref, acc_ref, ..., jnp.where(ok, s, NEG), vpad_ref[...].astype(BF16), shift=shift)

    @pl.when(p > 0)
    def _():
        ks, vs, off = _flat_kv(pg_refs)

        @pl.when(p == 1)
        def _():
            bias_ref[...] = _own_head_bias(row_head + off, bias_ref.shape[1])

        lower = (_iota((page, page), 0) > _iota((page, page), 1)).astype(F32)
        dks = [None] * n_pages
        for r in reversed(range(n_pages)):
            lft = lf_refs[r][0]
            dks[r] = _dot_exact(lft, lower) + car_ref[...]
            car_ref[...] = car_ref[...] + jnp.sum(lft, axis=1, keepdims=True)
        dk = jnp.concatenate(dks, axis=1)
        hi = dk.astype(BF16)
        rest = dk - hi.astype(F32)
        mid = rest.astype(BF16)
        terms = (hi, mid, (rest - mid.astype(F32)).astype(BF16))
        hh = _iota((rows, n_h), 0)
        pick = ((hh // (ts * gsz)) * gsz + hh % gsz == _iota((rows, n_h), 1)).astype(BF16)
        spread = ((_iota((page, page * SUBLANES), 1) >> 3) == _iota((page, page * SUBLANES), 0)).astype(BF16)
        on_rows = [_dot(pick, x).astype(BF16) for x in terms]
        pieces = jnp.concatenate([x[:, r * page:(r + 1) * page] for r in range(n_pages) for x in on_rows], axis=0)
        flat = _dot(pieces, spread)
        nt = len(terms)
        dkf = jnp.concatenate(
            [sum(flat[(r * nt + q) * rows:(r * nt + q + 1) * rows] for q in range(nt)) for r in range(n_pages)],
            axis=1) * LOG2E
        _state_update(m_ref, l_ref, acc_ref, ..., _flat_scores(qb, ks) + dkf + bias_ref[...], vs, shift=shift)

    @pl.when(p == pl.num_programs(1) - 1)
    def _():
        o_ref[0] = (acc_ref[...] / l_ref[...]).astype(BF16)


def _fox_sample(page_table, q_rows, k_new, v_new, dq_b, dkn_b, tot, cache_tiles, logf_t, page, n_kv, gsz, ts, n_pages):
    bs, rows, _ = q_rows.shape
    n_h = n_kv * gsz
    nsteps = page_table.shape[1] // n_pages
    first = lambda b, p, pt: (nsteps - jnp.maximum(p, 1)) * n_pages
    lf_specs = [pl.BlockSpec((1, n_h, page), lambda b, p, pt, r=r: (pt[b, first(b, p, pt) + r], 0, 0))
                for r in range(n_pages)]
    row_spec = lambda n: pl.BlockSpec((1, n, HEAD_DIM), lambda b, p, pt: (b, 0, 0))
    assert ts * n_kv <= LANES
    grid_spec = pltpu.PrefetchScalarGridSpec(
        num_scalar_prefetch=1,
        grid=(bs, nsteps + 1),
        in_specs=[
            row_spec(rows), row_spec(ts * n_kv), row_spec(ts * n_kv), row_spec(rows), row_spec(rows),
            pl.BlockSpec((1, n_h, 1), lambda b, p, pt: (b, 0, 0)),
        ] + _tile_specs(page, cache_tiles.shape[1], n_pages, first) + lf_specs,
        out_specs=row_spec(rows),
        scratch_shapes=[pltpu.VMEM((rows, 1), F32), pltpu.VMEM((rows, 1), F32), pltpu.VMEM((rows, HEAD_DIM), F32),
                        pltpu.VMEM((n_h, 1), F32), pltpu.VMEM((rows, n_pages * page * SUBLANES), F32),
                        pltpu.VMEM((LANES, HEAD_DIM), F32), pltpu.VMEM((LANES, HEAD_DIM), F32)],
    )
    return pl.pallas_call(
        functools.partial(_fox_sample_kernel, n_kv=n_kv, gsz=gsz, ts=ts, n_pages=n_pages),
        grid_spec=grid_spec,
        out_shape=jax.ShapeDtypeStruct((bs, rows, HEAD_DIM), BF16),
        compiler_params=_params("parallel", "arbitrary"),
    )(page_table, q_rows, k_new, v_new, dq_b, dkn_b, tot, *([cache_tiles] * n_pages), *([logf_t] * n_pages))


def _compress_mlp(acc_a, acc_b_next, w2):
    return _dot(_gelu_tanh(acc_a + acc_b_next).astype(BF16), w2).astype(BF16)


def _compress_prompt_kernel(z_ref, pe_ref, w1_ref, w2_ref, o_ref, *, nch):
    acc_a = jnp.zeros((nch, CMP_HID), F32)
    acc_b = jnp.zeros((nch, CMP_HID), F32)
    for l in range(0, CMP_STRIDE, 2):
        xs = [z_ref[0, pl.ds(l + i, nch, stride=CMP_STRIDE), :] for i in range(2)]
        for acc_is_b, off in ((False, l), (True, CMP_STRIDE + l)):
            lhs = jnp.concatenate([(xs[i] + pe_ref[0, off + i:off + i + 1, :]).astype(BF16) for i in range(2)], axis=1)
            term = _dot(lhs, w1_ref[0, off:off + 2].reshape(2 * HEAD_DIM, CMP_HID))
            if acc_is_b:
                acc_b = acc_b + term
            else:
                acc_a = acc_a + term
    o_ref[0, 0, 0] = _compress_mlp(acc_a, pltpu.roll(acc_b, nch - 1, axis=0), w2_ref[0])


def _compress_prompt(z3, pe, w1, w2, n_kv, col_blk0):
    b, t, _ = z3.shape
    nch = t // CMP_STRIDE
    return pl.pallas_call(
        functools.partial(_compress_prompt_kernel, nch=nch),
        grid=(b, 2 * n_kv),
        in_specs=[
            pl.BlockSpec((1, t, HEAD_DIM), lambda i, c: (i, 0, col_blk0 + c)),
            pl.BlockSpec((1, CMP_LEN, HEAD_DIM), lambda i, c: (c // n_kv, 0, 0)),
            pl.BlockSpec((1, CMP_LEN, HEAD_DIM, CMP_HID), lambda i, c: (c // n_kv, 0, 0, 0)),
            pl.BlockSpec((1, CMP_HID, HEAD_DIM), lambda i, c: (c // n_kv, 0, 0)),
        ],
        out_specs=pl.BlockSpec((1, 1, 1, nch, HEAD_DIM), lambda i, c: (i, c // n_kv, c % n_kv, 0, 0)),
        out_shape=jax.ShapeDtypeStruct((b, 2, n_kv, nch, HEAD_DIM), BF16),
        compiler_params=_params("parallel", "arbitrary"),
    )(z3, pe, w1, w2)


def _nsa_prompt_kernel(q_ref, cos_ref, sin_ref, kc_ref, vc_ref, ks_ref, vs_ref, kw_ref, vw_ref, gl_ref, bg_ref,
                       o_ref, ksb_ref, vsb_ref, kwb_ref, vwb_ref, s_ref, *, tq, gsz, n_kv, nblk, ns, tk):
    kh = pl.program_id(1)
    qi = pl.program_id(2)
    _cast_kv_once(qi, ((ks_ref, ksb_ref), (vs_ref, vsb_ref), (kw_ref, kwb_ref), (vw_ref, vwb_ref)))
    t0 = qi * tq
    qscale = HEAD_DIM ** -0.5 * LOG2E
    cos, sin = cos_ref[...], sin_ref[...]
    qs = [q_ref[0, :, g * HEAD_DIM:(g + 1) * HEAD_DIM] * qscale for g in range(gsz)]
    q = jnp.concatenate(qs, axis=0).astype(BF16)
    qr = jnp.concatenate([_rope_lanes(x, cos, sin, HEAD_DIM // 2) for x in qs], axis=0).astype(BF16)
    qpos = t0 + _iota((tq, 1), 0)

    def tile_g(x):
        return jnp.concatenate([x] * gsz, axis=0)

    nch = kc_ref.shape[3]
    n_i = _iota((tq, nch), 1)
    ok_c = ((n_i * CMP_STRIDE + CMP_LEN - 1 <= qpos) & (n_i < nblk)).astype(F32)
    p = _masked_softmax(_dot_nt(q, kc_ref[0, 0, 0]), tile_g(ok_c))
    o_cmp = _dot(p.astype(BF16), vc_ref[0, 0, 0])
    psum = p[0:tq]
    for g in range(1, gsz):
        psum = psum + p[g * tq:(g + 1) * tq]
    ov_t = _overlap(_iota((LANES, nch), 1) * CMP_STRIDE, _iota((LANES, nch), 0) * SLC_BLOCK)
    imp_t = _block_importance(_dot_nt_exact(ov_t, psum), t0 + _iota((1, tq), 1), ns, axis=0)
    sel = _topk_mask_t(imp_t, min(N_SELECT, ns), ns).T.astype(BF16)

    def slc_mask(k0):
        expand = (((k0 + _iota((LANES, tk), 1)) >> 6) == _iota((LANES, tk), 0)).astype(BF16)
        selk = _dot(sel, expand)
        return tile_g(jnp.where((selk > 0.5) & (k0 + _iota((tq, tk), 1) <= qpos), 0.0, NEG))

    n_blocks = (t0 + tq + tk - 1) // tk
    acc, l = _two_pass_attention(qr, ksb_ref, vsb_ref, n_blocks, tk, s_ref, add_fn=slc_mask)
    o_slc = acc / l

    wlen = WINDOW + tq
    w0 = pl.multiple_of(jnp.maximum(t0 - WINDOW, 0), tq)
    dist = qpos - (w0 + _iota((tq, wlen), 1))
    sw = _dot_nt(qr, kwb_ref[pl.ds(w0, wlen), :]) + tile_g(jnp.where((dist >= 0) & (dist < WINDOW), 0.0, NEG))
    ew = jnp.exp2(sw - jnp.max(sw, axis=-1, keepdims=True))
    o_win = _dot(ew.astype(BF16), vwb_ref[pl.ds(w0, wlen), :]) / jnp.sum(ew, axis=-1, keepdims=True)

    gates = jax.nn.sigmoid(gl_ref[0] + bg_ref[...])
    n_h = n_kv * gsz
    for g in range(gsz):
        rows = slice(g * tq, (g + 1) * tq)
        col = kh * gsz + g
        o = (_lane_pick(gates, col) * o_cmp[rows] + _lane_pick(gates, n_h + col) * o_slc[rows]
             + _lane_pick(gates, 2 * n_h + col) * o_win[rows])
        o_ref[0, :, g * HEAD_DIM:(g + 1) * HEAD_DIM] = o.astype(BF16)


def _nsa_prompt(z3, cos, sin, kvc, bg_pad, n_kv, gsz, ks_blk, vs_blk, kw_blk, vw_blk, gl_blk, tq, tk):
    b, t, _ = z3.shape
    nch = kvc.shape[3]
    qw = gsz * HEAD_DIM
    ns = -(-t // SLC_BLOCK)
    assert t >= WINDOW + tq and t % tk == 0 and tk % SLC_BLOCK == 0 and ns <= LANES
    kv_spec = lambda blk0: pl.BlockSpec((1, t, HEAD_DIM), lambda i, k, q: (i, 0, blk0 + k))
    return pl.pallas_call(
        functools.partial(_nsa_prompt_kernel, tq=tq, gsz=gsz, n_kv=n_kv, nblk=nch - 1, ns=ns, tk=tk),
        grid=(b, n_kv, t // tq),
        in_specs=[
            pl.BlockSpec((1, tq, qw), lambda i, k, q: (i, q, k)),
            pl.BlockSpec((tq, LANES), lambda i, k, q: (q, 0)),
            pl.BlockSpec((tq, LANES), lambda i, k, q: (q, 0)),
            pl.BlockSpec((1, 1, 1, nch, HEAD_DIM), lambda i, k, q: (i, 0, k, 0, 0)),
            pl.BlockSpec((1, 1, 1, nch, HEAD_DIM), lambda i, k, q: (i, 1, k, 0, 0)),
            kv_spec(ks_blk), kv_spec(vs_blk), kv_spec(kw_blk), kv_spec(vw_blk),
            pl.BlockSpec((1, tq, LANES), lambda i, k, q: (i, q, gl_blk)),
            pl.BlockSpec((1, LANES), lambda i, k, q: (0, 0)),
        ],
        out_specs=pl.BlockSpec((1, tq, qw), lambda i, k, q: (i, q, k)),
        out_shape=jax.ShapeDtypeStruct((b, t, n_kv * qw), BF16),
        scratch_shapes=[pltpu.VMEM((t, HEAD_DIM), BF16) for _ in range(4)] + [
            pltpu.VMEM((t // tk, gsz * tq, tk), F32)],
        compiler_params=_params("parallel", "parallel", "arbitrary"),
    )(z3, cos, sin, kvc, kvc, z3, z3, z3, z3, z3, bg_pad)


def _compress_sample_kernel(pt_ref, *refs, n_kv, n_pages, page):
    pg_refs = refs[:n_pages]
    pe_ref, w1_ref, w2_ref, o_ref, last_ref = refs[n_pages:]
    s = pl.program_id(1)
    cpp = page // CMP_STRIDE
    m = n_pages * cpp
    rpr = 2 * n_kv

    @pl.when(s == 0)
    def _():
        last_ref[...] = jnp.zeros(last_ref.shape, F32)

    for typ in range(2):
        acc_a = jnp.zeros((n_kv * m, CMP_HID), F32)
        acc_b = jnp.zeros((n_kv * m, CMP_HID), F32)
        for l in range(0, CMP_STRIDE, 2):
            xs = [jnp.concatenate(
                [pg[pl.ds((l + i) * rpr + typ * n_kv + h, cpp, stride=CMP_STRIDE * rpr), :]
                 for h in range(n_kv) for pg in pg_refs], axis=0) for i in range(2)]
            for acc_is_b, off in ((False, l), (True, CMP_STRIDE + l)):
                lhs = jnp.concatenate([(xs[i] + pe_ref[typ, off + i:off + i + 1, :]).astype(BF16)
                                       for i in range(2)], axis=1)
                term = _dot(lhs, w1_ref[typ, off:off + 2].reshape(2 * HEAD_DIM, CMP_HID))
                if acc_is_b:
                    acc_b = acc_b + term
                else:
                    acc_a = acc_a + term
        first = _iota((m, 1), 0) == 0
        for h in range(n_kv):
            a = acc_a[h * m:(h + 1) * m]
            idx = typ * n_kv + h
            prev = jnp.where(first, last_ref[idx, SUBLANES - 1:SUBLANES, :], pltpu.roll(a, 1, axis=0))
            last_ref[idx] = a[m - SUBLANES:m]
            o_ref[0, typ, h] = _compress_mlp(prev, acc_b[h * m:(h + 1) * m], w2_ref[typ])


def _compress_sample(page_table, cache_rows, pe, w1, w2, page, n_kv, n_pages):
    bs, npg = page_table.shape
    m = n_pages * (page // CMP_STRIDE)
    grid_spec = pltpu.PrefetchScalarGridSpec(
        num_scalar_prefetch=1,
        grid=(bs, npg // n_pages),
        in_specs=_page_specs(page, n_kv, n_pages, lambda b, s, pt: s * n_pages) + [
            pl.BlockSpec((2, CMP_LEN, HEAD_DIM), lambda b, s, pt: (0, 0, 0)),
            pl.BlockSpec((2, CMP_LEN, HEAD_DIM, CMP_HID), lambda b, s, pt: (0, 0, 0, 0)),
            pl.BlockSpec((2, CMP_HID, HEAD_DIM), lambda b, s, pt: (0, 0, 0)),
        ],
        out_specs=pl.BlockSpec((1, 2, n_kv, m, HEAD_DIM), lambda b, s, pt: (b, 0, 0, s, 0)),
        scratch_shapes=[pltpu.VMEM((2 * n_kv, SUBLANES, CMP_HID), F32)],
    )
    return pl.pallas_call(
        functools.partial(_compress_sample_kernel, n_kv=n_kv, n_pages=n_pages, page=page),
        grid_spec=grid_spec,
        out_shape=jax.ShapeDtypeStruct((bs, 2, n_kv, npg * (page // CMP_STRIDE), HEAD_DIM), BF16),
        compiler_params=_params("parallel", "arbitrary"),
    )(page_table, *([cache_rows] * n_pages), pe, w1, w2)


def _nsa_sample_a_kernel(q_ref, cos_ref, sin_ref, kc_ref, vc_ref, win_ref, kwn_ref,
                         ocmp_ref, owin_ref, sel_ref, qr_ref, kpad_ref, vpad_ref, *, n_kv, gsz, ts, p0, ns):
    qscale = HEAD_DIM ** -0.5 * LOG2E
    rows = ts * gsz
    nr = kc_ref.shape[3]
    nsp = sel_ref.shape[3]
    wrows = win_ref.shape[1] // (2 * n_kv)
    tok = _iota((rows, 1), 0) // gsz
    qpos = p0 + tok
    r_i = _iota((rows, nr), 1)
    ok_c = ((r_i >= 1) & ((r_i - 1) * CMP_STRIDE + CMP_LEN - 1 <= qpos)).astype(F32)
    ov = _overlap((_iota((nr, nsp), 0) - 1) * CMP_STRIDE, _iota((nr, nsp), 1) * SLC_BLOCK)
    group = ((_iota((rows, rows), 0) // gsz) == (_iota((rows, rows), 1) // gsz)).astype(F32)
    ok_w1 = _iota((rows, wrows), 1) > tok + (wrows - WINDOW)
    ok_w2 = _iota((rows, kpad_ref.shape[0]), 1) <= tok
    kvw = n_kv * HEAD_DIM
    for kh in range(n_kv):
        q = q_ref[0, kh]
        qr = _rope_lanes(q, cos_ref[...], sin_ref[...], HEAD_DIM // 2)
        qr_ref[0, kh] = qr
        qb, qrb = (q * qscale).astype(BF16), (qr * qscale).astype(BF16)
        p = _masked_softmax(_dot_nt(qb, kc_ref[0, 0, kh]), ok_c)
        ocmp_ref[0, kh] = _dot(p.astype(BF16), vc_ref[0, 0, kh])
        imp = _dot_exact(group, _dot_exact(p, ov))
        sel_ref[0, kh] = _topk_mask(_block_importance(imp, qpos, ns), min(N_SELECT, ns))
        sl = slice(kh * HEAD_DIM, (kh + 1) * HEAD_DIM)
        slv = slice(kvw + kh * HEAD_DIM, kvw + (kh + 1) * HEAD_DIM)
        _pad_rows_scratch(kpad_ref, kwn_ref[0, :, sl])
        _pad_rows_scratch(vpad_ref, kwn_ref[0, :, slv])
        k_win = win_ref[0, pl.ds(kh, wrows, stride=2 * n_kv), :].astype(BF16)
        v_win = win_ref[0, pl.ds(n_kv + kh, wrows, stride=2 * n_kv), :].astype(BF16)
        s1 = jnp.where(ok_w1, _dot_nt(qrb, k_win), NEG)
        s2 = jnp.where(ok_w2, _dot_nt(qrb, kpad_ref[...].astype(BF16)), NEG)
        m = jnp.maximum(jnp.max(s1, axis=-1, keepdims=True), jnp.max(s2, axis=-1, keepdims=True))
        e1, e2 = jnp.exp2(s1 - m), jnp.exp2(s2 - m)
        l = jnp.sum(e1, axis=-1, keepdims=True) + jnp.sum(e2, axis=-1, keepdims=True)
        o = _dot(e1.astype(BF16), v_win) + _dot(e2.astype(BF16), vpad_ref[...].astype(BF16))
        owin_ref[0, kh] = o / l


def _nsa_sample_a(q_rows, cos, sin, kvc, win_rows, kvw_new, n_kv, gsz, ts, p0, ns):
    bs = q_rows.shape[0]
    rows = ts * gsz
    nr = kvc.shape[3]
    nsp = _round_up(ns, LANES)
    wr, w = win_rows.shape[1], kvw_new.shape[2]
    assert wr == WINDOW * 2 * n_kv
    o_sds = jax.ShapeDtypeStruct((bs, n_kv, rows, HEAD_DIM), F32)
    o_spec = pl.BlockSpec((1, n_kv, rows, HEAD_DIM), lambda b: (b, 0, 0, 0))
    return pl.pallas_call(
        functools.partial(_nsa_sample_a_kernel, n_kv=n_kv, gsz=gsz, ts=ts, p0=p0, ns=ns),
        grid=(bs,),
        in_specs=[
            o_spec,
            pl.BlockSpec((rows, LANES), lambda b: (0, 0)),
            pl.BlockSpec((rows, LANES), lambda b: (0, 0)),
            pl.BlockSpec((1, 1, n_kv, nr, HEAD_DIM), lambda b: (b, 0, 0, 0, 0)),
            pl.BlockSpec((1, 1, n_kv, nr, HEAD_DIM), lambda b: (b, 1, 0, 0, 0)),
            pl.BlockSpec((1, wr, HEAD_DIM), lambda b: (b, 0, 0)),
            pl.BlockSpec((1, ts, w), lambda b: (b, 0, 0)),
        ],
        out_specs=[o_spec, o_spec, pl.BlockSpec((1, n_kv, rows, nsp), lambda b: (b, 0, 0, 0)), o_spec],
        out_shape=[o_sds, o_sds, jax.ShapeDtypeStruct((bs, n_kv, rows, nsp), F32), o_sds],
        scratch_shapes=[pltpu.VMEM((LANES, HEAD_DIM), F32), pltpu.VMEM((LANES, HEAD_DIM), F32)],
        compiler_params=_params("parallel"),
    )(q_rows, cos, sin, kvc, kvc, win_rows, kvw_new)


def _nsa_sample_b_kernel(pt_ref, q_ref, sel_ref, kn_ref, ocmp_ref, owin_ref, gl_ref, bg_ref, *refs,
                         n_kv, gsz, ts, n_pages):
    pg_refs = refs[:n_pages]
    o_ref, m_ref, l_ref, acc_ref, bias_ref, npad_ref = refs[n_pages:]
    p = pl.program_id(1)
    last = pl.num_programs(1) - 1
    rows = n_kv * ts * gsz
    page = pg_refs[0].shape[0]
    nsp = sel_ref.shape[2]
    bps = n_pages * (page // SLC_BLOCK)
    blk_cols = SLC_BLOCK * SUBLANES
    qb = (q_ref[0] * (HEAD_DIM ** -0.5 * LOG2E)).astype(BF16)
    rr = _iota((rows, 1), 0)
    row_head = rr // (ts * gsz)
    row_tok = (rr % (ts * gsz)) // gsz

    @pl.when(p == 0)
    def _():
        _state_init(m_ref, l_ref, acc_ref)

    @pl.when(p < last)
    def _():
        ks, vs, off = _flat_kv(pg_refs)

        @pl.when(p == 0)
        def _():
            bias_ref[...] = _own_head_bias(row_head + off, bias_ref.shape[1])

        shift = (_iota((nsp, LANES), 0) == p * bps + _iota((nsp, LANES), 1)).astype(BF16)
        flags = jnp.where(_dot(sel_ref[0].astype(BF16), shift) > 0.5, 0.0, NEG)
        sel_bias = jnp.concatenate([jnp.broadcast_to(flags[:, i:i + 1], (rows, blk_cols)) for i in range(bps)], axis=1)
        _state_update(m_ref, l_ref, acc_ref, ..., _flat_scores(qb, ks) + bias_ref[...] + sel_bias, vs)

    @pl.when(p == last)
    def _():
        _pad_rows_scratch(npad_ref, kn_ref[0])
        flat = npad_ref[...]
        s = _dot_nt(qb, pltpu.roll(flat, n_kv, axis=0).astype(BF16))
        c = _iota(s.shape, 1)
        ok = (((c % (2 * n_kv)) == n_kv + row_head) & ((c // (2 * n_kv)) <= row_tok)
              & (_lane_pick(sel_ref[0], last * bps) > 0.5))
        _state_update(m_ref, l_ref, acc_ref, ..., jnp.where(ok, s, NEG), flat.astype(BF16))
        gates = jax.nn.sigmoid(gl_ref[0] + bg_ref[...])
        o_ref[0] = (gates[:, 0:1] * ocmp_ref[0] + gates[:, 1:2] * (acc_ref[...] / l_ref[...])
                    + gates[:, 2:3] * owin_ref[0])


def _nsa_sample_b(page_table, q_rows, sel, kvs_new, o_cmp, o_win, gl_rows, bg_rows, cache_tiles,
                  page, n_kv, gsz, ts, n_pages):
    bs, rows, _ = q_rows.shape
    nsp = sel.shape[2]
    nsteps = page_table.shape[1] // n_pages
    first = lambda b, p, pt: jnp.minimum(p, nsteps - 1) * n_pages
    row_spec = lambda n, width: pl.BlockSpec((1, n, width), lambda b, p, pt: (b, 0, 0))
    assert ts * 2 * n_kv <= LANES and n_pages * (page // SLC_BLOCK) <= LANES
    grid_spec = pltpu.PrefetchScalarGridSpec(
        num_scalar_prefetch=1,
        grid=(bs, nsteps + 1),
        in_specs=[
            row_spec(rows, HEAD_DIM), row_spec(rows, nsp), row_spec(ts * 2 * n_kv, HEAD_DIM),
            row_spec(rows, HEAD_DIM), row_spec(rows, HEAD_DIM), row_spec(rows, LANES),
            pl.BlockSpec((rows, LANES), lambda b, p, pt: (0, 0)),
        ] + _tile_specs(page, cache_tiles.shape[1], n_pages, first),
        out_specs=row_spec(rows, HEAD_DIM),
        scratch_shapes=[pltpu.VMEM((rows, 1), F32), pltpu.VMEM((rows, 1), F32), pltpu.VMEM((rows, HEAD_DIM), F32),
                        pltpu.VMEM((rows, n_pages * page * SUBLANES), F32), pltpu.VMEM((LANES, HEAD_DIM), F32)],
    )
    return pl.pallas_call(
        functools.partial(_nsa_sample_b_kernel, n_kv=n_kv, gsz=gsz, ts=ts, n_pages=n_pages),
        grid_spec=grid_spec,
        out_shape=jax.ShapeDtypeStruct((bs, rows, HEAD_DIM), F32),
        compiler_params=_params("parallel", "arbitrary"),
    )(page_table, q_rows, sel, kvs_new, o_cmp, o_win, gl_rows, bg_rows, *([cache_tiles] * n_pages))


def _rope_tables(pos, half):
    inv = ROPE_THETA ** (-jnp.arange(half, dtype=F32) / half)
    ang = pos.astype(F32)[:, None] * inv
    cos, sin = jnp.cos(ang), jnp.sin(ang)
    reps = LANES // (2 * half)
    return (jnp.tile(jnp.concatenate([cos, cos], axis=-1), (1, reps)),
            jnp.tile(jnp.concatenate([-sin, sin], axis=-1), (1, reps)))


def _pad_cols(w, n):
    return jnp.pad(w, ((0, 0), (0, n - w.shape[1])))


def _head_rows(x, n_kv, gsz):
    b, t, _ = x.shape
    return x.reshape(b, t, n_kv, gsz, -1).transpose(0, 2, 1, 3, 4).reshape(b, n_kv, t * gsz, -1)


def _token_cols(x, ts, gsz):
    b, n_kv, _, f = x.shape
    return x.reshape(b, n_kv, ts, gsz, f).transpose(0, 2, 1, 3, 4).reshape(b * ts, n_kv * gsz * f)


def kernel(x_prompt, x_sample, cache_a_kv, cache_b_kv, cache_b_logf, cache_cmp_kv, cache_slc_kv, state_win_kv, page_table, norm_mix_e, w_in_e, b_forget, lam_q1, lam_k1, lam_q2, lam_k2, subln_g, w_out_e, norm_mix_o, w_in_o, b_gate, cmp_pe_k, cmp_w1_k, cmp_w2_k, cmp_pe_v, cmp_w1_v, cmp_w2_v, w_out_o, norm_ffn, w_gate, w_up, w_down, norm_final):
    bp, t, d = x_prompt.shape
    bs, ts, _ = x_sample.shape
    n_even, n_odd = w_in_e.shape[0], w_in_o.shape[0]
    depth = n_even + n_odd
    page = cache_a_kv.shape[2]
    npg = page_table.shape[1]
    p0 = npg * page
    h_all = d // HEAD_DIM
    h_b = cache_b_logf.shape[-1]
    h_a = h_all - h_b
    kv_a, kv_b, kv_c = cache_a_kv.shape[4], cache_b_kv.shape[4], cache_cmp_kv.shape[4]
    g_a, g_b, g_c = h_a // kv_a, h_b // kv_b, h_all // kv_c
    mp, ms = bp * t, bs * ts
    hd = HEAD_DIM

    e_qa, e_ka = 0, h_a * hd
    e_va = e_ka + kv_a * hd
    e_qf = e_va + kv_a * hd
    e_kf = e_qf + h_b * hd
    e_vf = e_kf + kv_b * hd
    e_fg = e_vf + kv_b * hd
    o_kvc = h_all * hd
    o_kvs = o_kvc + 2 * kv_c * hd
    o_kvw = o_kvs + 2 * kv_c * hd
    o_gl = o_kvw + 2 * kv_c * hd
    assert 3 * h_all <= LANES and h_b <= LANES

    tn_e = _pick_tile(e_va, (512, 256, 128))
    tn_o = _pick_tile(kv_c * hd, (512, 256, 128))
    ne_pad = _round_up(e_fg + h_b, tn_e)
    no_pad = _round_up(o_gl + 3 * h_all, tn_o)
    tm_p = _pick_tile(mp, (1024, 512, 256, 128))
    tm_f = _pick_tile(mp, (1024, 512, 256, 128))
    tn_out = _pick_tile(d, (1024, 512, 256, 128))
    tf = _pick_tile(w_gate.shape[2], (256, 128))
    tq_e = _pick_tile(t, (512, 256, 128))
    tk_e = _pick_tile(t, (512, 256))
    tq_o = _pick_tile(t, (256, 128))
    tk_o = _pick_tile(t, (512, 256, 128))
    n_pages = _pick_tile(npg, (8, 4, 2))
    n_pages_c = _pick_tile(npg, (16, 8, 4, 2))

    xp = x_prompt.reshape(mp, d)
    xs = x_sample.reshape(ms, d)
    pos_p = jnp.tile(jnp.arange(t, dtype=jnp.int32), bp)
    pos_s = p0 + jnp.tile(jnp.arange(ts, dtype=jnp.int32), bs)
    tab_a_p, tab_a_s = _rope_tables(pos_p, DA // 2), _rope_tables(pos_s, DA // 2)
    tab_c_p, tab_c_s = _rope_tables(pos_p, hd // 2), _rope_tables(pos_s, hd // 2)
    tab_c_q = _rope_tables(jnp.arange(t, dtype=jnp.int32), hd // 2)
    tab_c_rows = _rope_tables(p0 + jnp.repeat(jnp.arange(ts, dtype=jnp.int32), g_c), hd // 2)

    outs = {name: ([], []) for name in ("a", "b", "f", "c", "s", "w")}

    for layer in range(depth):
        i = layer // 2
        if layer % 2 == 0:
            lam_init = 0.8 - 0.6 * float(np.exp(-0.3 * layer))
            w_in = _pad_cols(w_in_e[i].astype(BF16), ne_pad)
            w_out = w_out_e[i].astype(BF16)
            rope_e = ((0, e_va // tn_e),)
            lam_vec = jnp.stack([lam_q1[i], lam_k1[i], lam_q2[i], lam_k2[i]])
            bf_pad = jnp.pad(b_forget[i], (0, LANES - h_b)).reshape(1, LANES)

            z = _proj(xp, norm_mix_e[i], w_in, *tab_a_p, rope_e, DA // 2, tm_p, tn_e)
            z3 = z.reshape(bp, t, ne_pad)
            outs["a"][0].append(z3[:, :, e_ka:e_qf].reshape(bp, t, 2, kv_a, hd))
            outs["b"][0].append(z3[:, :, e_kf:e_fg].reshape(bp, t, 2, kv_b, hd))
            lf, dkc, dkr = _logf_prompt(z3, bf_pad, e_fg // LANES)
            outs["f"][0].append(lf[:, :, :h_b])
            o_a = _diff_prompt(z3, lam_vec, subln_g[i], kv_a, g_a, e_qa // (g_a * hd), e_ka // hd, e_va // hd,
                               lam_init, tq_e, tk_e)
            o_b = _fox_prompt(z3, dkr[:, :h_b].reshape(bp, h_b, 1, t), dkc, kv_b, g_b,
                              e_qf // (g_b * hd), e_kf // hd, e_vf // hd, tq_e, tk_e)
            xp = _mm_res([o_a.reshape(mp, h_a * hd), o_b.reshape(mp, h_b * hd)], w_out, xp, tm_p, tn_out)

            z = _proj(xs, norm_mix_e[i], w_in, *tab_a_s, rope_e, DA // 2, ms, tn_e)
            z3 = z.reshape(bs, ts, ne_pad)
            outs["a"][1].append(z3[:, :, e_ka:e_qf].reshape(bs, ts, 2, kv_a, hd))
            outs["b"][1].append(z3[:, :, e_kf:e_fg].reshape(bs, ts, 2, kv_b, hd))
            lf, dq, tot = _logf_sample(z[:, e_fg:e_fg + LANES], bf_pad, ts)
            outs["f"][1].append(lf[:, :h_b].reshape(bs, ts, h_b))
            o_a = _diff_sample(page_table, lam_vec, subln_g[i],
                               _head_rows(z3[:, :, e_qa:e_ka], kv_a, g_a).reshape(bs, -1, hd),
                               z3[:, :, e_ka:e_va].reshape(bs, ts * kv_a, hd),
                               z3[:, :, e_va:e_qf].reshape(bs, ts * kv_a, hd),
                               _tile_view(cache_a_kv[i]), page, kv_a, g_a, ts, lam_init, n_pages)
            rows_b = kv_b * ts * g_b
            dq3 = dq[:, :h_b].reshape(bs, ts, h_b)
            dq_b = jnp.broadcast_to(_head_rows(dq3, kv_b, g_b).reshape(bs, rows_b, 1), (bs, rows_b, LANES))
            dkn = jnp.swapaxes(dq3, 1, 2).reshape(bs, kv_b, 1, g_b, ts, 1)
            dkn = jnp.broadcast_to(dkn, (bs, kv_b, ts, g_b, ts, kv_b)).reshape(bs, rows_b, ts * kv_b)
            dkn_b = jnp.pad(dkn, ((0, 0), (0, 0), (0, LANES - ts * kv_b)))
            tot3 = tot[:, :h_b].reshape(bs, ts, h_b)[:, 0, :, None]
            o_b = _fox_sample(page_table, _head_rows(z3[:, :, e_qf:e_kf], kv_b, g_b).reshape(bs, -1, hd),
                              z3[:, :, e_kf:e_vf].reshape(bs, ts * kv_b, hd),
                              z3[:, :, e_vf:e_fg].reshape(bs, ts * kv_b, hd), dq_b, dkn_b, tot3,
                              _tile_view(cache_b_kv[i]), jnp.swapaxes(cache_b_logf[i], 1, 2),
                              page, kv_b, g_b, ts, n_pages)
            xs = _mm_res([_token_cols(o_a.reshape(bs, kv_a, ts * g_a, hd), ts, g_a),
                          _token_cols(o_b.reshape(bs, kv_b, ts * g_b, hd), ts, g_b)], w_out, xs, ms, tn_out)
        else:
            w_in = _pad_cols(w_in_o[i].astype(BF16), no_pad)
            w_out = w_out_o[i].astype(BF16)
            rope_o = ((o_kvs // tn_o, (o_kvs + kv_c * hd) // tn_o), (o_kvw // tn_o, (o_kvw + kv_c * hd) // tn_o))
            bg_pad = jnp.pad(b_gate[i], (0, LANES - 3 * h_all)).reshape(1, LANES)
            pe = jnp.stack([cmp_pe_k[i], cmp_pe_v[i]])
            w1 = jnp.stack([cmp_w1_k[i], cmp_w1_v[i]]).astype(BF16)
            w2 = jnp.stack([cmp_w2_k[i], cmp_w2_v[i]]).astype(BF16)

            z = _proj(xp, norm_mix_o[i], w_in, *tab_c_p, rope_o, hd // 2, tm_p, tn_o)
            z3 = z.reshape(bp, t, no_pad)
            outs["c"][0].append(z3[:, :, o_kvc:o_kvs].reshape(bp, t, 2, kv_c, hd))
            outs["s"][0].append(z3[:, :, o_kvs:o_kvw].reshape(bp, t, 2, kv_c, hd))
            keep = min(WINDOW, t)
            outs["w"][0].append(z3[:, t - keep:, o_kvw:o_gl].reshape(bp, keep, 2, kv_c, hd))
            kvc = _compress_prompt(z3, pe, w1, w2, kv_c, o_kvc // hd)
            o = _nsa_prompt(z3, *tab_c_q, kvc, bg_pad, kv_c, g_c, o_kvs // hd, o_kvs // hd + kv_c,
                            o_kvw // hd, o_kvw // hd + kv_c, o_gl // LANES, tq_o, tk_o)
            xp = _mm_res([o.reshape(mp, d)], w_out, xp, tm_p, tn_out)

            z = _proj(xs, norm_mix_o[i], w_in, *tab_c_s, rope_o, hd // 2, ms, tn_o)
            z3 = z.reshape(bs, ts, no_pad)
            outs["c"][1].append(z3[:, :, o_kvc:o_kvs].reshape(bs, ts, 2, kv_c, hd))
            outs["s"][1].append(z3[:, :, o_kvs:o_kvw].reshape(bs, ts, 2, kv_c, hd))
            win = state_win_kv[i]
            kvw_new = z3[:, :, o_kvw:o_gl]
            win_all = jnp.concatenate([win, kvw_new.reshape(bs, ts, 2, kv_c, hd)], axis=1)
            outs["w"][1].append(win_all[:, -min(WINDOW, win.shape[1] + ts):])
            kvc = _compress_sample(page_table, cache_cmp_kv[i].reshape(-1, hd), pe, w1, w2, page, kv_c, min(16, npg))
            ns = -(-(p0 + ts) // SLC_BLOCK)
            o_cmp, o_win, sel, qr = _nsa_sample_a(_head_rows(z3[:, :, :o_kvc], kv_c, g_c), *tab_c_rows, kvc,
                                                  win.reshape(bs, -1, hd), kvw_new, kv_c, g_c, ts, p0, ns)
            rows = kv_c * ts * g_c
            gl = z3[:, :, o_gl:o_gl + 3 * h_all].reshape(bs, ts, 3, kv_c, g_c).transpose(0, 3, 1, 4, 2)
            gl_rows = jnp.pad(gl.reshape(bs, rows, 3), ((0, 0), (0, 0), (0, LANES - 3)))
            bg = jnp.broadcast_to(b_gate[i].reshape(1, 3, kv_c, g_c), (ts, 3, kv_c, g_c)).transpose(2, 0, 3, 1)
            bg_rows = jnp.pad(bg.reshape(rows, 3), ((0, 0), (0, LANES - 3)))
            o = _nsa_sample_b(page_table, qr.reshape(bs, rows, hd), sel.reshape(bs, rows, -1),
                              z3[:, :, o_kvs:o_kvw].reshape(bs, ts * 2 * kv_c, hd),
                              o_cmp.reshape(bs, rows, hd), o_win.reshape(bs, rows, hd), gl_rows, bg_rows,
                              _tile_view(cache_slc_kv[i]), page, kv_c, g_c, ts, n_pages_c)
            xs = _mm_res([_token_cols(o.reshape(bs, kv_c, ts * g_c, hd), ts, g_c).astype(BF16)], w_out, xs, ms, tn_out)

        g_final = norm_final if layer == depth - 1 else None
        wg, wu, wd = w_gate[layer].astype(BF16), w_up[layer].astype(BF16), w_down[layer].astype(BF16)
        xp = _ffn(xp, norm_ffn[layer], wg, wu, wd, g_final, tm_f, tf)
        xs = _ffn(xs, norm_ffn[layer], wg, wu, wd, g_final, ms, tf)

    def stack(name, which):
        return jnp.stack(outs[name][which])

    return (xp.reshape(bp, t, d), xs.reshape(bs, ts, d),
            stack("a", 0), stack("a", 1), stack("b", 0), stack("b", 1), stack("f", 0), stack("f", 1),
            stack("c", 0), stack("c", 1), stack("s", 0), stack("s", 1), stack("w", 0), stack("w", 1))
```

```python
import functools

import numpy as np
import jax
import jax.numpy as jnp
from jax import lax
from jax.experimental import pallas as pl
from jax.experimental.pallas import tpu as pltpu

F32 = jnp.float32
BF16 = jnp.bfloat16
HIGHEST = lax.Precision.HIGHEST

LANES = 128
SUBLANES = 8
VMEM_LIMIT = 56 * 1024 * 1024
VMEM_LIMIT_BIG = 63 * 1024 * 1024

HEAD_DIM = 128
DA = HEAD_DIM // 2
CMP_LEN = 32
CMP_STRIDE = 16
CMP_HID = 2 * HEAD_DIM
SLC_BLOCK = 64
N_SELECT = 16
WINDOW = 512
ROPE_THETA = 10000.0
NORM_EPS = 1e-6
LOG2E = 1.4426950408889634
NEG = -1e30
BIG = 1e30
LOWEST = -3e38


def _params(*sem):
    return pltpu.CompilerParams(dimension_semantics=sem, vmem_limit_bytes=VMEM_LIMIT)


def _dot(a, b):
    return jnp.dot(a, b, preferred_element_type=F32)


def _dot_nt(a, b):
    return lax.dot_general(a, b, (((1,), (1,)), ((), ())), preferred_element_type=F32)


def _dot_exact(a, b):
    return jnp.dot(a, b, precision=HIGHEST, preferred_element_type=F32)


def _dot_nt_exact(a, b):
    return lax.dot_general(a, b, (((1,), (1,)), ((), ())), precision=HIGHEST, preferred_element_type=F32)


def _iota(shape, axis):
    return lax.broadcasted_iota(jnp.int32, shape, axis)


def _round_up(n, m):
    return -(-n // m) * m


def _pick_tile(n, prefs):
    for p in prefs:
        if n % p == 0:
            return p
    return n


def _rope_lanes(a, cos, sin, half):
    if 2 * half == LANES:
        partner = pltpu.roll(a, half, axis=1)
    else:
        first = (_iota(a.shape, 1) & (2 * half - 1)) < half
        partner = jnp.where(first, pltpu.roll(a, LANES - half, axis=1), pltpu.roll(a, half, axis=1))
    return a * cos + partner * sin


def _rmsnorm_rows(x, g):
    ms = jnp.mean(x * x, axis=-1, keepdims=True)
    return x * lax.rsqrt(ms + NORM_EPS) * g


def _proj_kernel(x_ref, g_ref, w_ref, cos_ref, sin_ref, o_ref, xn_ref, *, rope_ranges, half):
    j = pl.program_id(1)

    @pl.when(j == 0)
    def _():
        rc = min(256, x_ref.shape[0])
        for r in range(x_ref.shape[0] // rc):
            rows = slice(r * rc, (r + 1) * rc)
            xn_ref[rows, :] = _rmsnorm_rows(x_ref[rows, :], g_ref[...]).astype(BF16)

    o_ref[...] = _dot(xn_ref[...], w_ref[...])
    is_rope = functools.reduce(jnp.logical_or, [(j >= a) & (j < b) for a, b in rope_ranges])

    @pl.when(is_rope)
    def _():
        cos, sin = cos_ref[...], sin_ref[...]
        for c in range(o_ref.shape[1] // LANES):
            sl = slice(c * LANES, (c + 1) * LANES)
            o_ref[:, sl] = _rope_lanes(o_ref[:, sl], cos, sin, half)


def _proj(x, g, w, cos, sin, rope_ranges, half, tm, tn):
    m, d = x.shape
    n = w.shape[1]
    return pl.pallas_call(
        functools.partial(_proj_kernel, rope_ranges=rope_ranges, half=half),
        grid=(m // tm, n // tn),
        in_specs=[
            pl.BlockSpec((tm, d), lambda i, j: (i, 0)),
            pl.BlockSpec((1, d), lambda i, j: (0, 0)),
            pl.BlockSpec((d, tn), lambda i, j: (0, j)),
            pl.BlockSpec((tm, LANES), lambda i, j: (i, 0)),
            pl.BlockSpec((tm, LANES), lambda i, j: (i, 0)),
        ],
        out_specs=pl.BlockSpec((tm, tn), lambda i, j: (i, j)),
        out_shape=jax.ShapeDtypeStruct((m, n), F32),
        scratch_shapes=[pltpu.VMEM((tm, d), BF16)],
        compiler_params=pltpu.CompilerParams(dimension_semantics=("parallel", "arbitrary"),
                                             vmem_limit_bytes=VMEM_LIMIT_BIG),
    )(x, g.reshape(1, d), w, cos, sin)


def _mm_res_kernel(*refs, n_lhs):
    a_refs, (w_ref, r_ref, o_ref) = refs[:n_lhs], refs[n_lhs:]
    acc = r_ref[...]
    off = 0
    for a_ref in a_refs:
        kk = a_ref.shape[1]
        acc = acc + _dot(a_ref[...], w_ref[off:off + kk, :])
        off += kk
    o_ref[...] = acc


def _mm_res(lhs, w, res, tm, tn):
    m, n = res.shape
    k = w.shape[0]
    return pl.pallas_call(
        functools.partial(_mm_res_kernel, n_lhs=len(lhs)),
        grid=(m // tm, n // tn),
        in_specs=[pl.BlockSpec((tm, a.shape[1]), lambda i, j: (i, 0)) for a in lhs] + [
            pl.BlockSpec((k, tn), lambda i, j: (0, j)),
            pl.BlockSpec((tm, tn), lambda i, j: (i, j)),
        ],
        out_specs=pl.BlockSpec((tm, tn), lambda i, j: (i, j)),
        out_shape=jax.ShapeDtypeStruct((m, n), F32),
        compiler_params=_params("parallel", "arbitrary"),
    )(*lhs, w, res)


def _ffn_kernel(x_ref, g_ref, wg_ref, wu_ref, wd_ref, gf_ref, o_ref, xn_ref, h_ref, *, final_norm):
    f = pl.program_id(1)
    n_tiles = pl.num_programs(1) - 1

    def hidden():
        xn = xn_ref[...]
        gate = _dot(xn, wg_ref[...])
        up = _dot(xn, wu_ref[...])
        return (gate * jax.nn.sigmoid(gate) * up).astype(BF16)

    rc = min(256, x_ref.shape[0])
    chunks = [slice(r * rc, (r + 1) * rc) for r in range(x_ref.shape[0] // rc)]

    @pl.when(f == 0)
    def _():
        for rows in chunks:
            x = x_ref[rows, :]
            xn_ref[rows, :] = _rmsnorm_rows(x, g_ref[...]).astype(BF16)
            o_ref[rows, :] = x
        h_ref[0] = hidden()

    @pl.when((f > 0) & (f < n_tiles))
    def _():
        h_ref[f % 2] = hidden()
        o_ref[...] += _dot(h_ref[(f - 1) % 2], wd_ref[...])

    @pl.when(f == n_tiles)
    def _():
        o_ref[...] += _dot(h_ref[(f - 1) % 2], wd_ref[...])
        if final_norm:
            for rows in chunks:
                o_ref[rows, :] = _rmsnorm_rows(o_ref[rows, :], gf_ref[...])


def _ffn(x, g, wg, wu, wd, g_final, tm, tf):
    m, d = x.shape
    n_tiles = wg.shape[1] // tf
    final_norm = g_final is not None
    gf = (g_final if final_norm else g).reshape(1, d)
    return pl.pallas_call(
        functools.partial(_ffn_kernel, final_norm=final_norm),
        grid=(m // tm, n_tiles + 1),
        in_specs=[
            pl.BlockSpec((tm, d), lambda i, f: (i, 0), pipeline_mode=pl.Buffered(1)),
            pl.BlockSpec((1, d), lambda i, f: (0, 0)),
            pl.BlockSpec((d, tf), lambda i, f: (0, jnp.minimum(f, n_tiles - 1))),
            pl.BlockSpec((d, tf), lambda i, f: (0, jnp.minimum(f, n_tiles - 1))),
            pl.BlockSpec((tf, d), lambda i, f: (jnp.maximum(f - 1, 0), 0)),
            pl.BlockSpec((1, d), lambda i, f: (0, 0)),
        ],
        out_specs=pl.BlockSpec((tm, d), lambda i, f: (i, 0), pipeline_mode=pl.Buffered(1)),
        out_shape=jax.ShapeDtypeStruct((m, d), F32),
        scratch_shapes=[pltpu.VMEM((tm, d), BF16), pltpu.VMEM((2, tm, tf), BF16)],
        compiler_params=pltpu.CompilerParams(dimension_semantics=("parallel", "arbitrary"),
                                             vmem_limit_bytes=VMEM_LIMIT_BIG),
    )(x, g.reshape(1, d), wg, wu, wd, gf)


def _online_update(carry, s, v, shift=None):
    m, l, acc = carry
    smax = jnp.max(s, axis=-1, keepdims=True)
    if shift is not None:
        smax = smax - shift
    m_new = jnp.maximum(m, smax)
    alpha = jnp.exp2(m - m_new)
    p = jnp.exp2(s - (m_new if shift is None else m_new + shift))
    l = alpha * l + jnp.sum(p, axis=-1, keepdims=True)
    pb = p.astype(BF16)
    if isinstance(v, (list, tuple)):
        n = v[0].shape[0]
        pv = sum(_dot(pb[:, r * n:(r + 1) * n], x) for r, x in enumerate(v))
    else:
        pv = _dot(pb, v)
    return m_new, l, alpha * acc + pv


def _state_init(m_ref, l_ref, acc_ref):
    m_ref[...] = jnp.full(m_ref.shape, NEG, F32)
    l_ref[...] = jnp.zeros(l_ref.shape, F32)
    acc_ref[...] = jnp.zeros(acc_ref.shape, F32)


def _state_update(m_ref, l_ref, acc_ref, k, s, v, shift=None):
    m, l, acc = _online_update((m_ref[k], l_ref[k], acc_ref[k]), s, v, shift)
    m_ref[k], l_ref[k], acc_ref[k] = m, l, acc


def _fold_lanes(x, op):
    out = x[:, :LANES]
    for i in range(1, x.shape[1] // LANES):
        out = op(out, x[:, i * LANES:(i + 1) * LANES])
    return out


def _two_pass_attention(q, kb_ref, vb_ref, n_blocks, tk, s_ref, add_fn=None, tail_fn=None, shift=None):
    rows = q.shape[0]

    def scores(kb, masked):
        k0 = pl.multiple_of(kb * tk, tk)
        s = _dot_nt(q, kb_ref[pl.ds(k0, tk), :])
        if add_fn is not None:
            s = s + add_fn(k0)
        if masked and tail_fn is not None:
            s = s + tail_fn(k0)
        s_ref[kb] = s
        return _fold_lanes(s, jnp.maximum)

    mx = lax.fori_loop(0, n_blocks - 1, lambda kb, mx: jnp.maximum(mx, scores(kb, False)),
                       jnp.full((rows, LANES), NEG, F32))
    mx = jnp.maximum(mx, scores(n_blocks - 1, True))
    m = jnp.max(mx, axis=-1, keepdims=True)
    if shift is not None:
        m = (m - shift) + shift

    def accumulate(kb, carry):
        ls, acc = carry
        k0 = pl.multiple_of(kb * tk, tk)
        p = jnp.exp2(s_ref[kb] - m)
        return ls + _fold_lanes(p, jnp.add), acc + _dot(p.astype(BF16), vb_ref[pl.ds(k0, tk), :])

    ls, acc = lax.fori_loop(0, n_blocks, accumulate,
                            (jnp.zeros((rows, LANES), F32), jnp.zeros((rows, HEAD_DIM), F32)))
    return acc, jnp.sum(ls, axis=-1, keepdims=True)


def _masked_softmax(s, okf):
    ok = okf > 0.5
    sm = jnp.where(ok, s, NEG)
    m = jnp.max(sm, axis=-1, keepdims=True)
    e = jnp.where(ok, jnp.exp2(sm - m), 0.0)
    return e / jnp.maximum(jnp.sum(e, axis=-1, keepdims=True), 1e-30)


def _lane_pick(x, lane_idx):
    return jnp.sum(jnp.where(_iota(x.shape, 1) == lane_idx, x, 0.0), axis=-1, keepdims=True)


def _topk_mask(imp, n_sel):
    lane = _iota(imp.shape, 1).astype(F32)
    sel = jnp.zeros(imp.shape, F32)
    x = imp
    for _ in range(n_sel):
        m = jnp.max(x, axis=-1, keepdims=True)
        idx = jnp.min(jnp.where(x == m, lane, 1e9), axis=-1, keepdims=True)
        hit = lane == idx
        sel = jnp.where(hit, 1.0, sel)
        x = jnp.where(hit, LOWEST, x)
    return sel


def _overlap(ci, sj):
    ov = jnp.minimum(ci + CMP_LEN, sj + SLC_BLOCK) - jnp.maximum(ci, sj)
    return jnp.maximum(ov, 0).astype(F32) * (1.0 / CMP_LEN)


def _topk_mask_t(imp_t, n_sel, ns):
    rows = _round_up(ns, SUBLANES)
    x = imp_t[0:rows]
    j = _iota(x.shape, 0)
    rank = jnp.zeros(x.shape, F32)
    for jp in range(ns):
        a = x[jp:jp + 1, :]
        rank = rank + jnp.where((a > x) | ((a == x) & (jp < j)), 1.0, 0.0)
    sel = jnp.where((rank < n_sel) & (j < ns), 1.0, 0.0)
    return jnp.concatenate([sel, jnp.zeros((imp_t.shape[0] - rows, imp_t.shape[1]), F32)], axis=0)


def _block_importance(imp, qpos, ns, axis=1):
    j = _iota(imp.shape, axis)
    cur = qpos >> 6
    forced = (j == 0) | (j == cur) | (j == cur - 1)
    valid = (j << 6) <= qpos
    out = jnp.where(forced, BIG, jnp.where(valid, imp, NEG))
    return jnp.where(j < ns, out, LOWEST)


def _lam_value(lam_ref, lam_init):
    v = lam_ref[...]
    a = jnp.exp(jnp.sum(v[0:1] * v[1:2], axis=-1, keepdims=True))
    b = jnp.exp(jnp.sum(v[2:3] * v[3:4], axis=-1, keepdims=True))
    return a - b + lam_init


def _gelu_tanh(x):
    return 0.5 * x * (1.0 + jnp.tanh(0.7978845608028654 * (x + 0.044715 * (x * x * x))))


def _log_sigmoid(z):
    return jnp.minimum(z, 0.0) - jnp.log1p(jnp.exp(-jnp.abs(z)))


def _cast_kv_once(step, pairs):
    @pl.when(step == 0)
    def _():
        for src, dst in pairs:
            dst[...] = src[0].astype(BF16)


def _logf_prompt_kernel(z_ref, b_ref, lf_ref, dkc_ref, dkr_ref):
    t = z_ref.shape[1]
    lf = _log_sigmoid(z_ref[0] + b_ref[...])
    lf_ref[0] = lf
    upper = (_iota((LANES, LANES), 1) > _iota((LANES, LANES), 0)).astype(F32)
    carry = jnp.zeros((1, LANES), F32)
    for blk in reversed(range(t // LANES)):
        sl = slice(blk * LANES, (blk + 1) * LANES)
        x = lf[sl]
        d = _dot_exact(upper, x) + carry
        dkc_ref[0, sl, :] = d
        dkr_ref[0, :, sl] = d.T
        carry = carry + jnp.sum(x, axis=0, keepdims=True)


def _logf_prompt(z3, b_pad, col_blk):
    b, t, _ = z3.shape
    return pl.pallas_call(
        _logf_prompt_kernel,
        grid=(b,),
        in_specs=[pl.BlockSpec((1, t, LANES), lambda i: (i, 0, col_blk)),
                  pl.BlockSpec((1, LANES), lambda i: (0, 0))],
        out_specs=[pl.BlockSpec((1, t, LANES), lambda i: (i, 0, 0)),
                   pl.BlockSpec((1, t, LANES), lambda i: (i, 0, 0)),
                   pl.BlockSpec((1, LANES, t), lambda i: (i, 0, 0))],
        out_shape=[jax.ShapeDtypeStruct((b, t, LANES), F32),
                   jax.ShapeDtypeStruct((b, t, LANES), F32),
                   jax.ShapeDtypeStruct((b, LANES, t), F32)],
        compiler_params=_params("parallel"),
    )(z3, b_pad)


def _causal_tail(qpos, tk):
    return lambda k0: jnp.where(k0 + _iota((qpos.shape[0], tk), 1) <= qpos, 0.0, NEG)


def _diff_prompt_kernel(lam_ref, g_ref, q_ref, k_ref, v_ref, o_ref, kb_ref, vb_ref, s_ref, *, tq, gsz, lam_init, tk):
    qi = pl.program_id(2)
    _cast_kv_once(qi, ((k_ref, kb_ref), (v_ref, vb_ref)))
    lam = _lam_value(lam_ref, lam_init)
    lane = _iota((tq, HEAD_DIM), 1)
    qscale = DA ** -0.5 * LOG2E
    qs = [q_ref[0, :, g * HEAD_DIM:(g + 1) * HEAD_DIM] * qscale for g in range(gsz)]
    q = jnp.concatenate([jnp.where(lane < DA, x, 0.0) for x in qs] + [jnp.where(lane >= DA, x, 0.0) for x in qs],
                        axis=0).astype(BF16)
    qpos = qi * tq + (_iota((2 * gsz * tq, 1), 0) & (tq - 1))
    n_blocks = (qi * tq) // tk + 1
    acc, l = _two_pass_attention(q, kb_ref, vb_ref, n_blocks, tk, s_ref, tail_fn=_causal_tail(qpos, tk))
    o = acc / l
    half = gsz * tq
    o = o[:half] - lam * o[half:]
    o = (_rmsnorm_rows(o, g_ref[...]) * (1.0 - lam_init)).astype(BF16)
    for g in range(gsz):
        o_ref[0, :, g * HEAD_DIM:(g + 1) * HEAD_DIM] = o[g * tq:(g + 1) * tq]


def _diff_prompt(z3, lam_vec, subln_g, n_kv, gsz, q_blk0, k_blk0, v_blk0, lam_init, tq, tk):
    b, t, _ = z3.shape
    qw = gsz * HEAD_DIM
    assert tk % tq == 0 and tq & (tq - 1) == 0
    return pl.pallas_call(
        functools.partial(_diff_prompt_kernel, tq=tq, gsz=gsz, lam_init=lam_init, tk=tk),
        grid=(b, n_kv, t // tq),
        in_specs=[
            pl.BlockSpec((4, DA), lambda i, k, q: (0, 0)),
            pl.BlockSpec((1, HEAD_DIM), lambda i, k, q: (0, 0)),
            pl.BlockSpec((1, tq, qw), lambda i, k, q: (i, q, q_blk0 + k)),
            pl.BlockSpec((1, t, HEAD_DIM), lambda i, k, q: (i, 0, k_blk0 + k)),
            pl.BlockSpec((1, t, HEAD_DIM), lambda i, k, q: (i, 0, v_blk0 + k)),
        ],
        out_specs=pl.BlockSpec((1, tq, qw), lambda i, k, q: (i, q, k)),
        out_shape=jax.ShapeDtypeStruct((b, t, n_kv * qw), BF16),
        scratch_shapes=[pltpu.VMEM((t, HEAD_DIM), BF16), pltpu.VMEM((t, HEAD_DIM), BF16),
                        pltpu.VMEM((t // tk, 2 * gsz * tq, tk), F32)],
        compiler_params=_params("parallel", "parallel", "arbitrary"),
    )(lam_vec, subln_g.reshape(1, HEAD_DIM), z3, z3, z3)


def _fox_prompt_kernel(q_ref, k_ref, v_ref, dkr_ref, dkc_ref, o_ref, kb_ref, vb_ref, s_ref, *, tq, gsz, tk):
    kh = pl.program_id(1)
    qi = pl.program_id(2)
    _cast_kv_once(qi, ((k_ref, kb_ref), (v_ref, vb_ref)))
    qscale = HEAD_DIM ** -0.5 * LOG2E
    qpos = qi * tq + (_iota((gsz * tq, 1), 0) & (tq - 1))
    n_blocks = (qi * tq) // tk + 1
    q = jnp.concatenate([q_ref[0, :, g * HEAD_DIM:(g + 1) * HEAD_DIM] * qscale for g in range(gsz)],
                        axis=0).astype(BF16)
    dq = jnp.concatenate([_lane_pick(dkc_ref[0], kh * gsz + g) for g in range(gsz)], axis=0) * LOG2E

    def dk_fn(k0):
        return jnp.concatenate([jnp.broadcast_to(dkr_ref[0, g, :, pl.ds(k0, tk)] * LOG2E, (tq, tk))
                                for g in range(gsz)], axis=0)

    acc, l = _two_pass_attention(q, kb_ref, vb_ref, n_blocks, tk, s_ref, add_fn=dk_fn,
                                 tail_fn=_causal_tail(qpos, tk), shift=dq)
    o = (acc / l).astype(BF16)
    for g in range(gsz):
        o_ref[0, :, g * HEAD_DIM:(g + 1) * HEAD_DIM] = o[g * tq:(g + 1) * tq]


def _fox_prompt(z3, dkr4, dkc, n_kv, gsz, q_blk0, k_blk0, v_blk0, tq, tk):
    b, t, _ = z3.shape
    qw = gsz * HEAD_DIM
    assert tk % tq == 0
    return pl.pallas_call(
        functools.partial(_fox_prompt_kernel, tq=tq, gsz=gsz, tk=tk),
        grid=(b, n_kv, t // tq),
        in_specs=[
            pl.BlockSpec((1, tq, qw), lambda i, k, q: (i, q, q_blk0 + k)),
            pl.BlockSpec((1, t, HEAD_DIM), lambda i, k, q: (i, 0, k_blk0 + k)),
            pl.BlockSpec((1, t, HEAD_DIM), lambda i, k, q: (i, 0, v_blk0 + k)),
            pl.BlockSpec((1, gsz, 1, t), lambda i, k, q: (i, k, 0, 0)),
            pl.BlockSpec((1, tq, LANES), lambda i, k, q: (i, q, 0)),
        ],
        out_specs=pl.BlockSpec((1, tq, qw), lambda i, k, q: (i, q, k)),
        out_shape=jax.ShapeDtypeStruct((b, t, n_kv * qw), BF16),
        scratch_shapes=[pltpu.VMEM((t, HEAD_DIM), BF16), pltpu.VMEM((t, HEAD_DIM), BF16),
                        pltpu.VMEM((t // tk, gsz * tq, tk), F32)],
        compiler_params=_params("parallel", "parallel", "arbitrary"),
    )(z3, z3, z3, dkr4, dkc)


def _logf_sample_kernel(z_ref, b_ref, lf_ref, dq_ref, tot_ref, *, ts):
    m = z_ref.shape[0]
    lf = _log_sigmoid(z_ref[...] + b_ref[...])
    lf_ref[...] = lf
    i, j = _iota((m, m), 0), _iota((m, m), 1)
    same = (i // ts) == (j // ts)
    dq_ref[...] = _dot_exact((same & (j > i)).astype(F32), lf)
    tot_ref[...] = _dot_exact(same.astype(F32), lf)


def _logf_sample(z_fg, b_pad, ts):
    m = z_fg.shape[0]
    sds = jax.ShapeDtypeStruct((m, LANES), F32)
    return pl.pallas_call(
        functools.partial(_logf_sample_kernel, ts=ts),
        out_shape=[sds, sds, sds],
    )(z_fg, b_pad)


def _pad_rows_scratch(ref, x):
    ref[...] = jnp.zeros(ref.shape, ref.dtype)
    ref[0:x.shape[0], :] = x


def _page_specs(page, n_kv, n_pages, first_page):
    rows = page * 2 * n_kv
    return [pl.BlockSpec((rows, HEAD_DIM), lambda b, p, pt, r=r: (pt[b, first_page(b, p, pt) + r], 0))
            for r in range(n_pages)]


def _tile_view(cache):
    n_kv = cache.shape[3]
    assert 2 * n_kv in (SUBLANES, 2 * SUBLANES)
    return cache.reshape(-1, 2 * n_kv // SUBLANES, SUBLANES, HEAD_DIM)


def _tile_specs(page, tiles, n_pages, first_page):
    return [pl.BlockSpec((page, tiles, SUBLANES, HEAD_DIM),
                         lambda b, p, pt, r=r: (pt[b, first_page(b, p, pt) + r], 0, 0, 0))
            for r in range(n_pages)]


def _flat_kv(pg_refs):
    n = pg_refs[0].shape[0] * SUBLANES
    if pg_refs[0].shape[1] == 2:
        ks = [pg[:, 0].reshape(n, HEAD_DIM) for pg in pg_refs]
        vs = [pg[:, 1].reshape(n, HEAD_DIM) for pg in pg_refs]
        off = 0
    else:
        vs = [pg[:, 0].reshape(n, HEAD_DIM) for pg in pg_refs]
        ks = [pltpu.roll(v, SUBLANES // 2, axis=0) for v in vs]
        off = SUBLANES // 2
    return [k.astype(BF16) for k in ks], [v.astype(BF16) for v in vs], off


def _flat_scores(q, ks):
    return jnp.concatenate([_dot_nt(q, k) for k in ks], axis=1)


def _own_head_bias(row_head, n_cols):
    return jnp.where((_iota((row_head.shape[0], n_cols), 1) & (SUBLANES - 1)) == row_head, 0.0, NEG)


def _diff_sample_kernel(pt_ref, lam_ref, g_ref, q_ref, kn_ref, vn_ref, *refs, n_kv, gsz, ts, lam_init, n_pages):
    pg_refs = refs[:n_pages]
    o_ref, m_ref, l_ref, acc_ref, bias_ref, kpad_ref, vpad_ref = refs[n_pages:]
    p = pl.program_id(1)
    last = pl.num_programs(1) - 1
    r1 = n_kv * ts * gsz
    q = q_ref[0] * (DA ** -0.5 * LOG2E)
    lane = _iota((r1, HEAD_DIM), 1)
    qb = jnp.concatenate([jnp.where(lane < DA, q, 0.0), jnp.where(lane >= DA, q, 0.0)], axis=0).astype(BF16)
    rr = _iota((2 * r1, 1), 0) % r1
    row_head = rr // (ts * gsz)
    row_tok = (rr % (ts * gsz)) // gsz

    @pl.when(p == 0)
    def _():
        _state_init(m_ref, l_ref, acc_ref)

    @pl.when(p < last)
    def _():
        ks, vs, off = _flat_kv(pg_refs)

        @pl.when(p == 0)
        def _():
            bias_ref[...] = _own_head_bias(row_head + off, bias_ref.shape[1])

        _state_update(m_ref, l_ref, acc_ref, ..., _flat_scores(qb, ks) + bias_ref[...], vs)

    @pl.when(p == last)
    def _():
        _pad_rows_scratch(kpad_ref, kn_ref[0])
        _pad_rows_scratch(vpad_ref, vn_ref[0])
        s = _dot_nt(qb, kpad_ref[...].astype(BF16))
        c = _iota(s.shape, 1)
        ok = ((c % n_kv) == row_head) & ((c // n_kv) <= row_tok)
        _state_update(m_ref, l_ref, acc_ref, ..., jnp.where(ok, s, NEG), vpad_ref[...].astype(BF16))
        o = acc_ref[...] / l_ref[...]
        o = o[:r1] - _lam_value(lam_ref, lam_init) * o[r1:]
        o_ref[0] = (_rmsnorm_rows(o, g_ref[...]) * (1.0 - lam_init)).astype(BF16)


def _diff_sample(page_table, lam_vec, subln_g, q_rows, k_new, v_new, cache_tiles, page, n_kv, gsz, ts, lam_init, n_pages):
    bs, r1, _ = q_rows.shape
    nsteps = page_table.shape[1] // n_pages
    first = lambda b, p, pt: jnp.minimum(p, nsteps - 1) * n_pages
    row_spec = lambda n: pl.BlockSpec((1, n, HEAD_DIM), lambda b, p, pt: (b, 0, 0))
    assert ts * n_kv <= LANES
    grid_spec = pltpu.PrefetchScalarGridSpec(
        num_scalar_prefetch=1,
        grid=(bs, nsteps + 1),
        in_specs=[
            pl.BlockSpec((4, DA), lambda b, p, pt: (0, 0)),
            pl.BlockSpec((1, HEAD_DIM), lambda b, p, pt: (0, 0)),
            row_spec(r1), row_spec(ts * n_kv), row_spec(ts * n_kv),
        ] + _tile_specs(page, cache_tiles.shape[1], n_pages, first),
        out_specs=row_spec(r1),
        scratch_shapes=[pltpu.VMEM((2 * r1, 1), F32), pltpu.VMEM((2 * r1, 1), F32), pltpu.VMEM((2 * r1, HEAD_DIM), F32),
                        pltpu.VMEM((2 * r1, n_pages * page * SUBLANES), F32),
                        pltpu.VMEM((LANES, HEAD_DIM), F32), pltpu.VMEM((LANES, HEAD_DIM), F32)],
    )
    return pl.pallas_call(
        functools.partial(_diff_sample_kernel, n_kv=n_kv, gsz=gsz, ts=ts, lam_init=lam_init, n_pages=n_pages),
        grid_spec=grid_spec,
        out_shape=jax.ShapeDtypeStruct((bs, r1, HEAD_DIM), BF16),
        compiler_params=_params("parallel", "arbitrary"),
    )(page_table, lam_vec, subln_g.reshape(1, HEAD_DIM), q_rows, k_new, v_new, *([cache_tiles] * n_pages))


def _fox_sample_kernel(pt_ref, q_ref, kn_ref, vn_ref, dq_ref, dkn_ref, tot_ref, *refs, n_kv, gsz, ts, n_pages):
    pg_refs, lf_refs = refs[:n_pages], refs[n_pages:2 * n_pages]
    o_ref, m_ref, l_ref, acc_ref, car_ref, bias_ref, kpad_ref, vpad_ref = refs[2 * n_pages:]
    p = pl.program_id(1)
    page = lf_refs[0].shape[2]
    rows = n_kv * ts * gsz
    n_h = n_kv * gsz
    qb = (q_ref[0] * (HEAD_DIM ** -0.5 * LOG2E)).astype(BF16)
    rr = _iota((rows, 1), 0)
    row_head = rr // (ts * gsz)
    row_tok = (rr % (ts * gsz)) // gsz
    shift = dq_ref[0][:, 0:1] * LOG2E

    @pl.when(p == 0)
    def _():
        _state_init(m_ref, l_ref, acc_ref)
        car_ref[...] = tot_ref[0]
        _pad_rows_scratch(kpad_ref, kn_ref[0])
        _pad_rows_scratch(vpad_ref, vn_ref[0])
        s = _dot_nt(qb, kpad_ref[...].astype(BF16)) + dkn_ref[0] * LOG2E
        c = _iota(s.shape, 1)
        ok = ((c % n_kv) == row_head) & ((c // n_kv) <= row_tok)
        _state_update(m_ref, l_ref, acc_ref, ..., jnp.where(ok, s, NEG), vpad_ref[...].astype(BF16), shift=shift)

    @pl.when(p > 0)
    def _():
        ks, vs, off = _flat_kv(pg_refs)

        @pl.when(p == 1)
        def _():
            bias_ref[...] = _own_head_bias(row_head + off, bias_ref.shape[1])

        lower = (_iota((page, page), 0) > _iota((page, page), 1)).astype(F32)
        dks = [None] * n_pages
        for r in reversed(range(n_pages)):
            lft = lf_refs[r][0]
            dks[r] = _dot_exact(lft, lower) + car_ref[...]
            car_ref[...] = car_ref[...] + jnp.sum(lft, axis=1, keepdims=True)
        dk = jnp.concatenate(dks, axis=1)
        hi = dk.astype(BF16)
        rest = dk - hi.astype(F32)
        mid = rest.astype(BF16)
        terms = (hi, mid, (rest - mid.astype(F32)).astype(BF16))
        hh = _iota((rows, n_h), 0)
        pick = ((hh // (ts * gsz)) * gsz + hh % gsz == _iota((rows, n_h), 1)).astype(BF16)
        spread = ((_iota((page, page * SUBLANES), 1) >> 3) == _iota((page, page * SUBLANES), 0)).astype(BF16)
        on_rows = [_dot(pick, x).astype(BF16) for x in terms]
        pieces = jnp.concatenate([x[:, r * page:(r + 1) * page] for r in range(n_pages) for x in on_rows], axis=0)
        flat = _dot(pieces, spread)
        nt = len(terms)
        dkf = jnp.concatenate(
            [sum(flat[(r * nt + q) * rows:(r * nt + q + 1) * rows] for q in range(nt)) for r in range(n_pages)],
            axis=1) * LOG2E
        _state_update(m_ref, l_ref, acc_ref, ..., _flat_scores(qb, ks) + dkf + bias_ref[...], vs, shift=shift)

    @pl.when(p == pl.num_programs(1) - 1)
    def _():
        o_ref[0] = (acc_ref[...] / l_ref[...]).astype(BF16)


def _fox_sample(page_table, q_rows, k_new, v_new, dq_b, dkn_b, tot, cache_tiles, logf_t, page, n_kv, gsz, ts, n_pages):
    bs, rows, _ = q_rows.shape
    n_h = n_kv * gsz
    nsteps = page_table.shape[1] // n_pages
    first = lambda b, p, pt: (nsteps - jnp.maximum(p, 1)) * n_pages
    lf_specs = [pl.BlockSpec((1, n_h, page), lambda b, p, pt, r=r: (pt[b, first(b, p, pt) + r], 0, 0))
                for r in range(n_pages)]
    row_spec = lambda n: pl.BlockSpec((1, n, HEAD_DIM), lambda b, p, pt: (b, 0, 0))
    assert ts * n_kv <= LANES
    grid_spec = pltpu.PrefetchScalarGridSpec(
        num_scalar_prefetch=1,
        grid=(bs, nsteps + 1),
        in_specs=[
            row_spec(rows), row_spec(ts * n_kv), row_spec(ts * n_kv), row_spec(rows), row_spec(rows),
            pl.BlockSpec((1, n_h, 1), lambda b, p, pt: (b, 0, 0)),
        ] + _tile_specs(page, cache_tiles.shape[1], n_pages, first) + lf_specs,
        out_specs=row_spec(rows),
        scratch_shapes=[pltpu.VMEM((rows, 1), F32), pltpu.VMEM((rows, 1), F32), pltpu.VMEM((rows, HEAD_DIM), F32),
                        pltpu.VMEM((n_h, 1), F32), pltpu.VMEM((rows, n_pages * page * SUBLANES), F32),
                        pltpu.VMEM((LANES, HEAD_DIM), F32), pltpu.VMEM((LANES, HEAD_DIM), F32)],
    )
    return pl.pallas_call(
        functools.partial(_fox_sample_kernel, n_kv=n_kv, gsz=gsz, ts=ts, n_pages=n_pages),
        grid_spec=grid_spec,
        out_shape=jax.ShapeDtypeStruct((bs, rows, HEAD_DIM), BF16),
        compiler_params=_params("parallel", "arbitrary"),
    )(page_table, q_rows, k_new, v_new, dq_b, dkn_b, tot, *([cache_tiles] * n_pages), *([logf_t] * n_pages))


def _compress_mlp(acc_a, acc_b_next, w2):
    return _dot(_gelu_tanh(acc_a + acc_b_next).astype(BF16), w2).astype(BF16)


def _compress_prompt_kernel(z_ref, pe_ref, w1_ref, w2_ref, o_ref, *, nch):
    acc_a = jnp.zeros((nch, CMP_HID), F32)
    acc_b = jnp.zeros((nch, CMP_HID), F32)
    for l in range(0, CMP_STRIDE, 2):
        xs = [z_ref[0, pl.ds(l + i, nch, stride=CMP_STRIDE), :] for i in range(2)]
        for acc_is_b, off in ((False, l), (True, CMP_STRIDE + l)):
            lhs = jnp.concatenate([(xs[i] + pe_ref[0, off + i:off + i + 1, :]).astype(BF16) for i in range(2)], axis=1)
            term = _dot(lhs, w1_ref[0, off:off + 2].reshape(2 * HEAD_DIM, CMP_HID))
            if acc_is_b:
                acc_b = acc_b + term
            else:
                acc_a = acc_a + term
    o_ref[0, 0, 0] = _compress_mlp(acc_a, pltpu.roll(acc_b, nch - 1, axis=0), w2_ref[0])


def _compress_prompt(z3, pe, w1, w2, n_kv, col_blk0):
    b, t, _ = z3.shape
    nch = t // CMP_STRIDE
    return pl.pallas_call(
        functools.partial(_compress_prompt_kernel, nch=nch),
        grid=(b, 2 * n_kv),
        in_specs=[
            pl.BlockSpec((1, t, HEAD_DIM), lambda i, c: (i, 0, col_blk0 + c)),
            pl.BlockSpec((1, CMP_LEN, HEAD_DIM), lambda i, c: (c // n_kv, 0, 0)),
            pl.BlockSpec((1, CMP_LEN, HEAD_DIM, CMP_HID), lambda i, c: (c // n_kv, 0, 0, 0)),
            pl.BlockSpec((1, CMP_HID, HEAD_DIM), lambda i, c: (c // n_kv, 0, 0)),
        ],
        out_specs=pl.BlockSpec((1, 1, 1, nch, HEAD_DIM), lambda i, c: (i, c // n_kv, c % n_kv, 0, 0)),
        out_shape=jax.ShapeDtypeStruct((b, 2, n_kv, nch, HEAD_DIM), BF16),
        compiler_params=_params("parallel", "arbitrary"),
    )(z3, pe, w1, w2)


def _nsa_prompt_kernel(q_ref, cos_ref, sin_ref, kc_ref, vc_ref, ks_ref, vs_ref, kw_ref, vw_ref, gl_ref, bg_ref,
                       o_ref, ksb_ref, vsb_ref, kwb_ref, vwb_ref, s_ref, *, tq, gsz, n_kv, nblk, ns, tk):
    kh = pl.program_id(1)
    qi = pl.program_id(2)
    _cast_kv_once(qi, ((ks_ref, ksb_ref), (vs_ref, vsb_ref), (kw_ref, kwb_ref), (vw_ref, vwb_ref)))
    t0 = qi * tq
    qscale = HEAD_DIM ** -0.5 * LOG2E
    cos, sin = cos_ref[...], sin_ref[...]
    qs = [q_ref[0, :, g * HEAD_DIM:(g + 1) * HEAD_DIM] * qscale for g in range(gsz)]
    q = jnp.concatenate(qs, axis=0).astype(BF16)
    qr = jnp.concatenate([_rope_lanes(x, cos, sin, HEAD_DIM // 2) for x in qs], axis=0).astype(BF16)
    qpos = t0 + _iota((tq, 1), 0)

    def tile_g(x):
        return jnp.concatenate([x] * gsz, axis=0)

    nch = kc_ref.shape[3]
    n_i = _iota((tq, nch), 1)
    ok_c = ((n_i * CMP_STRIDE + CMP_LEN - 1 <= qpos) & (n_i < nblk)).astype(F32)
    p = _masked_softmax(_dot_nt(q, kc_ref[0, 0, 0]), tile_g(ok_c))
    o_cmp = _dot(p.astype(BF16), vc_ref[0, 0, 0])
    psum = p[0:tq]
    for g in range(1, gsz):
        psum = psum + p[g * tq:(g + 1) * tq]
    ov_t = _overlap(_iota((LANES, nch), 1) * CMP_STRIDE, _iota((LANES, nch), 0) * SLC_BLOCK)
    imp_t = _block_importance(_dot_nt_exact(ov_t, psum), t0 + _iota((1, tq), 1), ns, axis=0)
    sel = _topk_mask_t(imp_t, min(N_SELECT, ns), ns).T.astype(BF16)

    def slc_mask(k0):
        expand = (((k0 + _iota((LANES, tk), 1)) >> 6) == _iota((LANES, tk), 0)).astype(BF16)
        selk = _dot(sel, expand)
        return tile_g(jnp.where((selk > 0.5) & (k0 + _iota((tq, tk), 1) <= qpos), 0.0, NEG))

    n_blocks = (t0 + tq + tk - 1) // tk
    acc, l = _two_pass_attention(qr, ksb_ref, vsb_ref, n_blocks, tk, s_ref, add_fn=slc_mask)
    o_slc = acc / l

    wlen = WINDOW + tq
    w0 = pl.multiple_of(jnp.maximum(t0 - WINDOW, 0), tq)
    dist = qpos - (w0 + _iota((tq, wlen), 1))
    sw = _dot_nt(qr, kwb_ref[pl.ds(w0, wlen), :]) + tile_g(jnp.where((dist >= 0) & (dist < WINDOW), 0.0, NEG))
    ew = jnp.exp2(sw - jnp.max(sw, axis=-1, keepdims=True))
    o_win = _dot(ew.astype(BF16), vwb_ref[pl.ds(w0, wlen), :]) / jnp.sum(ew, axis=-1, keepdims=True)

    gates = jax.nn.sigmoid(gl_ref[0] + bg_ref[...])
    n_h = n_kv * gsz
    for g in range(gsz):
        rows = slice(g * tq, (g + 1) * tq)
        col = kh * gsz + g
        o = (_lane_pick(gates, col) * o_cmp[rows] + _lane_pick(gates, n_h + col) * o_slc[rows]
             + _lane_pick(gates, 2 * n_h + col) * o_win[rows])
        o_ref[0, :, g * HEAD_DIM:(g + 1) * HEAD_DIM] = o.astype(BF16)


def _nsa_prompt(z3, cos, sin, kvc, bg_pad, n_kv, gsz, ks_blk, vs_blk, kw_blk, vw_blk, gl_blk, tq, tk):
    b, t, _ = z3.shape
    nch = kvc.shape[3]
    qw = gsz * HEAD_DIM
    ns = -(-t // SLC_BLOCK)
    assert t >= WINDOW + tq and t % tk == 0 and tk % SLC_BLOCK == 0 and ns <= LANES
    kv_spec = lambda blk0: pl.BlockSpec((1, t, HEAD_DIM), lambda i, k, q: (i, 0, blk0 + k))
    return pl.pallas_call(
        functools.partial(_nsa_prompt_kernel, tq=tq, gsz=gsz, n_kv=n_kv, nblk=nch - 1, ns=ns, tk=tk),
        grid=(b, n_kv, t // tq),
        in_specs=[
            pl.BlockSpec((1, tq, qw), lambda i, k, q: (i, q, k)),
            pl.BlockSpec((tq, LANES), lambda i, k, q: (q, 0)),
            pl.BlockSpec((tq, LANES), lambda i, k, q: (q, 0)),
            pl.BlockSpec((1, 1, 1, nch, HEAD_DIM), lambda i, k, q: (i, 0, k, 0, 0)),
            pl.BlockSpec((1, 1, 1, nch, HEAD_DIM), lambda i, k, q: (i, 1, k, 0, 0)),
            kv_spec(ks_blk), kv_spec(vs_blk), kv_spec(kw_blk), kv_spec(vw_blk),
            pl.BlockSpec((1, tq, LANES), lambda i, k, q: (i, q, gl_blk)),
            pl.BlockSpec((1, LANES), lambda i, k, q: (0, 0)),
        ],
        out_specs=pl.BlockSpec((1, tq, qw), lambda i, k, q: (i, q, k)),
        out_shape=jax.ShapeDtypeStruct((b, t, n_kv * qw), BF16),
        scratch_shapes=[pltpu.VMEM((t, HEAD_DIM), BF16) for _ in range(4)] + [
            pltpu.VMEM((t // tk, gsz * tq, tk), F32)],
        compiler_params=_params("parallel", "parallel", "arbitrary"),
    )(z3, cos, sin, kvc, kvc, z3, z3, z3, z3, z3, bg_pad)


def _compress_sample_kernel(pt_ref, *refs, n_kv, n_pages, page):
    pg_refs = refs[:n_pages]
    pe_ref, w1_ref, w2_ref, o_ref, last_ref = refs[n_pages:]
    s = pl.program_id(1)
    cpp = page // CMP_STRIDE
    m = n_pages * cpp
    rpr = 2 * n_kv

    @pl.when(s == 0)
    def _():
        last_ref[...] = jnp.zeros(last_ref.shape, F32)

    for typ in range(2):
        acc_a = jnp.zeros((n_kv * m, CMP_HID), F32)
        acc_b = jnp.zeros((n_kv * m, CMP_HID), F32)
        for l in range(0, CMP_STRIDE, 2):
            xs = [jnp.concatenate(
                [pg[pl.ds((l + i) * rpr + typ * n_kv + h, cpp, stride=CMP_STRIDE * rpr), :]
                 for h in range(n_kv) for pg in pg_refs], axis=0) for i in range(2)]
            for acc_is_b, off in ((False, l), (True, CMP_STRIDE + l)):
                lhs = jnp.concatenate([(xs[i] + pe_ref[typ, off + i:off + i + 1, :]).astype(BF16)
                                       for i in range(2)], axis=1)
                term = _dot(lhs, w1_ref[typ, off:off + 2].reshape(2 * HEAD_DIM, CMP_HID))
                if acc_is_b:
                    acc_b = acc_b + term
                else:
                    acc_a = acc_a + term
        first = _iota((m, 1), 0) == 0
        for h in range(n_kv):
            a = acc_a[h * m:(h + 1) * m]
            idx = typ * n_kv + h
            prev = jnp.where(first, last_ref[idx, SUBLANES - 1:SUBLANES, :], pltpu.roll(a, 1, axis=0))
            last_ref[idx] = a[m - SUBLANES:m]
            o_ref[0, typ, h] = _compress_mlp(prev, acc_b[h * m:(h + 1) * m], w2_ref[typ])


def _compress_sample(page_table, cache_rows, pe, w1, w2, page, n_kv, n_pages):
    bs, npg = page_table.shape
    m = n_pages * (page // CMP_STRIDE)
    grid_spec = pltpu.PrefetchScalarGridSpec(
        num_scalar_prefetch=1,
        grid=(bs, npg // n_pages),
        in_specs=_page_specs(page, n_kv, n_pages, lambda b, s, pt: s * n_pages) + [
            pl.BlockSpec((2, CMP_LEN, HEAD_DIM), lambda b, s, pt: (0, 0, 0)),
            pl.BlockSpec((2, CMP_LEN, HEAD_DIM, CMP_HID), lambda b, s, pt: (0, 0, 0, 0)),
            pl.BlockSpec((2, CMP_HID, HEAD_DIM), lambda b, s, pt: (0, 0, 0)),
        ],
        out_specs=pl.BlockSpec((1, 2, n_kv, m, HEAD_DIM), lambda b, s, pt: (b, 0, 0, s, 0)),
        scratch_shapes=[pltpu.VMEM((2 * n_kv, SUBLANES, CMP_HID), F32)],
    )
    return pl.pallas_call(
        functools.partial(_compress_sample_kernel, n_kv=n_kv, n_pages=n_pages, page=page),
        grid_spec=grid_spec,
        out_shape=jax.ShapeDtypeStruct((bs, 2, n_kv, npg * (page // CMP_STRIDE), HEAD_DIM), BF16),
        compiler_params=_params("parallel", "arbitrary"),
    )(page_table, *([cache_rows] * n_pages), pe, w1, w2)


def _nsa_sample_a_kernel(q_ref, cos_ref, sin_ref, kc_ref, vc_ref, win_ref, kwn_ref,
                         ocmp_ref, owin_ref, sel_ref, qr_ref, kpad_ref, vpad_ref, *, n_kv, gsz, ts, p0, ns):
    qscale = HEAD_DIM ** -0.5 * LOG2E
    rows = ts * gsz
    nr = kc_ref.shape[3]
    nsp = sel_ref.shape[3]
    wrows = win_ref.shape[1] // (2 * n_kv)
    tok = _iota((rows, 1), 0) // gsz
    qpos = p0 + tok
    r_i = _iota((rows, nr), 1)
    ok_c = ((r_i >= 1) & ((r_i - 1) * CMP_STRIDE + CMP_LEN - 1 <= qpos)).astype(F32)
    ov = _overlap((_iota((nr, nsp), 0) - 1) * CMP_STRIDE, _iota((nr, nsp), 1) * SLC_BLOCK)
    group = ((_iota((rows, rows), 0) // gsz) == (_iota((rows, rows), 1) // gsz)).astype(F32)
    ok_w1 = _iota((rows, wrows), 1) > tok + (wrows - WINDOW)
    ok_w2 = _iota((rows, kpad_ref.shape[0]), 1) <= tok
    kvw = n_kv * HEAD_DIM
    for kh in range(n_kv):
        q = q_ref[0, kh]
        qr = _rope_lanes(q, cos_ref[...], sin_ref[...], HEAD_DIM // 2)
        qr_ref[0, kh] = qr
        qb, qrb = (q * qscale).astype(BF16), (qr * qscale).astype(BF16)
        p = _masked_softmax(_dot_nt(qb, kc_ref[0, 0, kh]), ok_c)
        ocmp_ref[0, kh] = _dot(p.astype(BF16), vc_ref[0, 0, kh])
        imp = _dot_exact(group, _dot_exact(p, ov))
        sel_ref[0, kh] = _topk_mask(_block_importance(imp, qpos, ns), min(N_SELECT, ns))
        sl = slice(kh * HEAD_DIM, (kh + 1) * HEAD_DIM)
        slv = slice(kvw + kh * HEAD_DIM, kvw + (kh + 1) * HEAD_DIM)
        _pad_rows_scratch(kpad_ref, kwn_ref[0, :, sl])
        _pad_rows_scratch(vpad_ref, kwn_ref[0, :, slv])
        k_win = win_ref[0, pl.ds(kh, wrows, stride=2 * n_kv), :].astype(BF16)
        v_win = win_ref[0, pl.ds(n_kv + kh, wrows, stride=2 * n_kv), :].astype(BF16)
        s1 = jnp.where(ok_w1, _dot_nt(qrb, k_win), NEG)
        s2 = jnp.where(ok_w2, _dot_nt(qrb, kpad_ref[...].astype(BF16)), NEG)
        m = jnp.maximum(jnp.max(s1, axis=-1, keepdims=True), jnp.max(s2, axis=-1, keepdims=True))
        e1, e2 = jnp.exp2(s1 - m), jnp.exp2(s2 - m)
        l = jnp.sum(e1, axis=-1, keepdims=True) + jnp.sum(e2, axis=-1, keepdims=True)
        o = _dot(e1.astype(BF16), v_win) + _dot(e2.astype(BF16), vpad_ref[...].astype(BF16))
        owin_ref[0, kh] = o / l


def _nsa_sample_a(q_rows, cos, sin, kvc, win_rows, kvw_new, n_kv, gsz, ts, p0, ns):
    bs = q_rows.shape[0]
    rows = ts * gsz
    nr = kvc.shape[3]
    nsp = _round_up(ns, LANES)
    wr, w = win_rows.shape[1], kvw_new.shape[2]
    assert wr == WINDOW * 2 * n_kv
    o_sds = jax.ShapeDtypeStruct((bs, n_kv, rows, HEAD_DIM), F32)
    o_spec = pl.BlockSpec((1, n_kv, rows, HEAD_DIM), lambda b: (b, 0, 0, 0))
    return pl.pallas_call(
        functools.partial(_nsa_sample_a_kernel, n_kv=n_kv, gsz=gsz, ts=ts, p0=p0, ns=ns),
        grid=(bs,),
        in_specs=[
            o_spec,
            pl.BlockSpec((rows, LANES), lambda b: (0, 0)),
            pl.BlockSpec((rows, LANES), lambda b: (0, 0)),
            pl.BlockSpec((1, 1, n_kv, nr, HEAD_DIM), lambda b: (b, 0, 0, 0, 0)),
            pl.BlockSpec((1, 1, n_kv, nr, HEAD_DIM), lambda b: (b, 1, 0, 0, 0)),
            pl.BlockSpec((1, wr, HEAD_DIM), lambda b: (b, 0, 0)),
            pl.BlockSpec((1, ts, w), lambda b: (b, 0, 0)),
        ],
        out_specs=[o_spec, o_spec, pl.BlockSpec((1, n_kv, rows, nsp), lambda b: (b, 0, 0, 0)), o_spec],
        out_shape=[o_sds, o_sds, jax.ShapeDtypeStruct((bs, n_kv, rows, nsp), F32), o_sds],
        scratch_shapes=[pltpu.VMEM((LANES, HEAD_DIM), F32), pltpu.VMEM((LANES, HEAD_DIM), F32)],
        compiler_params=_params("parallel"),
    )(q_rows, cos, sin, kvc, kvc, win_rows, kvw_new)


def _nsa_sample_b_kernel(pt_ref, q_ref, sel_ref, kn_ref, ocmp_ref, owin_ref, gl_ref, bg_ref, *refs,
                         n_kv, gsz, ts, n_pages):
    pg_refs = refs[:n_pages]
    o_ref, m_ref, l_ref, acc_ref, bias_ref, npad_ref = refs[n_pages:]
    p = pl.program_id(1)
    last = pl.num_programs(1) - 1
    rows = n_kv * ts * gsz
    page = pg_refs[0].shape[0]
    nsp = sel_ref.shape[2]
    bps = n_pages * (page // SLC_BLOCK)
    blk_cols = SLC_BLOCK * SUBLANES
    qb = (q_ref[0] * (HEAD_DIM ** -0.5 * LOG2E)).astype(BF16)
    rr = _iota((rows, 1), 0)
    row_head = rr // (ts * gsz)
    row_tok = (rr % (ts * gsz)) // gsz

    @pl.when(p == 0)
    def _():
        _state_init(m_ref, l_ref, acc_ref)

    @pl.when(p < last)
    def _():
        ks, vs, off = _flat_kv(pg_refs)

        @pl.when(p == 0)
        def _():
            bias_ref[...] = _own_head_bias(row_head + off, bias_ref.shape[1])

        shift = (_iota((nsp, LANES), 0) == p * bps + _iota((nsp, LANES), 1)).astype(BF16)
        flags = jnp.where(_dot(sel_ref[0].astype(BF16), shift) > 0.5, 0.0, NEG)
        sel_bias = jnp.concatenate([jnp.broadcast_to(flags[:, i:i + 1], (rows, blk_cols)) for i in range(bps)], axis=1)
        _state_update(m_ref, l_ref, acc_ref, ..., _flat_scores(qb, ks) + bias_ref[...] + sel_bias, vs)

    @pl.when(p == last)
    def _():
        _pad_rows_scratch(npad_ref, kn_ref[0])
        flat = npad_ref[...]
        s = _dot_nt(qb, pltpu.roll(flat, n_kv, axis=0).astype(BF16))
        c = _iota(s.shape, 1)
        ok = (((c % (2 * n_kv)) == n_kv + row_head) & ((c // (2 * n_kv)) <= row_tok)
              & (_lane_pick(sel_ref[0], last * bps) > 0.5))
        _state_update(m_ref, l_ref, acc_ref, ..., jnp.where(ok, s, NEG), flat.astype(BF16))
        gates = jax.nn.sigmoid(gl_ref[0] + bg_ref[...])
        o_ref[0] = (gates[:, 0:1] * ocmp_ref[0] + gates[:, 1:2] * (acc_ref[...] / l_ref[...])
                    + gates[:, 2:3] * owin_ref[0])


def _nsa_sample_b(page_table, q_rows, sel, kvs_new, o_cmp, o_win, gl_rows, bg_rows, cache_tiles,
                  page, n_kv, gsz, ts, n_pages):
    bs, rows, _ = q_rows.shape
    nsp = sel.shape[2]
    nsteps = page_table.shape[1] // n_pages
    first = lambda b, p, pt: jnp.minimum(p, nsteps - 1) * n_pages
    row_spec = lambda n, width: pl.BlockSpec((1, n, width), lambda b, p, pt: (b, 0, 0))
    assert ts * 2 * n_kv <= LANES and n_pages * (page // SLC_BLOCK) <= LANES
    grid_spec = pltpu.PrefetchScalarGridSpec(
        num_scalar_prefetch=1,
        grid=(bs, nsteps + 1),
        in_specs=[
            row_spec(rows, HEAD_DIM), row_spec(rows, nsp), row_spec(ts * 2 * n_kv, HEAD_DIM),
            row_spec(rows, HEAD_DIM), row_spec(rows, HEAD_DIM), row_spec(rows, LANES),
            pl.BlockSpec((rows, LANES), lambda b, p, pt: (0, 0)),
        ] + _tile_specs(page, cache_tiles.shape[1], n_pages, first),
        out_specs=row_spec(rows, HEAD_DIM),
        scratch_shapes=[pltpu.VMEM((rows, 1), F32), pltpu.VMEM((rows, 1), F32), pltpu.VMEM((rows, HEAD_DIM), F32),
                        pltpu.VMEM((rows, n_pages * page * SUBLANES), F32), pltpu.VMEM((LANES, HEAD_DIM), F32)],
    )
    return pl.pallas_call(
        functools.partial(_nsa_sample_b_kernel, n_kv=n_kv, gsz=gsz, ts=ts, n_pages=n_pages),
        grid_spec=grid_spec,
        out_shape=jax.ShapeDtypeStruct((bs, rows, HEAD_DIM), F32),
        compiler_params=_params("parallel", "arbitrary"),
    )(page_table, q_rows, sel, kvs_new, o_cmp, o_win, gl_rows, bg_rows, *([cache_tiles] * n_pages))


def _rope_tables(pos, half):
    inv = ROPE_THETA ** (-jnp.arange(half, dtype=F32) / half)
    ang = pos.astype(F32)[:, None] * inv
    cos, sin = jnp.cos(ang), jnp.sin(ang)
    reps = LANES // (2 * half)
    return (jnp.tile(jnp.concatenate([cos, cos], axis=-1), (1, reps)),
            jnp.tile(jnp.concatenate([-sin, sin], axis=-1), (1, reps)))


def _pad_cols(w, n):
    return jnp.pad(w, ((0, 0), (0, n - w.shape[1])))


def _head_rows(x, n_kv, gsz):
    b, t, _ = x.shape
    return x.reshape(b, t, n_kv, gsz, -1).transpose(0, 2, 1, 3, 4).reshape(b, n_kv, t * gsz, -1)


def _token_cols(x, ts, gsz):
    b, n_kv, _, f = x.shape
    return x.reshape(b, n_kv, ts, gsz, f).transpose(0, 2, 1, 3, 4).reshape(b * ts, n_kv * gsz * f)


def kernel(x_prompt, x_sample, cache_a_kv, cache_b_kv, cache_b_logf, cache_cmp_kv, cache_slc_kv, state_win_kv, page_table, norm_mix_e, w_in_e, b_forget, lam_q1, lam_k1, lam_q2, lam_k2, subln_g, w_out_e, norm_mix_o, w_in_o, b_gate, cmp_pe_k, cmp_w1_k, cmp_w2_k, cmp_pe_v, cmp_w1_v, cmp_w2_v, w_out_o, norm_ffn, w_gate, w_up, w_down, norm_final):
    bp, t, d = x_prompt.shape
    bs, ts, _ = x_sample.shape
    n_even, n_odd = w_in_e.shape[0], w_in_o.shape[0]
    depth = n_even + n_odd
    page = cache_a_kv.shape[2]
    npg = page_table.shape[1]
    p0 = npg * page
    h_all = d // HEAD_DIM
    h_b = cache_b_logf.shape[-1]
    h_a = h_all - h_b
    kv_a, kv_b, kv_c = cache_a_kv.shape[4], cache_b_kv.shape[4], cache_cmp_kv.shape[4]
    g_a, g_b, g_c = h_a // kv_a, h_b // kv_b, h_all // kv_c
    mp, ms = bp * t, bs * ts
    hd = HEAD_DIM

    e_qa, e_ka = 0, h_a * hd
    e_va = e_ka + kv_a * hd
    e_qf = e_va + kv_a * hd
    e_kf = e_qf + h_b * hd
    e_vf = e_kf + kv_b * hd
    e_fg = e_vf + kv_b * hd
    o_kvc = h_all * hd
    o_kvs = o_kvc + 2 * kv_c * hd
    o_kvw = o_kvs + 2 * kv_c * hd
    o_gl = o_kvw + 2 * kv_c * hd
    assert 3 * h_all <= LANES and h_b <= LANES

    tn_e = _pick_tile(e_va, (512, 256, 128))
    tn_o = _pick_tile(kv_c * hd, (512, 256, 128))
    ne_pad = _round_up(e_fg + h_b, tn_e)
    no_pad = _round_up(o_gl + 3 * h_all, tn_o)
    tm_p = _pick_tile(mp, (1024, 512, 256, 128))
    tm_f = _pick_tile(mp, (1024, 512, 256, 128))
    tn_out = _pick_tile(d, (1024, 512, 256, 128))
    tf = _pick_tile(w_gate.shape[2], (256, 128))
    tq_e = _pick_tile(t, (512, 256, 128))
    tk_e = _pick_tile(t, (512, 256))
    tq_o = _pick_tile(t, (256, 128))
    tk_o = _pick_tile(t, (512, 256, 128))
    n_pages = _pick_tile(npg, (8, 4, 2))
    n_pages_c = _pick_tile(npg, (16, 8, 4, 2))

    xp = x_prompt.reshape(mp, d)
    xs = x_sample.reshape(ms, d)
    pos_p = jnp.tile(jnp.arange(t, dtype=jnp.int32), bp)
    pos_s = p0 + jnp.tile(jnp.arange(ts, dtype=jnp.int32), bs)
    tab_a_p, tab_a_s = _rope_tables(pos_p, DA // 2), _rope_tables(pos_s, DA // 2)
    tab_c_p, tab_c_s = _rope_tables(pos_p, hd // 2), _rope_tables(pos_s, hd // 2)
    tab_c_q = _rope_tables(jnp.arange(t, dtype=jnp.int32), hd // 2)
    tab_c_rows = _rope_tables(p0 + jnp.repeat(jnp.arange(ts, dtype=jnp.int32), g_c), hd // 2)

    outs = {name: ([], []) for name in ("a", "b", "f", "c", "s", "w")}

    for layer in range(depth):
        i = layer // 2
        if layer % 2 == 0:
            lam_init = 0.8 - 0.6 * float(np.exp(-0.3 * layer))
            w_in = _pad_cols(w_in_e[i].astype(BF16), ne_pad)
            w_out = w_out_e[i].astype(BF16)
            rope_e = ((0, e_va // tn_e),)
            lam_vec = jnp.stack([lam_q1[i], lam_k1[i], lam_q2[i], lam_k2[i]])
            bf_pad = jnp.pad(b_forget[i], (0, LANES - h_b)).reshape(1, LANES)

            z = _proj(xp, norm_mix_e[i], w_in, *tab_a_p, rope_e, DA // 2, tm_p, tn_e)
            z3 = z.reshape(bp, t, ne_pad)
            outs["a"][0].append(z3[:, :, e_ka:e_qf].reshape(bp, t, 2, kv_a, hd))
            outs["b"][0].append(z3[:, :, e_kf:e_fg].reshape(bp, t, 2, kv_b, hd))
            lf, dkc, dkr = _logf_prompt(z3, bf_pad, e_fg // LANES)
            outs["f"][0].append(lf[:, :, :h_b])
            o_a = _diff_prompt(z3, lam_vec, subln_g[i], kv_a, g_a, e_qa // (g_a * hd), e_ka // hd, e_va // hd,
                               lam_init, tq_e, tk_e)
            o_b = _fox_prompt(z3, dkr[:, :h_b].reshape(bp, h_b, 1, t), dkc, kv_b, g_b,
                              e_qf // (g_b * hd), e_kf // hd, e_vf // hd, tq_e, tk_e)
            xp = _mm_res([o_a.reshape(mp, h_a * hd), o_b.reshape(mp, h_b * hd)], w_out, xp, tm_p, tn_out)

            z = _proj(xs, norm_mix_e[i], w_in, *tab_a_s, rope_e, DA // 2, ms, tn_e)
            z3 = z.reshape(bs, ts, ne_pad)
            outs["a"][1].append(z3[:, :, e_ka:e_qf].reshape(bs, ts, 2, kv_a, hd))
            outs["b"][1].append(z3[:, :, e_kf:e_fg].reshape(bs, ts, 2, kv_b, hd))
            lf, dq, tot = _logf_sample(z[:, e_fg:e_fg + LANES], bf_pad, ts)
            outs["f"][1].append(lf[:, :h_b].reshape(bs, ts, h_b))
            o_a = _diff_sample(page_table, lam_vec, subln_g[i],
                               _head_rows(z3[:, :, e_qa:e_ka], kv_a, g_a).reshape(bs, -1, hd),
                               z3[:, :, e_ka:e_va].reshape(bs, ts * kv_a, hd),
                               z3[:, :, e_va:e_qf].reshape(bs, ts * kv_a, hd),
                               _tile_view(cache_a_kv[i]), page, kv_a, g_a, ts, lam_init, n_pages)
            rows_b = kv_b * ts * g_b
            dq3 = dq[:, :h_b].reshape(bs, ts, h_b)
            dq_b = jnp.broadcast_to(_head_rows(dq3, kv_b, g_b).reshape(bs, rows_b, 1), (bs, rows_b, LANES))
            dkn = jnp.swapaxes(dq3, 1, 2).reshape(bs, kv_b, 1, g_b, ts, 1)
            dkn = jnp.broadcast_to(dkn, (bs, kv_b, ts, g_b, ts, kv_b)).reshape(bs, rows_b, ts * kv_b)
            dkn_b = jnp.pad(dkn, ((0, 0), (0, 0), (0, LANES - ts * kv_b)))
            tot3 = tot[:, :h_b].reshape(bs, ts, h_b)[:, 0, :, None]
            o_b = _fox_sample(page_table, _head_rows(z3[:, :, e_qf:e_kf], kv_b, g_b).reshape(bs, -1, hd),
                              z3[:, :, e_kf:e_vf].reshape(bs, ts * kv_b, hd),
                              z3[:, :, e_vf:e_fg].reshape(bs, ts * kv_b, hd), dq_b, dkn_b, tot3,
                              _tile_view(cache_b_kv[i]), jnp.swapaxes(cache_b_logf[i], 1, 2),
                              page, kv_b, g_b, ts, n_pages)
            xs = _mm_res([_token_cols(o_a.reshape(bs, kv_a, ts * g_a, hd), ts, g_a),
                          _token_cols(o_b.reshape(bs, kv_b, ts * g_b, hd), ts, g_b)], w_out, xs, ms, tn_out)
        else:
            w_in = _pad_cols(w_in_o[i].astype(BF16), no_pad)
            w_out = w_out_o[i].astype(BF16)
            rope_o = ((o_kvs // tn_o, (o_kvs + kv_c * hd) // tn_o), (o_kvw // tn_o, (o_kvw + kv_c * hd) // tn_o))
            bg_pad = jnp.pad(b_gate[i], (0, LANES - 3 * h_all)).reshape(1, LANES)
            pe = jnp.stack([cmp_pe_k[i], cmp_pe_v[i]])
            w1 = jnp.stack([cmp_w1_k[i], cmp_w1_v[i]]).astype(BF16)
            w2 = jnp.stack([cmp_w2_k[i], cmp_w2_v[i]]).astype(BF16)

            z = _proj(xp, norm_mix_o[i], w_in, *tab_c_p, rope_o, hd // 2, tm_p, tn_o)
            z3 = z.reshape(bp, t, no_pad)
            outs["c"][0].append(z3[:, :, o_kvc:o_kvs].reshape(bp, t, 2, kv_c, hd))
            outs["s"][0].append(z3[:, :, o_kvs:o_kvw].reshape(bp, t, 2, kv_c, hd))
            keep = min(WINDOW, t)
            outs["w"][0].append(z3[:, t - keep:, o_kvw:o_gl].reshape(bp, keep, 2, kv_c, hd))
            kvc = _compress_prompt(z3, pe, w1, w2, kv_c, o_kvc // hd)
            o = _nsa_prompt(z3, *tab_c_q, kvc, bg_pad, kv_c, g_c, o_kvs // hd, o_kvs // hd + kv_c,
                            o_kvw // hd, o_kvw // hd + kv_c, o_gl // LANES, tq_o, tk_o)
            xp = _mm_res([o.reshape(mp, d)], w_out, xp, tm_p, tn_out)

            z = _proj(xs, norm_mix_o[i], w_in, *tab_c_s, rope_o, hd // 2, ms, tn_o)
            z3 = z.reshape(bs, ts, no_pad)
            outs["c"][1].append(z3[:, :, o_kvc:o_kvs].reshape(bs, ts, 2, kv_c, hd))
            outs["s"][1].append(z3[:, :, o_kvs:o_kvw].reshape(bs, ts, 2, kv_c, hd))
            win = state_win_kv[i]
            kvw_new = z3[:, :, o_kvw:o_gl]
            win_all = jnp.concatenate([win, kvw_new.reshape(bs, ts, 2, kv_c, hd)], axis=1)
            outs["w"][1].append(win_all[:, -min(WINDOW, win.shape[1] + ts):])
            kvc = _compress_sample(page_table, cache_cmp_kv[i].reshape(-1, hd), pe, w1, w2, page, kv_c, min(16, npg))
            ns = -(-(p0 + ts) // SLC_BLOCK)
            o_cmp, o_win, sel, qr = _nsa_sample_a(_head_rows(z3[:, :, :o_kvc], kv_c, g_c), *tab_c_rows, kvc,
                                                  win.reshape(bs, -1, hd), kvw_new, kv_c, g_c, ts, p0, ns)
            rows = kv_c * ts * g_c
            gl = z3[:, :, o_gl:o_gl + 3 * h_all].reshape(bs, ts, 3, kv_c, g_c).transpose(0, 3, 1, 4, 2)
            gl_rows = jnp.pad(gl.reshape(bs, rows, 3), ((0, 0), (0, 0), (0, LANES - 3)))
            bg = jnp.broadcast_to(b_gate[i].reshape(1, 3, kv_c, g_c), (ts, 3, kv_c, g_c)).transpose(2, 0, 3, 1)
            bg_rows = jnp.pad(bg.reshape(rows, 3), ((0, 0), (0, LANES - 3)))
            o = _nsa_sample_b(page_table, qr.reshape(bs, rows, hd), sel.reshape(bs, rows, -1),
                              z3[:, :, o_kvs:o_kvw].reshape(bs, ts * 2 * kv_c, hd),
                              o_cmp.reshape(bs, rows, hd), o_win.reshape(bs, rows, hd), gl_rows, bg_rows,
                              _tile_view(cache_slc_kv[i]), page, kv_c, g_c, ts, n_pages_c)
            xs = _mm_res([_token_cols(o.reshape(bs, kv_c, ts * g_c, hd), ts, g_c).astype(BF16)], w_out, xs, ms, tn_out)

        g_final = norm_final if layer == depth - 1 else None
        wg, wu, wd = w_gate[layer].astype(BF16), w_up[layer].astype(BF16), w_down[layer].astype(BF16)
        xp = _ffn(xp, norm_ffn[layer], wg, wu, wd, g_final, tm_f, tf)
        xs = _ffn(xs, norm_ffn[layer], wg, wu, wd, g_final, ms, tf)

    def stack(name, which):
        parts = outs[name][which]
        return parts[0][None] if len(parts) == 1 else jnp.stack(parts)

    return (xp.reshape(bp, t, d), xs.reshape(bs, ts, d),
            stack("a", 0), stack("a", 1), stack("b", 0), stack("b", 1), stack("f", 0), stack("f", 1),
            stack("c", 0), stack("c", 1), stack("s", 0), stack("s", 1), stack("w", 0), stack("w", 1))
```

```python
import functools

import numpy as np
import jax
import jax.numpy as jnp
from jax import lax
from jax.experimental import pallas as pl
from jax.experimental.pallas import tpu as pltpu

F32 = jnp.float32
BF16 = jnp.bfloat16
HIGHEST = lax.Precision.HIGHEST

LANES = 128
SUBLANES = 8
VMEM_LIMIT = 56 * 1024 * 1024
VMEM_LIMIT_BIG = 63 * 1024 * 1024

HEAD_DIM = 128
DA = HEAD_DIM // 2
CMP_LEN = 32
CMP_STRIDE = 16
CMP_HID = 2 * HEAD_DIM
SLC_BLOCK = 64
N_SELECT = 16
WINDOW = 512
ROPE_THETA = 10000.0
NORM_EPS = 1e-6
LOG2E = 1.4426950408889634
NEG = -1e30
BIG = 1e30
LOWEST = -3e38


def _params(*sem):
    return pltpu.CompilerParams(dimension_semantics=sem, vmem_limit_bytes=VMEM_LIMIT)


def _dot(a, b):
    return jnp.dot(a, b, preferred_element_type=F32)


def _dot_nt(a, b):
    return lax.dot_general(a, b, (((1,), (1,)), ((), ())), preferred_element_type=F32)


def _dot_exact(a, b):
    return jnp.dot(a, b, precision=HIGHEST, preferred_element_type=F32)


def _dot_nt_exact(a, b):
    return lax.dot_general(a, b, (((1,), (1,)), ((), ())), precision=HIGHEST, preferred_element_type=F32)


def _iota(shape, axis):
    return lax.broadcasted_iota(jnp.int32, shape, axis)


def _round_up(n, m):
    return -(-n // m) * m


def _pick_tile(n, prefs):
    for p in prefs:
        if n % p == 0:
            return p
    return n


def _rope_lanes(a, cos, sin, half):
    if 2 * half == LANES:
        partner = pltpu.roll(a, half, axis=1)
    else:
        first = (_iota(a.shape, 1) & (2 * half - 1)) < half
        partner = jnp.where(first, pltpu.roll(a, LANES - half, axis=1), pltpu.roll(a, half, axis=1))
    return a * cos + partner * sin


def _rmsnorm_rows(x, g):
    ms = jnp.mean(x * x, axis=-1, keepdims=True)
    return x * lax.rsqrt(ms + NORM_EPS) * g


def _proj_kernel(x_ref, g_ref, w_ref, cos_ref, sin_ref, o_ref, xn_ref, *, rope_ranges, half):
    j = pl.program_id(1)

    @pl.when(j == 0)
    def _():
        rc = min(256, x_ref.shape[0])
        for r in range(x_ref.shape[0] // rc):
            rows = slice(r * rc, (r + 1) * rc)
            xn_ref[rows, :] = _rmsnorm_rows(x_ref[rows, :], g_ref[...]).astype(BF16)

    o_ref[...] = _dot(xn_ref[...], w_ref[...])
    is_rope = functools.reduce(jnp.logical_or, [(j >= a) & (j < b) for a, b in rope_ranges])

    @pl.when(is_rope)
    def _():
        cos, sin = cos_ref[...], sin_ref[...]
        for c in range(o_ref.shape[1] // LANES):
            sl = slice(c * LANES, (c + 1) * LANES)
            o_ref[:, sl] = _rope_lanes(o_ref[:, sl], cos, sin, half)


def _proj(x, g, w, cos, sin, rope_ranges, half, tm, tn):
    m, d = x.shape
    n = w.shape[1]
    return pl.pallas_call(
        functools.partial(_proj_kernel, rope_ranges=rope_ranges, half=half),
        grid=(m // tm, n // tn),
        in_specs=[
            pl.BlockSpec((tm, d), lambda i, j: (i, 0)),
            pl.BlockSpec((1, d), lambda i, j: (0, 0)),
            pl.BlockSpec((d, tn), lambda i, j: (0, j)),
            pl.BlockSpec((tm, LANES), lambda i, j: (i, 0)),
            pl.BlockSpec((tm, LANES), lambda i, j: (i, 0)),
        ],
        out_specs=pl.BlockSpec((tm, tn), lambda i, j: (i, j)),
        out_shape=jax.ShapeDtypeStruct((m, n), F32),
        scratch_shapes=[pltpu.VMEM((tm, d), BF16)],
        compiler_params=pltpu.CompilerParams(dimension_semantics=("parallel", "arbitrary"),
                                             vmem_limit_bytes=VMEM_LIMIT_BIG),
    )(x, g.reshape(1, d), w, cos, sin)


def _mm_res_kernel(*refs, n_lhs):
    a_refs, (w_ref, r_ref, o_ref) = refs[:n_lhs], refs[n_lhs:]
    acc = r_ref[...]
    off = 0
    for a_ref in a_refs:
        kk = a_ref.shape[1]
        acc = acc + _dot(a_ref[...], w_ref[off:off + kk, :])
        off += kk
    o_ref[...] = acc


def _mm_res(lhs, w, res, tm, tn):
    m, n = res.shape
    k = w.shape[0]
    return pl.pallas_call(
        functools.partial(_mm_res_kernel, n_lhs=len(lhs)),
        grid=(m // tm, n // tn),
        in_specs=[pl.BlockSpec((tm, a.shape[1]), lambda i, j: (i, 0)) for a in lhs] + [
            pl.BlockSpec((k, tn), lambda i, j: (0, j)),
            pl.BlockSpec((tm, tn), lambda i, j: (i, j)),
        ],
        out_specs=pl.BlockSpec((tm, tn), lambda i, j: (i, j)),
        out_shape=jax.ShapeDtypeStruct((m, n), F32),
        compiler_params=_params("parallel", "arbitrary"),
    )(*lhs, w, res)


def _ffn_kernel(x_ref, g_ref, wg_ref, wu_ref, wd_ref, gf_ref, o_ref, xn_ref, h_ref, *, final_norm):
    f = pl.program_id(1)
    n_tiles = pl.num_programs(1) - 1

    def hidden():
        xn = xn_ref[...]
        gate = _dot(xn, wg_ref[...])
        up = _dot(xn, wu_ref[...])
        return (gate * jax.nn.sigmoid(gate) * up).astype(BF16)

    rc = min(256, x_ref.shape[0])
    chunks = [slice(r * rc, (r + 1) * rc) for r in range(x_ref.shape[0] // rc)]

    @pl.when(f == 0)
    def _():
        for rows in chunks:
            x = x_ref[rows, :]
            xn_ref[rows, :] = _rmsnorm_rows(x, g_ref[...]).astype(BF16)
            o_ref[rows, :] = x
        h_ref[0] = hidden()

    @pl.when((f > 0) & (f < n_tiles))
    def _():
        h_ref[f % 2] = hidden()
        o_ref[...] += _dot(h_ref[(f - 1) % 2], wd_ref[...])

    @pl.when(f == n_tiles)
    def _():
        o_ref[...] += _dot(h_ref[(f - 1) % 2], wd_ref[...])
        if final_norm:
            for rows in chunks:
                o_ref[rows, :] = _rmsnorm_rows(o_ref[rows, :], gf_ref[...])


def _ffn(x, g, wg, wu, wd, g_final, tm, tf):
    m, d = x.shape
    n_tiles = wg.shape[1] // tf
    final_norm = g_final is not None
    gf = (g_final if final_norm else g).reshape(1, d)
    return pl.pallas_call(
        functools.partial(_ffn_kernel, final_norm=final_norm),
        grid=(m // tm, n_tiles + 1),
        in_specs=[
            pl.BlockSpec((tm, d), lambda i, f: (i, 0), pipeline_mode=pl.Buffered(1)),
            pl.BlockSpec((1, d), lambda i, f: (0, 0)),
            pl.BlockSpec((d, tf), lambda i, f: (0, jnp.minimum(f, n_tiles - 1))),
            pl.BlockSpec((d, tf), lambda i, f: (0, jnp.minimum(f, n_tiles - 1))),
            pl.BlockSpec((tf, d), lambda i, f: (jnp.maximum(f - 1, 0), 0)),
            pl.BlockSpec((1, d), lambda i, f: (0, 0)),
        ],
        out_specs=pl.BlockSpec((tm, d), lambda i, f: (i, 0), pipeline_mode=pl.Buffered(1)),
        out_shape=jax.ShapeDtypeStruct((m, d), F32),
        scratch_shapes=[pltpu.VMEM((tm, d), BF16), pltpu.VMEM((2, tm, tf), BF16)],
        compiler_params=pltpu.CompilerParams(dimension_semantics=("parallel", "arbitrary"),
                                             vmem_limit_bytes=VMEM_LIMIT_BIG),
    )(x, g.reshape(1, d), wg, wu, wd, gf)


def _online_update(carry, s, v, shift=None):
    m, l, acc = carry
    smax = jnp.max(s, axis=-1, keepdims=True)
    if shift is not None:
        smax = smax - shift
    m_new = jnp.maximum(m, smax)
    alpha = jnp.exp2(m - m_new)
    p = jnp.exp2(s - (m_new if shift is None else m_new + shift))
    l = alpha * l + jnp.sum(p, axis=-1, keepdims=True)
    pb = p.astype(BF16)
    if isinstance(v, (list, tuple)):
        n = v[0].shape[0]
        pv = sum(_dot(pb[:, r * n:(r + 1) * n], x) for r, x in enumerate(v))
    else:
        pv = _dot(pb, v)
    return m_new, l, alpha * acc + pv


def _state_init(m_ref, l_ref, acc_ref):
    m_ref[...] = jnp.full(m_ref.shape, NEG, F32)
    l_ref[...] = jnp.zeros(l_ref.shape, F32)
    acc_ref[...] = jnp.zeros(acc_ref.shape, F32)


def _state_update(m_ref, l_ref, acc_ref, k, s, v, shift=None):
    m, l, acc = _online_update((m_ref[k], l_ref[k], acc_ref[k]), s, v, shift)
    m_ref[k], l_ref[k], acc_ref[k] = m, l, acc


def _fold_lanes(x, op):
    out = x[:, :LANES]
    for i in range(1, x.shape[1] // LANES):
        out = op(out, x[:, i * LANES:(i + 1) * LANES])
    return out


def _two_pass_attention(q, kb_ref, vb_ref, n_blocks, tk, s_ref, add_fn=None, tail_fn=None, shift=None):
    rows = q.shape[0]

    def scores(kb, masked):
        k0 = pl.multiple_of(kb * tk, tk)
        s = _dot_nt(q, kb_ref[pl.ds(k0, tk), :])
        if add_fn is not None:
            s = s + add_fn(k0)
        if masked and tail_fn is not None:
            s = s + tail_fn(k0)
        s_ref[kb] = s
        return _fold_lanes(s, jnp.maximum)

    mx = lax.fori_loop(0, n_blocks - 1, lambda kb, mx: jnp.maximum(mx, scores(kb, False)),
                       jnp.full((rows, LANES), NEG, F32))
    mx = jnp.maximum(mx, scores(n_blocks - 1, True))
    m = jnp.max(mx, axis=-1, keepdims=True)
    if shift is not None:
        m = (m - shift) + shift

    def accumulate(kb, carry):
        ls, acc = carry
        k0 = pl.multiple_of(kb * tk, tk)
        p = jnp.exp2(s_ref[kb] - m)
        return ls + _fold_lanes(p, jnp.add), acc + _dot(p.astype(BF16), vb_ref[pl.ds(k0, tk), :])

    ls, acc = lax.fori_loop(0, n_blocks, accumulate,
                            (jnp.zeros((rows, LANES), F32), jnp.zeros((rows, HEAD_DIM), F32)))
    return acc, jnp.sum(ls, axis=-1, keepdims=True)


def _masked_softmax(s, okf):
    ok = okf > 0.5
    sm = jnp.where(ok, s, NEG)
    m = jnp.max(sm, axis=-1, keepdims=True)
    e = jnp.where(ok, jnp.exp2(sm - m), 0.0)
    return e / jnp.maximum(jnp.sum(e, axis=-1, keepdims=True), 1e-30)


def _lane_pick(x, lane_idx):
    return jnp.sum(jnp.where(_iota(x.shape, 1) == lane_idx, x, 0.0), axis=-1, keepdims=True)


def _topk_mask(imp, n_sel):
    lane = _iota(imp.shape, 1).astype(F32)
    sel = jnp.zeros(imp.shape, F32)
    x = imp
    for _ in range(n_sel):
        m = jnp.max(x, axis=-1, keepdims=True)
        idx = jnp.min(jnp.where(x == m, lane, 1e9), axis=-1, keepdims=True)
        hit = lane == idx
        sel = jnp.where(hit, 1.0, sel)
        x = jnp.where(hit, LOWEST, x)
    return sel


def _overlap(ci, sj):
    ov = jnp.minimum(ci + CMP_LEN, sj + SLC_BLOCK) - jnp.maximum(ci, sj)
    return jnp.maximum(ov, 0).astype(F32) * (1.0 / CMP_LEN)


def _topk_mask_t(imp_t, n_sel, ns):
    rows = _round_up(ns, SUBLANES)
    x = imp_t[0:rows]
    j = _iota(x.shape, 0)
    rank = jnp.zeros(x.shape, F32)
    for jp in range(ns):
        a = x[jp:jp + 1, :]
        rank = rank + jnp.where((a > x) | ((a == x) & (jp < j)), 1.0, 0.0)
    sel = jnp.where((rank < n_sel) & (j < ns), 1.0, 0.0)
    return jnp.concatenate([sel, jnp.zeros((imp_t.shape[0] - rows, imp_t.shape[1]), F32)], axis=0)


def _block_importance(imp, qpos, ns, axis=1):
    j = _iota(imp.shape, axis)
    cur = qpos >> 6
    forced = (j == 0) | (j == cur) | (j == cur - 1)
    valid = (j << 6) <= qpos
    out = jnp.where(forced, BIG, jnp.where(valid, imp, NEG))
    return jnp.where(j < ns, out, LOWEST)


def _lam_value(lam_ref, lam_init):
    v = lam_ref[...]
    a = jnp.exp(jnp.sum(v[0:1] * v[1:2], axis=-1, keepdims=True))
    b = jnp.exp(jnp.sum(v[2:3] * v[3:4], axis=-1, keepdims=True))
    return a - b + lam_init


def _gelu_tanh(x):
    return 0.5 * x * (1.0 + jnp.tanh(0.7978845608028654 * (x + 0.044715 * (x * x * x))))


def _log_sigmoid(z):
    return jnp.minimum(z, 0.0) - jnp.log1p(jnp.exp(-jnp.abs(z)))


def _cast_kv_once(step, pairs):
    @pl.when(step == 0)
    def _():
        for src, dst in pairs:
            dst[...] = src[0].astype(BF16)


def _logf_prompt_kernel(z_ref, b_ref, lf_ref, dkc_ref, dkr_ref):
    t = z_ref.shape[1]
    lf = _log_sigmoid(z_ref[0] + b_ref[...])
    lf_ref[0] = lf
    upper = (_iota((LANES, LANES), 1) > _iota((LANES, LANES), 0)).astype(F32)
    carry = jnp.zeros((1, LANES), F32)
    for blk in reversed(range(t // LANES)):
        sl = slice(blk * LANES, (blk + 1) * LANES)
        x = lf[sl]
        d = _dot_exact(upper, x) + carry
        dkc_ref[0, sl, :] = d
        dkr_ref[0, :, sl] = d.T
        carry = carry + jnp.sum(x, axis=0, keepdims=True)


def _logf_prompt(z3, b_pad, col_blk):
    b, t, _ = z3.shape
    return pl.pallas_call(
        _logf_prompt_kernel,
        grid=(b,),
        in_specs=[pl.BlockSpec((1, t, LANES), lambda i: (i, 0, col_blk)),
                  pl.BlockSpec((1, LANES), lambda i: (0, 0))],
        out_specs=[pl.BlockSpec((1, t, LANES), lambda i: (i, 0, 0)),
                   pl.BlockSpec((1, t, LANES), lambda i: (i, 0, 0)),
                   pl.BlockSpec((1, LANES, t), lambda i: (i, 0, 0))],
        out_shape=[jax.ShapeDtypeStruct((b, t, LANES), F32),
                   jax.ShapeDtypeStruct((b, t, LANES), F32),
                   jax.ShapeDtypeStruct((b, LANES, t), F32)],
        compiler_params=_params("parallel"),
    )(z3, b_pad)


def _causal_tail(qpos, tk):
    return lambda k0: jnp.where(k0 + _iota((qpos.shape[0], tk), 1) <= qpos, 0.0, NEG)


def _diff_prompt_kernel(lam_ref, g_ref, q_ref, k_ref, v_ref, o_ref, kb_ref, vb_ref, s_ref, *, tq, gsz, lam_init, tk):
    qi = pl.program_id(2)
    _cast_kv_once(qi, ((k_ref, kb_ref), (v_ref, vb_ref)))
    lam = _lam_value(lam_ref, lam_init)
    lane = _iota((tq, HEAD_DIM), 1)
    qscale = DA ** -0.5 * LOG2E
    qs = [q_ref[0, :, g * HEAD_DIM:(g + 1) * HEAD_DIM] * qscale for g in range(gsz)]
    q = jnp.concatenate([jnp.where(lane < DA, x, 0.0) for x in qs] + [jnp.where(lane >= DA, x, 0.0) for x in qs],
                        axis=0).astype(BF16)
    qpos = qi * tq + (_iota((2 * gsz * tq, 1), 0) & (tq - 1))
    n_blocks = (qi * tq) // tk + 1
    acc, l = _two_pass_attention(q, kb_ref, vb_ref, n_blocks, tk, s_ref, tail_fn=_causal_tail(qpos, tk))
    o = acc / l
    half = gsz * tq
    o = o[:half] - lam * o[half:]
    o = (_rmsnorm_rows(o, g_ref[...]) * (1.0 - lam_init)).astype(BF16)
    for g in range(gsz):
        o_ref[0, :, g * HEAD_DIM:(g + 1) * HEAD_DIM] = o[g * tq:(g + 1) * tq]


def _diff_prompt(z3, lam_vec, subln_g, n_kv, gsz, q_blk0, k_blk0, v_blk0, lam_init, tq, tk):
    b, t, _ = z3.shape
    qw = gsz * HEAD_DIM
    assert tk % tq == 0 and tq & (tq - 1) == 0
    return pl.pallas_call(
        functools.partial(_diff_prompt_kernel, tq=tq, gsz=gsz, lam_init=lam_init, tk=tk),
        grid=(b, n_kv, t // tq),
        in_specs=[
            pl.BlockSpec((4, DA), lambda i, k, q: (0, 0)),
            pl.BlockSpec((1, HEAD_DIM), lambda i, k, q: (0, 0)),
            pl.BlockSpec((1, tq, qw), lambda i, k, q: (i, q, q_blk0 + k)),
            pl.BlockSpec((1, t, HEAD_DIM), lambda i, k, q: (i, 0, k_blk0 + k)),
            pl.BlockSpec((1, t, HEAD_DIM), lambda i, k, q: (i, 0, v_blk0 + k)),
        ],
        out_specs=pl.BlockSpec((1, tq, qw), lambda i, k, q: (i, q, k)),
        out_shape=jax.ShapeDtypeStruct((b, t, n_kv * qw), BF16),
        scratch_shapes=[pltpu.VMEM((t, HEAD_DIM), BF16), pltpu.VMEM((t, HEAD_DIM), BF16),
                        pltpu.VMEM((t // tk, 2 * gsz * tq, tk), F32)],
        compiler_params=_params("parallel", "parallel", "arbitrary"),
    )(lam_vec, subln_g.reshape(1, HEAD_DIM), z3, z3, z3)


def _fox_prompt_kernel(q_ref, k_ref, v_ref, dkr_ref, dkc_ref, o_ref, kb_ref, vb_ref, s_ref, *, tq, gsz, tk):
    kh = pl.program_id(1)
    qi = pl.program_id(2)
    _cast_kv_once(qi, ((k_ref, kb_ref), (v_ref, vb_ref)))
    qscale = HEAD_DIM ** -0.5 * LOG2E
    qpos = qi * tq + (_iota((gsz * tq, 1), 0) & (tq - 1))
    n_blocks = (qi * tq) // tk + 1
    q = jnp.concatenate([q_ref[0, :, g * HEAD_DIM:(g + 1) * HEAD_DIM] * qscale for g in range(gsz)],
                        axis=0).astype(BF16)
    dq = jnp.concatenate([_lane_pick(dkc_ref[0], kh * gsz + g) for g in range(gsz)], axis=0) * LOG2E

    def dk_fn(k0):
        return jnp.concatenate([jnp.broadcast_to(dkr_ref[0, g, :, pl.ds(k0, tk)] * LOG2E, (tq, tk))
                                for g in range(gsz)], axis=0)

    acc, l = _two_pass_attention(q, kb_ref, vb_ref, n_blocks, tk, s_ref, add_fn=dk_fn,
                                 tail_fn=_causal_tail(qpos, tk), shift=dq)
    o = (acc / l).astype(BF16)
    for g in range(gsz):
        o_ref[0, :, g * HEAD_DIM:(g + 1) * HEAD_DIM] = o[g * tq:(g + 1) * tq]


def _fox_prompt(z3, dkr4, dkc, n_kv, gsz, q_blk0, k_blk0, v_blk0, tq, tk):
    b, t, _ = z3.shape
    qw = gsz * HEAD_DIM
    assert tk % tq == 0
    return pl.pallas_call(
        functools.partial(_fox_prompt_kernel, tq=tq, gsz=gsz, tk=tk),
        grid=(b, n_kv, t // tq),
        in_specs=[
            pl.BlockSpec((1, tq, qw), lambda i, k, q: (i, q, q_blk0 + k)),
            pl.BlockSpec((1, t, HEAD_DIM), lambda i, k, q: (i, 0, k_blk0 + k)),
            pl.BlockSpec((1, t, HEAD_DIM), lambda i, k, q: (i, 0, v_blk0 + k)),
            pl.BlockSpec((1, gsz, 1, t), lambda i, k, q: (i, k, 0, 0)),
            pl.BlockSpec((1, tq, LANES), lambda i, k, q: (i, q, 0)),
        ],
        out_specs=pl.BlockSpec((1, tq, qw), lambda i, k, q: (i, q, k)),
        out_shape=jax.ShapeDtypeStruct((b, t, n_kv * qw), BF16),
        scratch_shapes=[pltpu.VMEM((t, HEAD_DIM), BF16), pltpu.VMEM((t, HEAD_DIM), BF16),
                        pltpu.VMEM((t // tk, gsz * tq, tk), F32)],
        compiler_params=_params("parallel", "parallel", "arbitrary"),
    )(z3, z3, z3, dkr4, dkc)


def _logf_sample_kernel(z_ref, b_ref, lf_ref, dq_ref, tot_ref, *, ts):
    m = z_ref.shape[0]
    lf = _log_sigmoid(z_ref[...] + b_ref[...])
    lf_ref[...] = lf
    i, j = _iota((m, m), 0), _iota((m, m), 1)
    same = (i // ts) == (j // ts)
    dq_ref[...] = _dot_exact((same & (j > i)).astype(F32), lf)
    tot_ref[...] = _dot_exact(same.astype(F32), lf)


def _logf_sample(z_fg, b_pad, ts):
    m = z_fg.shape[0]
    sds = jax.ShapeDtypeStruct((m, LANES), F32)
    return pl.pallas_call(
        functools.partial(_logf_sample_kernel, ts=ts),
        out_shape=[sds, sds, sds],
    )(z_fg, b_pad)


def _pad_rows_scratch(ref, x):
    ref[...] = jnp.zeros(ref.shape, ref.dtype)
    ref[0:x.shape[0], :] = x


def _page_specs(page, n_kv, n_pages, first_page):
    rows = page * 2 * n_kv
    return [pl.BlockSpec((rows, HEAD_DIM), lambda b, p, pt, r=r: (pt[b, first_page(b, p, pt) + r], 0))
            for r in range(n_pages)]


def _tile_view(cache):
    n_kv = cache.shape[3]
    assert 2 * n_kv in (SUBLANES, 2 * SUBLANES)
    return cache.reshape(-1, 2 * n_kv // SUBLANES, SUBLANES, HEAD_DIM)


def _tile_specs(page, tiles, n_pages, first_page):
    return [pl.BlockSpec((page, tiles, SUBLANES, HEAD_DIM),
                         lambda b, p, pt, r=r: (pt[b, first_page(b, p, pt) + r], 0, 0, 0))
            for r in range(n_pages)]


def _flat_kv(pg_refs):
    n = pg_refs[0].shape[0] * SUBLANES
    if pg_refs[0].shape[1] == 2:
        ks = [pg[:, 0].reshape(n, HEAD_DIM) for pg in pg_refs]
        vs = [pg[:, 1].reshape(n, HEAD_DIM) for pg in pg_refs]
        off = 0
    else:
        vs = [pg[:, 0].reshape(n, HEAD_DIM) for pg in pg_refs]
        ks = [pltpu.roll(v, SUBLANES // 2, axis=0) for v in vs]
        off = SUBLANES // 2
    return [k.astype(BF16) for k in ks], [v.astype(BF16) for v in vs], off


def _flat_scores(q, ks):
    return jnp.concatenate([_dot_nt(q, k) for k in ks], axis=1)


def _own_head_bias(row_head, n_cols):
    return jnp.where((_iota((row_head.shape[0], n_cols), 1) & (SUBLANES - 1)) == row_head, 0.0, NEG)


def _diff_sample_kernel(pt_ref, lam_ref, g_ref, q_ref, kn_ref, vn_ref, *refs, n_kv, gsz, ts, lam_init, n_pages):
    pg_refs = refs[:n_pages]
    o_ref, m_ref, l_ref, acc_ref, bias_ref, kpad_ref, vpad_ref = refs[n_pages:]
    p = pl.program_id(1)
    last = pl.num_programs(1) - 1
    r1 = n_kv * ts * gsz
    q = q_ref[0] * (DA ** -0.5 * LOG2E)
    lane = _iota((r1, HEAD_DIM), 1)
    qb = jnp.concatenate([jnp.where(lane < DA, q, 0.0), jnp.where(lane >= DA, q, 0.0)], axis=0).astype(BF16)
    rr = _iota((2 * r1, 1), 0) % r1
    row_head = rr // (ts * gsz)
    row_tok = (rr % (ts * gsz)) // gsz

    @pl.when(p == 0)
    def _():
        _state_init(m_ref, l_ref, acc_ref)

    @pl.when(p < last)
    def _():
        ks, vs, off = _flat_kv(pg_refs)

        @pl.when(p == 0)
        def _():
            bias_ref[...] = _own_head_bias(row_head + off, bias_ref.shape[1])

        _state_update(m_ref, l_ref, acc_ref, ..., _flat_scores(qb, ks) + bias_ref[...], vs)

    @pl.when(p == last)
    def _():
        _pad_rows_scratch(kpad_ref, kn_ref[0])
        _pad_rows_scratch(vpad_ref, vn_ref[0])
        s = _dot_nt(qb, kpad_ref[...].astype(BF16))
        c = _iota(s.shape, 1)
        ok = ((c % n_kv) == row_head) & ((c // n_kv) <= row_tok)
        _state_update(m_ref, l_ref, acc_ref, ..., jnp.where(ok, s, NEG), vpad_ref[...].astype(BF16))
        o = acc_ref[...] / l_ref[...]
        o = o[:r1] - _lam_value(lam_ref, lam_init) * o[r1:]
        o_ref[0] = (_rmsnorm_rows(o, g_ref[...]) * (1.0 - lam_init)).astype(BF16)


def _diff_sample(page_table, lam_vec, subln_g, q_rows, k_new, v_new, cache_tiles, page, n_kv, gsz, ts, lam_init, n_pages):
    bs, r1, _ = q_rows.shape
    nsteps = page_table.shape[1] // n_pages
    first = lambda b, p, pt: jnp.minimum(p, nsteps - 1) * n_pages
    row_spec = lambda n: pl.BlockSpec((1, n, HEAD_DIM), lambda b, p, pt: (b, 0, 0))
    assert ts * n_kv <= LANES
    grid_spec = pltpu.PrefetchScalarGridSpec(
        num_scalar_prefetch=1,
        grid=(bs, nsteps + 1),
        in_specs=[
            pl.BlockSpec((4, DA), lambda b, p, pt: (0, 0)),
            pl.BlockSpec((1, HEAD_DIM), lambda b, p, pt: (0, 0)),
            row_spec(r1), row_spec(ts * n_kv), row_spec(ts * n_kv),
        ] + _tile_specs(page, cache_tiles.shape[1], n_pages, first),
        out_specs=row_spec(r1),
        scratch_shapes=[pltpu.VMEM((2 * r1, 1), F32), pltpu.VMEM((2 * r1, 1), F32), pltpu.VMEM((2 * r1, HEAD_DIM), F32),
                        pltpu.VMEM((2 * r1, n_pages * page * SUBLANES), F32),
                        pltpu.VMEM((LANES, HEAD_DIM), F32), pltpu.VMEM((LANES, HEAD_DIM), F32)],
    )
    return pl.pallas_call(
        functools.partial(_diff_sample_kernel, n_kv=n_kv, gsz=gsz, ts=ts, lam_init=lam_init, n_pages=n_pages),
        grid_spec=grid_spec,
        out_shape=jax.ShapeDtypeStruct((bs, r1, HEAD_DIM), BF16),
        compiler_params=_params("parallel", "arbitrary"),
    )(page_table, lam_vec, subln_g.reshape(1, HEAD_DIM), q_rows, k_new, v_new, *([cache_tiles] * n_pages))


def _fox_sample_kernel(pt_ref, q_ref, kn_ref, vn_ref, dq_ref, dkn_ref, tot_ref, *refs, n_kv, gsz, ts, n_pages):
    pg_refs, lf_refs = refs[:n_pages], refs[n_pages:2 * n_pages]
    o_ref, m_ref, l_ref, acc_ref, car_ref, bias_ref, kpad_ref, vpad_ref = refs[2 * n_pages:]
    p = pl.program_id(1)
    page = lf_refs[0].shape[2]
    rows = n_kv * ts * gsz
    n_h = n_kv * gsz
    qb = (q_ref[0] * (HEAD_DIM ** -0.5 * LOG2E)).astype(BF16)
    rr = _iota((rows, 1), 0)
    row_head = rr // (ts * gsz)
    row_tok = (rr % (ts * gsz)) // gsz
    shift = dq_ref[0][:, 0:1] * LOG2E

    @pl.when(p == 0)
    def _():
        _state_init(m_ref, l_ref, acc_ref)
        car_ref[...] = tot_ref[0]
        _pad_rows_scratch(kpad_ref, kn_ref[0])
        _pad_rows_scratch(vpad_ref, vn_ref[0])
        s = _dot_nt(qb, kpad_ref[...].astype(BF16)) + dkn_ref[0] * LOG2E
        c = _iota(s.shape, 1)
        ok = ((c % n_kv) == row_head) & ((c // n_kv) <= row_tok)
        _state_update(m_ref, l_ref, acc_ref, ..., jnp.where(ok, s, NEG), vpad_ref[...].astype(BF16), shift=shift)

    @pl.when(p > 0)
    def _():
        ks, vs, off = _flat_kv(pg_refs)

        @pl.when(p == 1)
        def _():
            bias_ref[...] = _own_head_bias(row_head + off, bias_ref.shape[1])

        lower = (_iota((page, page), 0) > _iota((page, page), 1)).astype(F32)
        dks = [None] * n_pages
        for r in reversed(range(n_pages)):
            lft = lf_refs[r][0]
            dks[r] = _dot_exact(lft, lower) + car_ref[...]
            car_ref[...] = car_ref[...] + jnp.sum(lft, axis=1, keepdims=True)
        dk = jnp.concatenate(dks, axis=1)
        hi = dk.astype(BF16)
        rest = dk - hi.astype(F32)
        mid = rest.astype(BF16)
        terms = (hi, mid, (rest - mid.astype(F32)).astype(BF16))
        hh = _iota((rows, n_h), 0)
        pick = ((hh // (ts * gsz)) * gsz + hh % gsz == _iota((rows, n_h), 1)).astype(BF16)
        spread = ((_iota((page, page * SUBLANES), 1) >> 3) == _iota((page, page * SUBLANES), 0)).astype(BF16)
        on_rows = [_dot(pick, x).astype(BF16) for x in terms]
        pieces = jnp.concatenate([x[:, r * page:(r + 1) * page] for r in range(n_pages) for x in on_rows], axis=0)
        flat = _dot(pieces, spread)
        nt = len(terms)
        dkf = jnp.concatenate(
            [sum(flat[(r * nt + q) * rows:(r * nt + q + 1) * rows] for q in range(nt)) for r in range(n_pages)],
            axis=1) * LOG2E
        _state_update(m_ref, l_ref, acc_ref, ..., _flat_scores(qb, ks) + dkf + bias_ref[...], vs, shift=shift)

    @pl.when(p == pl.num_programs(1) - 1)
    def _():
        o_ref[0] = (acc_ref[...] / l_ref[...]).astype(BF16)


def _fox_sample(page_table, q_rows, k_new, v_new, dq_b, dkn_b, tot, cache_tiles, logf_t, page, n_kv, gsz, ts, n_pages):
    bs, rows, _ = q_rows.shape
    n_h = n_kv * gsz
    nsteps = page_table.shape[1] // n_pages
    first = lambda b, p, pt: (nsteps - jnp.maximum(p, 1)) * n_pages
    lf_specs = [pl.BlockSpec((1, n_h, page), lambda b, p, pt, r=r: (pt[b, first(b, p, pt) + r], 0, 0))
                for r in range(n_pages)]
    row_spec = lambda n: pl.BlockSpec((1, n, HEAD_DIM), lambda b, p, pt: (b, 0, 0))
    assert ts * n_kv <= LANES
    grid_spec = pltpu.PrefetchScalarGridSpec(
        num_scalar_prefetch=1,
        grid=(bs, nsteps + 1),
        in_specs=[
            row_spec(rows), row_spec(ts * n_kv), row_spec(ts * n_kv), row_spec(rows), row_spec(rows),
            pl.BlockSpec((1, n_h, 1), lambda b, p, pt: (b, 0, 0)),
        ] + _tile_specs(page, cache_tiles.shape[1], n_pages, first) + lf_specs,
        out_specs=row_spec(rows),
        scratch_shapes=[pltpu.VMEM((rows, 1), F32), pltpu.VMEM((rows, 1), F32), pltpu.VMEM((rows, HEAD_DIM), F32),
                        pltpu.VMEM((n_h, 1), F32), pltpu.VMEM((rows, n_pages * page * SUBLANES), F32),
                        pltpu.VMEM((LANES, HEAD_DIM), F32), pltpu.VMEM((LANES, HEAD_DIM), F32)],
    )
    return pl.pallas_call(
        functools.partial(_fox_sample_kernel, n_kv=n_kv, gsz=gsz, ts=ts, n_pages=n_pages),
        grid_spec=grid_spec,
        out_shape=jax.ShapeDtypeStruct((bs, rows, HEAD_DIM), BF16),
        compiler_params=_params("parallel", "arbitrary"),
    )(page_table, q_rows, k_new, v_new, dq_b, dkn_b, tot, *([cache_tiles] * n_pages), *([logf_t] * n_pages))


def _compress_mlp(acc_a, acc_b_next, w2):
    return _dot(_gelu_tanh(acc_a + acc_b_next).astype(BF16), w2).astype(BF16)


def _compress_prompt_kernel(z_ref, pe_ref, w1_ref, w2_ref, o_ref, *, nch):
    acc_a = jnp.zeros((nch, CMP_HID), F32)
    acc_b = jnp.zeros((nch, CMP_HID), F32)
    for l in range(0, CMP_STRIDE, 2):
        xs = [z_ref[0, pl.ds(l + i, nch, stride=CMP_STRIDE), :] for i in range(2)]
        for acc_is_b, off in ((False, l), (True, CMP_STRIDE + l)):
            lhs = jnp.concatenate([(xs[i] + pe_ref[0, off + i:off + i + 1, :]).astype(BF16) for i in range(2)], axis=1)
            term = _dot(lhs, w1_ref[0, off:off + 2].reshape(2 * HEAD_DIM, CMP_HID))
            if acc_is_b:
                acc_b = acc_b + term
            else:
                acc_a = acc_a + term
    o_ref[0, 0, 0] = _compress_mlp(acc_a, pltpu.roll(acc_b, nch - 1, axis=0), w2_ref[0])


def _compress_prompt(z3, pe, w1, w2, n_kv, col_blk0):
    b, t, _ = z3.shape
    nch = t // CMP_STRIDE
    return pl.pallas_call(
        functools.partial(_compress_prompt_kernel, nch=nch),
        grid=(b, 2 * n_kv),
        in_specs=[
            pl.BlockSpec((1, t, HEAD_DIM), lambda i, c: (i, 0, col_blk0 + c)),
            pl.BlockSpec((1, CMP_LEN, HEAD_DIM), lambda i, c: (c // n_kv, 0, 0)),
            pl.BlockSpec((1, CMP_LEN, HEAD_DIM, CMP_HID), lambda i, c: (c // n_kv, 0, 0, 0)),
            pl.BlockSpec((1, CMP_HID, HEAD_DIM), lambda i, c: (c // n_kv, 0, 0)),
        ],
        out_specs=pl.BlockSpec((1, 1, 1, nch, HEAD_DIM), lambda i, c: (i, c // n_kv, c % n_kv, 0, 0)),
        out_shape=jax.ShapeDtypeStruct((b, 2, n_kv, nch, HEAD_DIM), BF16),
        compiler_params=_params("parallel", "arbitrary"),
    )(z3, pe, w1, w2)


def _nsa_prompt_kernel(q_ref, cos_ref, sin_ref, kc_ref, vc_ref, ks_ref, vs_ref, kw_ref, vw_ref, gl_ref, bg_ref,
                       o_ref, ksb_ref, vsb_ref, kwb_ref, vwb_ref, s_ref, *, tq, gsz, n_kv, nblk, ns, tk):
    kh = pl.program_id(1)
    qi = pl.program_id(2)
    _cast_kv_once(qi, ((ks_ref, ksb_ref), (vs_ref, vsb_ref), (kw_ref, kwb_ref), (vw_ref, vwb_ref)))
    t0 = qi * tq
    qscale = HEAD_DIM ** -0.5 * LOG2E
    cos, sin = cos_ref[...], sin_ref[...]
    qs = [q_ref[0, :, g * HEAD_DIM:(g + 1) * HEAD_DIM] * qscale for g in range(gsz)]
    q = jnp.concatenate(qs, axis=0).astype(BF16)
    qr = jnp.concatenate([_rope_lanes(x, cos, sin, HEAD_DIM // 2) for x in qs], axis=0).astype(BF16)
    qpos = t0 + _iota((tq, 1), 0)

    def tile_g(x):
        return jnp.concatenate([x] * gsz, axis=0)

    nch = kc_ref.shape[3]
    n_i = _iota((tq, nch), 1)
    ok_c = ((n_i * CMP_STRIDE + CMP_LEN - 1 <= qpos) & (n_i < nblk)).astype(F32)
    p = _masked_softmax(_dot_nt(q, kc_ref[0, 0, 0]), tile_g(ok_c))
    o_cmp = _dot(p.astype(BF16), vc_ref[0, 0, 0])
    psum = p[0:tq]
    for g in range(1, gsz):
        psum = psum + p[g * tq:(g + 1) * tq]
    ov_t = _overlap(_iota((LANES, nch), 1) * CMP_STRIDE, _iota((LANES, nch), 0) * SLC_BLOCK)
    imp_t = _block_importance(_dot_nt_exact(ov_t, psum), t0 + _iota((1, tq), 1), ns, axis=0)
    sel = _topk_mask_t(imp_t, min(N_SELECT, ns), ns).T.astype(BF16)

    def slc_mask(k0):
        expand = (((k0 + _iota((LANES, tk), 1)) >> 6) == _iota((LANES, tk), 0)).astype(BF16)
        selk = _dot(sel, expand)
        return tile_g(jnp.where((selk > 0.5) & (k0 + _iota((tq, tk), 1) <= qpos), 0.0, NEG))

    n_blocks = (t0 + tq + tk - 1) // tk
    acc, l = _two_pass_attention(qr, ksb_ref, vsb_ref, n_blocks, tk, s_ref, add_fn=slc_mask)
    o_slc = acc / l

    wlen = WINDOW + tq
    w0 = pl.multiple_of(jnp.maximum(t0 - WINDOW, 0), tq)
    dist = qpos - (w0 + _iota((tq, wlen), 1))
    sw = _dot_nt(qr, kwb_ref[pl.ds(w0, wlen), :]) + tile_g(jnp.where((dist >= 0) & (dist < WINDOW), 0.0, NEG))
    ew = jnp.exp2(sw - jnp.max(sw, axis=-1, keepdims=True))
    o_win = _dot(ew.astype(BF16), vwb_ref[pl.ds(w0, wlen), :]) / jnp.sum(ew, axis=-1, keepdims=True)

    gates = jax.nn.sigmoid(gl_ref[0] + bg_ref[...])
    n_h = n_kv * gsz
    for g in range(gsz):
        rows = slice(g * tq, (g + 1) * tq)
        col = kh * gsz + g
        o = (_lane_pick(gates, col) * o_cmp[rows] + _lane_pick(gates, n_h + col) * o_slc[rows]
             + _lane_pick(gates, 2 * n_h + col) * o_win[rows])
        o_ref[0, :, g * HEAD_DIM:(g + 1) * HEAD_DIM] = o.astype(BF16)


def _nsa_prompt(z3, cos, sin, kvc, bg_pad, n_kv, gsz, ks_blk, vs_blk, kw_blk, vw_blk, gl_blk, tq, tk):
    b, t, _ = z3.shape
    nch = kvc.shape[3]
    qw = gsz * HEAD_DIM
    ns = -(-t // SLC_BLOCK)
    assert t >= WINDOW + tq and t % tk == 0 and tk % SLC_BLOCK == 0 and ns <= LANES
    kv_spec = lambda blk0: pl.BlockSpec((1, t, HEAD_DIM), lambda i, k, q: (i, 0, blk0 + k))
    return pl.pallas_call(
        functools.partial(_nsa_prompt_kernel, tq=tq, gsz=gsz, n_kv=n_kv, nblk=nch - 1, ns=ns, tk=tk),
        grid=(b, n_kv, t // tq),
        in_specs=[
            pl.BlockSpec((1, tq, qw), lambda i, k, q: (i, q, k)),
            pl.BlockSpec((tq, LANES), lambda i, k, q: (q, 0)),
            pl.BlockSpec((tq, LANES), lambda i, k, q: (q, 0)),
            pl.BlockSpec((1, 1, 1, nch, HEAD_DIM), lambda i, k, q: (i, 0, k, 0, 0)),
            pl.BlockSpec((1, 1, 1, nch, HEAD_DIM), lambda i, k, q: (i, 1, k, 0, 0)),
            kv_spec(ks_blk), kv_spec(vs_blk), kv_spec(kw_blk), kv_spec(vw_blk),
            pl.BlockSpec((1, tq, LANES), lambda i, k, q: (i, q, gl_blk)),
            pl.BlockSpec((1, LANES), lambda i, k, q: (0, 0)),
        ],
        out_specs=pl.BlockSpec((1, tq, qw), lambda i, k, q: (i, q, k)),
        out_shape=jax.ShapeDtypeStruct((b, t, n_kv * qw), BF16),
        scratch_shapes=[pltpu.VMEM((t, HEAD_DIM), BF16) for _ in range(4)] + [
            pltpu.VMEM((t // tk, gsz * tq, tk), F32)],
        compiler_params=_params("parallel", "parallel", "arbitrary"),
    )(z3, cos, sin, kvc, kvc, z3, z3, z3, z3, z3, bg_pad)


def _compress_sample_kernel(pt_ref, *refs, n_kv, n_pages, page):
    pg_refs = refs[:n_pages]
    pe_ref, w1_ref, w2_ref, o_ref, last_ref = refs[n_pages:]
    s = pl.program_id(1)
    cpp = page // CMP_STRIDE
    m = n_pages * cpp
    rpr = 2 * n_kv

    @pl.when(s == 0)
    def _():
        last_ref[...] = jnp.zeros(last_ref.shape, F32)

    for typ in range(2):
        acc_a = jnp.zeros((n_kv * m, CMP_HID), F32)
        acc_b = jnp.zeros((n_kv * m, CMP_HID), F32)
        for l in range(0, CMP_STRIDE, 2):
            xs = [jnp.concatenate(
                [pg[pl.ds((l + i) * rpr + typ * n_kv + h, cpp, stride=CMP_STRIDE * rpr), :]
                 for h in range(n_kv) for pg in pg_refs], axis=0) for i in range(2)]
            for acc_is_b, off in ((False, l), (True, CMP_STRIDE + l)):
                lhs = jnp.concatenate([(xs[i] + pe_ref[typ, off + i:off + i + 1, :]).astype(BF16)
                                       for i in range(2)], axis=1)
                term = _dot(lhs, w1_ref[typ, off:off + 2].reshape(2 * HEAD_DIM, CMP_HID))
                if acc_is_b:
                    acc_b = acc_b + term
                else:
                    acc_a = acc_a + term
        first = _iota((m, 1), 0) == 0
        for h in range(n_kv):
            a = acc_a[h * m:(h + 1) * m]
            idx = typ * n_kv + h
            prev = jnp.where(first, last_ref[idx, SUBLANES - 1:SUBLANES, :], pltpu.roll(a, 1, axis=0))
            last_ref[idx] = a[m - SUBLANES:m]
            o_ref[0, typ, h] = _compress_mlp(prev, acc_b[h * m:(h + 1) * m], w2_ref[typ])


def _compress_sample(page_table, cache_rows, pe, w1, w2, page, n_kv, n_pages):
    bs, npg = page_table.shape
    m = n_pages * (page // CMP_STRIDE)
    grid_spec = pltpu.PrefetchScalarGridSpec(
        num_scalar_prefetch=1,
        grid=(bs, npg // n_pages),
        in_specs=_page_specs(page, n_kv, n_pages, lambda b, s, pt: s * n_pages) + [
            pl.BlockSpec((2, CMP_LEN, HEAD_DIM), lambda b, s, pt: (0, 0, 0)),
            pl.BlockSpec((2, CMP_LEN, HEAD_DIM, CMP_HID), lambda b, s, pt: (0, 0, 0, 0)),
            pl.BlockSpec((2, CMP_HID, HEAD_DIM), lambda b, s, pt: (0, 0, 0)),
        ],
        out_specs=pl.BlockSpec((1, 2, n_kv, m, HEAD_DIM), lambda b, s, pt: (b, 0, 0, s, 0)),
        scratch_shapes=[pltpu.VMEM((2 * n_kv, SUBLANES, CMP_HID), F32)],
    )
    return pl.pallas_call(
        functools.partial(_compress_sample_kernel, n_kv=n_kv, n_pages=n_pages, page=page),
        grid_spec=grid_spec,
        out_shape=jax.ShapeDtypeStruct((bs, 2, n_kv, npg * (page // CMP_STRIDE), HEAD_DIM), BF16),
        compiler_params=_params("parallel", "arbitrary"),
    )(page_table, *([cache_rows] * n_pages), pe, w1, w2)


def _nsa_sample_a_kernel(q_ref, cos_ref, sin_ref, kc_ref, vc_ref, win_ref, kwn_ref,
                         ocmp_ref, owin_ref, sel_ref, qr_ref, kpad_ref, vpad_ref, *, n_kv, gsz, ts, p0, ns):
    qscale = HEAD_DIM ** -0.5 * LOG2E
    rows = ts * gsz
    nr = kc_ref.shape[3]
    nsp = sel_ref.shape[3]
    wrows = win_ref.shape[1] // (2 * n_kv)
    tok = _iota((rows, 1), 0) // gsz
    qpos = p0 + tok
    r_i = _iota((rows, nr), 1)
    ok_c = ((r_i >= 1) & ((r_i - 1) * CMP_STRIDE + CMP_LEN - 1 <= qpos)).astype(F32)
    ov = _overlap((_iota((nr, nsp), 0) - 1) * CMP_STRIDE, _iota((nr, nsp), 1) * SLC_BLOCK)
    group = ((_iota((rows, rows), 0) // gsz) == (_iota((rows, rows), 1) // gsz)).astype(F32)
    ok_w1 = _iota((rows, wrows), 1) > tok + (wrows - WINDOW)
    ok_w2 = _iota((rows, kpad_ref.shape[0]), 1) <= tok
    kvw = n_kv * HEAD_DIM
    for kh in range(n_kv):
        q = q_ref[0, kh]
        qr = _rope_lanes(q, cos_ref[...], sin_ref[...], HEAD_DIM // 2)
        qr_ref[0, kh] = qr
        qb, qrb = (q * qscale).astype(BF16), (qr * qscale).astype(BF16)
        p = _masked_softmax(_dot_nt(qb, kc_ref[0, 0, kh]), ok_c)
        ocmp_ref[0, kh] = _dot(p.astype(BF16), vc_ref[0, 0, kh])
        imp = _dot_exact(group, _dot_exact(p, ov))
        sel_ref[0, kh] = _topk_mask(_block_importance(imp, qpos, ns), min(N_SELECT, ns))
        sl = slice(kh * HEAD_DIM, (kh + 1) * HEAD_DIM)
        slv = slice(kvw + kh * HEAD_DIM, kvw + (kh + 1) * HEAD_DIM)
        _pad_rows_scratch(kpad_ref, kwn_ref[0, :, sl])
        _pad_rows_scratch(vpad_ref, kwn_ref[0, :, slv])
        k_win = win_ref[0, pl.ds(kh, wrows, stride=2 * n_kv), :].astype(BF16)
        v_win = win_ref[0, pl.ds(n_kv + kh, wrows, stride=2 * n_kv), :].astype(BF16)
        s1 = jnp.where(ok_w1, _dot_nt(qrb, k_win), NEG)
        s2 = jnp.where(ok_w2, _dot_nt(qrb, kpad_ref[...].astype(BF16)), NEG)
        m = jnp.maximum(jnp.max(s1, axis=-1, keepdims=True), jnp.max(s2, axis=-1, keepdims=True))
        e1, e2 = jnp.exp2(s1 - m), jnp.exp2(s2 - m)
        l = jnp.sum(e1, axis=-1, keepdims=True) + jnp.sum(e2, axis=-1, keepdims=True)
        o = _dot(e1.astype(BF16), v_win) + _dot(e2.astype(BF16), vpad_ref[...].astype(BF16))
        owin_ref[0, kh] = o / l


def _nsa_sample_a(q_rows, cos, sin, kvc, win_rows, kvw_new, n_kv, gsz, ts, p0, ns):
    bs = q_rows.shape[0]
    rows = ts * gsz
    nr = kvc.shape[3]
    nsp = _round_up(ns, LANES)
    wr, w = win_rows.shape[1], kvw_new.shape[2]
    assert wr == WINDOW * 2 * n_kv
    o_sds = jax.ShapeDtypeStruct((bs, n_kv, rows, HEAD_DIM), F32)
    o_spec = pl.BlockSpec((1, n_kv, rows, HEAD_DIM), lambda b: (b, 0, 0, 0))
    return pl.pallas_call(
        functools.partial(_nsa_sample_a_kernel, n_kv=n_kv, gsz=gsz, ts=ts, p0=p0, ns=ns),
        grid=(bs,),
        in_specs=[
            o_spec,
            pl.BlockSpec((rows, LANES), lambda b: (0, 0)),
            pl.BlockSpec((rows, LANES), lambda b: (0, 0)),
            pl.BlockSpec((1, 1, n_kv, nr, HEAD_DIM), lambda b: (b, 0, 0, 0, 0)),
            pl.BlockSpec((1, 1, n_kv, nr, HEAD_DIM), lambda b: (b, 1, 0, 0, 0)),
            pl.BlockSpec((1, wr, HEAD_DIM), lambda b: (b, 0, 0)),
            pl.BlockSpec((1, ts, w), lambda b: (b, 0, 0)),
        ],
        out_specs=[o_spec, o_spec, pl.BlockSpec((1, n_kv, rows, nsp), lambda b: (b, 0, 0, 0)), o_spec],
        out_shape=[o_sds, o_sds, jax.ShapeDtypeStruct((bs, n_kv, rows, nsp), F32), o_sds],
        scratch_shapes=[pltpu.VMEM((LANES, HEAD_DIM), F32), pltpu.VMEM((LANES, HEAD_DIM), F32)],
        compiler_params=_params("parallel"),
    )(q_rows, cos, sin, kvc, kvc, win_rows, kvw_new)


def _nsa_sample_b_kernel(pt_ref, q_ref, sel_ref, kn_ref, ocmp_ref, owin_ref, gl_ref, bg_ref, *refs,
                         n_kv, gsz, ts, n_pages):
    pg_refs = refs[:n_pages]
    o_ref, m_ref, l_ref, acc_ref, bias_ref, npad_ref = refs[n_pages:]
    p = pl.program_id(1)
    last = pl.num_programs(1) - 1
    rows = n_kv * ts * gsz
    page = pg_refs[0].shape[0]
    nsp = sel_ref.shape[2]
    bps = n_pages * (page // SLC_BLOCK)
    blk_cols = SLC_BLOCK * SUBLANES
    qb = (q_ref[0] * (HEAD_DIM ** -0.5 * LOG2E)).astype(BF16)
    rr = _iota((rows, 1), 0)
    row_head = rr // (ts * gsz)
    row_tok = (rr % (ts * gsz)) // gsz

    @pl.when(p == 0)
    def _():
        _state_init(m_ref, l_ref, acc_ref)

    @pl.when(p < last)
    def _():
        ks, vs, off = _flat_kv(pg_refs)

        @pl.when(p == 0)
        def _():
            bias_ref[...] = _own_head_bias(row_head + off, bias_ref.shape[1])

        shift = (_iota((nsp, LANES), 0) == p * bps + _iota((nsp, LANES), 1)).astype(BF16)
        flags = jnp.where(_dot(sel_ref[0].astype(BF16), shift) > 0.5, 0.0, NEG)
        sel_bias = jnp.concatenate([jnp.broadcast_to(flags[:, i:i + 1], (rows, blk_cols)) for i in range(bps)], axis=1)
        _state_update(m_ref, l_ref, acc_ref, ..., _flat_scores(qb, ks) + bias_ref[...] + sel_bias, vs)

    @pl.when(p == last)
    def _():
        _pad_rows_scratch(npad_ref, kn_ref[0])
        flat = npad_ref[...]
        s = _dot_nt(qb, pltpu.roll(flat, n_kv, axis=0).astype(BF16))
        c = _iota(s.shape, 1)
        ok = (((c % (2 * n_kv)) == n_kv + row_head) & ((c // (2 * n_kv)) <= row_tok)
              & (_lane_pick(sel_ref[0], last * bps) > 0.5))
        _state_update(m_ref, l_ref, acc_ref, ..., jnp.where(ok, s, NEG), flat.astype(BF16))
        gates = jax.nn.sigmoid(gl_ref[0] + bg_ref[...])
        o_ref[0] = (gates[:, 0:1] * ocmp_ref[0] + gates[:, 1:2] * (acc_ref[...] / l_ref[...])
                    + gates[:, 2:3] * owin_ref[0])


def _nsa_sample_b(page_table, q_rows, sel, kvs_new, o_cmp, o_win, gl_rows, bg_rows, cache_tiles,
                  page, n_kv, gsz, ts, n_pages):
    bs, rows, _ = q_rows.shape
    nsp = sel.shape[2]
    nsteps = page_table.shape[1] // n_pages
    first = lambda b, p, pt: jnp.minimum(p, nsteps - 1) * n_pages
    row_spec = lambda n, width: pl.BlockSpec((1, n, width), lambda b, p, pt: (b, 0, 0))
    assert ts * 2 * n_kv <= LANES and n_pages * (page // SLC_BLOCK) <= LANES
    grid_spec = pltpu.PrefetchScalarGridSpec(
        num_scalar_prefetch=1,
        grid=(bs, nsteps + 1),
        in_specs=[
            row_spec(rows, HEAD_DIM), row_spec(rows, nsp), row_spec(ts * 2 * n_kv, HEAD_DIM),
            row_spec(rows, HEAD_DIM), row_spec(rows, HEAD_DIM), row_spec(rows, LANES),
            pl.BlockSpec((rows, LANES), lambda b, p, pt: (0, 0)),
        ] + _tile_specs(page, cache_tiles.shape[1], n_pages, first),
        out_specs=row_spec(rows, HEAD_DIM),
        scratch_shapes=[pltpu.VMEM((rows, 1), F32), pltpu.VMEM((rows, 1), F32), pltpu.VMEM((rows, HEAD_DIM), F32),
                        pltpu.VMEM((rows, n_pages * page * SUBLANES), F32), pltpu.VMEM((LANES, HEAD_DIM), F32)],
    )
    return pl.pallas_call(
        functools.partial(_nsa_sample_b_kernel, n_kv=n_kv, gsz=gsz, ts=ts, n_pages=n_pages),
        grid_spec=grid_spec,
        out_shape=jax.ShapeDtypeStruct((bs, rows, HEAD_DIM), F32),
        compiler_params=_params("parallel", "arbitrary"),
    )(page_table, q_rows, sel, kvs_new, o_cmp, o_win, gl_rows, bg_rows, *([cache_tiles] * n_pages))


def _rope_tables(pos, half):
    inv = ROPE_THETA ** (-jnp.arange(half, dtype=F32) / half)
    ang = pos.astype(F32)[:, None] * inv
    cos, sin = jnp.cos(ang), jnp.sin(ang)
    reps = LANES // (2 * half)
    return (jnp.tile(jnp.concatenate([cos, cos], axis=-1), (1, reps)),
            jnp.tile(jnp.concatenate([-sin, sin], axis=-1), (1, reps)))


def _pad_cols(w, n):
    return jnp.concatenate([w.astype(BF16), jnp.zeros((w.shape[0], n - w.shape[1]), BF16)], axis=1)


def _head_rows(x, n_kv, gsz):
    b, t, _ = x.shape
    return x.reshape(b, t, n_kv, gsz, -1).transpose(0, 2, 1, 3, 4).reshape(b, n_kv, t * gsz, -1)


def _token_cols(x, ts, gsz):
    b, n_kv, _, f = x.shape
    return x.reshape(b, n_kv, ts, gsz, f).transpose(0, 2, 1, 3, 4).reshape(b * ts, n_kv * gsz * f)


def kernel(x_prompt, x_sample, cache_a_kv, cache_b_kv, cache_b_logf, cache_cmp_kv, cache_slc_kv, state_win_kv, page_table, norm_mix_e, w_in_e, b_forget, lam_q1, lam_k1, lam_q2, lam_k2, subln_g, w_out_e, norm_mix_o, w_in_o, b_gate, cmp_pe_k, cmp_w1_k, cmp_w2_k, cmp_pe_v, cmp_w1_v, cmp_w2_v, w_out_o, norm_ffn, w_gate, w_up, w_down, norm_final):
    bp, t, d = x_prompt.shape
    bs, ts, _ = x_sample.shape
    n_even, n_odd = w_in_e.shape[0], w_in_o.shape[0]
    depth = n_even + n_odd
    page = cache_a_kv.shape[2]
    npg = page_table.shape[1]
    p0 = npg * page
    h_all = d // HEAD_DIM
    h_b = cache_b_logf.shape[-1]
    h_a = h_all - h_b
    kv_a, kv_b, kv_c = cache_a_kv.shape[4], cache_b_kv.shape[4], cache_cmp_kv.shape[4]
    g_a, g_b, g_c = h_a // kv_a, h_b // kv_b, h_all // kv_c
    mp, ms = bp * t, bs * ts
    hd = HEAD_DIM

    e_qa, e_ka = 0, h_a * hd
    e_va = e_ka + kv_a * hd
    e_qf = e_va + kv_a * hd
    e_kf = e_qf + h_b * hd
    e_vf = e_kf + kv_b * hd
    e_fg = e_vf + kv_b * hd
    o_kvc = h_all * hd
    o_kvs = o_kvc + 2 * kv_c * hd
    o_kvw = o_kvs + 2 * kv_c * hd
    o_gl = o_kvw + 2 * kv_c * hd
    assert 3 * h_all <= LANES and h_b <= LANES

    tn_e = _pick_tile(e_va, (512, 256, 128))
    tn_o = _pick_tile(kv_c * hd, (512, 256, 128))
    ne_pad = _round_up(e_fg + h_b, tn_e)
    no_pad = _round_up(o_gl + 3 * h_all, tn_o)
    tm_p = _pick_tile(mp, (1024, 512, 256, 128))
    tm_f = _pick_tile(mp, (1024, 512, 256, 128))
    tn_out = _pick_tile(d, (1024, 512, 256, 128))
    tf = _pick_tile(w_gate.shape[2], (256, 128))
    tq_e = _pick_tile(t, (512, 256, 128))
    tk_e = _pick_tile(t, (512, 256))
    tq_o = _pick_tile(t, (256, 128))
    tk_o = _pick_tile(t, (512, 256, 128))
    n_pages = _pick_tile(npg, (8, 4, 2))
    n_pages_c = _pick_tile(npg, (16, 8, 4, 2))

    xp = x_prompt.reshape(mp, d)
    xs = x_sample.reshape(ms, d)
    pos_p = jnp.tile(jnp.arange(t, dtype=jnp.int32), bp)
    pos_s = p0 + jnp.tile(jnp.arange(ts, dtype=jnp.int32), bs)
    tab_a_p, tab_a_s = _rope_tables(pos_p, DA // 2), _rope_tables(pos_s, DA // 2)
    tab_c_p, tab_c_s = _rope_tables(pos_p, hd // 2), _rope_tables(pos_s, hd // 2)
    tab_c_q = _rope_tables(jnp.arange(t, dtype=jnp.int32), hd // 2)
    tab_c_rows = _rope_tables(p0 + jnp.repeat(jnp.arange(ts, dtype=jnp.int32), g_c), hd // 2)

    outs = {name: ([], []) for name in ("a", "b", "f", "c", "s", "w")}

    for layer in range(depth):
        i = layer // 2
        if layer % 2 == 0:
            lam_init = 0.8 - 0.6 * float(np.exp(-0.3 * layer))
            w_in = _pad_cols(w_in_e[i], ne_pad)
            w_out = w_out_e[i].astype(BF16)
            rope_e = ((0, e_va // tn_e),)
            lam_vec = jnp.stack([lam_q1[i], lam_k1[i], lam_q2[i], lam_k2[i]])
            bf_pad = jnp.pad(b_forget[i], (0, LANES - h_b)).reshape(1, LANES)

            z = _proj(xp, norm_mix_e[i], w_in, *tab_a_p, rope_e, DA // 2, tm_p, tn_e)
            z3 = z.reshape(bp, t, ne_pad)
            outs["a"][0].append(z3[:, :, e_ka:e_qf].reshape(bp, t, 2, kv_a, hd))
            outs["b"][0].append(z3[:, :, e_kf:e_fg].reshape(bp, t, 2, kv_b, hd))
            lf, dkc, dkr = _logf_prompt(z3, bf_pad, e_fg // LANES)
            outs["f"][0].append(lf[:, :, :h_b])
            o_a = _diff_prompt(z3, lam_vec, subln_g[i], kv_a, g_a, e_qa // (g_a * hd), e_ka // hd, e_va // hd,
                               lam_init, tq_e, tk_e)
            o_b = _fox_prompt(z3, dkr[:, :h_b].reshape(bp, h_b, 1, t), dkc, kv_b, g_b,
                              e_qf // (g_b * hd), e_kf // hd, e_vf // hd, tq_e, tk_e)
            xp = _mm_res([o_a.reshape(mp, h_a * hd), o_b.reshape(mp, h_b * hd)], w_out, xp, tm_p, tn_out)

            z = _proj(xs, norm_mix_e[i], w_in, *tab_a_s, rope_e, DA // 2, ms, tn_e)
            z3 = z.reshape(bs, ts, ne_pad)
            outs["a"][1].append(z3[:, :, e_ka:e_qf].reshape(bs, ts, 2, kv_a, hd))
            outs["b"][1].append(z3[:, :, e_kf:e_fg].reshape(bs, ts, 2, kv_b, hd))
            lf, dq, tot = _logf_sample(z[:, e_fg:e_fg + LANES], bf_pad, ts)
            outs["f"][1].append(lf[:, :h_b].reshape(bs, ts, h_b))
            o_a = _diff_sample(page_table, lam_vec, subln_g[i],
                               _head_rows(z3[:, :, e_qa:e_ka], kv_a, g_a).reshape(bs, -1, hd),
                               z3[:, :, e_ka:e_va].reshape(bs, ts * kv_a, hd),
                               z3[:, :, e_va:e_qf].reshape(bs, ts * kv_a, hd),
                               _tile_view(cache_a_kv[i]), page, kv_a, g_a, ts, lam_init, n_pages)
            rows_b = kv_b * ts * g_b
            dq3 = dq[:, :h_b].reshape(bs, ts, h_b)
            dq_b = jnp.broadcast_to(_head_rows(dq3, kv_b, g_b).reshape(bs, rows_b, 1), (bs, rows_b, LANES))
            dkn = jnp.swapaxes(dq3, 1, 2).reshape(bs, kv_b, 1, g_b, ts, 1)
            dkn = jnp.broadcast_to(dkn, (bs, kv_b, ts, g_b, ts, kv_b)).reshape(bs, rows_b, ts * kv_b)
            dkn_b = jnp.pad(dkn, ((0, 0), (0, 0), (0, LANES - ts * kv_b)))
            tot3 = tot[:, :h_b].reshape(bs, ts, h_b)[:, 0, :, None]
            o_b = _fox_sample(page_table, _head_rows(z3[:, :, e_qf:e_kf], kv_b, g_b).reshape(bs, -1, hd),
                              z3[:, :, e_kf:e_vf].reshape(bs, ts * kv_b, hd),
                              z3[:, :, e_vf:e_fg].reshape(bs, ts * kv_b, hd), dq_b, dkn_b, tot3,
                              _tile_view(cache_b_kv[i]), jnp.swapaxes(cache_b_logf[i], 1, 2),
                              page, kv_b, g_b, ts, n_pages)
            xs = _mm_res([_token_cols(o_a.reshape(bs, kv_a, ts * g_a, hd), ts, g_a),
                          _token_cols(o_b.reshape(bs, kv_b, ts * g_b, hd), ts, g_b)], w_out, xs, ms, tn_out)
        else:
            w_in = _pad_cols(w_in_o[i], no_pad)
            w_out = w_out_o[i].astype(BF16)
            rope_o = ((o_kvs // tn_o, (o_kvs + kv_c * hd) // tn_o), (o_kvw // tn_o, (o_kvw + kv_c * hd) // tn_o))
            bg_pad = jnp.pad(b_gate[i], (0, LANES - 3 * h_all)).reshape(1, LANES)
            pe = jnp.stack([cmp_pe_k[i], cmp_pe_v[i]])
            w1 = jnp.stack([cmp_w1_k[i], cmp_w1_v[i]]).astype(BF16)
            w2 = jnp.stack([cmp_w2_k[i], cmp_w2_v[i]]).astype(BF16)

            z = _proj(xp, norm_mix_o[i], w_in, *tab_c_p, rope_o, hd // 2, tm_p, tn_o)
            z3 = z.reshape(bp, t, no_pad)
            outs["c"][0].append(z3[:, :, o_kvc:o_kvs].reshape(bp, t, 2, kv_c, hd))
            outs["s"][0].append(z3[:, :, o_kvs:o_kvw].reshape(bp, t, 2, kv_c, hd))
            keep = min(WINDOW, t)
            outs["w"][0].append(z3[:, t - keep:, o_kvw:o_gl].reshape(bp, keep, 2, kv_c, hd))
            kvc = _compress_prompt(z3, pe, w1, w2, kv_c, o_kvc // hd)
            o = _nsa_prompt(z3, *tab_c_q, kvc, bg_pad, kv_c, g_c, o_kvs // hd, o_kvs // hd + kv_c,
                            o_kvw // hd, o_kvw // hd + kv_c, o_gl // LANES, tq_o, tk_o)
            xp = _mm_res([o.reshape(mp, d)], w_out, xp, tm_p, tn_out)

            z = _proj(xs, norm_mix_o[i], w_in, *tab_c_s, rope_o, hd // 2, ms, tn_o)
            z3 = z.reshape(bs, ts, no_pad)
            outs["c"][1].append(z3[:, :, o_kvc:o_kvs].reshape(bs, ts, 2, kv_c, hd))
            outs["s"][1].append(z3[:, :, o_kvs:o_kvw].reshape(bs, ts, 2, kv_c, hd))
            win = state_win_kv[i]
            kvw_new = z3[:, :, o_kvw:o_gl]
            win_all = jnp.concatenate([win, kvw_new.reshape(bs, ts, 2, kv_c, hd)], axis=1)
            outs["w"][1].append(win_all[:, -min(WINDOW, win.shape[1] + ts):])
            kvc = _compress_sample(page_table, cache_cmp_kv[i].reshape(-1, hd), pe, w1, w2, page, kv_c, min(16, npg))
            ns = -(-(p0 + ts) // SLC_BLOCK)
            o_cmp, o_win, sel, qr = _nsa_sample_a(_head_rows(z3[:, :, :o_kvc], kv_c, g_c), *tab_c_rows, kvc,
                                                  win.reshape(bs, -1, hd), kvw_new, kv_c, g_c, ts, p0, ns)
            rows = kv_c * ts * g_c
            gl = z3[:, :, o_gl:o_gl + 3 * h_all].reshape(bs, ts, 3, kv_c, g_c).transpose(0, 3, 1, 4, 2)
            gl_rows = jnp.pad(gl.reshape(bs, rows, 3), ((0, 0), (0, 0), (0, LANES - 3)))
            bg = jnp.broadcast_to(b_gate[i].reshape(1, 3, kv_c, g_c), (ts, 3, kv_c, g_c)).transpose(2, 0, 3, 1)
            bg_rows = jnp.pad(bg.reshape(rows, 3), ((0, 0), (0, LANES - 3)))
            o = _nsa_sample_b(page_table, qr.reshape(bs, rows, hd), sel.reshape(bs, rows, -1),
                              z3[:, :, o_kvs:o_kvw].reshape(bs, ts * 2 * kv_c, hd),
                              o_cmp.reshape(bs, rows, hd), o_win.reshape(bs, rows, hd), gl_rows, bg_rows,
                              _tile_view(cache_slc_kv[i]), page, kv_c, g_c, ts, n_pages_c)
            xs = _mm_res([_token_cols(o.reshape(bs, kv_c, ts * g_c, hd), ts, g_c).astype(BF16)], w_out, xs, ms, tn_out)

        g_final = norm_final if layer == depth - 1 else None
        wg, wu, wd = w_gate[layer].astype(BF16), w_up[layer].astype(BF16), w_down[layer].astype(BF16)
        xp = _ffn(xp, norm_ffn[layer], wg, wu, wd, g_final, tm_f, tf)
        xs = _ffn(xs, norm_ffn[layer], wg, wu, wd, g_final, ms, tf)

    def stack(name, which):
        parts = outs[name][which]
        return parts[0][None] if len(parts) == 1 else jnp.stack(parts)

    return (xp.reshape(bp, t, d), xs.reshape(bs, ts, d),
            stack("a", 0), stack("a", 1), stack("b", 0), stack("b", 1), stack("f", 0), stack("f", 1),
            stack("c", 0), stack("c", 1), stack("s", 0), stack("s", 1), stack("w", 0), stack("w", 1))
```
